```python
import math
import jax, jax.numpy as jnp
from jax import lax
import numpy as np

D_MODEL = 1024
BATCH = 16
SEQ = 256
DEPTH = 4
DEC_BATCH = 4
DEC_SEQ = 1024
PAST_LEN = 256

GRID_W = 64
N_BRANCH = 4
BRANCH_W = 256
RET_HEADS = 4
RET_DK = 64
RET_DV = 64
S5_GROUPS = 16
S5_GROUP_CH = 16
S5_STATE = 64
GLA_HEADS = 4
GLA_DK = 32
GLA_DV = 64
GLA_RANK = 16
GLA_TAU = 16.0
NA_HEADS = 4
NA_DH = 64
NA_WIN_H = 8
NA_WIN_W = 16
D_FF = 4 * D_MODEL
CHUNK = 64
QBLOCK = 128
ROPE_BASE = 10000.0
EPS = 1e-6

IN_SIZES = (RET_HEADS * RET_DK, RET_HEADS * RET_DK, RET_HEADS * RET_DV, RET_HEADS * RET_DV,
            S5_GROUPS * S5_GROUP_CH,
            GLA_HEADS * GLA_DK, GLA_HEADS * GLA_DK, GLA_HEADS * GLA_DV, GLA_HEADS * GLA_DV, 2 * GLA_RANK,
            NA_HEADS * NA_DH, NA_HEADS * NA_DH, NA_HEADS * NA_DH)
D_IN = sum(IN_SIZES)

kernel_name = 'hybrid_diffusion_gated_branch_step'


def _in_offsets():
    return [int(o) for o in np.cumsum(IN_SIZES)[:-1]]


def _flip(a):
    return jnp.flip(a, axis=1)


def rmsnorm(x, g):
    xf = x.astype(jnp.float32)
    y = xf * lax.rsqrt(jnp.mean(xf * xf, axis=-1, keepdims=True) + EPS)
    return (y * g.astype(jnp.float32)).astype(x.dtype)


def head_norm(o, g):
    B, L, H, d = o.shape
    of = o.astype(jnp.float32)
    mu = jnp.mean(of, axis=-1, keepdims=True)
    xc = of - mu
    y = xc * lax.rsqrt(jnp.mean(xc * xc, axis=-1, keepdims=True) + EPS)
    return y.reshape(B, L, H * d) * g.astype(jnp.float32)


def axial_rope(x):
    B, L, H, d = x.shape
    half = d // 2
    nf = half // 2
    t = jnp.arange(L)
    row = (t // GRID_W).astype(jnp.float32)
    col = (t % GRID_W).astype(jnp.float32)
    inv = ROPE_BASE ** (-jnp.arange(nf, dtype=jnp.float32) / nf)

    def rot(xa, pos):
        ang = pos[:, None] * inv[None, :]
        cos = jnp.cos(ang)[None, :, None, :]
        sin = jnp.sin(ang)[None, :, None, :]
        x1, x2 = xa[..., :nf], xa[..., nf:]
        return jnp.concatenate([x1 * cos - x2 * sin, x1 * sin + x2 * cos], axis=-1)

    xf = x.astype(jnp.float32)
    return jnp.concatenate([rot(xf[..., :half], row), rot(xf[..., half:], col)], axis=-1).astype(x.dtype)


def _chunks(a, B, n, H):
    return a.astype(jnp.float32).reshape(B, n, CHUNK, H, a.shape[-1]).transpose(1, 0, 3, 2, 4)


def retention_scan(q, k, v, log_gamma, s0):
    B, L, H, _ = q.shape
    dv = v.shape[-1]
    n = L // CHUNK
    lg = log_gamma.astype(jnp.float32)
    pos = jnp.arange(CHUNK, dtype=jnp.float32)
    rel = pos[:, None] - pos[None, :]
    causal = rel >= 0
    decay = jnp.where(causal[None], jnp.exp(lg[:, None, None] * jnp.where(causal, rel, 0.0)[None]), 0.0)
    q_dec = jnp.exp(lg[:, None] * (pos + 1.0)[None])
    k_dec = jnp.exp(lg[:, None] * (CHUNK - 1.0 - pos)[None])
    c_dec = jnp.exp(lg * CHUNK)

    def step(s, inp):
        qc, kc, vc = inp
        scores = jnp.einsum('bhid,bhjd->bhij', qc, kc) * decay[None]
        o = (jnp.einsum('bhij,bhjv->bhiv', scores, vc)
             + jnp.einsum('bhid,bhdv->bhiv', qc * q_dec[None, :, :, None], s))
        s = c_dec[None, :, None, None] * s + jnp.einsum('bhjd,bhjv->bhdv', kc * k_dec[None, :, :, None], vc)
        return s, o

    s_fin, o = lax.scan(step, s0.astype(jnp.float32),
                        (_chunks(q, B, n, H), _chunks(k, B, n, H), _chunks(v, B, n, H)))
    return o.transpose(1, 0, 3, 2, 4).reshape(B, L, H, dv), s_fin


def gla_scan(q, k, v, log_a, s0):
    B, L, H, _ = q.shape
    dv = v.shape[-1]
    n = L // CHUNK
    causal = jnp.tril(jnp.ones((CHUNK, CHUNK), dtype=bool))

    def step(s, inp):
        qc, kc, vc, gc = inp
        b = jnp.cumsum(gc, axis=2)
        diff = b[:, :, :, None, :] - b[:, :, None, :, :]
        w = jnp.exp(jnp.where(causal[None, None, :, :, None], diff, -jnp.inf))
        attn = jnp.einsum('bhid,bhjd,bhijd->bhij', qc, kc, w)
        o = (jnp.einsum('bhij,bhjv->bhiv', attn, vc)
             + jnp.einsum('bhid,bhdv->bhiv', qc * jnp.exp(b), s))
        b_end = b[:, :, -1, :]
        s = (jnp.exp(b_end)[..., None] * s
             + jnp.einsum('bhjd,bhjv->bhdv', kc * jnp.exp(b_end[:, :, None, :] - b), vc))
        return s, o

    s_fin, o = lax.scan(step, s0.astype(jnp.float32),
                        (_chunks(q, B, n, H), _chunks(k, B, n, H), _chunks(v, B, n, H), _chunks(log_a, B, n, H)))
    return o.transpose(1, 0, 3, 2, 4).reshape(B, L, H, dv), s_fin


def _ssm_combine(e1, e2):
    a1, b1 = e1
    a2, b2 = e2
    return a1 * a2, a2 * b1 + b2


def s5_scan(u, lam_re, lam_im, log_dt, b_re, b_im, c_re, c_im, x0):
    L = u.shape[1]
    lam = lax.complex(lam_re.astype(jnp.float32), lam_im.astype(jnp.float32))
    lam_dt = lam * jnp.exp(log_dt.astype(jnp.float32))[:, None]
    lam_bar = jnp.exp(lam_dt)
    b_bar = ((lam_bar - 1.0) / lam)[:, :, None] * lax.complex(b_re.astype(jnp.float32), b_im.astype(jnp.float32))
    c_mat = lax.complex(c_re.astype(jnp.float32), c_im.astype(jnp.float32))
    bu = jnp.einsum('blgh,gph->blgp', u.astype(jnp.complex64), b_bar)
    a = jnp.broadcast_to(lam_bar, bu.shape)
    _, xs = lax.associative_scan(_ssm_combine, (a, bu), axis=1)
    steps = jnp.arange(1, L + 1, dtype=jnp.float32)
    xs = xs + jnp.exp(lam_dt[None] * steps[:, None, None])[None] * x0[:, None]
    y = jnp.einsum('ghp,blgp->blgh', c_mat, xs).real
    return y, xs[:, -1]


def s5_direction(u, lp, d, x0):
    return s5_scan(u, lp['s5_lambda_re'][d], lp['s5_lambda_im'][d], lp['s5_log_dt'][d],
                   lp['s5_b_re'][d], lp['s5_b_im'][d], lp['s5_c_re'][d], lp['s5_c_im'][d], x0)


def context_attention(q, k, v):
    B, L, H, d = q.shape
    nb = L // QBLOCK
    scale = d ** -0.5
    qb = q.reshape(B, nb, QBLOCK, H, d).transpose(1, 0, 2, 3, 4)

    def blk(qi):
        s = jnp.einsum('bqhd,bkhd->bhqk', qi, k).astype(jnp.float32) * scale
        p = jax.nn.softmax(s, axis=-1)
        return jnp.einsum('bhqk,bkhd->bqhd', p, v.astype(jnp.float32))

    o = lax.map(blk, qb)
    return o.transpose(1, 0, 2, 3, 4).reshape(B, L, H, d)


def na_latent(q, k, v, k_ctx, v_ctx, rpb):
    B, L, H, d = q.shape
    rows = L // GRID_W
    kh = min(NA_WIN_H, rows)
    scale = d ** -0.5
    qg = q.reshape(B, rows, GRID_W, H, d)
    kg = k.reshape(B, rows, GRID_W, H, d)
    vg = v.reshape(B, rows, GRID_W, H, d)
    col = jnp.arange(GRID_W)
    col_start = jnp.clip(col - NA_WIN_W // 2, 0, GRID_W - NA_WIN_W)
    col_in = (col[None, :] >= col_start[:, None]) & (col[None, :] < col_start[:, None] + NA_WIN_W)
    col_idx = jnp.clip(col[None, :] - col[:, None] + NA_WIN_W - 1, 0, 2 * NA_WIN_W - 2)
    rpb_cols = rpb[:, :, col_idx]
    n_loc = kh * GRID_W

    def row_fn(r):
        rs = jnp.clip(r - kh // 2, 0, rows - kh)
        q_r = lax.dynamic_index_in_dim(qg, r, axis=1, keepdims=False)
        k_r = lax.dynamic_slice_in_dim(kg, rs, kh, axis=1)
        v_r = lax.dynamic_slice_in_dim(vg, rs, kh, axis=1)
        row_idx = rs + jnp.arange(kh) - r + NA_WIN_H - 1
        bias = jnp.take(rpb_cols, row_idx, axis=1).transpose(0, 2, 1, 3)
        s_loc = jnp.einsum('bqhd,bikhd->bhqik', q_r, k_r).astype(jnp.float32) * scale + bias[None]
        s_loc = jnp.where(col_in[None, None, :, None, :], s_loc, -jnp.inf).reshape(B, H, GRID_W, n_loc)
        s_ctx = jnp.einsum('bqhd,bchd->bhqc', q_r, k_ctx).astype(jnp.float32) * scale
        p = jax.nn.softmax(jnp.concatenate([s_loc, s_ctx], axis=-1), axis=-1)
        return (jnp.einsum('bhqn,bnhd->bqhd', p[..., :n_loc], v_r.reshape(B, n_loc, H, d).astype(jnp.float32))
                + jnp.einsum('bhqc,bchd->bqhd', p[..., n_loc:], v_ctx.astype(jnp.float32)))

    o = lax.map(row_fn, jnp.arange(rows))
    return o.transpose(1, 0, 2, 3, 4).reshape(B, L, H, d)


def token_mixer(h, lp, cache):
    B, L, _ = h.shape
    latent = cache is not None
    proj = h @ lp['w_in']
    (rq, rk, rv, rg, su, gq, gk, gv, gg, glr, nq, nk, nv) = jnp.split(proj, _in_offsets(), axis=-1)
    if latent:
        c_k, c_v, st_ret0, st_s5_0, st_gla0 = cache
        s5_x0 = lax.complex(st_s5_0[..., 0].astype(jnp.float32), st_s5_0[..., 1].astype(jnp.float32))
    else:
        st_ret0 = jnp.zeros((B, 2, RET_HEADS, RET_DK, RET_DV), jnp.float32)
        s5_x0 = jnp.zeros((B, 2, S5_GROUPS, S5_STATE), jnp.complex64)
        st_gla0 = jnp.zeros((B, 2, GLA_HEADS, GLA_DK, GLA_DV), jnp.float32)

    rq = rq.reshape(B, L, RET_HEADS, RET_DK)
    rk = rk.reshape(B, L, RET_HEADS, RET_DK)
    rv = rv.reshape(B, L, RET_HEADS, RET_DV)
    if latent:
        rq = axial_rope(rq)
        rk = axial_rope(rk)
    rk = rk * RET_DK ** -0.5
    o_f, sr_f = retention_scan(rq, rk, rv, lp['ret_log_decay'][0], st_ret0[:, 0])
    o_b, sr_b = retention_scan(_flip(rq), _flip(rk), _flip(rv), lp['ret_log_decay'][1], st_ret0[:, 1])
    ret_out = head_norm(o_f + _flip(o_b), lp['ret_gn']) * jax.nn.silu(rg.astype(jnp.float32))

    u = su.reshape(B, L, S5_GROUPS, S5_GROUP_CH).astype(jnp.float32)
    y_f, xs_f = s5_direction(u, lp, 0, s5_x0[:, 0])
    y_b, xs_b = s5_direction(_flip(u), lp, 1, s5_x0[:, 1])
    y = y_f + _flip(y_b) + lp['s5_d'].astype(jnp.float32).reshape(S5_GROUPS, S5_GROUP_CH) * u
    y = jax.nn.gelu(y.reshape(B, L, BRANCH_W))
    glu_a, glu_g = jnp.split(y @ lp['s5_w_glu'] + lp['s5_b_glu'], 2, axis=-1)
    s5_out = glu_a * jax.nn.sigmoid(glu_g)

    gq = gq.reshape(B, L, GLA_HEADS, GLA_DK) * GLA_DK ** -0.5
    gk = gk.reshape(B, L, GLA_HEADS, GLA_DK)
    gv = gv.reshape(B, L, GLA_HEADS, GLA_DV)
    lr_f, lr_b = jnp.split(glr, 2, axis=-1)
    la_f = (jax.nn.log_sigmoid((lr_f @ lp['gla_w_gate'][0] + lp['gla_b_gate'][0]).astype(jnp.float32))
            / GLA_TAU).reshape(B, L, GLA_HEADS, GLA_DK)
    la_b = (jax.nn.log_sigmoid((lr_b @ lp['gla_w_gate'][1] + lp['gla_b_gate'][1]).astype(jnp.float32))
            / GLA_TAU).reshape(B, L, GLA_HEADS, GLA_DK)
    og_f, sg_f = gla_scan(gq, gk, gv, la_f, st_gla0[:, 0])
    og_b, sg_b = gla_scan(_flip(gq), _flip(gk), _flip(gv), _flip(la_b), st_gla0[:, 1])
    gla_out = head_norm(og_f + _flip(og_b), lp['gla_gn']) * jax.nn.silu(gg.astype(jnp.float32))

    nq = nq.reshape(B, L, NA_HEADS, NA_DH)
    nk = nk.reshape(B, L, NA_HEADS, NA_DH)
    nv = nv.reshape(B, L, NA_HEADS, NA_DH)
    if latent:
        na_o = na_latent(nq, nk, nv, c_k, c_v, lp['na_rpb'])
    else:
        na_o = context_attention(nq, nk, nv)
    na_out = na_o.reshape(B, L, BRANCH_W)

    branches = jnp.stack([ret_out, s5_out, gla_out, na_out], axis=2).astype(h.dtype)
    up = jnp.einsum('blnw,nwd->blnd', branches, lp['w_branch'])
    gates = jax.nn.sigmoid(h @ lp['w_merge'] + lp['b_merge']).reshape(B, L, N_BRANCH, D_MODEL)
    out = jnp.sum(gates * up, axis=2) @ lp['w_out']
    if latent:
        return out, None
    st_s5 = jnp.stack([xs_f, xs_b], axis=1)
    new = (nk, nv, jnp.stack([sr_f, sr_b], axis=1),
           jnp.stack([st_s5.real, st_s5.imag], axis=-1),
           jnp.stack([sg_f, sg_b], axis=1))
    return out, new


def trunk_layer(x, mod, lp, cache):
    sh1, sc1, g1, sh2, sc2, g2 = jnp.split(mod, 6, axis=-1)
    h = rmsnorm(x, lp['g_norm'][0]) * (1.0 + sc1) + sh1
    m, ctx_tensors = token_mixer(h, lp, cache)
    x = x + g1 * rmsnorm(m, lp['g_norm'][1])
    h = rmsnorm(x, lp['g_norm'][2]) * (1.0 + sc2) + sh2
    f = jnp.square(jax.nn.relu(h @ lp['w_mlp1'])) @ lp['w_mlp2']
    x = x + g2 * rmsnorm(f, lp['g_norm'][3])
    return x, ctx_tensors


def setup_inputs(seed: int = 0) -> dict:
    key = jax.random.key(seed)
    ks = jax.random.split(key, 36)

    def nrm(k, shape, s):
        return jax.random.normal(k, shape, jnp.float32) * s

    ret_base = jnp.log(1.0 - 2.0 ** (-5.0 - jnp.arange(RET_HEADS, dtype=jnp.float32)))
    return {
        'x_prompt': nrm(ks[0], (BATCH, SEQ, D_MODEL), 1.0),
        'x_sample': nrm(ks[1], (DEC_BATCH, DEC_SEQ, D_MODEL), 1.0),
        'c': nrm(ks[2], (DEC_BATCH, D_MODEL), 1.0),
        'cache_na_k': nrm(ks[3], (DEC_BATCH, DEPTH, PAST_LEN, NA_HEADS, NA_DH), 1.0),
        'cache_na_v': nrm(ks[4], (DEC_BATCH, DEPTH, PAST_LEN, NA_HEADS, NA_DH), 1.0),
        'state_ret': nrm(ks[5], (DEC_BATCH, DEPTH, 2, RET_HEADS, RET_DK, RET_DV), 1.0),
        'state_s5': nrm(ks[6], (DEC_BATCH, DEPTH, 2, S5_GROUPS, S5_STATE, 2), 0.3),
        'state_gla': nrm(ks[7], (DEC_BATCH, DEPTH, 2, GLA_HEADS, GLA_DK, GLA_DV), 1.0),
        'c_ctx': nrm(ks[8], (D_MODEL,), 1.0),
        'w_ada': nrm(ks[9], (DEPTH, D_MODEL, 6 * D_MODEL), 0.5 * D_MODEL ** -0.5),
        'b_ada': nrm(ks[10], (DEPTH, 6 * D_MODEL), 0.02),
        'g_norm': 1.0 + nrm(ks[11], (DEPTH, 4, D_MODEL), 0.02),
        'w_in': nrm(ks[12], (DEPTH, D_MODEL, D_IN), D_MODEL ** -0.5),
        'ret_log_decay': ret_base[None, None, :] * (1.0 + nrm(ks[13], (DEPTH, 2, RET_HEADS), 0.05)),
        'ret_gn': 1.0 + nrm(ks[14], (DEPTH, BRANCH_W), 0.02),
        's5_lambda_re': -0.5 + nrm(ks[15], (DEPTH, 2, S5_GROUPS, S5_STATE), 0.01),
        's5_lambda_im': math.pi * jnp.arange(S5_STATE, dtype=jnp.float32) + nrm(ks[16], (DEPTH, 2, S5_GROUPS, S5_STATE), 0.01),
        's5_log_dt': jax.random.uniform(ks[17], (DEPTH, 2, S5_GROUPS), jnp.float32, math.log(1e-3), math.log(1e-1)),
        's5_b_re': nrm(ks[18], (DEPTH, 2, S5_GROUPS, S5_STATE, S5_GROUP_CH), (2 * S5_GROUP_CH) ** -0.5),
        's5_b_im': nrm(ks[19], (DEPTH, 2, S5_GROUPS, S5_STATE, S5_GROUP_CH), (2 * S5_GROUP_CH) ** -0.5),
        's5_c_re': nrm(ks[20], (DEPTH, 2, S5_GROUPS, S5_GROUP_CH, S5_STATE), S5_STATE ** -0.5),
        's5_c_im': nrm(ks[21], (DEPTH, 2, S5_GROUPS, S5_GROUP_CH, S5_STATE), S5_STATE ** -0.5),
        's5_d': nrm(ks[22], (DEPTH, BRANCH_W), 1.0),
        's5_w_glu': nrm(ks[23], (DEPTH, BRANCH_W, 2 * BRANCH_W), BRANCH_W ** -0.5),
        's5_b_glu': nrm(ks[24], (DEPTH, 2 * BRANCH_W), 0.02),
        'gla_w_gate': nrm(ks[25], (DEPTH, 2, GLA_RANK, GLA_HEADS * GLA_DK), GLA_RANK ** -0.5),
        'gla_b_gate': nrm(ks[26], (DEPTH, 2, GLA_HEADS * GLA_DK), 0.1),
        'gla_gn': 1.0 + nrm(ks[27], (DEPTH, BRANCH_W), 0.02),
        'na_rpb': nrm(ks[28], (DEPTH, NA_HEADS, 2 * NA_WIN_H - 1, 2 * NA_WIN_W - 1), 0.02),
        'w_branch': nrm(ks[29], (DEPTH, N_BRANCH, BRANCH_W, D_MODEL), BRANCH_W ** -0.5),
        'w_merge': nrm(ks[30], (DEPTH, D_MODEL, N_BRANCH * D_MODEL), D_MODEL ** -0.5),
        'b_merge': nrm(ks[31], (DEPTH, N_BRANCH * D_MODEL), 0.02),
        'w_out': nrm(ks[32], (DEPTH, D_MODEL, D_MODEL), D_MODEL ** -0.5),
        'w_mlp1': nrm(ks[33], (DEPTH, D_MODEL, D_FF), D_MODEL ** -0.5),
        'w_mlp2': nrm(ks[34], (DEPTH, D_FF, D_MODEL), D_FF ** -0.5),
    }


def reference(x_prompt, x_sample, c, cache_na_k, cache_na_v, state_ret, state_s5, state_gla,
              c_ctx, w_ada, b_ada, g_norm, w_in, ret_log_decay, ret_gn,
              s5_lambda_re, s5_lambda_im, s5_log_dt, s5_b_re, s5_b_im, s5_c_re, s5_c_im,
              s5_d, s5_w_glu, s5_b_glu, gla_w_gate, gla_b_gate, gla_gn, na_rpb,
              w_branch, w_merge, b_merge, w_out, w_mlp1, w_mlp2):
    y_prompt = x_prompt
    y_sample = x_sample
    ks_l, vs_l, ret_l, s5_l, gla_l = [], [], [], [], []
    for l in range(DEPTH):
        lp = {'g_norm': g_norm[l], 'w_in': w_in[l], 'ret_log_decay': ret_log_decay[l], 'ret_gn': ret_gn[l],
              's5_lambda_re': s5_lambda_re[l], 's5_lambda_im': s5_lambda_im[l], 's5_log_dt': s5_log_dt[l],
              's5_b_re': s5_b_re[l], 's5_b_im': s5_b_im[l], 's5_c_re': s5_c_re[l], 's5_c_im': s5_c_im[l],
              's5_d': s5_d[l], 's5_w_glu': s5_w_glu[l], 's5_b_glu': s5_b_glu[l],
              'gla_w_gate': gla_w_gate[l], 'gla_b_gate': gla_b_gate[l], 'gla_gn': gla_gn[l],
              'na_rpb': na_rpb[l], 'w_branch': w_branch[l], 'w_merge': w_merge[l], 'b_merge': b_merge[l],
              'w_out': w_out[l], 'w_mlp1': w_mlp1[l], 'w_mlp2': w_mlp2[l]}
        mod_ctx = (jax.nn.silu(c_ctx) @ w_ada[l] + b_ada[l])[None, None, :]
        y_prompt, (k_l, v_l, r_l, s_l, g_l) = trunk_layer(y_prompt, mod_ctx, lp, None)
        ks_l.append(k_l)
        vs_l.append(v_l)
        ret_l.append(r_l)
        s5_l.append(s_l)
        gla_l.append(g_l)
        mod_lat = (jax.nn.silu(c) @ w_ada[l] + b_ada[l])[:, None, :]
        cache_l = (cache_na_k[:, l], cache_na_v[:, l], state_ret[:, l], state_s5[:, l], state_gla[:, l])
        y_sample, _ = trunk_layer(y_sample, mod_lat, lp, cache_l)
    new_na_k = jnp.stack(ks_l, axis=1)
    new_na_v = jnp.stack(vs_l, axis=1)
    new_state_ret = jnp.stack(ret_l, axis=1)
    new_state_s5 = jnp.stack(s5_l, axis=1)
    new_state_gla = jnp.stack(gla_l, axis=1)
    return (y_prompt, y_sample, new_na_k, new_na_v, new_state_ret, new_state_s5, new_state_gla)
```

```python
import functools
import math

import jax
import jax.numpy as jnp
from jax import lax
from jax.experimental import pallas as pl
from jax.experimental.pallas import tpu as pltpu

F32 = jnp.float32
BF16 = jnp.bfloat16

D_MODEL = 1024
BATCH = 16
SEQ = 256
DEPTH = 4
DEC_BATCH = 4
DEC_SEQ = 1024
PAST_LEN = 256
GRID_W = 64
N_BRANCH = 4
BRANCH_W = 256
RET_HEADS = 4
RET_DK = 64
RET_DV = 64
S5_GROUPS = 16
S5_GROUP_CH = 16
S5_STATE = 64
GLA_HEADS = 4
GLA_DK = 32
GLA_DV = 64
GLA_RANK = 16
GLA_TAU = 16.0
NA_HEADS = 4
NA_DH = 64
NA_WIN_H = 8
NA_WIN_W = 16
D_FF = 4 * D_MODEL
ROPE_BASE = 10000.0
EPS = 1e-6

T_CTX = BATCH * SEQ
T_LAT = DEC_BATCH * DEC_SEQ
T_ALL = T_CTX + T_LAT
N_MOD_ROWS = 8
TOKEN_TILE = 512
GLA_CHUNK = 64
S5_CHUNK = 16
S5_GROUP_BLOCK = 4
S5_LAT_ROWS = 8
RET_QBLOCK = 256
NA_QBLOCK = 256
VMEM_LIMIT = 56 * 1024 * 1024

_IN_SIZES = (256, 256, 256, 256, 256, 128, 128, 256, 256, 32, 256, 256, 256)
_IN_OFFS = [0]
for _s in _IN_SIZES:
    _IN_OFFS.append(_IN_OFFS[-1] + _s)
W_IN_PACKED = 2944


def _cparams(sem):
    return pltpu.CompilerParams(dimension_semantics=sem, vmem_limit_bytes=VMEM_LIMIT)


def _bdot(a, b):
    return jnp.dot(a.astype(BF16), b.astype(BF16), preferred_element_type=F32)


def _bdot_nt(a, b):
    return lax.dot_general(a.astype(BF16), b.astype(BF16), (((1,), (1,)), ((), ())),
                           preferred_element_type=F32)


def _bdot_tn(a, b):
    return lax.dot_general(a.astype(BF16), b.astype(BF16), (((0,), (0,)), ((), ())),
                           preferred_element_type=F32)


def _split(a):
    hi = a.astype(BF16)
    lo = (a - hi.astype(F32)).astype(BF16)
    return hi, lo


def _dot3(a, b):
    ah, al = _split(a)
    bh, bl = _split(b)
    d = functools.partial(jnp.dot, preferred_element_type=F32)
    return d(ah, bh) + d(al, bh) + d(ah, bl)


def _sigmoid(x):
    return 1.0 / (1.0 + jnp.exp(-x))


def _silu(x):
    return x * _sigmoid(x)


def _rms(x, g):
    return x * lax.rsqrt(jnp.mean(x * x, axis=-1, keepdims=True) + EPS) * g


def _group_norm(o, g):
    mu = jnp.mean(o, axis=-1, keepdims=True)
    xc = o - mu
    return xc * lax.rsqrt(jnp.mean(xc * xc, axis=-1, keepdims=True) + EPS) * g


def _mod_row(i):
    ctx_tiles = T_CTX // TOKEN_TILE
    return jnp.where(i < ctx_tiles, 0, 1 + (i - ctx_tiles) // (DEC_SEQ // TOKEN_TILE))


ADA_TILE = 1536


def _ada_kernel(c_ref, w_ref, b_ref, o_ref):
    a = _silu(c_ref[...])
    o_ref[0] = _bdot(a, w_ref[0]) + b_ref[0]


def _ada(cc, w_ada, b_ada):
    n = 6 * D_MODEL
    return pl.pallas_call(
        _ada_kernel,
        grid=(DEPTH, n // ADA_TILE),
        in_specs=[pl.BlockSpec((N_MOD_ROWS, D_MODEL), lambda l, j: (0, 0)),
                  pl.BlockSpec((1, D_MODEL, ADA_TILE), lambda l, j: (l, 0, j)),
                  pl.BlockSpec((1, 1, ADA_TILE), lambda l, j: (l, 0, j))],
        out_specs=pl.BlockSpec((1, N_MOD_ROWS, ADA_TILE), lambda l, j: (l, 0, j)),
        out_shape=jax.ShapeDtypeStruct((DEPTH, N_MOD_ROWS, n), F32),
        compiler_params=_cparams(("parallel", "parallel")),
        name="ada_mod",
    )(cc, w_ada, b_ada.reshape(DEPTH, 1, n))


def _inproj_kernel(x_ref, mod_ref, g_ref, w_ref,
                   ret_ref, s5_ref, gqk_ref, gv_ref, gg_ref, glr_ref, nq_ref, nk_ref, nv_ref):
    mod = mod_ref[0]
    h = _rms(x_ref[...], g_ref[0:1]) * (1.0 + mod[1:2]) + mod[0:1]
    hb = h.astype(BF16)

    def proj(lo, hi):
        return jnp.dot(hb, w_ref[:, lo:hi], preferred_element_type=F32)

    ret_ref[...] = proj(0, 1024)
    s5_ref[...] = proj(1024, 1280)
    gqk_ref[...] = proj(1280, 1536)
    gv_ref[...] = proj(1536, 1792)
    gg_ref[...] = proj(1792, 2048)
    for ref, lo in ((nq_ref, 2048), (nk_ref, 2304), (nv_ref, 2560)):
        r = proj(lo, lo + 256)
        for hh in range(NA_HEADS):
            ref[hh] = r[:, hh * NA_DH:(hh + 1) * NA_DH]
    glr_ref[...] = proj(2816, 2944)


def _inproj(x, mod_l, g_l, w_in_p):
    tm = TOKEN_TILE
    tok = lambda w: pl.BlockSpec((tm, w), lambda i: (i, 0))
    head = pl.BlockSpec((NA_HEADS, tm, NA_DH), lambda i: (0, i, 0))
    tshape = lambda w: jax.ShapeDtypeStruct((T_ALL, w), F32)
    hshape = jax.ShapeDtypeStruct((NA_HEADS, T_ALL, NA_DH), F32)
    return pl.pallas_call(
        _inproj_kernel,
        grid=(T_ALL // tm,),
        in_specs=[tok(D_MODEL),
                  pl.BlockSpec((1, 6, D_MODEL), lambda i: (_mod_row(i), 0, 0)),
                  pl.BlockSpec((4, D_MODEL), lambda i: (0, 0)),
                  pl.BlockSpec((D_MODEL, W_IN_PACKED), lambda i: (0, 0))],
        out_specs=[tok(1024), tok(256), tok(256), tok(256), tok(256), tok(128), head, head, head],
        out_shape=[tshape(1024), tshape(256), tshape(256), tshape(256), tshape(256), tshape(128),
                   hshape, hshape, hshape],
        compiler_params=_cparams(("parallel",)),
        name="in_proj",
    )(x, mod_l, g_l, w_in_p)


def _rope_rotate(x, lane):
    first = (lane % 32) < 16
    w = x.shape[-1]
    return jnp.where(first, pltpu.roll(x, w - 16, 1), pltpu.roll(x, 16, 1))


def _ret_kernel(ld_ref, ret_ref, gn_ref, *rest, seq, latent):
    if latent:
        cos_ref, sin_ref, s0_ref, out_ref = rest
    else:
        out_ref, st_ref = rest
    q = ret_ref[:, 0:256]
    k = ret_ref[:, 256:512]
    if latent:
        lane = lax.broadcasted_iota(jnp.int32, (seq, 256), 1)
        cos = cos_ref[...]
        sin = sin_ref[...]
        q = q * cos + _rope_rotate(q, lane) * sin
        k = k * cos + _rope_rotate(k, lane) * sin
    k = k * (RET_DK ** -0.5)
    tq = RET_QBLOCK
    col = lax.broadcasted_iota(jnp.int32, (tq, seq), 1)
    row = lax.broadcasted_iota(jnp.int32, (tq, seq), 0)
    pos_c = lax.broadcasted_iota(jnp.int32, (seq, 1), 0).astype(F32)
    for h in range(RET_HEADS):
        lgf = ld_ref[0, h]
        lgb = ld_ref[1, h]
        sl = slice(h * RET_DK, (h + 1) * RET_DK)
        qh = q[:, sl]
        kh = k[:, sl]
        vh = ret_ref[:, 512 + h * RET_DV:512 + (h + 1) * RET_DV]
        kb = kh.astype(BF16)
        vb = vh.astype(BF16)
        if latent:
            q_init = jnp.concatenate([qh * jnp.exp(lgf * (pos_c + 1.0)),
                                      qh * jnp.exp(lgb * (seq - pos_c))], axis=1)
            s_init = jnp.concatenate([s0_ref[0, 0, h], s0_ref[0, 1, h]], axis=0)
        for qb in range(seq // tq):
            rel = (row + (qb * tq) - col).astype(F32)
            decay = (jnp.where(rel >= 0, jnp.exp(lgf * jnp.maximum(rel, 0.0)), 0.0)
                     + jnp.where(rel <= 0, jnp.exp(lgb * jnp.maximum(-rel, 0.0)), 0.0))
            rows = slice(qb * tq, (qb + 1) * tq)
            s = _bdot_nt(qh[rows], kb) * decay
            o = _bdot(s, vb)
            if latent:
                o = o + _bdot(q_init[rows], s_init)
            g = ret_ref[rows, 768 + h * RET_DV:768 + (h + 1) * RET_DV]
            out_ref[rows, sl] = _group_norm(o, gn_ref[:, sl]) * _silu(g)
        if not latent:
            st_ref[0, 0, h] = _bdot_tn(kh * jnp.exp(lgf * (seq - 1.0 - pos_c)), vb)
            st_ref[0, 1, h] = _bdot_tn(kh * jnp.exp(lgb * pos_c), vb)


def _retention(ret, ld, gn, *, latent, cos=None, sin=None, s0=None):
    seq = DEC_SEQ if latent else SEQ
    nb = DEC_BATCH if latent else BATCH
    off = T_CTX // DEC_SEQ if latent else 0
    in_specs = [pl.BlockSpec(memory_space=pltpu.SMEM),
                pl.BlockSpec((seq, 1024), lambda b: (b + off, 0)),
                pl.BlockSpec((1, 256), lambda b: (0, 0))]
    args = [ld, ret, gn]
    out_specs = [pl.BlockSpec((seq, 256), lambda b: (b, 0))]
    out_shape = [jax.ShapeDtypeStruct((nb * seq, 256), F32)]
    if latent:
        in_specs += [pl.BlockSpec((seq, 256), lambda b: (0, 0)),
                     pl.BlockSpec((seq, 256), lambda b: (0, 0)),
                     pl.BlockSpec((1, 2, RET_HEADS, RET_DK, RET_DV), lambda b: (b, 0, 0, 0, 0))]
        args += [cos, sin, s0]
    else:
        out_specs.append(pl.BlockSpec((1, 2, RET_HEADS, RET_DK, RET_DV), lambda b: (b, 0, 0, 0, 0)))
        out_shape.append(jax.ShapeDtypeStruct((nb, 2, RET_HEADS, RET_DK, RET_DV), F32))
    return pl.pallas_call(
        functools.partial(_ret_kernel, seq=seq, latent=latent),
        grid=(nb,),
        in_specs=in_specs, out_specs=out_specs, out_shape=out_shape,
        compiler_params=_cparams(("parallel",)),
        name="retention_lat" if latent else "retention_ctx",
    )(*args)


def _gla_kernel(gqk_ref, gv_ref, gg_ref, glr_ref, wg_ref, bg_ref, gn_ref, *rest, seq, latent):
    if latent:
        s0_ref, out_ref, gate_s, o_s, st_s = rest
    else:
        out_ref, st_ref, gate_s, o_s, st_s = rest
    c = GLA_CHUNK
    n = seq // c
    hk = GLA_HEADS * GLA_DK
    lr = glr_ref[...]
    for d in range(2):
        pre = _bdot(lr[:, d * GLA_RANK:(d + 1) * GLA_RANK], wg_ref[d]) + bg_ref[d:d + 1]
        gate_s[d] = (jnp.minimum(pre, 0.0) - jnp.log(1.0 + jnp.exp(-jnp.abs(pre)))) / GLA_TAU
        st_s[d] = s0_ref[0, d] if latent else jnp.zeros((GLA_HEADS * GLA_DV, hk), F32)

    ti = lax.broadcasted_iota(jnp.int32, (c, c), 0)
    tj = lax.broadcasted_iota(jnp.int32, (c, c), 1)
    tri = [(tj <= ti).astype(BF16), (tj >= ti).astype(BF16)]
    lane_k = lax.broadcasted_iota(jnp.int32, (c, hk), 1)
    head_mask = [(lane_k // GLA_DK) == h for h in range(GLA_HEADS)]
    ai = lax.broadcasted_iota(jnp.int32, (GLA_HEADS * c, c), 0) % c
    aj = lax.broadcasted_iota(jnp.int32, (GLA_HEADS * c, c), 1)
    keep = [aj <= ai, aj >= ai]
    sr = lax.broadcasted_iota(jnp.int32, (GLA_HEADS * GLA_DV, hk), 0) // GLA_DV
    sc = lax.broadcasted_iota(jnp.int32, (GLA_HEADS * GLA_DV, hk), 1) // GLA_DK
    diag = sr == sc
    mid = (c // 2 - 1, c // 2)
    end = (c - 1, 0)
    scale = GLA_DK ** -0.5

    def one(ci, d):
        rows = pl.ds(pl.multiple_of(ci * c, c), c)
        q = gqk_ref[rows, 0:hk] * scale
        k = gqk_ref[rows, hk:2 * hk]
        v = gv_ref[rows, :].astype(BF16)
        gh, gl = _split(gate_s[d, rows, :])
        b = (jnp.dot(tri[d], gh, preferred_element_type=F32)
             + jnp.dot(tri[d], gl, preferred_element_type=F32))
        b_mid = b[mid[d]:mid[d] + 1]
        b_end = b[end[d]:end[d] + 1]
        q_att = q * jnp.exp(b - b_mid)
        k_att = k * jnp.exp(b_mid - b)
        q_st = q * jnp.exp(b)
        k_st = k * jnp.exp(b_end - b)
        q_stack = jnp.concatenate([jnp.where(head_mask[h], q_att, 0.0) for h in range(GLA_HEADS)], axis=0)
        att = jnp.where(keep[d], _bdot_nt(q_stack, k_att), 0.0).astype(BF16)
        o = jnp.concatenate(
            [jnp.dot(att[h * c:(h + 1) * c], v[:, h * GLA_DV:(h + 1) * GLA_DV], preferred_element_type=F32)
             for h in range(GLA_HEADS)], axis=1)
        st = st_s[d]
        o_s[d, rows, :] = o + _bdot_nt(q_st, st)
        st_s[d] = st * jnp.exp(b_end) + jnp.where(diag, _bdot_tn(v, k_st), 0.0)

    def body(i, carry):
        one(i, 0)
        one(n - 1 - i, 1)
        return carry

    lax.fori_loop(0, n, body, 0)

    o = o_s[0] + o_s[1]
    for h in range(GLA_HEADS):
        sl = slice(h * GLA_DV, (h + 1) * GLA_DV)
        out_ref[:, sl] = _group_norm(o[:, sl], gn_ref[:, sl]) * _silu(gg_ref[:, sl])
    if not latent:
        st_ref[0, 0] = st_s[0]
        st_ref[0, 1] = st_s[1]


def _gla(gqk, gv, gg, glr, wg, bg, gn, *, latent, s0=None):
    seq = DEC_SEQ if latent else SEQ
    nb = DEC_BATCH if latent else BATCH
    off = T_CTX // DEC_SEQ if latent else 0
    hk = GLA_HEADS * GLA_DK
    hv = GLA_HEADS * GLA_DV
    tok = lambda w: pl.BlockSpec((seq, w), lambda b: (b + off, 0))
    in_specs = [tok(256), tok(256), tok(256), tok(128),
                pl.BlockSpec((2, GLA_RANK, hk), lambda b: (0, 0, 0)),
                pl.BlockSpec((2, hk), lambda b: (0, 0)),
                pl.BlockSpec((1, 256), lambda b: (0, 0))]
    args = [gqk, gv, gg, glr, wg, bg, gn]
    out_specs = [pl.BlockSpec((seq, 256), lambda b: (b, 0))]
    out_shape = [jax.ShapeDtypeStruct((nb * seq, 256), F32)]
    st_spec = pl.BlockSpec((1, 2, hv, hk), lambda b: (b, 0, 0, 0))
    if latent:
        in_specs.append(st_spec)
        args.append(s0)
    else:
        out_specs.append(st_spec)
        out_shape.append(jax.ShapeDtypeStruct((nb, 2, hv, hk), F32))
    return pl.pallas_call(
        functools.partial(_gla_kernel, seq=seq, latent=latent),
        grid=(nb,),
        in_specs=in_specs, out_specs=out_specs, out_shape=out_shape,
        scratch_shapes=[pltpu.VMEM((2, seq, hk), F32), pltpu.VMEM((2, seq, hv), F32),
                        pltpu.VMEM((2, hv, hk), F32)],
        compiler_params=_cparams(("parallel",)),
        name="gla_lat" if latent else "gla_ctx",
    )(*args)


def _s5_lagk_kernel(c_ref, w_ref, o_ref):
    o_ref[0] = _dot3(c_ref[0], w_ref[0])


def _s5_lag_kernels(cc, wm):
    n, hc, p2 = cc.shape
    w = wm.shape[-1]
    return pl.pallas_call(
        _s5_lagk_kernel,
        grid=(n,),
        in_specs=[pl.BlockSpec((1, hc, p2), lambda i: (i, 0, 0)),
                  pl.BlockSpec((1, p2, w), lambda i: (i, 0, 0))],
        out_specs=pl.BlockSpec((1, hc, w), lambda i: (i, 0, 0)),
        out_shape=jax.ShapeDtypeStruct((n, hc, w), F32),
        compiler_params=_cparams(("parallel",)),
        name="s5_lag_kernels",
    )(cc, wm)


S5_W = S5_CHUNK * S5_GROUP_CH
S5_P2 = 2 * S5_STATE
S5_ROWS_CTX = (SEQ // S5_CHUNK) * BATCH
S5_ROWS_LAT = (DEC_SEQ // S5_CHUNK) * S5_LAT_ROWS
S5_ROWS = S5_ROWS_CTX + S5_ROWS_LAT


def _s5_kernel(u_ref, tef_ref, ff_ref, lam_ref, x0_ref, y_ref, fin_ref, xs_s, ps_s):
    gb = S5_GROUP_BLOCK
    for g in range(gb):
        u = u_ref[:, g * S5_W:(g + 1) * S5_W]
        r = _dot3(u, tef_ref[g])
        y_ref[:, g * S5_W:(g + 1) * S5_W] = r[:, 0:S5_W]
        xs_s[0, :, g * S5_P2:(g + 1) * S5_P2] = r[:, S5_W:S5_W + S5_P2]
        xs_s[1, :, g * S5_P2:(g + 1) * S5_P2] = r[:, S5_W + S5_P2:S5_W + 2 * S5_P2]

    w = gb * S5_P2
    lane = lax.broadcasted_iota(jnp.int32, (1, w), 1)
    first = (lane % S5_P2) < S5_STATE

    def swap(s):
        return jnp.where(first, pltpu.roll(s, w - S5_STATE, 1), pltpu.roll(s, S5_STATE, 1))

    def carry(base, nrows, nchunks, init_f, init_b):
        a_f, b_f = lam_ref[0, 0:1], lam_ref[0, 1:2]
        a_b, b_b = lam_ref[1, 0:1], lam_ref[1, 1:2]

        def body(i, st):
            sf, sb = st
            rf = pl.ds(pl.multiple_of(base + i * nrows, 8), nrows)
            rb = pl.ds(pl.multiple_of(base + (nchunks - 1 - i) * nrows, 8), nrows)
            ps_s[0, rf, :] = sf
            ps_s[1, rb, :] = sb
            sf = a_f * sf + b_f * swap(sf) + xs_s[0, rf, :]
            sb = a_b * sb + b_b * swap(sb) + xs_s[1, rb, :]
            return sf, sb

        return lax.fori_loop(0, nchunks, body, (init_f, init_b))

    zeros = jnp.zeros((BATCH, w), F32)
    fin_f, fin_b = carry(0, BATCH, SEQ // S5_CHUNK, zeros, zeros)
    fin_ref[0] = fin_f
    fin_ref[1] = fin_b
    carry(S5_ROWS_CTX, S5_LAT_ROWS, DEC_SEQ // S5_CHUNK, x0_ref[0], x0_ref[1])

    for g in range(gb):
        sl = slice(g * S5_P2, (g + 1) * S5_P2)
        p = jnp.concatenate([ps_s[0, :, sl], ps_s[1, :, sl]], axis=1)
        y_ref[:, g * S5_W:(g + 1) * S5_W] += _dot3(p, ff_ref[g])


def _s5(u_rows, tef, ff, lam, x0):
    gb = S5_GROUP_BLOCK
    w = gb * S5_P2
    return pl.pallas_call(
        _s5_kernel,
        grid=(S5_GROUPS // gb,),
        in_specs=[pl.BlockSpec((S5_ROWS, gb * S5_W), lambda i: (0, i)),
                  pl.BlockSpec((gb, S5_W, S5_W + 2 * S5_P2), lambda i: (i, 0, 0)),
                  pl.BlockSpec((gb, 2 * S5_P2, S5_W), lambda i: (i, 0, 0)),
                  pl.BlockSpec((2, 2, w), lambda i: (0, 0, i)),
                  pl.BlockSpec((2, S5_LAT_ROWS, w), lambda i: (0, 0, i))],
        out_specs=[pl.BlockSpec((S5_ROWS, gb * S5_W), lambda i: (0, i)),
                   pl.BlockSpec((2, BATCH, w), lambda i: (0, 0, i))],
        out_shape=[jax.ShapeDtypeStruct((S5_ROWS, S5_GROUPS * S5_W), F32),
                   jax.ShapeDtypeStruct((2, BATCH, S5_GROUPS * S5_P2), F32)],
        scratch_shapes=[pltpu.VMEM((2, S5_ROWS, w), F32), pltpu.VMEM((2, S5_ROWS, w), F32)],
        compiler_params=_cparams(("parallel",)),
        name="s5_scan",
    )(u_rows, tef, ff, lam, x0)


def _s5_tables(lam_re, lam_im, log_dt, b_re, b_im, c_re, c_im):
    C, G, P, H = S5_CHUNK, S5_GROUPS, S5_STATE, S5_GROUP_CH
    lam = lax.complex(lam_re, lam_im)
    lam_dt = lam * jnp.exp(log_dt)[..., None]
    tau = jnp.arange(C + 1, dtype=F32)
    pw = jnp.exp(lam_dt[:, None] * tau[None, :, None, None])
    b_bar = ((jnp.exp(lam_dt) - 1.0) / lam)[..., None] * lax.complex(b_re, b_im)
    c_mat = lax.complex(c_re, c_im)
    wt = pw[:, :, :, :, None] * b_bar[:, None]
    wm = jnp.concatenate([wt.real, wt.imag], axis=3)[:, :C]
    wm = wm.transpose(0, 2, 3, 1, 4).reshape(2 * G, 2 * P, C * H)
    cc = jnp.concatenate([c_re, -c_im], axis=-1).reshape(2 * G, H, 2 * P)
    kt = _s5_lag_kernels(cc, wm).reshape(2, G, H, C, H)
    k4 = kt.transpose(0, 1, 3, 4, 2)
    jj = jnp.arange(C)[:, None]
    ii = jnp.arange(C)[None, :]
    t_f = jnp.where((ii >= jj)[None, :, :, None, None], k4[0][:, jnp.clip(ii - jj, 0, C - 1)], 0.0)
    t_b = jnp.where((jj >= ii)[None, :, :, None, None], k4[1][:, jnp.clip(jj - ii, 0, C - 1)], 0.0)
    t_sum = (t_f + t_b).transpose(0, 1, 3, 2, 4).reshape(G, C * H, C * H)
    w_f = wt[0, :C][::-1]
    w_b = wt[1, :C]
    def e_of(wx):
        e = jnp.concatenate([wx.real, wx.imag], axis=2)
        return e.transpose(1, 0, 3, 2).reshape(G, C * H, 2 * P)
    tef = jnp.concatenate([t_sum, e_of(w_f), e_of(w_b)], axis=-1)
    def f_of(pwx, cm):
        cp = cm[None] * pwx[:, :, None, :]
        f = jnp.concatenate([cp.real, -cp.imag], axis=-1)
        return f.transpose(1, 3, 0, 2).reshape(G, 2 * P, C * H)
    ff = jnp.concatenate([f_of(pw[0, 1:C + 1], c_mat[0]), f_of(pw[1, 1:C + 1][::-1], c_mat[1])], axis=1)
    lam_c = pw[:, C]
    a = jnp.concatenate([lam_c.real, lam_c.real], axis=-1).reshape(2, 1, G * 2 * P)
    b = jnp.concatenate([-lam_c.imag, lam_c.imag], axis=-1).reshape(2, 1, G * 2 * P)
    return tef, ff, jnp.concatenate([a, b], axis=1)


def _s5_to_rows(su):
    C, G, H = S5_CHUNK, S5_GROUPS, S5_GROUP_CH
    ctx = su[:T_CTX].reshape(BATCH, SEQ // C, C, G, H).transpose(1, 0, 3, 2, 4)
    ctx = ctx.reshape(S5_ROWS_CTX, G * C * H)
    lat = su[T_CTX:].reshape(DEC_BATCH, DEC_SEQ // C, C, G, H).transpose(1, 0, 3, 2, 4)
    lat = jnp.pad(lat, ((0, 0), (0, S5_LAT_ROWS - DEC_BATCH), (0, 0), (0, 0), (0, 0)))
    return jnp.concatenate([ctx, lat.reshape(S5_ROWS_LAT, G * C * H)], axis=0)


def _s5_from_rows(y):
    C, G, H = S5_CHUNK, S5_GROUPS, S5_GROUP_CH
    ctx = y[:S5_ROWS_CTX].reshape(SEQ // C, BATCH, G, C, H).transpose(1, 0, 3, 2, 4).reshape(T_CTX, G * H)
    lat = y[S5_ROWS_CTX:].reshape(DEC_SEQ // C, S5_LAT_ROWS, G, C, H)[:, :DEC_BATCH]
    lat = lat.transpose(1, 0, 3, 2, 4).reshape(T_LAT, G * H)
    return jnp.concatenate([ctx, lat], axis=0)


def _softmax_pv(s_parts, v_parts):
    m = s_parts[0].max(axis=-1, keepdims=True)
    for s in s_parts[1:]:
        m = jnp.maximum(m, s.max(axis=-1, keepdims=True))
    o = None
    l = None
    for s, v in zip(s_parts, v_parts):
        p = jnp.exp(s - m)
        pl_ = p.sum(axis=-1, keepdims=True)
        po = _bdot(p, v)
        o = po if o is None else o + po
        l = pl_ if l is None else l + pl_
    return o / l


def _attn_ctx_kernel(q_ref, k_ref, v_ref, o_ref):
    scale = NA_DH ** -0.5
    for h in range(NA_HEADS):
        s = _bdot_nt(q_ref[h], k_ref[h]) * scale
        o_ref[h] = _softmax_pv([s], [v_ref[h]])


def _attn_ctx(nq, nk, nv):
    spec = pl.BlockSpec((NA_HEADS, SEQ, NA_DH), lambda b: (0, b, 0))
    return pl.pallas_call(
        _attn_ctx_kernel,
        grid=(BATCH,),
        in_specs=[spec, spec, spec],
        out_specs=spec,
        out_shape=jax.ShapeDtypeStruct((NA_HEADS, T_CTX, NA_DH), F32),
        compiler_params=_cparams(("parallel",)),
        name="attn_ctx",
    )(nq, nk, nv)


def _attn_lat_kernel(q_ref, k_ref, v_ref, kc_ref, vc_ref, bias_ref, o_ref):
    scale = NA_DH ** -0.5
    kb = k_ref[0].astype(BF16)
    vb = v_ref[0].astype(BF16)
    kc = kc_ref[0, 0].astype(BF16)
    vc = vc_ref[0, 0].astype(BF16)
    tq = NA_QBLOCK
    for qb in range(DEC_SEQ // tq):
        rows = slice(qb * tq, (qb + 1) * tq)
        qh = q_ref[0, rows, :].astype(BF16)
        s_loc = _bdot_nt(qh, kb) * scale + bias_ref[0, rows, :]
        s_ctx = _bdot_nt(qh, kc) * scale
        o_ref[0, rows, :] = _softmax_pv([s_loc, s_ctx], [vb, vc])


def _attn_lat(nq, nk, nv, kc, vc, bias):
    off = T_CTX // DEC_SEQ
    tok = pl.BlockSpec((1, DEC_SEQ, NA_DH), lambda h, b: (h, b + off, 0))
    cache = pl.BlockSpec((1, 1, PAST_LEN, NA_DH), lambda h, b: (b, h, 0, 0))
    return pl.pallas_call(
        _attn_lat_kernel,
        grid=(NA_HEADS, DEC_BATCH),
        in_specs=[tok, tok, tok, cache, cache,
                  pl.BlockSpec((1, DEC_SEQ, DEC_SEQ), lambda h, b: (h, 0, 0))],
        out_specs=pl.BlockSpec((1, DEC_SEQ, NA_DH), lambda h, b: (h, b, 0)),
        out_shape=jax.ShapeDtypeStruct((NA_HEADS, T_LAT, NA_DH), F32),
        compiler_params=_cparams(("parallel", "parallel")),
        name="attn_lat",
    )(nq, nk, nv, kc, vc, bias)


def _na_bias(rpb):
    rows = DEC_SEQ // GRID_W
    kh = min(NA_WIN_H, rows)
    col = jnp.arange(GRID_W)
    col_start = jnp.clip(col - NA_WIN_W // 2, 0, GRID_W - NA_WIN_W)
    col_in = (col[None, :] >= col_start[:, None]) & (col[None, :] < col_start[:, None] + NA_WIN_W)
    col_idx = jnp.clip(col[None, :] - col[:, None] + NA_WIN_W - 1, 0, 2 * NA_WIN_W - 2)
    r = jnp.arange(rows)
    row_start = jnp.clip(r - kh // 2, 0, rows - kh)
    row_in = (r[None, :] >= row_start[:, None]) & (r[None, :] < row_start[:, None] + kh)
    row_idx = jnp.clip(r[None, :] - r[:, None] + NA_WIN_H - 1, 0, 2 * NA_WIN_H - 2)
    b = rpb[:, row_idx[:, None, :, None], col_idx[None, :, None, :]]
    ok = row_in[:, None, :, None] & col_in[None, :, None, :]
    b = jnp.where(ok[None], b, -jnp.inf)
    return b.reshape(NA_HEADS, DEC_SEQ, DEC_SEQ)


def _merge_kernel(x_ref, mod_ref, g_ref, ret_ref, s5y_ref, s5u_ref, gla_ref, na_ref,
                  s5d_ref, wglu_ref, bglu_ref, wbr_ref, wmg_ref, bmg_ref, wout_ref, o_ref):
    x = x_ref[...]
    mod = mod_ref[0]
    hb = (_rms(x, g_ref[0:1]) * (1.0 + mod[1:2]) + mod[0:1]).astype(BF16)

    y = s5y_ref[...] + s5d_ref[...] * s5u_ref[...]
    y = 0.5 * y * (1.0 + jnp.tanh(math.sqrt(2.0 / math.pi) * (y + 0.044715 * (y * y * y))))
    z = _bdot(y, wglu_ref[...]) + bglu_ref[...]
    s5_out = z[:, 0:BRANCH_W] * _sigmoid(z[:, BRANCH_W:2 * BRANCH_W])

    def gate(n):
        return _sigmoid(jnp.dot(hb, wmg_ref[:, n * D_MODEL:(n + 1) * D_MODEL], preferred_element_type=F32)
                        + bmg_ref[:, n * D_MODEL:(n + 1) * D_MODEL])

    acc = gate(0) * _bdot(ret_ref[...], wbr_ref[0])
    acc += gate(1) * _bdot(s5_out, wbr_ref[1])
    acc += gate(2) * _bdot(gla_ref[...], wbr_ref[2])
    up = _bdot(na_ref[0], wbr_ref[3, 0:NA_DH, :])
    for hh in range(1, NA_HEADS):
        up += _bdot(na_ref[hh], wbr_ref[3, hh * NA_DH:(hh + 1) * NA_DH, :])
    acc += gate(3) * up
    m = _bdot(acc, wout_ref[...])
    o_ref[...] = x + mod[2:3] * _rms(m, g_ref[1:2])


def _merge(x, mod_l, g_l, ret_o, s5_y, s5_u, gla_o, na_o, s5d, wglu, bglu, wbr, wmg, bmg, wout):
    tm = TOKEN_TILE
    tok = lambda w: pl.BlockSpec((tm, w), lambda i: (i, 0))
    full = lambda *shape: pl.BlockSpec(shape, lambda i: (0,) * len(shape))
    return pl.pallas_call(
        _merge_kernel,
        grid=(T_ALL // tm,),
        in_specs=[tok(D_MODEL),
                  pl.BlockSpec((1, 6, D_MODEL), lambda i: (_mod_row(i), 0, 0)),
                  full(4, D_MODEL),
                  tok(256), tok(256), tok(256), tok(256),
                  pl.BlockSpec((NA_HEADS, tm, NA_DH), lambda i: (0, i, 0)),
                  full(1, 256), full(256, 512), full(1, 512),
                  full(N_BRANCH, BRANCH_W, D_MODEL), full(D_MODEL, N_BRANCH * D_MODEL),
                  full(1, N_BRANCH * D_MODEL), full(D_MODEL, D_MODEL)],
        out_specs=tok(D_MODEL),
        out_shape=jax.ShapeDtypeStruct((T_ALL, D_MODEL), F32),
        compiler_params=_cparams(("parallel",)),
        name="merge",
    )(x, mod_l, g_l, ret_o, s5_y, s5_u, gla_o, na_o, s5d, wglu, bglu, wbr, wmg, bmg, wout)


FF_TILE = 1024


def _mlp_kernel(x_ref, mod_ref, g_ref, w1_ref, w2_ref, o_ref):
    x = x_ref[...]
    mod = mod_ref[0]
    hb = (_rms(x, g_ref[2:3]) * (1.0 + mod[4:5]) + mod[3:4]).astype(BF16)
    f = None
    for j in range(D_FF // FF_TILE):
        a = jnp.maximum(jnp.dot(hb, w1_ref[:, j * FF_TILE:(j + 1) * FF_TILE], preferred_element_type=F32), 0.0)
        part = _bdot(a * a, w2_ref[j * FF_TILE:(j + 1) * FF_TILE, :])
        f = part if f is None else f + part
    o_ref[...] = x + mod[5:6] * _rms(f, g_ref[3:4])


def _mlp(x, mod_l, g_l, w1, w2):
    tm = TOKEN_TILE
    tok = pl.BlockSpec((tm, D_MODEL), lambda i: (i, 0))
    return pl.pallas_call(
        _mlp_kernel,
        grid=(T_ALL // tm,),
        in_specs=[tok,
                  pl.BlockSpec((1, 6, D_MODEL), lambda i: (_mod_row(i), 0, 0)),
                  pl.BlockSpec((4, D_MODEL), lambda i: (0, 0)),
                  pl.BlockSpec((D_MODEL, D_FF), lambda i: (0, 0)),
                  pl.BlockSpec((D_FF, D_MODEL), lambda i: (0, 0))],
        out_specs=tok,
        out_shape=jax.ShapeDtypeStruct((T_ALL, D_MODEL), F32),
        compiler_params=_cparams(("parallel",)),
        name="mlp",
    )(x, mod_l, g_l, w1, w2)


def _rope_tables():
    half = RET_DK // 2
    nf = half // 2
    t = jnp.arange(DEC_SEQ)
    row = (t // GRID_W).astype(F32)
    col = (t % GRID_W).astype(F32)
    inv = ROPE_BASE ** (-jnp.arange(nf, dtype=F32) / nf)
    ang_r = row[:, None] * inv[None, :]
    ang_c = col[:, None] * inv[None, :]
    cos = jnp.concatenate([jnp.cos(ang_r)] * 2 + [jnp.cos(ang_c)] * 2, axis=1)
    sin = jnp.concatenate([-jnp.sin(ang_r), jnp.sin(ang_r), -jnp.sin(ang_c), jnp.sin(ang_c)], axis=1)
    return jnp.tile(cos, (1, RET_HEADS)), jnp.tile(sin, (1, RET_HEADS))


def _pack_w_in(w):
    offs = [0, 256, 512, 768, 1024, 1280, 1408, 1536, 1792, 2048, 2080, 2336, 2592, 2848]
    seg = lambda i: w[:, offs[i]:offs[i + 1]]
    order = [0, 1, 2, 3, 4, 5, 6, 7, 8, 10, 11, 12, 9]
    packed = jnp.concatenate([seg(i) for i in order], axis=1)
    return jnp.pad(packed, ((0, 0), (0, W_IN_PACKED - packed.shape[1]))).astype(BF16)


def _gla_state_in(st):
    eye = jnp.eye(GLA_HEADS, dtype=st.dtype)
    t = jnp.einsum('bdhkv,hg->bdhvgk', st, eye)
    return t.reshape(st.shape[0], 2, GLA_HEADS * GLA_DV, GLA_HEADS * GLA_DK)


def _gla_state_out(st):
    t = st.reshape(st.shape[0], 2, GLA_HEADS, GLA_DV, GLA_HEADS, GLA_DK)
    idx = jnp.arange(GLA_HEADS)
    t = t[:, :, idx, :, idx, :]
    return t.transpose(1, 2, 0, 4, 3)


def kernel(x_prompt, x_sample, c, cache_na_k, cache_na_v, state_ret, state_s5, state_gla, c_ctx, w_ada, b_ada, g_norm, w_in, ret_log_decay, ret_gn, s5_lambda_re, s5_lambda_im, s5_log_dt, s5_b_re, s5_b_im, s5_c_re, s5_c_im, s5_d, s5_w_glu, s5_b_glu, gla_w_gate, gla_b_gate, gla_gn, na_rpb, w_branch, w_merge, b_merge, w_out, w_mlp1, w_mlp2):
    x = jnp.concatenate([x_prompt.reshape(T_CTX, D_MODEL), x_sample.reshape(T_LAT, D_MODEL)], axis=0)
    cc = jnp.concatenate([c_ctx[None], c, jnp.zeros((N_MOD_ROWS - 1 - DEC_BATCH, D_MODEL), F32)], axis=0)
    mod = _ada(cc, w_ada, b_ada).reshape(DEPTH, N_MOD_ROWS, 6, D_MODEL)
    cos, sin = _rope_tables()
    cache_k = cache_na_k.transpose(0, 1, 3, 2, 4)
    cache_v = cache_na_v.transpose(0, 1, 3, 2, 4)

    ks_l, vs_l, ret_l, s5_l, gla_l = [], [], [], [], []
    for l in range(DEPTH):
        mod_l = mod[l]
        g_l = g_norm[l]
        ret, su, gqk, gv, gg, glr, nq, nk, nv = _inproj(x, mod_l, g_l, _pack_w_in(w_in[l]))

        gn_r = ret_gn[l].reshape(1, BRANCH_W)
        ret_c, st_ret = _retention(ret, ret_log_decay[l], gn_r, latent=False)
        ret_s, = _retention(ret, ret_log_decay[l], gn_r, latent=True, cos=cos, sin=sin, s0=state_ret[:, l])
        ret_o = jnp.concatenate([ret_c, ret_s], axis=0)

        tef, ff, lam = _s5_tables(s5_lambda_re[l], s5_lambda_im[l], s5_log_dt[l], s5_b_re[l], s5_b_im[l],
                                  s5_c_re[l], s5_c_im[l])
        x0 = state_s5[:, l].transpose(1, 0, 2, 4, 3).reshape(2, DEC_BATCH, S5_GROUPS * S5_P2)
        x0 = jnp.pad(x0, ((0, 0), (0, S5_LAT_ROWS - DEC_BATCH), (0, 0)))
        y_rows, s5_fin = _s5(_s5_to_rows(su), tef, ff, lam, x0)
        s5_y = _s5_from_rows(y_rows)

        gn_g = gla_gn[l].reshape(1, BRANCH_W)
        gla_c, st_gla = _gla(gqk, gv, gg, glr, gla_w_gate[l], gla_b_gate[l], gn_g, latent=False)
        gla_s, = _gla(gqk, gv, gg, glr, gla_w_gate[l], gla_b_gate[l], gn_g, latent=True,
                      s0=_gla_state_in(state_gla[:, l]))
        gla_o = jnp.concatenate([gla_c, gla_s], axis=0)

        na_c = _attn_ctx(nq, nk, nv)
        na_s = _attn_lat(nq, nk, nv, cache_k[:, l], cache_v[:, l], _na_bias(na_rpb[l]))
        na_o = jnp.concatenate([na_c, na_s], axis=1)

        x = _merge(x, mod_l, g_l, ret_o, s5_y, su, gla_o, na_o,
                   s5_d[l].reshape(1, BRANCH_W), s5_w_glu[l].astype(BF16), s5_b_glu[l].reshape(1, -1),
                   w_branch[l].astype(BF16), w_merge[l].astype(BF16), b_merge[l].reshape(1, -1),
                   w_out[l].astype(BF16))
        x = _mlp(x, mod_l, g_l, w_mlp1[l].astype(BF16), w_mlp2[l].astype(BF16))

        ks_l.append(nk[:, :T_CTX].reshape(NA_HEADS, BATCH, SEQ, NA_DH).transpose(1, 2, 0, 3))
        vs_l.append(nv[:, :T_CTX].reshape(NA_HEADS, BATCH, SEQ, NA_DH).transpose(1, 2, 0, 3))
        ret_l.append(st_ret)
        s5_l.append(s5_fin.reshape(2, BATCH, S5_GROUPS, 2, S5_STATE).transpose(1, 0, 2, 4, 3))
        gla_l.append(_gla_state_out(st_gla))

    y_prompt = x[:T_CTX].reshape(BATCH, SEQ, D_MODEL)
    y_sample = x[T_CTX:].reshape(DEC_BATCH, DEC_SEQ, D_MODEL)
    return (y_prompt, y_sample, jnp.stack(ks_l, axis=1), jnp.stack(vs_l, axis=1),
            jnp.stack(ret_l, axis=1), jnp.stack(s5_l, axis=1), jnp.stack(gla_l, axis=1))
```

```python
import functools
import math

import numpy as np
import jax
import jax.numpy as jnp
from jax import lax
from jax.experimental import pallas as pl
from jax.experimental.pallas import tpu as pltpu

F32 = jnp.float32
BF16 = jnp.bfloat16

D_MODEL = 1024
BATCH = 16
SEQ = 256
DEPTH = 4
DEC_BATCH = 4
DEC_SEQ = 1024
PAST_LEN = 256
GRID_W = 64
N_BRANCH = 4
BRANCH_W = 256
RET_HEADS = 4
RET_DK = 64
RET_DV = 64
S5_GROUPS = 16
S5_GROUP_CH = 16
S5_STATE = 64
GLA_HEADS = 4
GLA_DK = 32
GLA_DV = 64
GLA_RANK = 16
GLA_TAU = 16.0
NA_HEADS = 4
NA_DH = 64
NA_WIN_H = 8
NA_WIN_W = 16
D_FF = 4 * D_MODEL
ROPE_BASE = 10000.0
EPS = 1e-6

T_CTX = BATCH * SEQ
T_LAT = DEC_BATCH * DEC_SEQ
T_ALL = T_CTX + T_LAT
LAT_BLOCK0 = T_CTX // DEC_SEQ
N_MOD_ROWS = 8
TOKEN_TILE = 512
GLA_CHUNK = 64
S5_CHUNK = 16
S5_GROUP_BLOCK = 4
S5_LAT_ROWS = 8
RET_QBLOCK = 256
NA_QBLOCK = 256
GRID_ROWS = DEC_SEQ // GRID_W
NA_KH = min(NA_WIN_H, GRID_ROWS)
NA_REL_ROWS = 2 * NA_WIN_H - 1
VMEM_LIMIT = 56 * 1024 * 1024
W_IN_PACKED = 2944


def _cparams(sem):
    return pltpu.CompilerParams(dimension_semantics=sem, vmem_limit_bytes=VMEM_LIMIT)


def _bdot(a, b):
    return jnp.dot(a.astype(BF16), b.astype(BF16), preferred_element_type=F32)


def _bdot_nt(a, b):
    return lax.dot_general(a.astype(BF16), b.astype(BF16), (((1,), (1,)), ((), ())),
                           preferred_element_type=F32)


def _bdot_tn(a, b):
    return lax.dot_general(a.astype(BF16), b.astype(BF16), (((0,), (0,)), ((), ())),
                           preferred_element_type=F32)


def _split(a):
    hi = a.astype(BF16)
    lo = (a - hi.astype(F32)).astype(BF16)
    return hi, lo


def _dot3(a, b):
    ah, al = _split(a)
    bh, bl = _split(b)
    d = functools.partial(jnp.dot, preferred_element_type=F32)
    return d(ah, bh) + d(al, bh) + d(ah, bl)


def _sigmoid(x):
    return 1.0 / (1.0 + jnp.exp(-x))


def _silu(x):
    return x * _sigmoid(x)


def _rms(x, g):
    return x * lax.rsqrt(jnp.mean(x * x, axis=-1, keepdims=True) + EPS) * g


def _group_norm(o, g):
    mu = jnp.mean(o, axis=-1, keepdims=True)
    xc = o - mu
    return xc * lax.rsqrt(jnp.mean(xc * xc, axis=-1, keepdims=True) + EPS) * g


def _mod_row(i):
    ctx_tiles = T_CTX // TOKEN_TILE
    return jnp.where(i < ctx_tiles, 0, 1 + (i - ctx_tiles) // (DEC_SEQ // TOKEN_TILE))


def _mod_spec(l):
    return pl.BlockSpec((1, 1, 6, D_MODEL), lambda i: (l, _mod_row(i), 0, 0))


def _layer_spec(l, *shape):
    return pl.BlockSpec((1,) + shape, lambda *_: (l,) + (0,) * len(shape))


_ANY = pl.BlockSpec(memory_space=pl.ANY)


ADA_TILE = 1536


def _ada_kernel(c_ref, w_ref, b_ref, o_ref):
    a = _silu(c_ref[...])
    o_ref[0] = _bdot(a, w_ref[0]) + b_ref[0]


def _ada(cc, w_ada, b_ada):
    n = 6 * D_MODEL
    return pl.pallas_call(
        _ada_kernel,
        grid=(DEPTH, n // ADA_TILE),
        in_specs=[pl.BlockSpec((N_MOD_ROWS, D_MODEL), lambda l, j: (0, 0)),
                  pl.BlockSpec((1, D_MODEL, ADA_TILE), lambda l, j: (l, 0, j)),
                  pl.BlockSpec((1, 1, ADA_TILE), lambda l, j: (l, 0, j))],
        out_specs=pl.BlockSpec((1, N_MOD_ROWS, ADA_TILE), lambda l, j: (l, 0, j)),
        out_shape=jax.ShapeDtypeStruct((DEPTH, N_MOD_ROWS, n), F32),
        compiler_params=_cparams(("parallel", "parallel")),
        name="ada_mod",
    )(cc, w_ada, b_ada.reshape(DEPTH, 1, n))


def _inproj_kernel(x_ref, mod_ref, g_ref, w_ref,
                   ret_ref, s5_ref, gqk_ref, gv_ref, gg_ref, glr_ref, nq_ref, nk_ref, nv_ref):
    mod = mod_ref[0, 0]
    h = _rms(x_ref[...], g_ref[0, 0:1]) * (1.0 + mod[1:2]) + mod[0:1]
    hb = h.astype(BF16)

    def proj(lo, hi):
        return jnp.dot(hb, w_ref[0, :, lo:hi], preferred_element_type=F32)

    ret_ref[...] = proj(0, 1024)
    s5_ref[...] = proj(1024, 1280)
    gqk_ref[...] = proj(1280, 1536)
    gv_ref[...] = proj(1536, 1792)
    gg_ref[...] = proj(1792, 2048)
    for ref, lo in ((nq_ref, 2048), (nk_ref, 2304), (nv_ref, 2560)):
        r = proj(lo, lo + 256)
        for hh in range(NA_HEADS):
            ref[hh] = r[:, hh * NA_DH:(hh + 1) * NA_DH]
    glr_ref[...] = proj(2816, 2944)


def _inproj(l, x, mod, g_norm, w_in_p):
    tm = TOKEN_TILE
    tok = lambda w: pl.BlockSpec((tm, w), lambda i: (i, 0))
    head = pl.BlockSpec((NA_HEADS, tm, NA_DH), lambda i: (0, i, 0))
    tshape = lambda w: jax.ShapeDtypeStruct((T_ALL, w), F32)
    hshape = jax.ShapeDtypeStruct((NA_HEADS, T_ALL, NA_DH), F32)
    return pl.pallas_call(
        _inproj_kernel,
        grid=(T_ALL // tm,),
        in_specs=[tok(D_MODEL), _mod_spec(l), _layer_spec(l, 4, D_MODEL),
                  _layer_spec(l, D_MODEL, W_IN_PACKED)],
        out_specs=[tok(1024), tok(256), tok(256), tok(256), tok(256), tok(128), head, head, head],
        out_shape=[tshape(1024), tshape(256), tshape(256), tshape(256), tshape(256), tshape(128),
                   hshape, hshape, hshape],
        compiler_params=_cparams(("parallel",)),
        name="in_proj",
    )(x, mod, g_norm, w_in_p)


def _rope_rotate(x, lane):
    first = (lane % 32) < 16
    w = x.shape[-1]
    return jnp.where(first, pltpu.roll(x, w - 16, 1), pltpu.roll(x, 16, 1))


def _ret_kernel(ld_ref, ret_ref, gn_ref, *rest, layer, seq, latent):
    if latent:
        cos_ref, sin_ref, s0_ref, _, out_ref = rest
    else:
        out_ref, st_ref = rest
    q = ret_ref[:, 0:256]
    k = ret_ref[:, 256:512]
    if latent:
        lane = lax.broadcasted_iota(jnp.int32, (seq, 256), 1)
        cos = cos_ref[...]
        sin = sin_ref[...]
        q = q * cos + _rope_rotate(q, lane) * sin
        k = k * cos + _rope_rotate(k, lane) * sin
    k = k * (RET_DK ** -0.5)
    tq = RET_QBLOCK
    col = lax.broadcasted_iota(jnp.int32, (tq, seq), 1)
    row = lax.broadcasted_iota(jnp.int32, (tq, seq), 0)
    pos_c = lax.broadcasted_iota(jnp.int32, (seq, 1), 0).astype(F32)
    for h in range(RET_HEADS):
        lgf = ld_ref[layer, 0, h]
        lgb = ld_ref[layer, 1, h]
        sl = slice(h * RET_DK, (h + 1) * RET_DK)
        qh = q[:, sl]
        kh = k[:, sl]
        vh = ret_ref[:, 512 + h * RET_DV:512 + (h + 1) * RET_DV]
        kb = kh.astype(BF16)
        vb = vh.astype(BF16)
        if latent:
            q_init = jnp.concatenate([qh * jnp.exp(lgf * (pos_c + 1.0)),
                                      qh * jnp.exp(lgb * (seq - pos_c))], axis=1)
            s_init = jnp.concatenate([s0_ref[0, 0, 0, h], s0_ref[0, 0, 1, h]], axis=0)
        for qb in range(seq // tq):
            rel = (row + (qb * tq) - col).astype(F32)
            decay = (jnp.where(rel >= 0, jnp.exp(lgf * jnp.maximum(rel, 0.0)), 0.0)
                     + jnp.where(rel <= 0, jnp.exp(lgb * jnp.maximum(-rel, 0.0)), 0.0))
            rows = slice(qb * tq, (qb + 1) * tq)
            s = _bdot_nt(qh[rows], kb) * decay
            o = _bdot(s, vb)
            if latent:
                o = o + _bdot(q_init[rows], s_init)
            g = ret_ref[rows, 768 + h * RET_DV:768 + (h + 1) * RET_DV]
            out_ref[rows, sl] = _group_norm(o, gn_ref[0, :, sl]) * _silu(g)
        if not latent:
            st_ref[0, 0, h] = _bdot_tn(kh * jnp.exp(lgf * (seq - 1.0 - pos_c)), vb)
            st_ref[0, 1, h] = _bdot_tn(kh * jnp.exp(lgb * pos_c), vb)


def _retention(l, ret, ld, gn, *, latent, cos=None, sin=None, s0=None, prev=None):
    seq = DEC_SEQ if latent else SEQ
    nb = DEC_BATCH if latent else BATCH
    off = LAT_BLOCK0 if latent else 0
    in_specs = [pl.BlockSpec(memory_space=pltpu.SMEM),
                pl.BlockSpec((seq, 1024), lambda b: (b + off, 0)),
                _layer_spec(l, 1, 256)]
    args = [ld, ret, gn]
    out_specs = [pl.BlockSpec((seq, 256), lambda b: (b + off, 0))]
    out_shape = [jax.ShapeDtypeStruct((T_ALL, 256), F32)]
    aliases = {}
    if latent:
        in_specs += [pl.BlockSpec((seq, 256), lambda b: (0, 0)),
                     pl.BlockSpec((seq, 256), lambda b: (0, 0)),
                     pl.BlockSpec((1, 1, 2, RET_HEADS, RET_DK, RET_DV), lambda b: (b, l, 0, 0, 0, 0)),
                     _ANY]
        args += [cos, sin, s0, prev]
        aliases = {6: 0}
    else:
        out_specs.append(pl.BlockSpec((1, 2, RET_HEADS, RET_DK, RET_DV), lambda b: (b, 0, 0, 0, 0)))
        out_shape.append(jax.ShapeDtypeStruct((nb, 2, RET_HEADS, RET_DK, RET_DV), F32))
    return pl.pallas_call(
        functools.partial(_ret_kernel, layer=l, seq=seq, latent=latent),
        grid=(nb,),
        in_specs=in_specs, out_specs=out_specs, out_shape=out_shape,
        input_output_aliases=aliases,
        compiler_params=_cparams(("parallel",)),
        name="retention_lat" if latent else "retention_ctx",
    )(*args)


def _gla_kernel(gqk_ref, gv_ref, gg_ref, glr_ref, wg_ref, bg_ref, gn_ref, *rest, seq, latent):
    if latent:
        s0_ref, _, out_ref, gate_s, o_s, st_s = rest
    else:
        out_ref, st_ref, gate_s, o_s, st_s = rest
    c = GLA_CHUNK
    n = seq // c
    hk = GLA_HEADS * GLA_DK
    lr = glr_ref[...]
    for d in range(2):
        pre = _bdot(lr[:, d * GLA_RANK:(d + 1) * GLA_RANK], wg_ref[0, d]) + bg_ref[0, d:d + 1]
        gate_s[d] = (jnp.minimum(pre, 0.0) - jnp.log(1.0 + jnp.exp(-jnp.abs(pre)))) / GLA_TAU
        st_s[d] = s0_ref[0, 0, d] if latent else jnp.zeros((GLA_HEADS * GLA_DV, hk), F32)

    ti = lax.broadcasted_iota(jnp.int32, (c, c), 0)
    tj = lax.broadcasted_iota(jnp.int32, (c, c), 1)
    tri = [(tj <= ti).astype(BF16), (tj >= ti).astype(BF16)]
    lane_k = lax.broadcasted_iota(jnp.int32, (c, hk), 1)
    head_mask = [(lane_k // GLA_DK) == h for h in range(GLA_HEADS)]
    ai = lax.broadcasted_iota(jnp.int32, (GLA_HEADS * c, c), 0) % c
    aj = lax.broadcasted_iota(jnp.int32, (GLA_HEADS * c, c), 1)
    keep = [aj <= ai, aj >= ai]
    sr = lax.broadcasted_iota(jnp.int32, (GLA_HEADS * GLA_DV, hk), 0) // GLA_DV
    sc = lax.broadcasted_iota(jnp.int32, (GLA_HEADS * GLA_DV, hk), 1) // GLA_DK
    diag = sr == sc
    mid = (c // 2 - 1, c // 2)
    end = (c - 1, 0)
    scale = GLA_DK ** -0.5

    def one(ci, d):
        rows = pl.ds(pl.multiple_of(ci * c, c), c)
        q = gqk_ref[rows, 0:hk] * scale
        k = gqk_ref[rows, hk:2 * hk]
        v = gv_ref[rows, :].astype(BF16)
        gh, gl = _split(gate_s[d, rows, :])
        b = (jnp.dot(tri[d], gh, preferred_element_type=F32)
             + jnp.dot(tri[d], gl, preferred_element_type=F32))
        b_mid = b[mid[d]:mid[d] + 1]
        b_end = b[end[d]:end[d] + 1]
        q_att = q * jnp.exp(b - b_mid)
        k_att = k * jnp.exp(b_mid - b)
        q_st = q * jnp.exp(b)
        k_st = k * jnp.exp(b_end - b)
        q_stack = jnp.concatenate([jnp.where(head_mask[h], q_att, 0.0) for h in range(GLA_HEADS)], axis=0)
        att = jnp.where(keep[d], _bdot_nt(q_stack, k_att), 0.0).astype(BF16)
        o = jnp.concatenate(
            [jnp.dot(att[h * c:(h + 1) * c], v[:, h * GLA_DV:(h + 1) * GLA_DV], preferred_element_type=F32)
             for h in range(GLA_HEADS)], axis=1)
        st = st_s[d]
        o_s[d, rows, :] = o + _bdot_nt(q_st, st)
        st_s[d] = st * jnp.exp(b_end) + jnp.where(diag, _bdot_tn(v, k_st), 0.0)

    def body(i, carry):
        one(i, 0)
        one(n - 1 - i, 1)
        return carry

    lax.fori_loop(0, n, body, 0)

    o = o_s[0] + o_s[1]
    for h in range(GLA_HEADS):
        sl = slice(h * GLA_DV, (h + 1) * GLA_DV)
        out_ref[:, sl] = _group_norm(o[:, sl], gn_ref[0, :, sl]) * _silu(gg_ref[:, sl])
    if not latent:
        st_ref[0, 0] = st_s[0]
        st_ref[0, 1] = st_s[1]


def _gla(l, gqk, gv, gg, glr, wg, bg, gn, *, latent, s0=None, prev=None):
    seq = DEC_SEQ if latent else SEQ
    nb = DEC_BATCH if latent else BATCH
    off = LAT_BLOCK0 if latent else 0
    hk = GLA_HEADS * GLA_DK
    hv = GLA_HEADS * GLA_DV
    tok = lambda w: pl.BlockSpec((seq, w), lambda b: (b + off, 0))
    in_specs = [tok(256), tok(256), tok(256), tok(128),
                _layer_spec(l, 2, GLA_RANK, hk), _layer_spec(l, 2, hk), _layer_spec(l, 1, 256)]
    args = [gqk, gv, gg, glr, wg, bg, gn]
    out_specs = [tok(256)]
    out_shape = [jax.ShapeDtypeStruct((T_ALL, 256), F32)]
    aliases = {}
    if latent:
        in_specs += [pl.BlockSpec((1, 1, 2, hv, hk), lambda b: (b, l, 0, 0, 0)), _ANY]
        args += [s0, prev]
        aliases = {8: 0}
    else:
        out_specs.append(pl.BlockSpec((1, 2, hv, hk), lambda b: (b, 0, 0, 0)))
        out_shape.append(jax.ShapeDtypeStruct((nb, 2, hv, hk), F32))
    return pl.pallas_call(
        functools.partial(_gla_kernel, seq=seq, latent=latent),
        grid=(nb,),
        in_specs=in_specs, out_specs=out_specs, out_shape=out_shape,
        input_output_aliases=aliases,
        scratch_shapes=[pltpu.VMEM((2, seq, hk), F32), pltpu.VMEM((2, seq, hv), F32),
                        pltpu.VMEM((2, hv, hk), F32)],
        compiler_params=_cparams(("parallel",)),
        name="gla_lat" if latent else "gla_ctx",
    )(*args)


def _s5_lagk_kernel(c_ref, w_ref, o_ref):
    o_ref[0] = _dot3(c_ref[0], w_ref[0])


def _s5_lag_kernels(cc, wm):
    n, hc, p2 = cc.shape
    w = wm.shape[-1]
    return pl.pallas_call(
        _s5_lagk_kernel,
        grid=(n,),
        in_specs=[pl.BlockSpec((1, hc, p2), lambda i: (i, 0, 0)),
                  pl.BlockSpec((1, p2, w), lambda i: (i, 0, 0))],
        out_specs=pl.BlockSpec((1, hc, w), lambda i: (i, 0, 0)),
        out_shape=jax.ShapeDtypeStruct((n, hc, w), F32),
        compiler_params=_cparams(("parallel",)),
        name="s5_lag_kernels",
    )(cc, wm)


S5_W = S5_CHUNK * S5_GROUP_CH
S5_P2 = 2 * S5_STATE
S5_ROWS_CTX = (SEQ // S5_CHUNK) * BATCH
S5_ROWS_LAT = (DEC_SEQ // S5_CHUNK) * S5_LAT_ROWS
S5_ROWS = S5_ROWS_CTX + S5_ROWS_LAT


def _s5_kernel(u_ref, tef_ref, ff_ref, lam_ref, x0_ref, y_ref, fin_ref, xs_s, ps_s):
    gb = S5_GROUP_BLOCK
    for g in range(gb):
        u = u_ref[:, g * S5_W:(g + 1) * S5_W]
        r = _dot3(u, tef_ref[0, g])
        y_ref[:, g * S5_W:(g + 1) * S5_W] = r[:, 0:S5_W]
        xs_s[0, :, g * S5_P2:(g + 1) * S5_P2] = r[:, S5_W:S5_W + S5_P2]
        xs_s[1, :, g * S5_P2:(g + 1) * S5_P2] = r[:, S5_W + S5_P2:S5_W + 2 * S5_P2]

    w = gb * S5_P2
    lane = lax.broadcasted_iota(jnp.int32, (1, w), 1)
    first = (lane % S5_P2) < S5_STATE

    def swap(s):
        return jnp.where(first, pltpu.roll(s, w - S5_STATE, 1), pltpu.roll(s, S5_STATE, 1))

    def carry(base, nrows, nchunks, init_f, init_b):
        a_f, b_f = lam_ref[0, 0, 0:1], lam_ref[0, 0, 1:2]
        a_b, b_b = lam_ref[0, 1, 0:1], lam_ref[0, 1, 1:2]

        def body(i, st):
            sf, sb = st
            rf = pl.ds(pl.multiple_of(base + i * nrows, 8), nrows)
            rb = pl.ds(pl.multiple_of(base + (nchunks - 1 - i) * nrows, 8), nrows)
            ps_s[0, rf, :] = sf
            ps_s[1, rb, :] = sb
            sf = a_f * sf + b_f * swap(sf) + xs_s[0, rf, :]
            sb = a_b * sb + b_b * swap(sb) + xs_s[1, rb, :]
            return sf, sb

        return lax.fori_loop(0, nchunks, body, (init_f, init_b))

    zeros = jnp.zeros((BATCH, w), F32)
    fin_f, fin_b = carry(0, BATCH, SEQ // S5_CHUNK, zeros, zeros)
    fin_ref[0] = fin_f
    fin_ref[1] = fin_b
    carry(S5_ROWS_CTX, S5_LAT_ROWS, DEC_SEQ // S5_CHUNK, x0_ref[0, 0], x0_ref[0, 1])

    for g in range(gb):
        sl = slice(g * S5_P2, (g + 1) * S5_P2)
        p = jnp.concatenate([ps_s[0, :, sl], ps_s[1, :, sl]], axis=1)
        y_ref[:, g * S5_W:(g + 1) * S5_W] += _dot3(p, ff_ref[0, g])


def _s5(l, u_rows, tef, ff, lam, x0):
    gb = S5_GROUP_BLOCK
    w = gb * S5_P2
    return pl.pallas_call(
        _s5_kernel,
        grid=(S5_GROUPS // gb,),
        in_specs=[pl.BlockSpec((S5_ROWS, gb * S5_W), lambda i: (0, i)),
                  pl.BlockSpec((1, gb, S5_W, S5_W + 2 * S5_P2), lambda i: (l, i, 0, 0)),
                  pl.BlockSpec((1, gb, 2 * S5_P2, S5_W), lambda i: (l, i, 0, 0)),
                  pl.BlockSpec((1, 2, 2, w), lambda i: (l, 0, 0, i)),
                  pl.BlockSpec((1, 2, S5_LAT_ROWS, w), lambda i: (l, 0, 0, i))],
        out_specs=[pl.BlockSpec((S5_ROWS, gb * S5_W), lambda i: (0, i)),
                   pl.BlockSpec((2, BATCH, w), lambda i: (0, 0, i))],
        out_shape=[jax.ShapeDtypeStruct((S5_ROWS, S5_GROUPS * S5_W), F32),
                   jax.ShapeDtypeStruct((2, BATCH, S5_GROUPS * S5_P2), F32)],
        scratch_shapes=[pltpu.VMEM((2, S5_ROWS, w), F32), pltpu.VMEM((2, S5_ROWS, w), F32)],
        compiler_params=_cparams(("parallel",)),
        name="s5_scan",
    )(u_rows, tef, ff, lam, x0)


def _s5_tables(lam_re, lam_im, log_dt, b_re, b_im, c_re, c_im):
    C, G, P, H = S5_CHUNK, S5_GROUPS, S5_STATE, S5_GROUP_CH
    L = lam_re.shape[0]
    lam = lax.complex(lam_re, lam_im)
    lam_dt = lam * jnp.exp(log_dt)[..., None]
    tau = jnp.arange(C + 1, dtype=F32)
    pw = jnp.exp(lam_dt[:, :, None] * tau[None, None, :, None, None])
    b_bar = ((jnp.exp(lam_dt) - 1.0) / lam)[..., None] * lax.complex(b_re, b_im)
    c_mat = lax.complex(c_re, c_im)
    wt = pw[..., None] * b_bar[:, :, None]
    wm = jnp.concatenate([wt.real, wt.imag], axis=4)[:, :, :C]
    wm = wm.transpose(0, 1, 3, 4, 2, 5).reshape(L * 2 * G, 2 * P, C * H)
    cc = jnp.concatenate([c_re, -c_im], axis=-1).reshape(L * 2 * G, H, 2 * P)
    kt = _s5_lag_kernels(cc, wm).reshape(L, 2, G, H, C, H)
    k4 = kt.transpose(0, 1, 2, 4, 5, 3)
    jj = np.arange(C)[None, :, None]
    ii = np.arange(C)[None, None, :]
    tt = np.arange(C)[:, None, None]
    sel_f = jnp.asarray((ii - jj == tt).astype(np.float32))
    sel_b = jnp.asarray((jj - ii == tt).astype(np.float32))
    toep = lambda sel, k: jnp.einsum('tji,lgtab->lgjaib', sel, k, precision=lax.Precision.HIGHEST)
    t_sum = (toep(sel_f, k4[:, 0]) + toep(sel_b, k4[:, 1])).reshape(L, G, C * H, C * H)

    def e_of(wx):
        e = jnp.concatenate([wx.real, wx.imag], axis=3)
        return e.transpose(0, 2, 1, 4, 3).reshape(L, G, C * H, 2 * P)

    tef = jnp.concatenate([t_sum, e_of(wt[:, 0, :C][:, ::-1]), e_of(wt[:, 1, :C])], axis=-1)

    def f_of(pwx, cm):
        cp = cm[:, None] * pwx[:, :, :, None, :]
        f = jnp.concatenate([cp.real, -cp.imag], axis=-1)
        return f.transpose(0, 2, 4, 1, 3).reshape(L, G, 2 * P, C * H)

    ff = jnp.concatenate([f_of(pw[:, 0, 1:C + 1], c_mat[:, 0]),
                          f_of(pw[:, 1, 1:C + 1][:, ::-1], c_mat[:, 1])], axis=2)
    lam_c = pw[:, :, C]
    a = jnp.concatenate([lam_c.real, lam_c.real], axis=-1).reshape(L, 2, 1, G * 2 * P)
    b = jnp.concatenate([-lam_c.imag, lam_c.imag], axis=-1).reshape(L, 2, 1, G * 2 * P)
    return tef, ff, jnp.concatenate([a, b], axis=2)


def _s5_to_rows(su):
    C, G, H = S5_CHUNK, S5_GROUPS, S5_GROUP_CH
    ctx = su[:T_CTX].reshape(BATCH, SEQ // C, C, G, H).transpose(1, 0, 3, 2, 4)
    ctx = ctx.reshape(S5_ROWS_CTX, G * C * H)
    lat = su[T_CTX:].reshape(DEC_BATCH, DEC_SEQ // C, C, G, H).transpose(1, 0, 3, 2, 4)
    lat = jnp.pad(lat, ((0, 0), (0, S5_LAT_ROWS - DEC_BATCH), (0, 0), (0, 0), (0, 0)))
    return jnp.concatenate([ctx, lat.reshape(S5_ROWS_LAT, G * C * H)], axis=0)


def _s5_from_rows(y):
    C, G, H = S5_CHUNK, S5_GROUPS, S5_GROUP_CH
    ctx = y[:S5_ROWS_CTX].reshape(SEQ // C, BATCH, G, C, H).transpose(1, 0, 3, 2, 4).reshape(T_CTX, G * H)
    lat = y[S5_ROWS_CTX:].reshape(DEC_SEQ // C, S5_LAT_ROWS, G, C, H)[:, :DEC_BATCH]
    lat = lat.transpose(1, 0, 3, 2, 4).reshape(T_LAT, G * H)
    return jnp.concatenate([ctx, lat], axis=0)


def _softmax_pv(s_parts, v_parts):
    m = s_parts[0].max(axis=-1, keepdims=True)
    for s in s_parts[1:]:
        m = jnp.maximum(m, s.max(axis=-1, keepdims=True))
    o = None
    l = None
    for s, v in zip(s_parts, v_parts):
        p = jnp.exp(s - m)
        pl_ = p.sum(axis=-1, keepdims=True)
        po = _bdot(p, v)
        o = po if o is None else o + po
        l = pl_ if l is None else l + pl_
    return o / l


def _attn_ctx_kernel(q_ref, k_ref, v_ref, o_ref):
    scale = NA_DH ** -0.5
    for h in range(NA_HEADS):
        s = _bdot_nt(q_ref[h], k_ref[h]) * scale
        o_ref[h] = _softmax_pv([s], [v_ref[h]])


def _attn_ctx(nq, nk, nv):
    spec = pl.BlockSpec((NA_HEADS, SEQ, NA_DH), lambda b: (0, b, 0))
    return pl.pallas_call(
        _attn_ctx_kernel,
        grid=(BATCH,),
        in_specs=[spec, spec, spec],
        out_specs=spec,
        out_shape=jax.ShapeDtypeStruct((NA_HEADS, T_ALL, NA_DH), F32),
        compiler_params=_cparams(("parallel",)),
        name="attn_ctx",
    )(nq, nk, nv)


def _attn_lat_kernel(q_ref, k_ref, v_ref, kc_ref, vc_ref, tb_ref, _, o_ref, bias_s):
    @pl.when(pl.program_id(1) == 0)
    def _build_bias():
        bias_s[...] = jnp.full((DEC_SEQ, DEC_SEQ), -jnp.inf, F32)
        for r in range(GRID_ROWS):
            rs = min(max(r - NA_KH // 2, 0), GRID_ROWS - NA_KH)
            dr0 = rs - r + NA_WIN_H - 1
            bias_s[r * GRID_W:(r + 1) * GRID_W, rs * GRID_W:(rs + NA_KH) * GRID_W] = (
                tb_ref[0, 0, :, dr0 * GRID_W:(dr0 + NA_KH) * GRID_W])

    scale = NA_DH ** -0.5
    kb = k_ref[0].astype(BF16)
    vb = v_ref[0].astype(BF16)
    kc = kc_ref[0, 0, 0].astype(BF16)
    vc = vc_ref[0, 0, 0].astype(BF16)
    tq = NA_QBLOCK
    for qb in range(DEC_SEQ // tq):
        rows = slice(qb * tq, (qb + 1) * tq)
        qh = q_ref[0, rows, :].astype(BF16)
        s_loc = _bdot_nt(qh, kb) * scale + bias_s[rows, :]
        s_ctx = _bdot_nt(qh, kc) * scale
        o_ref[0, rows, :] = _softmax_pv([s_loc, s_ctx], [vb, vc])


def _attn_lat(l, nq, nk, nv, kc, vc, tb, prev):
    tok = pl.BlockSpec((1, DEC_SEQ, NA_DH), lambda h, b: (h, b + LAT_BLOCK0, 0))
    cache = pl.BlockSpec((1, 1, 1, PAST_LEN, NA_DH), lambda h, b: (b, l, h, 0, 0))
    return pl.pallas_call(
        _attn_lat_kernel,
        grid=(NA_HEADS, DEC_BATCH),
        in_specs=[tok, tok, tok, cache, cache,
                  pl.BlockSpec((1, 1, GRID_W, NA_REL_ROWS * GRID_W), lambda h, b: (l, h, 0, 0)),
                  _ANY],
        out_specs=tok,
        out_shape=jax.ShapeDtypeStruct((NA_HEADS, T_ALL, NA_DH), F32),
        input_output_aliases={6: 0},
        scratch_shapes=[pltpu.VMEM((DEC_SEQ, DEC_SEQ), F32)],
        compiler_params=_cparams(("arbitrary", "arbitrary")),
        name="attn_lat",
    )(nq, nk, nv, kc, vc, tb, prev)


def _na_tables(rpb):
    col = np.arange(GRID_W)
    col_start = np.clip(col - NA_WIN_W // 2, 0, GRID_W - NA_WIN_W)
    col_in = (col[None, :] >= col_start[:, None]) & (col[None, :] < col_start[:, None] + NA_WIN_W)
    col_idx = np.clip(col[None, :] - col[:, None] + NA_WIN_W - 1, 0, 2 * NA_WIN_W - 2)
    onehot = (col_idx[:, :, None] == np.arange(2 * NA_WIN_W - 1)[None, None, :]).astype(np.float32)
    tb = jnp.einsum('lhrd,qkd->lhqrk', rpb, jnp.asarray(onehot), precision=lax.Precision.HIGHEST)
    tb = jnp.where(jnp.asarray(col_in)[None, None, :, None, :], tb, -jnp.inf)
    return tb.reshape(rpb.shape[0], NA_HEADS, GRID_W, NA_REL_ROWS * GRID_W)


def _merge_kernel(x_ref, mod_ref, g_ref, ret_ref, s5y_ref, s5u_ref, gla_ref, na_ref,
                  s5d_ref, wglu_ref, bglu_ref, wbr_ref, wmg_ref, bmg_ref, wout_ref, o_ref):
    x = x_ref[...]
    mod = mod_ref[0, 0]
    hb = (_rms(x, g_ref[0, 0:1]) * (1.0 + mod[1:2]) + mod[0:1]).astype(BF16)

    y = s5y_ref[...] + s5d_ref[0] * s5u_ref[...]
    y = 0.5 * y * (1.0 + jnp.tanh(math.sqrt(2.0 / math.pi) * (y + 0.044715 * (y * y * y))))
    z = _bdot(y, wglu_ref[0]) + bglu_ref[0]
    s5_out = z[:, 0:BRANCH_W] * _sigmoid(z[:, BRANCH_W:2 * BRANCH_W])

    def gate(n):
        return _sigmoid(jnp.dot(hb, wmg_ref[0, :, n * D_MODEL:(n + 1) * D_MODEL], preferred_element_type=F32)
                        + bmg_ref[0, :, n * D_MODEL:(n + 1) * D_MODEL])

    acc = gate(0) * _bdot(ret_ref[...], wbr_ref[0, 0])
    acc += gate(1) * _bdot(s5_out, wbr_ref[0, 1])
    acc += gate(2) * _bdot(gla_ref[...], wbr_ref[0, 2])
    up = _bdot(na_ref[0], wbr_ref[0, 3, 0:NA_DH, :])
    for hh in range(1, NA_HEADS):
        up += _bdot(na_ref[hh], wbr_ref[0, 3, hh * NA_DH:(hh + 1) * NA_DH, :])
    acc += gate(3) * up
    m = _bdot(acc, wout_ref[0])
    o_ref[...] = x + mod[2:3] * _rms(m, g_ref[0, 1:2])


def _merge(l, x, mod, g_norm, ret_o, s5_y, s5_u, gla_o, na_o, s5d, wglu, bglu, wbr, wmg, bmg, wout):
    tm = TOKEN_TILE
    tok = lambda w: pl.BlockSpec((tm, w), lambda i: (i, 0))
    return pl.pallas_call(
        _merge_kernel,
        grid=(T_ALL // tm,),
        in_specs=[tok(D_MODEL), _mod_spec(l), _layer_spec(l, 4, D_MODEL),
                  tok(256), tok(256), tok(256), tok(256),
                  pl.BlockSpec((NA_HEADS, tm, NA_DH), lambda i: (0, i, 0)),
                  _layer_spec(l, 1, 256), _layer_spec(l, 256, 512), _layer_spec(l, 1, 512),
                  _layer_spec(l, N_BRANCH, BRANCH_W, D_MODEL), _layer_spec(l, D_MODEL, N_BRANCH * D_MODEL),
                  _layer_spec(l, 1, N_BRANCH * D_MODEL), _layer_spec(l, D_MODEL, D_MODEL)],
        out_specs=tok(D_MODEL),
        out_shape=jax.ShapeDtypeStruct((T_ALL, D_MODEL), F32),
        compiler_params=_cparams(("parallel",)),
        name="merge",
    )(x, mod, g_norm, ret_o, s5_y, s5_u, gla_o, na_o, s5d, wglu, bglu, wbr, wmg, bmg, wout)


FF_TILE = 1024


def _mlp_kernel(x_ref, mod_ref, g_ref, w1_ref, w2_ref, o_ref):
    x = x_ref[...]
    mod = mod_ref[0, 0]
    hb = (_rms(x, g_ref[0, 2:3]) * (1.0 + mod[4:5]) + mod[3:4]).astype(BF16)
    f = None
    for j in range(D_FF // FF_TILE):
        a = jnp.maximum(jnp.dot(hb, w1_ref[0, :, j * FF_TILE:(j + 1) * FF_TILE],
                                preferred_element_type=F32), 0.0)
        part = _bdot(a * a, w2_ref[0, j * FF_TILE:(j + 1) * FF_TILE, :])
        f = part if f is None else f + part
    o_ref[...] = x + mod[5:6] * _rms(f, g_ref[0, 3:4])


def _mlp(l, x, mod, g_norm, w1, w2):
    tm = TOKEN_TILE
    tok = pl.BlockSpec((tm, D_MODEL), lambda i: (i, 0))
    return pl.pallas_call(
        _mlp_kernel,
        grid=(T_ALL // tm,),
        in_specs=[tok, _mod_spec(l), _layer_spec(l, 4, D_MODEL),
                  _layer_spec(l, D_MODEL, D_FF), _layer_spec(l, D_FF, D_MODEL)],
        out_specs=tok,
        out_shape=jax.ShapeDtypeStruct((T_ALL, D_MODEL), F32),
        compiler_params=_cparams(("parallel",)),
        name="mlp",
    )(x, mod, g_norm, w1, w2)


def _rope_tables():
    half = RET_DK // 2
    nf = half // 2
    t = jnp.arange(DEC_SEQ)
    row = (t // GRID_W).astype(F32)
    col = (t % GRID_W).astype(F32)
    inv = ROPE_BASE ** (-jnp.arange(nf, dtype=F32) / nf)
    ang_r = row[:, None] * inv[None, :]
    ang_c = col[:, None] * inv[None, :]
    cos = jnp.concatenate([jnp.cos(ang_r)] * 2 + [jnp.cos(ang_c)] * 2, axis=1)
    sin = jnp.concatenate([-jnp.sin(ang_r), jnp.sin(ang_r), -jnp.sin(ang_c), jnp.sin(ang_c)], axis=1)
    return jnp.tile(cos, (1, RET_HEADS)), jnp.tile(sin, (1, RET_HEADS))


def _pack_w_in(w):
    offs = [0, 256, 512, 768, 1024, 1280, 1408, 1536, 1792, 2048, 2080, 2336, 2592, 2848]
    seg = lambda i: w[:, :, offs[i]:offs[i + 1]]
    order = [0, 1, 2, 3, 4, 5, 6, 7, 8, 10, 11, 12, 9]
    packed = jnp.concatenate([seg(i) for i in order], axis=2)
    return jnp.pad(packed, ((0, 0), (0, 0), (0, W_IN_PACKED - packed.shape[2]))).astype(BF16)


def _gla_state_in(st):
    eye = jnp.eye(GLA_HEADS, dtype=st.dtype)
    t = jnp.einsum('bldhkv,hg->bldhvgk', st, eye)
    return t.reshape(st.shape[0], st.shape[1], 2, GLA_HEADS * GLA_DV, GLA_HEADS * GLA_DK)


def _gla_state_out(st):
    t = st.reshape(st.shape[0], 2, GLA_HEADS, GLA_DV, GLA_HEADS, GLA_DK)
    t = jnp.stack([t[:, :, h, :, h, :] for h in range(GLA_HEADS)], axis=2)
    return t.transpose(0, 1, 2, 4, 3)


def kernel(x_prompt, x_sample, c, cache_na_k, cache_na_v, state_ret, state_s5, state_gla, c_ctx, w_ada, b_ada, g_norm, w_in, ret_log_decay, ret_gn, s5_lambda_re, s5_lambda_im, s5_log_dt, s5_b_re, s5_b_im, s5_c_re, s5_c_im, s5_d, s5_w_glu, s5_b_glu, gla_w_gate, gla_b_gate, gla_gn, na_rpb, w_branch, w_merge, b_merge, w_out, w_mlp1, w_mlp2):
    depth = w_in.shape[0]
    x = jnp.concatenate([x_prompt.reshape(T_CTX, D_MODEL), x_sample.reshape(T_LAT, D_MODEL)], axis=0)
    cc = jnp.concatenate([c_ctx[None], c, jnp.zeros((N_MOD_ROWS - 1 - DEC_BATCH, D_MODEL), F32)], axis=0)
    mod = _ada(cc, w_ada, b_ada).reshape(depth, N_MOD_ROWS, 6, D_MODEL)

    cos, sin = _rope_tables()
    w_in_p = _pack_w_in(w_in)
    w_glu_b, w_br_b, w_mg_b, w_out_b = (a.astype(BF16) for a in (s5_w_glu, w_branch, w_merge, w_out))
    w1_b, w2_b = w_mlp1.astype(BF16), w_mlp2.astype(BF16)
    ret_gn3, gla_gn3, s5_d3 = (a.reshape(depth, 1, BRANCH_W) for a in (ret_gn, gla_gn, s5_d))
    b_glu3 = s5_b_glu.reshape(depth, 1, 2 * BRANCH_W)
    b_mg3 = b_merge.reshape(depth, 1, N_BRANCH * D_MODEL)
    cache_k = cache_na_k.transpose(0, 1, 3, 2, 4)
    cache_v = cache_na_v.transpose(0, 1, 3, 2, 4)
    na_tb = _na_tables(na_rpb)
    gla_s0 = _gla_state_in(state_gla)
    tef, ff, lam = _s5_tables(s5_lambda_re, s5_lambda_im, s5_log_dt, s5_b_re, s5_b_im, s5_c_re, s5_c_im)
    s5_x0 = state_s5.transpose(1, 2, 0, 3, 5, 4).reshape(depth, 2, DEC_BATCH, S5_GROUPS * S5_P2)
    s5_x0 = jnp.pad(s5_x0, ((0, 0), (0, 0), (0, S5_LAT_ROWS - DEC_BATCH), (0, 0)))

    ks_l, vs_l, ret_l, s5_l, gla_l = [], [], [], [], []
    for l in range(depth):
        ret, su, gqk, gv, gg, glr, nq, nk, nv = _inproj(l, x, mod, g_norm, w_in_p)

        ret_o, st_ret = _retention(l, ret, ret_log_decay, ret_gn3, latent=False)
        ret_o, = _retention(l, ret, ret_log_decay, ret_gn3, latent=True, cos=cos, sin=sin, s0=state_ret,
                            prev=ret_o)

        y_rows, s5_fin = _s5(l, _s5_to_rows(su), tef, ff, lam, s5_x0)
        s5_y = _s5_from_rows(y_rows)

        gla_o, st_gla = _gla(l, gqk, gv, gg, glr, gla_w_gate, gla_b_gate, gla_gn3, latent=False)
        gla_o, = _gla(l, gqk, gv, gg, glr, gla_w_gate, gla_b_gate, gla_gn3, latent=True, s0=gla_s0,
                      prev=gla_o)

        na_o = _attn_ctx(nq, nk, nv)
        na_o = _attn_lat(l, nq, nk, nv, cache_k, cache_v, na_tb, na_o)

        x = _merge(l, x, mod, g_norm, ret_o, s5_y, su, gla_o, na_o,
                   s5_d3, w_glu_b, b_glu3, w_br_b, w_mg_b, b_mg3, w_out_b)
        x = _mlp(l, x, mod, g_norm, w1_b, w2_b)

        ks_l.append(nk[:, :T_CTX].reshape(NA_HEADS, BATCH, SEQ, NA_DH).transpose(1, 2, 0, 3))
        vs_l.append(nv[:, :T_CTX].reshape(NA_HEADS, BATCH, SEQ, NA_DH).transpose(1, 2, 0, 3))
        ret_l.append(st_ret)
        s5_l.append(s5_fin.reshape(2, BATCH, S5_GROUPS, 2, S5_STATE).transpose(1, 0, 2, 4, 3))
        gla_l.append(_gla_state_out(st_gla))

    y_prompt = x[:T_CTX].reshape(BATCH, SEQ, D_MODEL)
    y_sample = x[T_CTX:].reshape(DEC_BATCH, DEC_SEQ, D_MODEL)
    return (y_prompt, y_sample, jnp.stack(ks_l, axis=1), jnp.stack(vs_l, axis=1),
            jnp.stack(ret_l, axis=1), jnp.stack(s5_l, axis=1), jnp.stack(gla_l, axis=1))
```

```python
import functools
import math

import numpy as np
import jax
import jax.numpy as jnp
from jax import lax
from jax.experimental import pallas as pl
from jax.experimental.pallas import tpu as pltpu

F32 = jnp.float32
BF16 = jnp.bfloat16

D_MODEL = 1024
BATCH = 16
SEQ = 256
DEPTH = 4
DEC_BATCH = 4
DEC_SEQ = 1024
PAST_LEN = 256
GRID_W = 64
N_BRANCH = 4
BRANCH_W = 256
RET_HEADS = 4
RET_DK = 64
RET_DV = 64
S5_GROUPS = 16
S5_GROUP_CH = 16
S5_STATE = 64
GLA_HEADS = 4
GLA_DK = 32
GLA_DV = 64
GLA_RANK = 16
GLA_TAU = 16.0
NA_HEADS = 4
NA_DH = 64
NA_WIN_H = 8
NA_WIN_W = 16
D_FF = 4 * D_MODEL
ROPE_BASE = 10000.0
EPS = 1e-6

T_CTX = BATCH * SEQ
T_LAT = DEC_BATCH * DEC_SEQ
T_ALL = T_CTX + T_LAT
LAT_BLOCK0 = T_CTX // DEC_SEQ
N_MOD_ROWS = 8
TOKEN_TILE = 512
GLA_CHUNK = 64
S5_CHUNK = 16
S5_RELAYOUT_ROWS = 64
RET_QBLOCK = 256
NA_QBLOCK = 256
GRID_ROWS = DEC_SEQ // GRID_W
NA_KH = min(NA_WIN_H, GRID_ROWS)
NA_REL_ROWS = 2 * NA_WIN_H - 1
VMEM_LIMIT = 56 * 1024 * 1024
W_IN_PACKED = 2944


def _cparams(sem):
    return pltpu.CompilerParams(dimension_semantics=sem, vmem_limit_bytes=VMEM_LIMIT)


def _bdot(a, b):
    return jnp.dot(a.astype(BF16), b.astype(BF16), preferred_element_type=F32)


def _bdot_nt(a, b):
    return lax.dot_general(a.astype(BF16), b.astype(BF16), (((1,), (1,)), ((), ())),
                           preferred_element_type=F32)


def _bdot_tn(a, b):
    return lax.dot_general(a.astype(BF16), b.astype(BF16), (((0,), (0,)), ((), ())),
                           preferred_element_type=F32)


def _split(a):
    hi = a.astype(BF16)
    lo = (a - hi.astype(F32)).astype(BF16)
    return hi, lo


def _dot3(a, b):
    ah, al = _split(a)
    bh, bl = _split(b)
    d = functools.partial(jnp.dot, preferred_element_type=F32)
    return d(ah, bh) + d(al, bh) + d(ah, bl)


def _sigmoid(x):
    return 1.0 / (1.0 + jnp.exp(-x))


def _silu(x):
    return x * _sigmoid(x)


def _rms(x, g):
    return x * lax.rsqrt(jnp.mean(x * x, axis=-1, keepdims=True) + EPS) * g


def _group_norm(o, g):
    mu = jnp.mean(o, axis=-1, keepdims=True)
    xc = o - mu
    return xc * lax.rsqrt(jnp.mean(xc * xc, axis=-1, keepdims=True) + EPS) * g


def _mod_row(i):
    ctx_tiles = T_CTX // TOKEN_TILE
    return jnp.where(i < ctx_tiles, 0, 1 + (i - ctx_tiles) // (DEC_SEQ // TOKEN_TILE))


def _mod_spec(l):
    return pl.BlockSpec((1, 1, 6, D_MODEL), lambda i: (l, _mod_row(i), 0, 0))


def _layer_spec(l, *shape):
    return pl.BlockSpec((1,) + shape, lambda *_: (l,) + (0,) * len(shape))


_ANY = pl.BlockSpec(memory_space=pl.ANY)


ADA_TILE = 1536


def _ada_kernel(c_ref, w_ref, b_ref, o_ref):
    a = _silu(c_ref[...])
    o_ref[0] = _bdot(a, w_ref[0]) + b_ref[0]


def _ada(cc, w_ada, b_ada):
    n = 6 * D_MODEL
    return pl.pallas_call(
        _ada_kernel,
        grid=(DEPTH, n // ADA_TILE),
        in_specs=[pl.BlockSpec((N_MOD_ROWS, D_MODEL), lambda l, j: (0, 0)),
                  pl.BlockSpec((1, D_MODEL, ADA_TILE), lambda l, j: (l, 0, j)),
                  pl.BlockSpec((1, 1, ADA_TILE), lambda l, j: (l, 0, j))],
        out_specs=pl.BlockSpec((1, N_MOD_ROWS, ADA_TILE), lambda l, j: (l, 0, j)),
        out_shape=jax.ShapeDtypeStruct((DEPTH, N_MOD_ROWS, n), F32),
        compiler_params=_cparams(("parallel", "parallel")),
        name="ada_mod",
    )(cc, w_ada, b_ada.reshape(DEPTH, 1, n))


def _inproj_kernel(x_ref, mod_ref, g_ref, w_ref,
                   ret_ref, s5_ref, gqk_ref, gv_ref, gg_ref, glr_ref, nq_ref, nk_ref, nv_ref):
    mod = mod_ref[0, 0]
    h = _rms(x_ref[...], g_ref[0, 0:1]) * (1.0 + mod[1:2]) + mod[0:1]
    hb = h.astype(BF16)

    def proj(lo, hi):
        return jnp.dot(hb, w_ref[0, :, lo:hi], preferred_element_type=F32)

    ret_ref[...] = proj(0, 1024)
    s5_ref[...] = proj(1024, 1280)
    gqk_ref[...] = proj(1280, 1536)
    gv_ref[...] = proj(1536, 1792)
    gg_ref[...] = proj(1792, 2048)
    for ref, lo in ((nq_ref, 2048), (nk_ref, 2304), (nv_ref, 2560)):
        r = proj(lo, lo + 256)
        for hh in range(NA_HEADS):
            ref[hh] = r[:, hh * NA_DH:(hh + 1) * NA_DH]
    glr_ref[...] = proj(2816, 2944)


def _inproj(l, x, mod, g_norm, w_in_p):
    tm = TOKEN_TILE
    tok = lambda w: pl.BlockSpec((tm, w), lambda i: (i, 0))
    head = pl.BlockSpec((NA_HEADS, tm, NA_DH), lambda i: (0, i, 0))
    tshape = lambda w: jax.ShapeDtypeStruct((T_ALL, w), F32)
    hshape = jax.ShapeDtypeStruct((NA_HEADS, T_ALL, NA_DH), F32)
    return pl.pallas_call(
        _inproj_kernel,
        grid=(T_ALL // tm,),
        in_specs=[tok(D_MODEL), _mod_spec(l), _layer_spec(l, 4, D_MODEL),
                  _layer_spec(l, D_MODEL, W_IN_PACKED)],
        out_specs=[tok(1024), tok(256), tok(256), tok(256), tok(256), tok(128), head, head, head],
        out_shape=[tshape(1024), tshape(256), tshape(256), tshape(256), tshape(256), tshape(128),
                   hshape, hshape, hshape],
        compiler_params=_cparams(("parallel",)),
        name="in_proj",
    )(x, mod, g_norm, w_in_p)


def _rope_rotate(x, lane):
    first = (lane % 32) < 16
    w = x.shape[-1]
    return jnp.where(first, pltpu.roll(x, w - 16, 1), pltpu.roll(x, 16, 1))


def _ret_kernel(ld_ref, ret_ref, gn_ref, *rest, layer, seq, latent):
    if latent:
        cos_ref, sin_ref, s0_ref, _, out_ref = rest
    else:
        out_ref, st_ref = rest
    q = ret_ref[:, 0:256]
    k = ret_ref[:, 256:512]
    if latent:
        lane = lax.broadcasted_iota(jnp.int32, (seq, 256), 1)
        cos = cos_ref[...]
        sin = sin_ref[...]
        q = q * cos + _rope_rotate(q, lane) * sin
        k = k * cos + _rope_rotate(k, lane) * sin
    k = k * (RET_DK ** -0.5)
    tq = RET_QBLOCK
    col = lax.broadcasted_iota(jnp.int32, (tq, seq), 1)
    row = lax.broadcasted_iota(jnp.int32, (tq, seq), 0)
    pos_c = lax.broadcasted_iota(jnp.int32, (seq, 1), 0).astype(F32)
    for h in range(RET_HEADS):
        lgf = ld_ref[layer, 0, h]
        lgb = ld_ref[layer, 1, h]
        sl = slice(h * RET_DK, (h + 1) * RET_DK)
        qh = q[:, sl]
        kh = k[:, sl]
        vh = ret_ref[:, 512 + h * RET_DV:512 + (h + 1) * RET_DV]
        kb = kh.astype(BF16)
        vb = vh.astype(BF16)
        if latent:
            q_init = jnp.concatenate([qh * jnp.exp(lgf * (pos_c + 1.0)),
                                      qh * jnp.exp(lgb * (seq - pos_c))], axis=1)
            s_init = jnp.concatenate([s0_ref[0, 0, 0, h], s0_ref[0, 0, 1, h]], axis=0)
        for qb in range(seq // tq):
            rel = (row + (qb * tq) - col).astype(F32)
            decay = (jnp.where(rel >= 0, jnp.exp(lgf * jnp.maximum(rel, 0.0)), 0.0)
                     + jnp.where(rel <= 0, jnp.exp(lgb * jnp.maximum(-rel, 0.0)), 0.0))
            rows = slice(qb * tq, (qb + 1) * tq)
            s = _bdot_nt(qh[rows], kb) * decay
            o = _bdot(s, vb)
            if latent:
                o = o + _bdot(q_init[rows], s_init)
            g = ret_ref[rows, 768 + h * RET_DV:768 + (h + 1) * RET_DV]
            out_ref[rows, sl] = _group_norm(o, gn_ref[0, :, sl]) * _silu(g)
        if not latent:
            st_ref[0, 0, h] = _bdot_tn(kh * jnp.exp(lgf * (seq - 1.0 - pos_c)), vb)
            st_ref[0, 1, h] = _bdot_tn(kh * jnp.exp(lgb * pos_c), vb)


def _retention(l, ret, ld, gn, *, latent, cos=None, sin=None, s0=None, prev=None):
    seq = DEC_SEQ if latent else SEQ
    nb = DEC_BATCH if latent else BATCH
    off = LAT_BLOCK0 if latent else 0
    in_specs = [pl.BlockSpec(memory_space=pltpu.SMEM),
                pl.BlockSpec((seq, 1024), lambda b: (b + off, 0)),
                _layer_spec(l, 1, 256)]
    args = [ld, ret, gn]
    out_specs = [pl.BlockSpec((seq, 256), lambda b: (b + off, 0))]
    out_shape = [jax.ShapeDtypeStruct((T_ALL, 256), F32)]
    aliases = {}
    if latent:
        in_specs += [pl.BlockSpec((seq, 256), lambda b: (0, 0)),
                     pl.BlockSpec((seq, 256), lambda b: (0, 0)),
                     pl.BlockSpec((1, 1, 2, RET_HEADS, RET_DK, RET_DV), lambda b: (b, l, 0, 0, 0, 0)),
                     _ANY]
        args += [cos, sin, s0, prev]
        aliases = {6: 0}
    else:
        out_specs.append(pl.BlockSpec((1, 2, RET_HEADS, RET_DK, RET_DV), lambda b: (b, 0, 0, 0, 0)))
        out_shape.append(jax.ShapeDtypeStruct((nb, 2, RET_HEADS, RET_DK, RET_DV), F32))
    return pl.pallas_call(
        functools.partial(_ret_kernel, layer=l, seq=seq, latent=latent),
        grid=(nb,),
        in_specs=in_specs, out_specs=out_specs, out_shape=out_shape,
        input_output_aliases=aliases,
        compiler_params=_cparams(("parallel",)),
        name="retention_lat" if latent else "retention_ctx",
    )(*args)


def _gla_kernel(gqk_ref, gv_ref, gg_ref, glr_ref, wg_ref, bg_ref, gn_ref, *rest, seq, latent):
    if latent:
        s0_ref, _, out_ref, gate_s, o_s, st_s = rest
    else:
        out_ref, st_ref, gate_s, o_s, st_s = rest
    c = GLA_CHUNK
    n = seq // c
    hk = GLA_HEADS * GLA_DK
    lr = glr_ref[...]
    for d in range(2):
        pre = _bdot(lr[:, d * GLA_RANK:(d + 1) * GLA_RANK], wg_ref[0, d]) + bg_ref[0, d:d + 1]
        gate_s[d] = (jnp.minimum(pre, 0.0) - jnp.log(1.0 + jnp.exp(-jnp.abs(pre)))) / GLA_TAU
        st_s[d] = s0_ref[0, 0, d] if latent else jnp.zeros((GLA_HEADS * GLA_DV, hk), F32)

    ti = lax.broadcasted_iota(jnp.int32, (c, c), 0)
    tj = lax.broadcasted_iota(jnp.int32, (c, c), 1)
    tri = [(tj <= ti).astype(BF16), (tj >= ti).astype(BF16)]
    lane_k = lax.broadcasted_iota(jnp.int32, (c, hk), 1)
    head_mask = [(lane_k // GLA_DK) == h for h in range(GLA_HEADS)]
    ai = lax.broadcasted_iota(jnp.int32, (GLA_HEADS * c, c), 0) % c
    aj = lax.broadcasted_iota(jnp.int32, (GLA_HEADS * c, c), 1)
    keep = [aj <= ai, aj >= ai]
    sr = lax.broadcasted_iota(jnp.int32, (GLA_HEADS * GLA_DV, hk), 0) // GLA_DV
    sc = lax.broadcasted_iota(jnp.int32, (GLA_HEADS * GLA_DV, hk), 1) // GLA_DK
    diag = sr == sc
    mid = (c // 2 - 1, c // 2)
    end = (c - 1, 0)
    scale = GLA_DK ** -0.5

    def one(ci, d):
        rows = pl.ds(pl.multiple_of(ci * c, c), c)
        q = gqk_ref[rows, 0:hk] * scale
        k = gqk_ref[rows, hk:2 * hk]
        v = gv_ref[rows, :].astype(BF16)
        gh, gl = _split(gate_s[d, rows, :])
        b = (jnp.dot(tri[d], gh, preferred_element_type=F32)
             + jnp.dot(tri[d], gl, preferred_element_type=F32))
        b_mid = b[mid[d]:mid[d] + 1]
        b_end = b[end[d]:end[d] + 1]
        q_att = q * jnp.exp(b - b_mid)
        k_att = k * jnp.exp(b_mid - b)
        q_st = q * jnp.exp(b)
        k_st = k * jnp.exp(b_end - b)
        q_stack = jnp.concatenate([jnp.where(head_mask[h], q_att, 0.0) for h in range(GLA_HEADS)], axis=0)
        att = jnp.where(keep[d], _bdot_nt(q_stack, k_att), 0.0).astype(BF16)
        o = jnp.concatenate(
            [jnp.dot(att[h * c:(h + 1) * c], v[:, h * GLA_DV:(h + 1) * GLA_DV], preferred_element_type=F32)
             for h in range(GLA_HEADS)], axis=1)
        st = st_s[d]
        o_s[d, rows, :] = o + _bdot_nt(q_st, st)
        st_s[d] = st * jnp.exp(b_end) + jnp.where(diag, _bdot_tn(v, k_st), 0.0)

    def body(i, carry):
        one(i, 0)
        one(n - 1 - i, 1)
        return carry

    lax.fori_loop(0, n, body, 0)

    o = o_s[0] + o_s[1]
    for h in range(GLA_HEADS):
        sl = slice(h * GLA_DV, (h + 1) * GLA_DV)
        out_ref[:, sl] = _group_norm(o[:, sl], gn_ref[0, :, sl]) * _silu(gg_ref[:, sl])
    if not latent:
        st_ref[0, 0] = st_s[0]
        st_ref[0, 1] = st_s[1]


def _gla(l, gqk, gv, gg, glr, wg, bg, gn, *, latent, s0=None, prev=None):
    seq = DEC_SEQ if latent else SEQ
    nb = DEC_BATCH if latent else BATCH
    off = LAT_BLOCK0 if latent else 0
    hk = GLA_HEADS * GLA_DK
    hv = GLA_HEADS * GLA_DV
    tok = lambda w: pl.BlockSpec((seq, w), lambda b: (b + off, 0))
    in_specs = [tok(256), tok(256), tok(256), tok(128),
                _layer_spec(l, 2, GLA_RANK, hk), _layer_spec(l, 2, hk), _layer_spec(l, 1, 256)]
    args = [gqk, gv, gg, glr, wg, bg, gn]
    out_specs = [tok(256)]
    out_shape = [jax.ShapeDtypeStruct((T_ALL, 256), F32)]
    aliases = {}
    if latent:
        in_specs += [pl.BlockSpec((1, 1, 2, hv, hk), lambda b: (b, l, 0, 0, 0)), _ANY]
        args += [s0, prev]
        aliases = {8: 0}
    else:
        out_specs.append(pl.BlockSpec((1, 2, hv, hk), lambda b: (b, 0, 0, 0)))
        out_shape.append(jax.ShapeDtypeStruct((nb, 2, hv, hk), F32))
    return pl.pallas_call(
        functools.partial(_gla_kernel, seq=seq, latent=latent),
        grid=(nb,),
        in_specs=in_specs, out_specs=out_specs, out_shape=out_shape,
        input_output_aliases=aliases,
        scratch_shapes=[pltpu.VMEM((2, seq, hk), F32), pltpu.VMEM((2, seq, hv), F32),
                        pltpu.VMEM((2, hv, hk), F32)],
        compiler_params=_cparams(("parallel",)),
        name="gla_lat" if latent else "gla_ctx",
    )(*args)


S5_W = S5_CHUNK * S5_GROUP_CH
S5_P2 = 2 * S5_STATE
S5_ROWS = T_ALL // S5_CHUNK
S5_ROWS_CTX = T_CTX // S5_CHUNK
S5_LANE_GROUPS = 128 // S5_GROUP_CH


def _s5_toeplitz_kernel(cc_ref, wfr_ref, wb_ref, o_ref):
    kf = _dot3(cc_ref[0, 0], wfr_ref[0])
    kb = _dot3(cc_ref[0, 1], wb_ref[0])
    lane = lax.broadcasted_iota(jnp.int32, (S5_GROUP_CH, S5_W), 1)
    for i in range(S5_CHUNK):
        sf = ((i + 1 - S5_CHUNK) * S5_GROUP_CH) % S5_W
        fwd = jnp.where(lane < (i + 1) * S5_GROUP_CH, pltpu.roll(kf, sf, 1) if sf else kf, 0.0)
        bwd = jnp.where(lane >= i * S5_GROUP_CH, pltpu.roll(kb, i * S5_GROUP_CH, 1) if i else kb, 0.0)
        o_ref[0, i * S5_GROUP_CH:(i + 1) * S5_GROUP_CH, :] = fwd + bwd
    o_ref[0, S5_W:S5_W + S5_P2, :] = wfr_ref[0]
    o_ref[0, S5_W + S5_P2:S5_W + 2 * S5_P2, :] = wb_ref[0]


def _s5_toeplitz(cc, wfr, wb):
    n = cc.shape[0]
    return pl.pallas_call(
        _s5_toeplitz_kernel,
        grid=(n,),
        in_specs=[pl.BlockSpec((1, 2, S5_GROUP_CH, S5_P2), lambda i: (i, 0, 0, 0)),
                  pl.BlockSpec((1, S5_P2, S5_W), lambda i: (i, 0, 0)),
                  pl.BlockSpec((1, S5_P2, S5_W), lambda i: (i, 0, 0))],
        out_specs=pl.BlockSpec((1, S5_W + 2 * S5_P2, S5_W), lambda i: (i, 0, 0)),
        out_shape=jax.ShapeDtypeStruct((n, S5_W + 2 * S5_P2, S5_W), F32),
        compiler_params=_cparams(("parallel",)),
        name="s5_toeplitz",
    )(cc, wfr, wb)


def _s5_kernel(u_ref, te_ref, ff_ref, lam_ref, x0_ref, y_ref, fin_ref, ug_s, xs_s, ps_s):
    C = S5_CHUNK
    H = S5_GROUP_CH
    ng = S5_LANE_GROUPS
    rc_rows = S5_RELAYOUT_ROWS
    piece = lax.broadcasted_iota(jnp.int32, (rc_rows, 128), 1) // H

    def gather_body(rc, carry):
        r0 = pl.multiple_of(rc * rc_rows, rc_rows)
        for m in range(S5_W // 128):
            for g in range(ng):
                acc = None
                for jl in range(ng):
                    blk = u_ref[pl.ds(r0 * C + m * ng + jl, rc_rows, stride=C), :]
                    sh = ((jl - g) * H) % 128
                    if sh:
                        blk = pltpu.roll(blk, sh, 1)
                    acc = blk if acc is None else jnp.where(piece == jl, blk, acc)
                col = g * S5_W + m * 128
                ug_s[pl.ds(r0, rc_rows), col:col + 128] = acc
        return carry

    lax.fori_loop(0, S5_ROWS // rc_rows, gather_body, 0)

    nt = lambda a, b: lax.dot_general(a, b, (((1,), (1,)), ((), ())), preferred_element_type=F32)
    for g in range(ng):
        uh, ul = _split(ug_s[:, g * S5_W:(g + 1) * S5_W])
        th, tl = _split(te_ref[0, g])
        r = nt(uh, th) + nt(ul, th) + nt(uh, tl)
        ug_s[:, g * S5_W:(g + 1) * S5_W] = r[:, 0:S5_W]
        xs_s[g] = r[:, S5_W:S5_W + S5_P2]
        xs_s[ng + g] = r[:, S5_W + S5_P2:S5_W + 2 * S5_P2]

    w = ng * S5_P2
    lane = lax.broadcasted_iota(jnp.int32, (1, w), 1)
    first = (lane % S5_P2) < S5_STATE

    def swap(s):
        return jnp.where(first, pltpu.roll(s, w - S5_STATE, 1), pltpu.roll(s, S5_STATE, 1))

    def carry(base, nseq, nchunks, init_f, init_b):
        a_f, b_f = lam_ref[0, 0, 0:1], lam_ref[0, 0, 1:2]
        a_b, b_b = lam_ref[0, 1, 0:1], lam_ref[0, 1, 1:2]

        def body(i, st):
            sf, sb = st
            rf = pl.ds(base + i, nseq, stride=nchunks)
            rb = pl.ds(base + (nchunks - 1 - i), nseq, stride=nchunks)
            for g in range(ng):
                ps_s[g, rf, :] = sf[:, g * S5_P2:(g + 1) * S5_P2]
                ps_s[ng + g, rb, :] = sb[:, g * S5_P2:(g + 1) * S5_P2]
            xf = jnp.concatenate([xs_s[g, rf, :] for g in range(ng)], axis=1)
            xb = jnp.concatenate([xs_s[ng + g, rb, :] for g in range(ng)], axis=1)
            sf = a_f * sf + b_f * swap(sf) + xf
            sb = a_b * sb + b_b * swap(sb) + xb
            return sf, sb

        return lax.fori_loop(0, nchunks, body, (init_f, init_b))

    zeros = jnp.zeros((BATCH, w), F32)
    fin_f, fin_b = carry(0, BATCH, SEQ // C, zeros, zeros)
    fin_ref[0] = fin_f
    fin_ref[1] = fin_b
    carry(S5_ROWS_CTX, DEC_BATCH, DEC_SEQ // C, x0_ref[0, 0], x0_ref[0, 1])

    for g in range(ng):
        p = jnp.concatenate([ps_s[g], ps_s[ng + g]], axis=1)
        ug_s[:, g * S5_W:(g + 1) * S5_W] += _dot3(p, ff_ref[0, g])

    def scatter_body(rc, carry):
        r0 = pl.multiple_of(rc * rc_rows, rc_rows)
        for i in range(C):
            acc = None
            for g in range(ng):
                col = g * S5_W + (i // ng) * 128
                blk = ug_s[pl.ds(r0, rc_rows), col:col + 128]
                sh = ((g - i % ng) * H) % 128
                if sh:
                    blk = pltpu.roll(blk, sh, 1)
                acc = blk if acc is None else jnp.where(piece == g, blk, acc)
            y_ref[pl.ds(r0 * C + i, rc_rows, stride=C), :] = acc
        return carry

    lax.fori_loop(0, S5_ROWS // rc_rows, scatter_body, 0)


def _s5(l, su, te, ff, lam, x0):
    ng = S5_LANE_GROUPS
    w = ng * S5_P2
    return pl.pallas_call(
        _s5_kernel,
        grid=(S5_GROUPS // ng,),
        in_specs=[pl.BlockSpec((T_ALL, 128), lambda i: (0, i)),
                  pl.BlockSpec((1, ng, S5_W + 2 * S5_P2, S5_W), lambda i: (l, i, 0, 0)),
                  pl.BlockSpec((1, ng, 2 * S5_P2, S5_W), lambda i: (l, i, 0, 0)),
                  pl.BlockSpec((1, 2, 2, w), lambda i: (l, 0, 0, i)),
                  pl.BlockSpec((1, 2, DEC_BATCH, w), lambda i: (l, 0, 0, i))],
        out_specs=[pl.BlockSpec((T_ALL, 128), lambda i: (0, i)),
                   pl.BlockSpec((2, BATCH, w), lambda i: (0, 0, i))],
        out_shape=[jax.ShapeDtypeStruct((T_ALL, S5_GROUPS * S5_GROUP_CH), F32),
                   jax.ShapeDtypeStruct((2, BATCH, S5_GROUPS * S5_P2), F32)],
        scratch_shapes=[pltpu.VMEM((S5_ROWS, ng * S5_W), F32),
                        pltpu.VMEM((2 * ng, S5_ROWS, S5_P2), F32), pltpu.VMEM((2 * ng, S5_ROWS, S5_P2), F32)],
        compiler_params=_cparams(("parallel",)),
        name="s5_scan",
    )(su, te, ff, lam, x0)


def _s5_tables(lam_re, lam_im, log_dt, b_re, b_im, c_re, c_im):
    C, G, P, H = S5_CHUNK, S5_GROUPS, S5_STATE, S5_GROUP_CH
    L = lam_re.shape[0]
    dt = jnp.exp(log_dt)[..., None]
    ar, ai = lam_re * dt, lam_im * dt

    def power(d, t):
        mag = jnp.exp(ar[:, d, :, :, None] * t)
        ang = ai[:, d, :, :, None] * t
        return mag * jnp.cos(ang), mag * jnp.sin(ang)

    lr, li = jnp.exp(ar) * jnp.cos(ai), jnp.exp(ar) * jnp.sin(ai)
    den = lam_re * lam_re + lam_im * lam_im
    qr = ((lr - 1.0) * lam_re + li * lam_im) / den
    qi = (li * lam_re - (lr - 1.0) * lam_im) / den
    bbr = qr[..., None] * b_re - qi[..., None] * b_im
    bbi = qr[..., None] * b_im + qi[..., None] * b_re
    lanes = lambda a: jnp.tile(a, (1, 1, 1, 1, C))
    bbr, bbi = lanes(bbr), lanes(bbi)
    c_t = lambda a: lanes(a.transpose(0, 1, 2, 4, 3))
    ctr, cti = c_t(c_re), c_t(c_im)
    tau = (jnp.arange(C * H) // H).astype(F32)

    def w_of(d, t):
        pr, pi = power(d, t)
        return jnp.concatenate([pr * bbr[:, d] - pi * bbi[:, d], pr * bbi[:, d] + pi * bbr[:, d]], axis=-2)

    def f_of(d, t):
        pr, pi = power(d, t)
        return jnp.concatenate([ctr[:, d] * pr - cti[:, d] * pi, -(ctr[:, d] * pi + cti[:, d] * pr)], axis=-2)

    wfr = w_of(0, (C - 1.0) - tau).reshape(L * G, 2 * P, C * H)
    wb = w_of(1, tau).reshape(L * G, 2 * P, C * H)
    cc = jnp.concatenate([c_re, -c_im], axis=-1).transpose(0, 2, 1, 3, 4).reshape(L * G, 2, H, 2 * P)
    te = _s5_toeplitz(cc, wfr, wb).reshape(L, G, C * H + 4 * P, C * H)
    ff = jnp.concatenate([f_of(0, tau + 1.0), f_of(1, C - tau)], axis=-2)
    cr, ci = jnp.exp(ar * C) * jnp.cos(ai * C), jnp.exp(ar * C) * jnp.sin(ai * C)
    a = jnp.concatenate([cr, cr], axis=-1).reshape(L, 2, 1, G * 2 * P)
    b = jnp.concatenate([-ci, ci], axis=-1).reshape(L, 2, 1, G * 2 * P)
    return te, ff, jnp.concatenate([a, b], axis=2)


def _softmax_pv(s_parts, v_parts):
    m = s_parts[0].max(axis=-1, keepdims=True)
    for s in s_parts[1:]:
        m = jnp.maximum(m, s.max(axis=-1, keepdims=True))
    o = None
    l = None
    for s, v in zip(s_parts, v_parts):
        p = jnp.exp(s - m)
        pl_ = p.sum(axis=-1, keepdims=True)
        po = _bdot(p, v)
        o = po if o is None else o + po
        l = pl_ if l is None else l + pl_
    return o / l


def _attn_ctx_kernel(q_ref, k_ref, v_ref, o_ref):
    scale = NA_DH ** -0.5
    for h in range(NA_HEADS):
        s = _bdot_nt(q_ref[h], k_ref[h]) * scale
        o_ref[h] = _softmax_pv([s], [v_ref[h]])


def _attn_ctx(nq, nk, nv):
    spec = pl.BlockSpec((NA_HEADS, SEQ, NA_DH), lambda b: (0, b, 0))
    return pl.pallas_call(
        _attn_ctx_kernel,
        grid=(BATCH,),
        in_specs=[spec, spec, spec],
        out_specs=spec,
        out_shape=jax.ShapeDtypeStruct((NA_HEADS, T_ALL, NA_DH), F32),
        compiler_params=_cparams(("parallel",)),
        name="attn_ctx",
    )(nq, nk, nv)


def _attn_lat_kernel(q_ref, k_ref, v_ref, kc_ref, vc_ref, tb_ref, _, o_ref, bias_s):
    @pl.when(pl.program_id(1) == 0)
    def _build_bias():
        bias_s[...] = jnp.full((DEC_SEQ, DEC_SEQ), -jnp.inf, F32)
        for r in range(GRID_ROWS):
            rs = min(max(r - NA_KH // 2, 0), GRID_ROWS - NA_KH)
            dr0 = rs - r + NA_WIN_H - 1
            bias_s[r * GRID_W:(r + 1) * GRID_W, rs * GRID_W:(rs + NA_KH) * GRID_W] = (
                tb_ref[0, 0, :, dr0 * GRID_W:(dr0 + NA_KH) * GRID_W])

    scale = NA_DH ** -0.5
    kb = k_ref[0].astype(BF16)
    vb = v_ref[0].astype(BF16)
    kc = kc_ref[0, 0, 0].astype(BF16)
    vc = vc_ref[0, 0, 0].astype(BF16)
    tq = NA_QBLOCK
    for qb in range(DEC_SEQ // tq):
        rows = slice(qb * tq, (qb + 1) * tq)
        qh = q_ref[0, rows, :].astype(BF16)
        s_loc = _bdot_nt(qh, kb) * scale + bias_s[rows, :]
        s_ctx = _bdot_nt(qh, kc) * scale
        o_ref[0, rows, :] = _softmax_pv([s_loc, s_ctx], [vb, vc])


def _attn_lat(l, nq, nk, nv, kc, vc, tb, prev):
    tok = pl.BlockSpec((1, DEC_SEQ, NA_DH), lambda h, b: (h, b + LAT_BLOCK0, 0))
    cache = pl.BlockSpec((1, 1, 1, PAST_LEN, NA_DH), lambda h, b: (b, l, h, 0, 0))
    return pl.pallas_call(
        _attn_lat_kernel,
        grid=(NA_HEADS, DEC_BATCH),
        in_specs=[tok, tok, tok, cache, cache,
                  pl.BlockSpec((1, 1, GRID_W, NA_REL_ROWS * GRID_W), lambda h, b: (l, h, 0, 0)),
                  _ANY],
        out_specs=tok,
        out_shape=jax.ShapeDtypeStruct((NA_HEADS, T_ALL, NA_DH), F32),
        input_output_aliases={6: 0},
        scratch_shapes=[pltpu.VMEM((DEC_SEQ, DEC_SEQ), F32)],
        compiler_params=_cparams(("arbitrary", "arbitrary")),
        name="attn_lat",
    )(nq, nk, nv, kc, vc, tb, prev)


def _na_tables(rpb):
    col = np.arange(GRID_W)
    col_start = np.clip(col - NA_WIN_W // 2, 0, GRID_W - NA_WIN_W)
    col_in = (col[None, :] >= col_start[:, None]) & (col[None, :] < col_start[:, None] + NA_WIN_W)
    col_idx = np.clip(col[None, :] - col[:, None] + NA_WIN_W - 1, 0, 2 * NA_WIN_W - 2)
    onehot = (col_idx[:, :, None] == np.arange(2 * NA_WIN_W - 1)[None, None, :]).astype(np.float32)
    tb = jnp.einsum('lhrd,qkd->lhqrk', rpb, jnp.asarray(onehot), precision=lax.Precision.HIGHEST)
    tb = jnp.where(jnp.asarray(col_in)[None, None, :, None, :], tb, -jnp.inf)
    return tb.reshape(rpb.shape[0], NA_HEADS, GRID_W, NA_REL_ROWS * GRID_W)


def _merge_kernel(x_ref, mod_ref, g_ref, ret_ref, s5y_ref, s5u_ref, gla_ref, na_ref,
                  s5d_ref, wglu_ref, bglu_ref, wbr_ref, wmg_ref, bmg_ref, wout_ref, o_ref):
    x = x_ref[...]
    mod = mod_ref[0, 0]
    hb = (_rms(x, g_ref[0, 0:1]) * (1.0 + mod[1:2]) + mod[0:1]).astype(BF16)

    y = s5y_ref[...] + s5d_ref[0] * s5u_ref[...]
    y = 0.5 * y * (1.0 + jnp.tanh(math.sqrt(2.0 / math.pi) * (y + 0.044715 * (y * y * y))))
    z = _bdot(y, wglu_ref[0]) + bglu_ref[0]
    s5_out = z[:, 0:BRANCH_W] * _sigmoid(z[:, BRANCH_W:2 * BRANCH_W])

    def gate(n):
        return _sigmoid(jnp.dot(hb, wmg_ref[0, :, n * D_MODEL:(n + 1) * D_MODEL], preferred_element_type=F32)
                        + bmg_ref[0, :, n * D_MODEL:(n + 1) * D_MODEL])

    acc = gate(0) * _bdot(ret_ref[...], wbr_ref[0, 0])
    acc += gate(1) * _bdot(s5_out, wbr_ref[0, 1])
    acc += gate(2) * _bdot(gla_ref[...], wbr_ref[0, 2])
    up = _bdot(na_ref[0], wbr_ref[0, 3, 0:NA_DH, :])
    for hh in range(1, NA_HEADS):
        up += _bdot(na_ref[hh], wbr_ref[0, 3, hh * NA_DH:(hh + 1) * NA_DH, :])
    acc += gate(3) * up
    m = _bdot(acc, wout_ref[0])
    o_ref[...] = x + mod[2:3] * _rms(m, g_ref[0, 1:2])


def _merge(l, x, mod, g_norm, ret_o, s5_y, s5_u, gla_o, na_o, s5d, wglu, bglu, wbr, wmg, bmg, wout):
    tm = TOKEN_TILE
    tok = lambda w: pl.BlockSpec((tm, w), lambda i: (i, 0))
    return pl.pallas_call(
        _merge_kernel,
        grid=(T_ALL // tm,),
        in_specs=[tok(D_MODEL), _mod_spec(l), _layer_spec(l, 4, D_MODEL),
                  tok(256), tok(256), tok(256), tok(256),
                  pl.BlockSpec((NA_HEADS, tm, NA_DH), lambda i: (0, i, 0)),
                  _layer_spec(l, 1, 256), _layer_spec(l, 256, 512), _layer_spec(l, 1, 512),
                  _layer_spec(l, N_BRANCH, BRANCH_W, D_MODEL), _layer_spec(l, D_MODEL, N_BRANCH * D_MODEL),
                  _layer_spec(l, 1, N_BRANCH * D_MODEL), _layer_spec(l, D_MODEL, D_MODEL)],
        out_specs=tok(D_MODEL),
        out_shape=jax.ShapeDtypeStruct((T_ALL, D_MODEL), F32),
        compiler_params=_cparams(("parallel",)),
        name="merge",
    )(x, mod, g_norm, ret_o, s5_y, s5_u, gla_o, na_o, s5d, wglu, bglu, wbr, wmg, bmg, wout)


FF_TILE = 1024


def _mlp_kernel(x_ref, mod_ref, g_ref, w1_ref, w2_ref, o_ref):
    x = x_ref[...]
    mod = mod_ref[0, 0]
    hb = (_rms(x, g_ref[0, 2:3]) * (1.0 + mod[4:5]) + mod[3:4]).astype(BF16)
    f = None
    for j in range(D_FF // FF_TILE):
        a = jnp.maximum(jnp.dot(hb, w1_ref[0, :, j * FF_TILE:(j + 1) * FF_TILE],
                                preferred_element_type=F32), 0.0)
        part = _bdot(a * a, w2_ref[0, j * FF_TILE:(j + 1) * FF_TILE, :])
        f = part if f is None else f + part
    o_ref[...] = x + mod[5:6] * _rms(f, g_ref[0, 3:4])


def _mlp(l, x, mod, g_norm, w1, w2):
    tm = TOKEN_TILE
    tok = pl.BlockSpec((tm, D_MODEL), lambda i: (i, 0))
    return pl.pallas_call(
        _mlp_kernel,
        grid=(T_ALL // tm,),
        in_specs=[tok, _mod_spec(l), _layer_spec(l, 4, D_MODEL),
                  _layer_spec(l, D_MODEL, D_FF), _layer_spec(l, D_FF, D_MODEL)],
        out_specs=tok,
        out_shape=jax.ShapeDtypeStruct((T_ALL, D_MODEL), F32),
        compiler_params=_cparams(("parallel",)),
        name="mlp",
    )(x, mod, g_norm, w1, w2)


def _rope_tables():
    half = RET_DK // 2
    nf = half // 2
    t = jnp.arange(DEC_SEQ)
    row = (t // GRID_W).astype(F32)
    col = (t % GRID_W).astype(F32)
    inv = ROPE_BASE ** (-jnp.arange(nf, dtype=F32) / nf)
    ang_r = row[:, None] * inv[None, :]
    ang_c = col[:, None] * inv[None, :]
    cos = jnp.concatenate([jnp.cos(ang_r)] * 2 + [jnp.cos(ang_c)] * 2, axis=1)
    sin = jnp.concatenate([-jnp.sin(ang_r), jnp.sin(ang_r), -jnp.sin(ang_c), jnp.sin(ang_c)], axis=1)
    return jnp.tile(cos, (1, RET_HEADS)), jnp.tile(sin, (1, RET_HEADS))


def _pack_w_in(w):
    offs = [0, 256, 512, 768, 1024, 1280, 1408, 1536, 1792, 2048, 2080, 2336, 2592, 2848]
    seg = lambda i: w[:, :, offs[i]:offs[i + 1]]
    order = [0, 1, 2, 3, 4, 5, 6, 7, 8, 10, 11, 12, 9]
    packed = jnp.concatenate([seg(i) for i in order], axis=2)
    return jnp.pad(packed, ((0, 0), (0, 0), (0, W_IN_PACKED - packed.shape[2]))).astype(BF16)


def _gla_state_in(st):
    eye = jnp.eye(GLA_HEADS, dtype=st.dtype)
    t = jnp.einsum('bldhkv,hg->bldhvgk', st, eye)
    return t.reshape(st.shape[0], st.shape[1], 2, GLA_HEADS * GLA_DV, GLA_HEADS * GLA_DK)


def kernel(x_prompt, x_sample, c, cache_na_k, cache_na_v, state_ret, state_s5, state_gla, c_ctx, w_ada, b_ada, g_norm, w_in, ret_log_decay, ret_gn, s5_lambda_re, s5_lambda_im, s5_log_dt, s5_b_re, s5_b_im, s5_c_re, s5_c_im, s5_d, s5_w_glu, s5_b_glu, gla_w_gate, gla_b_gate, gla_gn, na_rpb, w_branch, w_merge, b_merge, w_out, w_mlp1, w_mlp2):
    depth = w_in.shape[0]
    x = jnp.concatenate([x_prompt.reshape(T_CTX, D_MODEL), x_sample.reshape(T_LAT, D_MODEL)], axis=0)
    cc = jnp.concatenate([c_ctx[None], c, jnp.zeros((N_MOD_ROWS - 1 - DEC_BATCH, D_MODEL), F32)], axis=0)
    mod = _ada(cc, w_ada, b_ada).reshape(depth, N_MOD_ROWS, 6, D_MODEL)

    cos, sin = _rope_tables()
    w_in_p = _pack_w_in(w_in)
    w_glu_b, w_br_b, w_mg_b, w_out_b = (a.astype(BF16) for a in (s5_w_glu, w_branch, w_merge, w_out))
    w1_b, w2_b = w_mlp1.astype(BF16), w_mlp2.astype(BF16)
    ret_gn3, gla_gn3, s5_d3 = (a.reshape(depth, 1, BRANCH_W) for a in (ret_gn, gla_gn, s5_d))
    b_glu3 = s5_b_glu.reshape(depth, 1, 2 * BRANCH_W)
    b_mg3 = b_merge.reshape(depth, 1, N_BRANCH * D_MODEL)
    cache_k = cache_na_k.transpose(0, 1, 3, 2, 4)
    cache_v = cache_na_v.transpose(0, 1, 3, 2, 4)
    na_tb = _na_tables(na_rpb)
    gla_s0 = _gla_state_in(state_gla)
    s5_te, s5_ff, s5_lam = _s5_tables(s5_lambda_re, s5_lambda_im, s5_log_dt, s5_b_re, s5_b_im,
                                      s5_c_re, s5_c_im)
    s5_x0 = state_s5.transpose(1, 2, 0, 3, 5, 4).reshape(depth, 2, DEC_BATCH, S5_GROUPS * S5_P2)

    ks_l, vs_l, ret_l, s5_l, gla_l = [], [], [], [], []
    for l in range(depth):
        ret, su, gqk, gv, gg, glr, nq, nk, nv = _inproj(l, x, mod, g_norm, w_in_p)

        ret_o, st_ret = _retention(l, ret, ret_log_decay, ret_gn3, latent=False)
        ret_o, = _retention(l, ret, ret_log_decay, ret_gn3, latent=True, cos=cos, sin=sin, s0=state_ret,
                            prev=ret_o)

        s5_y, s5_fin = _s5(l, su, s5_te, s5_ff, s5_lam, s5_x0)

        gla_o, st_gla = _gla(l, gqk, gv, gg, glr, gla_w_gate, gla_b_gate, gla_gn3, latent=False)
        gla_o, = _gla(l, gqk, gv, gg, glr, gla_w_gate, gla_b_gate, gla_gn3, latent=True, s0=gla_s0,
                      prev=gla_o)

        na_o = _attn_ctx(nq, nk, nv)
        na_o = _attn_lat(l, nq, nk, nv, cache_k, cache_v, na_tb, na_o)

        x = _merge(l, x, mod, g_norm, ret_o, s5_y, su, gla_o, na_o,
                   s5_d3, w_glu_b, b_glu3, w_br_b, w_mg_b, b_mg3, w_out_b)
        x = _mlp(l, x, mod, g_norm, w1_b, w2_b)

        ks_l.append(nk)
        vs_l.append(nv)
        ret_l.append(st_ret)
        s5_l.append(s5_fin)
        gla_l.append(st_gla)

    y_prompt = x[:T_CTX].reshape(BATCH, SEQ, D_MODEL)
    y_sample = x[T_CTX:].reshape(DEC_BATCH, DEC_SEQ, D_MODEL)

    def cache_out(per_layer):
        a = jnp.stack(per_layer, axis=0)[:, :, :T_CTX].reshape(depth, NA_HEADS, BATCH, SEQ, NA_DH)
        return a.transpose(2, 0, 3, 1, 4)

    s5_out = jnp.stack(s5_l, axis=0).reshape(depth, 2, BATCH, S5_GROUPS, 2, S5_STATE)
    gla_out = jnp.stack(gla_l, axis=1)
    gla_out = gla_out.reshape(BATCH, depth, 2, GLA_HEADS, GLA_DV, GLA_HEADS, GLA_DK)
    gla_out = jnp.stack([gla_out[:, :, :, h, :, h, :] for h in range(GLA_HEADS)], axis=3)
    return (y_prompt, y_sample, cache_out(ks_l), cache_out(vs_l), jnp.stack(ret_l, axis=1),
            s5_out.transpose(2, 0, 1, 3, 5, 4), gla_out.transpose(0, 1, 2, 3, 5, 4))
```

```python
import functools
import math

import numpy as np
import jax
import jax.numpy as jnp
from jax import lax
from jax.experimental import pallas as pl
from jax.experimental.pallas import tpu as pltpu

F32 = jnp.float32
BF16 = jnp.bfloat16

D_MODEL = 1024
BATCH = 16
SEQ = 256
DEPTH = 4
DEC_BATCH = 4
DEC_SEQ = 1024
PAST_LEN = 256
GRID_W = 64
N_BRANCH = 4
BRANCH_W = 256
RET_HEADS = 4
RET_DK = 64
RET_DV = 64
S5_GROUPS = 16
S5_GROUP_CH = 16
S5_STATE = 64
GLA_HEADS = 4
GLA_DK = 32
GLA_DV = 64
GLA_RANK = 16
GLA_TAU = 16.0
NA_HEADS = 4
NA_DH = 64
NA_WIN_H = 8
NA_WIN_W = 16
D_FF = 4 * D_MODEL
ROPE_BASE = 10000.0
EPS = 1e-6

T_CTX = BATCH * SEQ
T_LAT = DEC_BATCH * DEC_SEQ
T_ALL = T_CTX + T_LAT
LAT_BLOCK0 = T_CTX // DEC_SEQ
N_MOD_ROWS = 8
TOKEN_TILE = 512
GLA_CHUNK = 64
GLA_BLOCK_CHUNKS = 4
S5_CHUNK = 16
S5_RELAYOUT_ROWS = 32
RET_QBLOCK = 256
NA_QBLOCK = 256
GRID_ROWS = DEC_SEQ // GRID_W
NA_KH = min(NA_WIN_H, GRID_ROWS)
NA_REL_ROWS = 2 * NA_WIN_H - 1
VMEM_LIMIT = 56 * 1024 * 1024
W_IN_PACKED = 2944


def _cparams(sem):
    return pltpu.CompilerParams(dimension_semantics=sem, vmem_limit_bytes=VMEM_LIMIT)


def _bdot(a, b):
    return jnp.dot(a.astype(BF16), b.astype(BF16), preferred_element_type=F32)


def _bdot_nt(a, b):
    return lax.dot_general(a.astype(BF16), b.astype(BF16), (((1,), (1,)), ((), ())),
                           preferred_element_type=F32)


def _bdot_tn(a, b):
    return lax.dot_general(a.astype(BF16), b.astype(BF16), (((0,), (0,)), ((), ())),
                           preferred_element_type=F32)


def _split(a):
    hi = a.astype(BF16)
    lo = (a - hi.astype(F32)).astype(BF16)
    return hi, lo


def _dot3(a, b):
    ah, al = _split(a)
    bh, bl = _split(b)
    d = functools.partial(jnp.dot, preferred_element_type=F32)
    return d(ah, bh) + d(al, bh) + d(ah, bl)


def _sigmoid(x):
    return 1.0 / (1.0 + jnp.exp(-x))


def _silu(x):
    return x * _sigmoid(x)


def _rms(x, g):
    return x * lax.rsqrt(jnp.mean(x * x, axis=-1, keepdims=True) + EPS) * g


def _group_norm(o, g):
    mu = jnp.mean(o, axis=-1, keepdims=True)
    xc = o - mu
    return xc * lax.rsqrt(jnp.mean(xc * xc, axis=-1, keepdims=True) + EPS) * g


def _mod_row(i):
    ctx_tiles = T_CTX // TOKEN_TILE
    return jnp.where(i < ctx_tiles, 0, 1 + (i - ctx_tiles) // (DEC_SEQ // TOKEN_TILE))


def _mod_spec(l):
    return pl.BlockSpec((1, 1, 6, D_MODEL), lambda i: (l, _mod_row(i), 0, 0))


def _layer_spec(l, *shape):
    return pl.BlockSpec((1,) + shape, lambda *_: (l,) + (0,) * len(shape))


_ANY = pl.BlockSpec(memory_space=pl.ANY)


ADA_TILE = 1536


def _ada_kernel(c_ref, w_ref, b_ref, o_ref):
    a = _silu(c_ref[...])
    o_ref[0] = _bdot(a, w_ref[0]) + b_ref[0]


def _ada(cc, w_ada, b_ada):
    n = 6 * D_MODEL
    return pl.pallas_call(
        _ada_kernel,
        grid=(DEPTH, n // ADA_TILE),
        in_specs=[pl.BlockSpec((N_MOD_ROWS, D_MODEL), lambda l, j: (0, 0)),
                  pl.BlockSpec((1, D_MODEL, ADA_TILE), lambda l, j: (l, 0, j)),
                  pl.BlockSpec((1, 1, ADA_TILE), lambda l, j: (l, 0, j))],
        out_specs=pl.BlockSpec((1, N_MOD_ROWS, ADA_TILE), lambda l, j: (l, 0, j)),
        out_shape=jax.ShapeDtypeStruct((DEPTH, N_MOD_ROWS, n), F32),
        compiler_params=_cparams(("parallel", "parallel")),
        name="ada_mod",
    )(cc, w_ada, b_ada.reshape(DEPTH, 1, n))


def _inproj_kernel(x_ref, mod_ref, g_ref, w_ref,
                   ret_ref, s5_ref, gqk_ref, gv_ref, gg_ref, glr_ref, nq_ref, nk_ref, nv_ref):
    mod = mod_ref[0, 0]
    h = _rms(x_ref[...], g_ref[0, 0:1]) * (1.0 + mod[1:2]) + mod[0:1]
    hb = h.astype(BF16)

    def proj(lo, hi):
        return jnp.dot(hb, w_ref[0, :, lo:hi], preferred_element_type=F32)

    ret_ref[...] = proj(0, 1024)
    s5_ref[...] = proj(1024, 1280)
    gqk_ref[...] = proj(1280, 1536)
    gv_ref[...] = proj(1536, 1792)
    gg_ref[...] = proj(1792, 2048)
    for ref, lo in ((nq_ref, 2048), (nk_ref, 2304), (nv_ref, 2560)):
        r = proj(lo, lo + 256)
        for hh in range(NA_HEADS):
            ref[hh] = r[:, hh * NA_DH:(hh + 1) * NA_DH]
    glr_ref[...] = proj(2816, 2944)


def _inproj(l, x, mod, g_norm, w_in_p):
    tm = TOKEN_TILE
    tok = lambda w: pl.BlockSpec((tm, w), lambda i: (i, 0))
    head = pl.BlockSpec((NA_HEADS, tm, NA_DH), lambda i: (0, i, 0))
    tshape = lambda w: jax.ShapeDtypeStruct((T_ALL, w), F32)
    hshape = jax.ShapeDtypeStruct((NA_HEADS, T_ALL, NA_DH), F32)
    return pl.pallas_call(
        _inproj_kernel,
        grid=(T_ALL // tm,),
        in_specs=[tok(D_MODEL), _mod_spec(l), _layer_spec(l, 4, D_MODEL),
                  _layer_spec(l, D_MODEL, W_IN_PACKED)],
        out_specs=[tok(1024), tok(256), tok(256), tok(256), tok(256), tok(128), head, head, head],
        out_shape=[tshape(1024), tshape(256), tshape(256), tshape(256), tshape(256), tshape(128),
                   hshape, hshape, hshape],
        compiler_params=_cparams(("parallel",)),
        name="in_proj",
    )(x, mod, g_norm, w_in_p)


def _rope_rotate(x, lane):
    first = (lane % 32) < 16
    w = x.shape[-1]
    return jnp.where(first, pltpu.roll(x, w - 16, 1), pltpu.roll(x, 16, 1))


def _ret_kernel(ld_ref, ret_ref, gn_ref, *rest, layer, seq, latent):
    if latent:
        cos_ref, sin_ref, s0_ref, _, out_ref, dec_s = rest
    else:
        out_ref, st_ref, dec_s = rest
    tq = RET_QBLOCK
    nq = seq // tq
    width = dec_s.shape[-1]

    @pl.when(pl.program_id(0) == 0)
    def _build_decay():
        rel = (lax.broadcasted_iota(jnp.int32, (tq, width), 0) + (nq - 1) * tq
               - lax.broadcasted_iota(jnp.int32, (tq, width), 1)).astype(F32)
        for h in range(RET_HEADS):
            dec_s[h] = (jnp.where(rel >= 0, jnp.exp(ld_ref[layer, 0, h] * jnp.maximum(rel, 0.0)), 0.0)
                        + jnp.where(rel <= 0, jnp.exp(ld_ref[layer, 1, h] * jnp.maximum(-rel, 0.0)), 0.0))

    q = ret_ref[:, 0:256]
    k = ret_ref[:, 256:512]
    if latent:
        lane = lax.broadcasted_iota(jnp.int32, (seq, 256), 1)
        cos = cos_ref[...]
        sin = sin_ref[...]
        q = q * cos + _rope_rotate(q, lane) * sin
        k = k * cos + _rope_rotate(k, lane) * sin
    k = k * (RET_DK ** -0.5)
    pos_c = lax.broadcasted_iota(jnp.int32, (seq, 1), 0).astype(F32)
    for h in range(RET_HEADS):
        lgf = ld_ref[layer, 0, h]
        lgb = ld_ref[layer, 1, h]
        sl = slice(h * RET_DK, (h + 1) * RET_DK)
        qh = q[:, sl]
        kh = k[:, sl]
        vh = ret_ref[:, 512 + h * RET_DV:512 + (h + 1) * RET_DV]
        kb = kh.astype(BF16)
        vb = vh.astype(BF16)
        if latent:
            q_init = jnp.concatenate([qh * jnp.exp(lgf * (pos_c + 1.0)),
                                      qh * jnp.exp(lgb * (seq - pos_c))], axis=1)
            s_init = jnp.concatenate([s0_ref[0, 0, 0, h], s0_ref[0, 0, 1, h]], axis=0)
        for qb in range(nq):
            w0 = (nq - 1 - qb) * tq
            rows = slice(qb * tq, (qb + 1) * tq)
            s = _bdot_nt(qh[rows], kb) * dec_s[h, :, w0:w0 + seq]
            o = _bdot(s, vb)
            if latent:
                o = o + _bdot(q_init[rows], s_init)
            g = ret_ref[rows, 768 + h * RET_DV:768 + (h + 1) * RET_DV]
            out_ref[rows, sl] = _group_norm(o, gn_ref[0, :, sl]) * _silu(g)
        if not latent:
            st_ref[0, 0, h] = _bdot_tn(kh * jnp.exp(lgf * (seq - 1.0 - pos_c)), vb)
            st_ref[0, 1, h] = _bdot_tn(kh * jnp.exp(lgb * pos_c), vb)


def _retention(l, ret, ld, gn, *, latent, cos=None, sin=None, s0=None, prev=None):
    seq = DEC_SEQ if latent else SEQ
    nb = DEC_BATCH if latent else BATCH
    off = LAT_BLOCK0 if latent else 0
    in_specs = [pl.BlockSpec(memory_space=pltpu.SMEM),
                pl.BlockSpec((seq, 1024), lambda b: (b + off, 0)),
                _layer_spec(l, 1, 256)]
    args = [ld, ret, gn]
    out_specs = [pl.BlockSpec((seq, 256), lambda b: (b + off, 0))]
    out_shape = [jax.ShapeDtypeStruct((T_ALL, 256), F32)]
    aliases = {}
    if latent:
        in_specs += [pl.BlockSpec((seq, 256), lambda b: (0, 0)),
                     pl.BlockSpec((seq, 256), lambda b: (0, 0)),
                     pl.BlockSpec((1, 1, 2, RET_HEADS, RET_DK, RET_DV), lambda b: (b, l, 0, 0, 0, 0)),
                     _ANY]
        args += [cos, sin, s0, prev]
        aliases = {6: 0}
    else:
        out_specs.append(pl.BlockSpec((1, 2, RET_HEADS, RET_DK, RET_DV), lambda b: (b, 0, 0, 0, 0)))
        out_shape.append(jax.ShapeDtypeStruct((nb, 2, RET_HEADS, RET_DK, RET_DV), F32))
    return pl.pallas_call(
        functools.partial(_ret_kernel, layer=l, seq=seq, latent=latent),
        grid=(nb,),
        in_specs=in_specs, out_specs=out_specs, out_shape=out_shape,
        input_output_aliases=aliases,
        scratch_shapes=[pltpu.VMEM((RET_HEADS, RET_QBLOCK, 2 * seq - RET_QBLOCK), F32)],
        compiler_params=_cparams(("arbitrary",)),
        name="retention_lat" if latent else "retention_ctx",
    )(*args)


def _gla_kernel(gqk_ref, gv_ref, gg_ref, glr_ref, wg_ref, bg_ref, gn_ref, *rest, seq, latent):
    if latent:
        s0_ref, _, out_ref, gate_s, o_s, st_s, qst_s, ds_s, e_s = rest
    else:
        out_ref, st_ref, gate_s, o_s, st_s, qst_s, ds_s, e_s = rest
    c = GLA_CHUNK
    n = seq // c
    hk = GLA_HEADS * GLA_DK
    lr = glr_ref[...]
    for d in range(2):
        pre = _bdot(lr[:, d * GLA_RANK:(d + 1) * GLA_RANK], wg_ref[0, d]) + bg_ref[0, d:d + 1]
        gate_s[d] = (jnp.minimum(pre, 0.0) - jnp.log(1.0 + jnp.exp(-jnp.abs(pre)))) / GLA_TAU
        st_s[d] = s0_ref[0, 0, d] if latent else jnp.zeros((GLA_HEADS * GLA_DV, hk), F32)

    nc = GLA_BLOCK_CHUNKS
    rb = nc * c
    ti = lax.broadcasted_iota(jnp.int32, (rb, rb), 0)
    tj = lax.broadcasted_iota(jnp.int32, (rb, rb), 1)
    same = (ti // c) == (tj // c)
    ones = lambda m: (same & m).astype(BF16)
    tri = [ones(tj <= ti), ones(tj >= ti)]
    mid = [ones((tj % c) < c // 2), ones((tj % c) >= c // 2)]
    tot = ones(tj == tj)
    lane_k = lax.broadcasted_iota(jnp.int32, (c, hk), 1)
    head_mask = [(lane_k // GLA_DK) == h for h in range(GLA_HEADS)]
    ai = lax.broadcasted_iota(jnp.int32, (GLA_HEADS * c, c), 0) % c
    aj = lax.broadcasted_iota(jnp.int32, (GLA_HEADS * c, c), 1)
    keep = [aj <= ai, aj >= ai]
    sr = lax.broadcasted_iota(jnp.int32, (GLA_HEADS * GLA_DV, hk), 0) // GLA_DV
    sc = lax.broadcasted_iota(jnp.int32, (GLA_HEADS * GLA_DV, hk), 1) // GLA_DK
    diag = sr == sc
    scale = GLA_DK ** -0.5
    d32 = functools.partial(jnp.dot, preferred_element_type=F32)

    def rows_of(i, size):
        return pl.ds(i * size, size) if isinstance(i, int) else pl.ds(pl.multiple_of(i * size, size), size)

    def local(bi):
        rows = rows_of(bi, rb)
        q = gqk_ref[rows, 0:hk] * scale
        k = gqk_ref[rows, hk:2 * hk]
        v = gv_ref[rows, :].astype(BF16)
        q_att, k_att, k_st = [], [], []
        for d in range(2):
            gh, gl = _split(gate_s[d, rows, :])
            b = d32(tri[d], gh) + d32(tri[d], gl)
            b_mid = d32(mid[d], gh) + d32(mid[d], gl)
            b_end = d32(tot, gh) + d32(tot, gl)
            q_att.append(q * jnp.exp(b - b_mid))
            k_att.append(k * jnp.exp(b_mid - b))
            k_st.append((k * jnp.exp(b_end - b)).astype(BF16))
            qst_s[d, rows, :] = (q * jnp.exp(b)).astype(BF16)
            decay = jnp.exp(b_end)
            for cc in range(nc):
                e_s[d, bi * nc + cc] = decay[cc * c:cc * c + 8]
        pairs = [(d, cc) for d in range(2) for cc in range(nc)]
        att = {}
        for d, cc in pairs:
            r = slice(cc * c, (cc + 1) * c)
            qa = q_att[d][r]
            q_stack = jnp.concatenate([jnp.where(head_mask[h], qa, 0.0) for h in range(GLA_HEADS)], axis=0)
            att[d, cc] = jnp.where(keep[d], _bdot_nt(q_stack, k_att[d][r]), 0.0).astype(BF16)
        for d, cc in pairs:
            r = slice(cc * c, (cc + 1) * c)
            o = jnp.concatenate([d32(att[d, cc][h * c:(h + 1) * c], v[r, h * GLA_DV:(h + 1) * GLA_DV])
                                 for h in range(GLA_HEADS)], axis=1)
            o_s[d, rows_of(bi * nc + cc, c), :] = o
        for d, cc in pairs:
            r = slice(cc * c, (cc + 1) * c)
            ds_s[d, bi * nc + cc] = jnp.where(diag, lax.dot_general(
                v[r], k_st[d][r], (((0,), (0,)), ((), ())), preferred_element_type=F32), 0.0)

    def recur(ci, d):
        rows = rows_of(ci, c)
        st = st_s[d]
        o_s[d, rows, :] += _bdot_nt(qst_s[d, rows, :], st)
        st_s[d] = st * e_s[d, ci, 0:1] + ds_s[d, ci]

    def recur_body(i, carry):
        recur(i, 0)
        recur(n - 1 - i, 1)
        return carry

    if n == nc:
        local(0)
        for i in range(n):
            recur_body(i, 0)
    else:
        lax.fori_loop(0, n // nc, lambda i, carry: (local(i), carry)[1], 0)
        lax.fori_loop(0, n, recur_body, 0, unroll=2)

    o = o_s[0] + o_s[1]
    for h in range(GLA_HEADS):
        sl = slice(h * GLA_DV, (h + 1) * GLA_DV)
        out_ref[:, sl] = _group_norm(o[:, sl], gn_ref[0, :, sl]) * _silu(gg_ref[:, sl])
    if not latent:
        st_ref[0, 0] = st_s[0]
        st_ref[0, 1] = st_s[1]


def _gla(l, gqk, gv, gg, glr, wg, bg, gn, *, latent, s0=None, prev=None):
    seq = DEC_SEQ if latent else SEQ
    nb = DEC_BATCH if latent else BATCH
    off = LAT_BLOCK0 if latent else 0
    hk = GLA_HEADS * GLA_DK
    hv = GLA_HEADS * GLA_DV
    tok = lambda w: pl.BlockSpec((seq, w), lambda b: (b + off, 0))
    in_specs = [tok(256), tok(256), tok(256), tok(128),
                _layer_spec(l, 2, GLA_RANK, hk), _layer_spec(l, 2, hk), _layer_spec(l, 1, 256)]
    args = [gqk, gv, gg, glr, wg, bg, gn]
    out_specs = [tok(256)]
    out_shape = [jax.ShapeDtypeStruct((T_ALL, 256), F32)]
    aliases = {}
    if latent:
        in_specs += [pl.BlockSpec((1, 1, 2, hv, hk), lambda b: (b, l, 0, 0, 0)), _ANY]
        args += [s0, prev]
        aliases = {8: 0}
    else:
        out_specs.append(pl.BlockSpec((1, 2, hv, hk), lambda b: (b, 0, 0, 0)))
        out_shape.append(jax.ShapeDtypeStruct((nb, 2, hv, hk), F32))
    return pl.pallas_call(
        functools.partial(_gla_kernel, seq=seq, latent=latent),
        grid=(nb,),
        in_specs=in_specs, out_specs=out_specs, out_shape=out_shape,
        input_output_aliases=aliases,
        scratch_shapes=[pltpu.VMEM((2, seq, hk), F32), pltpu.VMEM((2, seq, hv), F32),
                        pltpu.VMEM((2, hv, hk), F32), pltpu.VMEM((2, seq, hk), BF16),
                        pltpu.VMEM((2, seq // GLA_CHUNK, hv, hk), F32),
                        pltpu.VMEM((2, seq // GLA_CHUNK, 8, hk), F32)],
        compiler_params=_cparams(("parallel",)),
        name="gla_lat" if latent else "gla_ctx",
    )(*args)


S5_W = S5_CHUNK * S5_GROUP_CH
S5_P2 = 2 * S5_STATE
S5_ROWS = T_ALL // S5_CHUNK
S5_ROWS_CTX = T_CTX // S5_CHUNK
S5_LANE_GROUPS = 128 // S5_GROUP_CH


S5_TE_ROWS = S5_W + 4 * S5_P2


def _s5_toeplitz_kernel(cc_ref, wfr_ref, wb_ref, ff_ref, te_hi_ref, te_lo_ref, ff_hi_ref, ff_lo_ref):
    def put(ref_hi, ref_lo, rows, val):
        hi, lo = _split(val)
        ref_hi[0, rows, :] = hi
        ref_lo[0, rows, :] = lo

    wfr = wfr_ref[0]
    wb = wb_ref[0]
    kf = _dot3(cc_ref[0, 0], wfr)
    kb = _dot3(cc_ref[0, 1], wb)
    lane = lax.broadcasted_iota(jnp.int32, (S5_GROUP_CH, S5_W), 1)
    blocks = []
    for i in range(S5_CHUNK):
        sf = ((i + 1 - S5_CHUNK) * S5_GROUP_CH) % S5_W
        fwd = jnp.where(lane < (i + 1) * S5_GROUP_CH, pltpu.roll(kf, sf, 1) if sf else kf, 0.0)
        bwd = jnp.where(lane >= i * S5_GROUP_CH, pltpu.roll(kb, i * S5_GROUP_CH, 1) if i else kb, 0.0)
        blocks.append(fwd + bwd)
    swapped = lambda t: jnp.concatenate([t[S5_STATE:], t[:S5_STATE]], axis=0)
    put(te_hi_ref, te_lo_ref, slice(0, S5_TE_ROWS),
        jnp.concatenate(blocks + [wfr, wb, swapped(wfr), swapped(wb)], axis=0))
    put(ff_hi_ref, ff_lo_ref, slice(0, 2 * S5_P2), ff_ref[0])


def _s5_toeplitz(cc, wfr, wb, ff):
    n = cc.shape[0]
    tbl = pl.BlockSpec((1, S5_P2, S5_W), lambda i: (i, 0, 0))
    te_spec = pl.BlockSpec((1, S5_TE_ROWS, S5_W), lambda i: (i, 0, 0))
    ff_spec = pl.BlockSpec((1, 2 * S5_P2, S5_W), lambda i: (i, 0, 0))
    te_shape = jax.ShapeDtypeStruct((n, S5_TE_ROWS, S5_W), BF16)
    ff_shape = jax.ShapeDtypeStruct((n, 2 * S5_P2, S5_W), BF16)
    return pl.pallas_call(
        _s5_toeplitz_kernel,
        grid=(n,),
        in_specs=[pl.BlockSpec((1, 2, S5_GROUP_CH, S5_P2), lambda i: (i, 0, 0, 0)), tbl, tbl, ff_spec],
        out_specs=[te_spec, te_spec, ff_spec, ff_spec],
        out_shape=[te_shape, te_shape, ff_shape, ff_shape],
        compiler_params=_cparams(("parallel",)),
        name="s5_toeplitz",
    )(cc, wfr, wb, ff)


def _piece_transpose(blocks, piece):
    x = list(blocks)
    n = len(x)
    d = n // 2
    while d >= 1:
        low = (piece & d) == 0
        for v in range(n):
            if v & d:
                continue
            a, b = x[v], x[v + d]
            x[v] = jnp.where(low, a, pltpu.roll(b, d * S5_GROUP_CH, 1))
            x[v + d] = jnp.where(low, pltpu.roll(a, 128 - d * S5_GROUP_CH, 1), b)
        d //= 2
    return x


def _s5_kernel(u_ref, te_hi_ref, te_lo_ref, ff_hi_ref, ff_lo_ref, lam_ref, x0_ref, y_ref, fin_ref,
               ug_s, xs_s, ps_s):
    C = S5_CHUNK
    ng = S5_LANE_GROUPS
    rc_rows = S5_RELAYOUT_ROWS
    piece = lax.broadcasted_iota(jnp.int32, (rc_rows, 128), 1) // S5_GROUP_CH

    def gather_body(rc, carry):
        r0 = pl.multiple_of(rc * rc_rows, rc_rows)
        for m in range(C // ng):
            out = _piece_transpose(
                [u_ref[pl.ds(r0 * C + m * ng + jl, rc_rows, stride=C), :] for jl in range(ng)], piece)
            for g in range(ng):
                col = g * S5_W + m * 128
                ug_s[pl.ds(r0, rc_rows), col:col + 128] = out[g]
        return carry

    lax.fori_loop(0, S5_ROWS // rc_rows, gather_body, 0)

    nt = lambda a, b: lax.dot_general(a, b, (((1,), (1,)), ((), ())), preferred_element_type=F32)
    for g in range(ng):
        uh, ul = _split(ug_s[:, g * S5_W:(g + 1) * S5_W])
        r = nt(uh, te_hi_ref[0, g]) + nt(ul, te_hi_ref[0, g]) + nt(uh, te_lo_ref[0, g])
        ug_s[:, g * S5_W:(g + 1) * S5_W] = r[:, 0:S5_W]
        for t in range(4):
            xs_s[t * ng + g] = r[:, S5_W + t * S5_P2:S5_W + (t + 1) * S5_P2]

    w = ng * S5_P2

    def carry(base, nseq, nchunks, init_f, init_b):
        a_f, b_f = lam_ref[0, 0, 0:1], lam_ref[0, 0, 1:2]
        a_b, b_b = lam_ref[0, 1, 0:1], lam_ref[0, 1, 1:2]
        load = lambda t, rows: jnp.concatenate([xs_s[t * ng + g, rows, :] for g in range(ng)], axis=1)

        def body(i, st):
            sf, tf, sb, tb = st
            rf = pl.ds(base + i, nseq, stride=nchunks)
            rb = pl.ds(base + (nchunks - 1 - i), nseq, stride=nchunks)
            for g in range(ng):
                ps_s[g, rf, :] = sf[:, g * S5_P2:(g + 1) * S5_P2]
                ps_s[ng + g, rb, :] = sb[:, g * S5_P2:(g + 1) * S5_P2]
            return (a_f * sf + b_f * tf + load(0, rf), a_f * tf - b_f * sf + load(2, rf),
                    a_b * sb + b_b * tb + load(1, rb), a_b * tb - b_b * sb + load(3, rb))

        first = (lax.broadcasted_iota(jnp.int32, (1, w), 1) % S5_P2) < S5_STATE
        swap = lambda s: jnp.where(first, pltpu.roll(s, w - S5_STATE, 1), pltpu.roll(s, S5_STATE, 1))
        return lax.fori_loop(0, nchunks, body, (init_f, swap(init_f), init_b, swap(init_b)))

    zeros = jnp.zeros((BATCH, w), F32)
    fin = carry(0, BATCH, SEQ // C, zeros, zeros)
    fin_ref[0] = fin[0]
    fin_ref[1] = fin[2]
    carry(S5_ROWS_CTX, DEC_BATCH, DEC_SEQ // C, x0_ref[0, 0], x0_ref[0, 1])

    for g in range(ng):
        ph, pl_ = _split(jnp.concatenate([ps_s[g], ps_s[ng + g]], axis=1))
        d = functools.partial(jnp.dot, preferred_element_type=F32)
        ug_s[:, g * S5_W:(g + 1) * S5_W] += (d(ph, ff_hi_ref[0, g]) + d(pl_, ff_hi_ref[0, g])
                                             + d(ph, ff_lo_ref[0, g]))

    def scatter_body(rc, carry):
        r0 = pl.multiple_of(rc * rc_rows, rc_rows)
        for m in range(C // ng):
            out = _piece_transpose(
                [ug_s[pl.ds(r0, rc_rows), g * S5_W + m * 128:g * S5_W + (m + 1) * 128] for g in range(ng)],
                piece)
            for il in range(ng):
                y_ref[pl.ds(r0 * C + m * ng + il, rc_rows, stride=C), :] = out[il]
        return carry

    lax.fori_loop(0, S5_ROWS // rc_rows, scatter_body, 0)


def _s5(l, su, te_hi, te_lo, ff_hi, ff_lo, lam, x0):
    ng = S5_LANE_GROUPS
    w = ng * S5_P2
    te_spec = pl.BlockSpec((1, ng, S5_TE_ROWS, S5_W), lambda i: (l, i, 0, 0))
    ff_spec = pl.BlockSpec((1, ng, 2 * S5_P2, S5_W), lambda i: (l, i, 0, 0))
    return pl.pallas_call(
        _s5_kernel,
        grid=(S5_GROUPS // ng,),
        in_specs=[pl.BlockSpec((T_ALL, 128), lambda i: (0, i)), te_spec, te_spec, ff_spec, ff_spec,
                  pl.BlockSpec((1, 2, 2, w), lambda i: (l, 0, 0, i)),
                  pl.BlockSpec((1, 2, DEC_BATCH, w), lambda i: (l, 0, 0, i))],
        out_specs=[pl.BlockSpec((T_ALL, 128), lambda i: (0, i)),
                   pl.BlockSpec((2, BATCH, w), lambda i: (0, 0, i))],
        out_shape=[jax.ShapeDtypeStruct((T_ALL, S5_GROUPS * S5_GROUP_CH), F32),
                   jax.ShapeDtypeStruct((2, BATCH, S5_GROUPS * S5_P2), F32)],
        scratch_shapes=[pltpu.VMEM((S5_ROWS, ng * S5_W), F32),
                        pltpu.VMEM((4 * ng, S5_ROWS, S5_P2), F32), pltpu.VMEM((2 * ng, S5_ROWS, S5_P2), F32)],
        compiler_params=_cparams(("parallel",)),
        name="s5_scan",
    )(su, te_hi, te_lo, ff_hi, ff_lo, lam, x0)


def _s5_tables(lam_re, lam_im, log_dt, b_re, b_im, c_re, c_im):
    C, G, P, H = S5_CHUNK, S5_GROUPS, S5_STATE, S5_GROUP_CH
    L = lam_re.shape[0]
    dt = jnp.exp(log_dt)[..., None]
    ar, ai = lam_re * dt, lam_im * dt

    def power(d, t):
        mag = jnp.exp(ar[:, d, :, :, None] * t)
        ang = ai[:, d, :, :, None] * t
        return mag * jnp.cos(ang), mag * jnp.sin(ang)

    lr, li = jnp.exp(ar) * jnp.cos(ai), jnp.exp(ar) * jnp.sin(ai)
    den = lam_re * lam_re + lam_im * lam_im
    qr = ((lr - 1.0) * lam_re + li * lam_im) / den
    qi = (li * lam_re - (lr - 1.0) * lam_im) / den
    bbr = qr[..., None] * b_re - qi[..., None] * b_im
    bbi = qr[..., None] * b_im + qi[..., None] * b_re
    lanes = lambda a: jnp.tile(a, (1, 1, 1, 1, C))
    bbr, bbi = lanes(bbr), lanes(bbi)
    c_t = lambda a: lanes(a.transpose(0, 1, 2, 4, 3))
    ctr, cti = c_t(c_re), c_t(c_im)
    tau = (jnp.arange(C * H) // H).astype(F32)

    def w_of(d, t):
        pr, pi = power(d, t)
        return jnp.concatenate([pr * bbr[:, d] - pi * bbi[:, d], pr * bbi[:, d] + pi * bbr[:, d]], axis=-2)

    def f_of(d, t):
        pr, pi = power(d, t)
        return jnp.concatenate([ctr[:, d] * pr - cti[:, d] * pi, -(ctr[:, d] * pi + cti[:, d] * pr)], axis=-2)

    wfr = w_of(0, (C - 1.0) - tau).reshape(L * G, 2 * P, C * H)
    wb = w_of(1, tau).reshape(L * G, 2 * P, C * H)
    cc = jnp.concatenate([c_re, -c_im], axis=-1).transpose(0, 2, 1, 3, 4).reshape(L * G, 2, H, 2 * P)
    ff = jnp.concatenate([f_of(0, tau + 1.0), f_of(1, C - tau)], axis=-2)
    tables = _s5_toeplitz(cc, wfr, wb, ff.reshape(L * G, 4 * P, C * H))
    tables = [t.reshape((L, G) + t.shape[1:]) for t in tables]
    cr, ci = jnp.exp(ar * C) * jnp.cos(ai * C), jnp.exp(ar * C) * jnp.sin(ai * C)
    a = jnp.concatenate([cr, cr], axis=-1).reshape(L, 2, 1, G * 2 * P)
    b = jnp.concatenate([-ci, ci], axis=-1).reshape(L, 2, 1, G * 2 * P)
    return tables, jnp.concatenate([a, b], axis=2)


def _softmax_pv(s_parts, v_parts):
    m = s_parts[0].max(axis=-1, keepdims=True)
    for s in s_parts[1:]:
        m = jnp.maximum(m, s.max(axis=-1, keepdims=True))
    o = None
    l = None
    for s, v in zip(s_parts, v_parts):
        p = jnp.exp(s - m)
        pl_ = p.sum(axis=-1, keepdims=True)
        po = _bdot(p, v)
        o = po if o is None else o + po
        l = pl_ if l is None else l + pl_
    return o / l


def _attn_ctx_kernel(q_ref, k_ref, v_ref, o_ref):
    scale = NA_DH ** -0.5
    for h in range(NA_HEADS):
        s = _bdot_nt(q_ref[h], k_ref[h]) * scale
        o_ref[h] = _softmax_pv([s], [v_ref[h]])


def _attn_ctx(nq, nk, nv):
    spec = pl.BlockSpec((NA_HEADS, SEQ, NA_DH), lambda b: (0, b, 0))
    return pl.pallas_call(
        _attn_ctx_kernel,
        grid=(BATCH,),
        in_specs=[spec, spec, spec],
        out_specs=spec,
        out_shape=jax.ShapeDtypeStruct((NA_HEADS, T_ALL, NA_DH), F32),
        compiler_params=_cparams(("parallel",)),
        name="attn_ctx",
    )(nq, nk, nv)


def _attn_lat_kernel(q_ref, k_ref, v_ref, kc_ref, vc_ref, tb_ref, _, o_ref, bias_s):
    @pl.when(pl.program_id(1) == 0)
    def _build_bias():
        bias_s[...] = jnp.full((DEC_SEQ, DEC_SEQ), -jnp.inf, F32)
        for r in range(GRID_ROWS):
            rs = min(max(r - NA_KH // 2, 0), GRID_ROWS - NA_KH)
            dr0 = rs - r + NA_WIN_H - 1
            bias_s[r * GRID_W:(r + 1) * GRID_W, rs * GRID_W:(rs + NA_KH) * GRID_W] = (
                tb_ref[0, 0, :, dr0 * GRID_W:(dr0 + NA_KH) * GRID_W])

    scale = NA_DH ** -0.5
    kb = k_ref[0].astype(BF16)
    vb = v_ref[0].astype(BF16)
    kc = kc_ref[0, 0, 0].astype(BF16)
    vc = vc_ref[0, 0, 0].astype(BF16)
    tq = NA_QBLOCK
    for qb in range(DEC_SEQ // tq):
        rows = slice(qb * tq, (qb + 1) * tq)
        qh = q_ref[0, rows, :].astype(BF16)
        s_loc = _bdot_nt(qh, kb) * scale + bias_s[rows, :]
        s_ctx = _bdot_nt(qh, kc) * scale
        o_ref[0, rows, :] = _softmax_pv([s_loc, s_ctx], [vb, vc])


def _attn_lat(l, nq, nk, nv, kc, vc, tb, prev):
    tok = pl.BlockSpec((1, DEC_SEQ, NA_DH), lambda h, b: (h, b + LAT_BLOCK0, 0))
    cache = pl.BlockSpec((1, 1, 1, PAST_LEN, NA_DH), lambda h, b: (b, l, h, 0, 0))
    return pl.pallas_call(
        _attn_lat_kernel,
        grid=(NA_HEADS, DEC_BATCH),
        in_specs=[tok, tok, tok, cache, cache,
                  pl.BlockSpec((1, 1, GRID_W, NA_REL_ROWS * GRID_W), lambda h, b: (l, h, 0, 0)),
                  _ANY],
        out_specs=tok,
        out_shape=jax.ShapeDtypeStruct((NA_HEADS, T_ALL, NA_DH), F32),
        input_output_aliases={6: 0},
        scratch_shapes=[pltpu.VMEM((DEC_SEQ, DEC_SEQ), F32)],
        compiler_params=_cparams(("arbitrary", "arbitrary")),
        name="attn_lat",
    )(nq, nk, nv, kc, vc, tb, prev)


def _na_tables(rpb):
    col = np.arange(GRID_W)
    col_start = np.clip(col - NA_WIN_W // 2, 0, GRID_W - NA_WIN_W)
    col_in = (col[None, :] >= col_start[:, None]) & (col[None, :] < col_start[:, None] + NA_WIN_W)
    col_idx = np.clip(col[None, :] - col[:, None] + NA_WIN_W - 1, 0, 2 * NA_WIN_W - 2)
    onehot = (col_idx[:, :, None] == np.arange(2 * NA_WIN_W - 1)[None, None, :]).astype(np.float32)
    tb = jnp.einsum('lhrd,qkd->lhqrk', rpb, jnp.asarray(onehot), precision=lax.Precision.HIGHEST)
    tb = jnp.where(jnp.asarray(col_in)[None, None, :, None, :], tb, -jnp.inf)
    return tb.reshape(rpb.shape[0], NA_HEADS, GRID_W, NA_REL_ROWS * GRID_W)


def _merge_kernel(x_ref, mod_ref, g_ref, ret_ref, s5y_ref, s5u_ref, gla_ref, na_ref,
                  s5d_ref, wglu_ref, bglu_ref, wbr_ref, wmg_ref, bmg_ref, wout_ref, o_ref):
    x = x_ref[...]
    mod = mod_ref[0, 0]
    hb = (_rms(x, g_ref[0, 0:1]) * (1.0 + mod[1:2]) + mod[0:1]).astype(BF16)

    y = s5y_ref[...] + s5d_ref[0] * s5u_ref[...]
    y = 0.5 * y * (1.0 + jnp.tanh(math.sqrt(2.0 / math.pi) * (y + 0.044715 * (y * y * y))))
    z = _bdot(y, wglu_ref[0]) + bglu_ref[0]
    s5_out = z[:, 0:BRANCH_W] * _sigmoid(z[:, BRANCH_W:2 * BRANCH_W])

    def gate(n):
        return _sigmoid(jnp.dot(hb, wmg_ref[0, :, n * D_MODEL:(n + 1) * D_MODEL], preferred_element_type=F32)
                        + bmg_ref[0, :, n * D_MODEL:(n + 1) * D_MODEL])

    acc = gate(0) * _bdot(ret_ref[...], wbr_ref[0, 0])
    acc += gate(1) * _bdot(s5_out, wbr_ref[0, 1])
    acc += gate(2) * _bdot(gla_ref[...], wbr_ref[0, 2])
    up = _bdot(na_ref[0], wbr_ref[0, 3, 0:NA_DH, :])
    for hh in range(1, NA_HEADS):
        up += _bdot(na_ref[hh], wbr_ref[0, 3, hh * NA_DH:(hh + 1) * NA_DH, :])
    acc += gate(3) * up
    m = _bdot(acc, wout_ref[0])
    o_ref[...] = x + mod[2:3] * _rms(m, g_ref[0, 1:2])


def _merge(l, x, mod, g_norm, ret_o, s5_y, s5_u, gla_o, na_o, s5d, wglu, bglu, wbr, wmg, bmg, wout):
    tm = TOKEN_TILE
    tok = lambda w: pl.BlockSpec((tm, w), lambda i: (i, 0))
    return pl.pallas_call(
        _merge_kernel,
        grid=(T_ALL // tm,),
        in_specs=[tok(D_MODEL), _mod_spec(l), _layer_spec(l, 4, D_MODEL),
                  tok(256), tok(256), tok(256), tok(256),
                  pl.BlockSpec((NA_HEADS, tm, NA_DH), lambda i: (0, i, 0)),
                  _layer_spec(l, 1, 256), _layer_spec(l, 256, 512), _layer_spec(l, 1, 512),
                  _layer_spec(l, N_BRANCH, BRANCH_W, D_MODEL), _layer_spec(l, D_MODEL, N_BRANCH * D_MODEL),
                  _layer_spec(l, 1, N_BRANCH * D_MODEL), _layer_spec(l, D_MODEL, D_MODEL)],
        out_specs=tok(D_MODEL),
        out_shape=jax.ShapeDtypeStruct((T_ALL, D_MODEL), F32),
        compiler_params=_cparams(("parallel",)),
        name="merge",
    )(x, mod, g_norm, ret_o, s5_y, s5_u, gla_o, na_o, s5d, wglu, bglu, wbr, wmg, bmg, wout)


FF_TILE = 1024


def _mlp_kernel(x_ref, mod_ref, g_ref, w1_ref, w2_ref, o_ref):
    x = x_ref[...]
    mod = mod_ref[0, 0]
    hb = (_rms(x, g_ref[0, 2:3]) * (1.0 + mod[4:5]) + mod[3:4]).astype(BF16)
    f = None
    for j in range(D_FF // FF_TILE):
        a = jnp.maximum(jnp.dot(hb, w1_ref[0, :, j * FF_TILE:(j + 1) * FF_TILE],
                                preferred_element_type=F32), 0.0)
        part = _bdot(a * a, w2_ref[0, j * FF_TILE:(j + 1) * FF_TILE, :])
        f = part if f is None else f + part
    o_ref[...] = x + mod[5:6] * _rms(f, g_ref[0, 3:4])


def _mlp(l, x, mod, g_norm, w1, w2):
    tm = TOKEN_TILE
    tok = pl.BlockSpec((tm, D_MODEL), lambda i: (i, 0))
    return pl.pallas_call(
        _mlp_kernel,
        grid=(T_ALL // tm,),
        in_specs=[tok, _mod_spec(l), _layer_spec(l, 4, D_MODEL),
                  _layer_spec(l, D_MODEL, D_FF), _layer_spec(l, D_FF, D_MODEL)],
        out_specs=tok,
        out_shape=jax.ShapeDtypeStruct((T_ALL, D_MODEL), F32),
        compiler_params=_cparams(("parallel",)),
        name="mlp",
    )(x, mod, g_norm, w1, w2)


def _rope_tables():
    half = RET_DK // 2
    nf = half // 2
    t = jnp.arange(DEC_SEQ)
    row = (t // GRID_W).astype(F32)
    col = (t % GRID_W).astype(F32)
    inv = ROPE_BASE ** (-jnp.arange(nf, dtype=F32) / nf)
    ang_r = row[:, None] * inv[None, :]
    ang_c = col[:, None] * inv[None, :]
    cos = jnp.concatenate([jnp.cos(ang_r)] * 2 + [jnp.cos(ang_c)] * 2, axis=1)
    sin = jnp.concatenate([-jnp.sin(ang_r), jnp.sin(ang_r), -jnp.sin(ang_c), jnp.sin(ang_c)], axis=1)
    return jnp.tile(cos, (1, RET_HEADS)), jnp.tile(sin, (1, RET_HEADS))


def _pack_w_in(w):
    offs = [0, 256, 512, 768, 1024, 1280, 1408, 1536, 1792, 2048, 2080, 2336, 2592, 2848]
    seg = lambda i: w[:, :, offs[i]:offs[i + 1]]
    order = [0, 1, 2, 3, 4, 5, 6, 7, 8, 10, 11, 12, 9]
    packed = jnp.concatenate([seg(i) for i in order], axis=2)
    return jnp.pad(packed, ((0, 0), (0, 0), (0, W_IN_PACKED - packed.shape[2]))).astype(BF16)


def _gla_state_in(st):
    eye = jnp.eye(GLA_HEADS, dtype=st.dtype)
    t = jnp.einsum('bldhkv,hg->bldhvgk', st, eye)
    return t.reshape(st.shape[0], st.shape[1], 2, GLA_HEADS * GLA_DV, GLA_HEADS * GLA_DK)


def kernel(x_prompt, x_sample, c, cache_na_k, cache_na_v, state_ret, state_s5, state_gla, c_ctx, w_ada, b_ada, g_norm, w_in, ret_log_decay, ret_gn, s5_lambda_re, s5_lambda_im, s5_log_dt, s5_b_re, s5_b_im, s5_c_re, s5_c_im, s5_d, s5_w_glu, s5_b_glu, gla_w_gate, gla_b_gate, gla_gn, na_rpb, w_branch, w_merge, b_merge, w_out, w_mlp1, w_mlp2):
    depth = w_in.shape[0]
    x = jnp.concatenate([x_prompt.reshape(T_CTX, D_MODEL), x_sample.reshape(T_LAT, D_MODEL)], axis=0)
    cc = jnp.concatenate([c_ctx[None], c, jnp.zeros((N_MOD_ROWS - 1 - DEC_BATCH, D_MODEL), F32)], axis=0)
    mod = _ada(cc, w_ada, b_ada).reshape(depth, N_MOD_ROWS, 6, D_MODEL)

    cos, sin = _rope_tables()
    w_in_p = _pack_w_in(w_in)
    w_glu_b, w_br_b, w_mg_b, w_out_b = (a.astype(BF16) for a in (s5_w_glu, w_branch, w_merge, w_out))
    w1_b, w2_b = w_mlp1.astype(BF16), w_mlp2.astype(BF16)
    ret_gn3, gla_gn3, s5_d3 = (a.reshape(depth, 1, BRANCH_W) for a in (ret_gn, gla_gn, s5_d))
    b_glu3 = s5_b_glu.reshape(depth, 1, 2 * BRANCH_W)
    b_mg3 = b_merge.reshape(depth, 1, N_BRANCH * D_MODEL)
    cache_k = cache_na_k.transpose(0, 1, 3, 2, 4)
    cache_v = cache_na_v.transpose(0, 1, 3, 2, 4)
    na_tb = _na_tables(na_rpb)
    gla_s0 = _gla_state_in(state_gla)
    s5_tables, s5_lam = _s5_tables(s5_lambda_re, s5_lambda_im, s5_log_dt, s5_b_re, s5_b_im,
                                   s5_c_re, s5_c_im)
    s5_x0 = state_s5.transpose(1, 2, 0, 3, 5, 4).reshape(depth, 2, DEC_BATCH, S5_GROUPS * S5_P2)

    ks_l, vs_l, ret_l, s5_l, gla_l = [], [], [], [], []
    for l in range(depth):
        ret, su, gqk, gv, gg, glr, nq, nk, nv = _inproj(l, x, mod, g_norm, w_in_p)

        ret_o, st_ret = _retention(l, ret, ret_log_decay, ret_gn3, latent=False)
        ret_o, = _retention(l, ret, ret_log_decay, ret_gn3, latent=True, cos=cos, sin=sin, s0=state_ret,
                            prev=ret_o)

        s5_y, s5_fin = _s5(l, su, *s5_tables, s5_lam, s5_x0)

        gla_o, st_gla = _gla(l, gqk, gv, gg, glr, gla_w_gate, gla_b_gate, gla_gn3, latent=False)
        gla_o, = _gla(l, gqk, gv, gg, glr, gla_w_gate, gla_b_gate, gla_gn3, latent=True, s0=gla_s0,
                      prev=gla_o)

        na_o = _attn_ctx(nq, nk, nv)
        na_o = _attn_lat(l, nq, nk, nv, cache_k, cache_v, na_tb, na_o)

        x = _merge(l, x, mod, g_norm, ret_o, s5_y, su, gla_o, na_o,
                   s5_d3, w_glu_b, b_glu3, w_br_b, w_mg_b, b_mg3, w_out_b)
        x = _mlp(l, x, mod, g_norm, w1_b, w2_b)

        ks_l.append(nk)
        vs_l.append(nv)
        ret_l.append(st_ret)
        s5_l.append(s5_fin)
        gla_l.append(st_gla)

    y_prompt = x[:T_CTX].reshape(BATCH, SEQ, D_MODEL)
    y_sample = x[T_CTX:].reshape(DEC_BATCH, DEC_SEQ, D_MODEL)

    def cache_out(per_layer):
        a = jnp.stack(per_layer, axis=0)[:, :, :T_CTX].reshape(depth, NA_HEADS, BATCH, SEQ, NA_DH)
        return a.transpose(2, 0, 3, 1, 4)

    s5_out = jnp.stack(s5_l, axis=0).reshape(depth, 2, BATCH, S5_GROUPS, 2, S5_STATE)
    gla_out = jnp.stack(gla_l, axis=1)
    gla_out = gla_out.reshape(BATCH, depth, 2, GLA_HEADS, GLA_DV, GLA_HEADS, GLA_DK)
    gla_out = jnp.stack([gla_out[:, :, :, h, :, h, :] for h in range(GLA_HEADS)], axis=3)
    return (y_prompt, y_sample, cache_out(ks_l), cache_out(vs_l), jnp.stack(ret_l, axis=1),
            s5_out.transpose(2, 0, 1, 3, 5, 4), gla_out.transpose(0, 1, 2, 3, 5, 4))
```

```python
import functools
import math

import numpy as np
import jax
import jax.numpy as jnp
from jax import lax
from jax.experimental import pallas as pl
from jax.experimental.pallas import tpu as pltpu

F32 = jnp.float32
BF16 = jnp.bfloat16

D_MODEL = 1024
BATCH = 16
SEQ = 256
DEPTH = 4
DEC_BATCH = 4
DEC_SEQ = 1024
PAST_LEN = 256
GRID_W = 64
N_BRANCH = 4
BRANCH_W = 256
RET_HEADS = 4
RET_DK = 64
RET_DV = 64
S5_GROUPS = 16
S5_GROUP_CH = 16
S5_STATE = 64
GLA_HEADS = 4
GLA_DK = 32
GLA_DV = 64
GLA_RANK = 16
GLA_TAU = 16.0
NA_HEADS = 4
NA_DH = 64
NA_WIN_H = 8
NA_WIN_W = 16
D_FF = 4 * D_MODEL
ROPE_BASE = 10000.0
EPS = 1e-6

T_CTX = BATCH * SEQ
T_LAT = DEC_BATCH * DEC_SEQ
T_ALL = T_CTX + T_LAT
LAT_BLOCK0 = T_CTX // DEC_SEQ
N_MOD_ROWS = 8
TOKEN_TILE = 512
GLA_CHUNK = 64
GLA_BLOCK_CHUNKS = 4
S5_CHUNK = 16
S5_RELAYOUT_ROWS = 32
RET_QBLOCK = 256
NA_QBLOCK = 256
GRID_ROWS = DEC_SEQ // GRID_W
NA_KH = min(NA_WIN_H, GRID_ROWS)
NA_REL_ROWS = 2 * NA_WIN_H - 1
VMEM_LIMIT = 56 * 1024 * 1024
W_IN_PACKED = 2944


def _cparams(sem):
    return pltpu.CompilerParams(dimension_semantics=sem, vmem_limit_bytes=VMEM_LIMIT)


def _bdot(a, b):
    return jnp.dot(a.astype(BF16), b.astype(BF16), preferred_element_type=F32)


def _bdot_nt(a, b):
    return lax.dot_general(a.astype(BF16), b.astype(BF16), (((1,), (1,)), ((), ())),
                           preferred_element_type=F32)


def _bdot_tn(a, b):
    return lax.dot_general(a.astype(BF16), b.astype(BF16), (((0,), (0,)), ((), ())),
                           preferred_element_type=F32)


def _split(a):
    hi = a.astype(BF16)
    lo = (a - hi.astype(F32)).astype(BF16)
    return hi, lo


def _dot3(a, b):
    ah, al = _split(a)
    bh, bl = _split(b)
    d = functools.partial(jnp.dot, preferred_element_type=F32)
    return d(ah, bh) + d(al, bh) + d(ah, bl)


def _sigmoid(x):
    return 1.0 / (1.0 + jnp.exp(-x))


def _silu(x):
    return x * _sigmoid(x)


def _rms(x, g):
    return x * lax.rsqrt(jnp.mean(x * x, axis=-1, keepdims=True) + EPS) * g


def _group_norm(o, g):
    mu = jnp.mean(o, axis=-1, keepdims=True)
    xc = o - mu
    return xc * lax.rsqrt(jnp.mean(xc * xc, axis=-1, keepdims=True) + EPS) * g


def _mod_row(i):
    ctx_tiles = T_CTX // TOKEN_TILE
    return jnp.where(i < ctx_tiles, 0, 1 + (i - ctx_tiles) // (DEC_SEQ // TOKEN_TILE))


def _mod_spec(l):
    return pl.BlockSpec((1, 1, 6, D_MODEL), lambda i: (l, _mod_row(i), 0, 0))


def _layer_spec(l, *shape, single_buffer=False):
    mode = pl.Buffered(1) if single_buffer else None
    return pl.BlockSpec((1,) + shape, lambda *_: (l,) + (0,) * len(shape), pipeline_mode=mode)


_ANY = pl.BlockSpec(memory_space=pl.ANY)


ADA_TILE = 1536


def _ada_kernel(c_ref, w_ref, b_ref, o_ref):
    a = _silu(c_ref[...])
    o_ref[0] = _bdot(a, w_ref[0]) + b_ref[0]


def _ada(cc, w_ada, b_ada):
    n = 6 * D_MODEL
    return pl.pallas_call(
        _ada_kernel,
        grid=(DEPTH, n // ADA_TILE),
        in_specs=[pl.BlockSpec((N_MOD_ROWS, D_MODEL), lambda l, j: (0, 0)),
                  pl.BlockSpec((1, D_MODEL, ADA_TILE), lambda l, j: (l, 0, j)),
                  pl.BlockSpec((1, 1, ADA_TILE), lambda l, j: (l, 0, j))],
        out_specs=pl.BlockSpec((1, N_MOD_ROWS, ADA_TILE), lambda l, j: (l, 0, j)),
        out_shape=jax.ShapeDtypeStruct((DEPTH, N_MOD_ROWS, n), F32),
        compiler_params=_cparams(("parallel", "parallel")),
        name="ada_mod",
    )(cc, w_ada, b_ada.reshape(DEPTH, 1, n))


def _inproj_kernel(*refs, first):
    if first:
        (xa_ref, xb_ref, mod_ref, g_ref, w_ref, x_out_ref,
         ret_ref, s5_ref, gqk_ref, gv_ref, gg_ref, glr_ref, nq_ref, nk_ref, nv_ref) = refs
        x = jnp.where(pl.program_id(0) < T_CTX // TOKEN_TILE, xa_ref[...], xb_ref[...])
        x_out_ref[...] = x
    else:
        (x_ref, mod_ref, g_ref, w_ref,
         ret_ref, s5_ref, gqk_ref, gv_ref, gg_ref, glr_ref, nq_ref, nk_ref, nv_ref) = refs
        x = x_ref[...]
    mod = mod_ref[0, 0]
    h = _rms(x, g_ref[0, 0:1]) * (1.0 + mod[1:2]) + mod[0:1]
    hb = h.astype(BF16)

    def proj(lo, hi):
        return jnp.dot(hb, w_ref[0, :, lo:hi], preferred_element_type=F32)

    ret_ref[...] = proj(0, 1024)
    s5_ref[...] = proj(1024, 1280)
    gqk_ref[...] = proj(1280, 1536)
    gv_ref[...] = proj(1536, 1792)
    gg_ref[...] = proj(1792, 2048)
    for ref, lo in ((nq_ref, 2048), (nk_ref, 2304), (nv_ref, 2560)):
        r = proj(lo, lo + 256)
        for hh in range(NA_HEADS):
            ref[hh] = r[:, hh * NA_DH:(hh + 1) * NA_DH]
    glr_ref[...] = proj(2816, 2944)


def _inproj(l, xs, mod, g_norm, w_in_p):
    tm = TOKEN_TILE
    first = isinstance(xs, tuple)
    tok = lambda w: pl.BlockSpec((tm, w), lambda i: (i, 0))
    head = pl.BlockSpec((NA_HEADS, tm, NA_DH), lambda i: (0, i, 0))
    tshape = lambda w: jax.ShapeDtypeStruct((T_ALL, w), F32)
    hshape = jax.ShapeDtypeStruct((NA_HEADS, T_ALL, NA_DH), F32)
    ctx_tiles = T_CTX // tm
    if first:
        x_specs = [pl.BlockSpec((tm, D_MODEL), lambda i: (jnp.minimum(i, ctx_tiles - 1), 0)),
                   pl.BlockSpec((tm, D_MODEL), lambda i: (jnp.maximum(i - ctx_tiles, 0), 0))]
        x_args = list(xs)
    else:
        x_specs, x_args = [tok(D_MODEL)], [xs]
    return pl.pallas_call(
        functools.partial(_inproj_kernel, first=first),
        grid=(T_ALL // tm,),
        in_specs=x_specs + [_mod_spec(l), _layer_spec(l, 4, D_MODEL), _layer_spec(l, D_MODEL, W_IN_PACKED)],
        out_specs=([tok(D_MODEL)] if first else [])
        + [tok(1024), tok(256), tok(256), tok(256), tok(256), tok(128), head, head, head],
        out_shape=([tshape(D_MODEL)] if first else [])
        + [tshape(1024), tshape(256), tshape(256), tshape(256), tshape(256), tshape(128),
           hshape, hshape, hshape],
        compiler_params=_cparams(("parallel",)),
        name="in_proj",
    )(*x_args, mod, g_norm, w_in_p)


def _rope_rotate(x, lane):
    first = (lane % 32) < 16
    w = x.shape[-1]
    return jnp.where(first, pltpu.roll(x, w - 16, 1), pltpu.roll(x, 16, 1))


def _ret_kernel(ld_ref, ret_ref, gn_ref, *rest, layer, seq, latent):
    if latent:
        cos_ref, sin_ref, s0_ref, _, out_ref, dec_s = rest
    else:
        out_ref, st_ref, dec_s = rest
    tq = RET_QBLOCK
    nq = seq // tq
    width = dec_s.shape[-1]

    @pl.when(pl.program_id(0) == 0)
    def _build_decay():
        rel = (lax.broadcasted_iota(jnp.int32, (tq, width), 0) + (nq - 1) * tq
               - lax.broadcasted_iota(jnp.int32, (tq, width), 1)).astype(F32)
        for h in range(RET_HEADS):
            dec_s[h] = (jnp.where(rel >= 0, jnp.exp(ld_ref[layer, 0, h] * jnp.maximum(rel, 0.0)), 0.0)
                        + jnp.where(rel <= 0, jnp.exp(ld_ref[layer, 1, h] * jnp.maximum(-rel, 0.0)), 0.0))

    q = ret_ref[:, 0:256]
    k = ret_ref[:, 256:512]
    if latent:
        lane = lax.broadcasted_iota(jnp.int32, (seq, 256), 1)
        cos = cos_ref[...]
        sin = sin_ref[...]
        q = q * cos + _rope_rotate(q, lane) * sin
        k = k * cos + _rope_rotate(k, lane) * sin
    k = k * (RET_DK ** -0.5)
    pos_c = lax.broadcasted_iota(jnp.int32, (seq, 1), 0).astype(F32)
    for h in range(RET_HEADS):
        lgf = ld_ref[layer, 0, h]
        lgb = ld_ref[layer, 1, h]
        sl = slice(h * RET_DK, (h + 1) * RET_DK)
        qh = q[:, sl]
        kh = k[:, sl]
        vh = ret_ref[:, 512 + h * RET_DV:512 + (h + 1) * RET_DV]
        kb = kh.astype(BF16)
        vb = vh.astype(BF16)
        if latent:
            q_init = jnp.concatenate([qh * jnp.exp(lgf * (pos_c + 1.0)),
                                      qh * jnp.exp(lgb * (seq - pos_c))], axis=1)
            s_init = jnp.concatenate([s0_ref[0, 0, 0, h], s0_ref[0, 0, 1, h]], axis=0)
        for qb in range(nq):
            w0 = (nq - 1 - qb) * tq
            rows = slice(qb * tq, (qb + 1) * tq)
            s = _bdot_nt(qh[rows], kb) * dec_s[h, :, w0:w0 + seq]
            o = _bdot(s, vb)
            if latent:
                o = o + _bdot(q_init[rows], s_init)
            g = ret_ref[rows, 768 + h * RET_DV:768 + (h + 1) * RET_DV]
            out_ref[rows, sl] = _group_norm(o, gn_ref[0, :, sl]) * _silu(g)
        if not latent:
            st_ref[0, 0, h] = _bdot_tn(kh * jnp.exp(lgf * (seq - 1.0 - pos_c)), vb)
            st_ref[0, 1, h] = _bdot_tn(kh * jnp.exp(lgb * pos_c), vb)


def _retention(l, ret, ld, gn, *, latent, cos=None, sin=None, s0=None, prev=None):
    seq = DEC_SEQ if latent else SEQ
    nb = DEC_BATCH if latent else BATCH
    off = LAT_BLOCK0 if latent else 0
    in_specs = [pl.BlockSpec(memory_space=pltpu.SMEM),
                pl.BlockSpec((seq, 1024), lambda b: (b + off, 0)),
                _layer_spec(l, 1, 256)]
    args = [ld, ret, gn]
    out_specs = [pl.BlockSpec((seq, 256), lambda b: (b + off, 0))]
    out_shape = [jax.ShapeDtypeStruct((T_ALL, 256), F32)]
    aliases = {}
    if latent:
        in_specs += [pl.BlockSpec((seq, 256), lambda b: (0, 0)),
                     pl.BlockSpec((seq, 256), lambda b: (0, 0)),
                     pl.BlockSpec((1, 1, 2, RET_HEADS, RET_DK, RET_DV), lambda b: (b, l, 0, 0, 0, 0)),
                     _ANY]
        args += [cos, sin, s0, prev]
        aliases = {6: 0}
    else:
        out_specs.append(pl.BlockSpec((1, 2, RET_HEADS, RET_DK, RET_DV), lambda b: (b, 0, 0, 0, 0)))
        out_shape.append(jax.ShapeDtypeStruct((nb, 2, RET_HEADS, RET_DK, RET_DV), F32))
    return pl.pallas_call(
        functools.partial(_ret_kernel, layer=l, seq=seq, latent=latent),
        grid=(nb,),
        in_specs=in_specs, out_specs=out_specs, out_shape=out_shape,
        input_output_aliases=aliases,
        scratch_shapes=[pltpu.VMEM((RET_HEADS, RET_QBLOCK, 2 * seq - RET_QBLOCK), F32)],
        compiler_params=_cparams(("arbitrary",)),
        name="retention_lat" if latent else "retention_ctx",
    )(*args)


def _gla_kernel(gqk_ref, gv_ref, gg_ref, glr_ref, wg_ref, bg_ref, gn_ref, *rest, seq, latent):
    if latent:
        s0_ref, _, out_ref, gate_s, o_s, st_s, qst_s, ds_s, e_s = rest
    else:
        out_ref, st_ref, gate_s, o_s, st_s, qst_s, ds_s, e_s = rest
    c = GLA_CHUNK
    n = seq // c
    hk = GLA_HEADS * GLA_DK
    lr = glr_ref[...]
    for d in range(2):
        pre = _bdot(lr[:, d * GLA_RANK:(d + 1) * GLA_RANK], wg_ref[0, d]) + bg_ref[0, d:d + 1]
        gate_s[d] = (jnp.minimum(pre, 0.0) - jnp.log(1.0 + jnp.exp(-jnp.abs(pre)))) / GLA_TAU
        st_s[d] = s0_ref[0, 0, d] if latent else jnp.zeros((GLA_HEADS * GLA_DV, hk), F32)

    nc = GLA_BLOCK_CHUNKS
    rb = nc * c
    ti = lax.broadcasted_iota(jnp.int32, (rb, rb), 0)
    tj = lax.broadcasted_iota(jnp.int32, (rb, rb), 1)
    same = (ti // c) == (tj // c)
    ones = lambda m: (same & m).astype(BF16)
    tri = [ones(tj <= ti), ones(tj >= ti)]
    mid = [ones((tj % c) < c // 2), ones((tj % c) >= c // 2)]
    tot = ones(tj == tj)
    lane_k = lax.broadcasted_iota(jnp.int32, (c, hk), 1)
    head_mask = [(lane_k // GLA_DK) == h for h in range(GLA_HEADS)]
    ai = lax.broadcasted_iota(jnp.int32, (GLA_HEADS * c, c), 0) % c
    aj = lax.broadcasted_iota(jnp.int32, (GLA_HEADS * c, c), 1)
    keep = [aj <= ai, aj >= ai]
    sr = lax.broadcasted_iota(jnp.int32, (GLA_HEADS * GLA_DV, hk), 0) // GLA_DV
    sc = lax.broadcasted_iota(jnp.int32, (GLA_HEADS * GLA_DV, hk), 1) // GLA_DK
    diag = sr == sc
    scale = GLA_DK ** -0.5
    d32 = functools.partial(jnp.dot, preferred_element_type=F32)

    def rows_of(i, size):
        return pl.ds(i * size, size) if isinstance(i, int) else pl.ds(pl.multiple_of(i * size, size), size)

    def local(bi):
        rows = rows_of(bi, rb)
        q = gqk_ref[rows, 0:hk] * scale
        k = gqk_ref[rows, hk:2 * hk]
        v = gv_ref[rows, :].astype(BF16)
        q_att, k_att, k_st = [], [], []
        for d in range(2):
            gh, gl = _split(gate_s[d, rows, :])
            b = d32(tri[d], gh) + d32(tri[d], gl)
            b_mid = d32(mid[d], gh) + d32(mid[d], gl)
            b_end = d32(tot, gh) + d32(tot, gl)
            q_att.append(q * jnp.exp(b - b_mid))
            k_att.append(k * jnp.exp(b_mid - b))
            k_st.append((k * jnp.exp(b_end - b)).astype(BF16))
            qst_s[d, rows, :] = (q * jnp.exp(b)).astype(BF16)
            decay = jnp.exp(b_end)
            for cc in range(nc):
                e_s[d, bi * nc + cc] = decay[cc * c:cc * c + 8]
        pairs = [(d, cc) for d in range(2) for cc in range(nc)]
        att = {}
        for d, cc in pairs:
            r = slice(cc * c, (cc + 1) * c)
            qa = q_att[d][r]
            q_stack = jnp.concatenate([jnp.where(head_mask[h], qa, 0.0) for h in range(GLA_HEADS)], axis=0)
            att[d, cc] = jnp.where(keep[d], _bdot_nt(q_stack, k_att[d][r]), 0.0).astype(BF16)
        for d, cc in pairs:
            r = slice(cc * c, (cc + 1) * c)
            o = jnp.concatenate([d32(att[d, cc][h * c:(h + 1) * c], v[r, h * GLA_DV:(h + 1) * GLA_DV])
                                 for h in range(GLA_HEADS)], axis=1)
            o_s[d, rows_of(bi * nc + cc, c), :] = o
        for d, cc in pairs:
            r = slice(cc * c, (cc + 1) * c)
            ds_s[d, bi * nc + cc] = jnp.where(diag, lax.dot_general(
                v[r], k_st[d][r], (((0,), (0,)), ((), ())), preferred_element_type=F32), 0.0)

    def recur(ci, d):
        rows = rows_of(ci, c)
        st = st_s[d]
        o_s[d, rows, :] += _bdot_nt(qst_s[d, rows, :], st)
        st_s[d] = st * e_s[d, ci, 0:1] + ds_s[d, ci]

    def recur_body(i, carry):
        recur(i, 0)
        recur(n - 1 - i, 1)
        return carry

    if n == nc:
        local(0)
        for i in range(n):
            recur_body(i, 0)
    else:
        lax.fori_loop(0, n // nc, lambda i, carry: (local(i), carry)[1], 0)
        lax.fori_loop(0, n, recur_body, 0, unroll=2)

    o = o_s[0] + o_s[1]
    for h in range(GLA_HEADS):
        sl = slice(h * GLA_DV, (h + 1) * GLA_DV)
        out_ref[:, sl] = _group_norm(o[:, sl], gn_ref[0, :, sl]) * _silu(gg_ref[:, sl])
    if not latent:
        hv = GLA_HEADS * GLA_DV
        eye = (lax.broadcasted_iota(jnp.int32, (hv, hv), 0)
               == lax.broadcasted_iota(jnp.int32, (hv, hv), 1)).astype(BF16)
        tn = lambda a: lax.dot_general(a, eye, (((0,), (0,)), ((), ())), preferred_element_type=F32)
        for d in range(2):
            hi, lo = _split(st_s[d])
            lo2 = (st_s[d] - hi.astype(F32) - lo.astype(F32)).astype(BF16)
            s_all = tn(hi) + tn(lo) + tn(lo2)
            for h in range(GLA_HEADS):
                st_ref[0, d, h] = s_all[h * GLA_DK:(h + 1) * GLA_DK, h * GLA_DV:(h + 1) * GLA_DV]


def _gla(l, gqk, gv, gg, glr, wg, bg, gn, *, latent, s0=None, prev=None):
    seq = DEC_SEQ if latent else SEQ
    nb = DEC_BATCH if latent else BATCH
    off = LAT_BLOCK0 if latent else 0
    hk = GLA_HEADS * GLA_DK
    hv = GLA_HEADS * GLA_DV
    tok = lambda w: pl.BlockSpec((seq, w), lambda b: (b + off, 0))
    in_specs = [tok(256), tok(256), tok(256), tok(128),
                _layer_spec(l, 2, GLA_RANK, hk), _layer_spec(l, 2, hk), _layer_spec(l, 1, 256)]
    args = [gqk, gv, gg, glr, wg, bg, gn]
    out_specs = [tok(256)]
    out_shape = [jax.ShapeDtypeStruct((T_ALL, 256), F32)]
    aliases = {}
    if latent:
        in_specs += [pl.BlockSpec((1, 1, 2, hv, hk), lambda b: (b, l, 0, 0, 0)), _ANY]
        args += [s0, prev]
        aliases = {8: 0}
    else:
        out_specs.append(pl.BlockSpec((1, 2, GLA_HEADS, GLA_DK, GLA_DV), lambda b: (b, 0, 0, 0, 0)))
        out_shape.append(jax.ShapeDtypeStruct((nb, 2, GLA_HEADS, GLA_DK, GLA_DV), F32))
    return pl.pallas_call(
        functools.partial(_gla_kernel, seq=seq, latent=latent),
        grid=(nb,),
        in_specs=in_specs, out_specs=out_specs, out_shape=out_shape,
        input_output_aliases=aliases,
        scratch_shapes=[pltpu.VMEM((2, seq, hk), F32), pltpu.VMEM((2, seq, hv), F32),
                        pltpu.VMEM((2, hv, hk), F32), pltpu.VMEM((2, seq, hk), BF16),
                        pltpu.VMEM((2, seq // GLA_CHUNK, hv, hk), F32),
                        pltpu.VMEM((2, seq // GLA_CHUNK, 8, hk), F32)],
        compiler_params=_cparams(("parallel",)),
        name="gla_lat" if latent else "gla_ctx",
    )(*args)


S5_W = S5_CHUNK * S5_GROUP_CH
S5_P2 = 2 * S5_STATE
S5_ROWS = T_ALL // S5_CHUNK
S5_ROWS_CTX = T_CTX // S5_CHUNK
S5_LANE_GROUPS = 128 // S5_GROUP_CH


S5_TE_ROWS = S5_W + 4 * S5_P2


def _s5_toeplitz_kernel(cc_ref, wfr_ref, wb_ref, ff_ref, te_hi_ref, te_lo_ref, ff_hi_ref, ff_lo_ref):
    def put(ref_hi, ref_lo, rows, val):
        hi, lo = _split(val)
        ref_hi[0, rows, :] = hi
        ref_lo[0, rows, :] = lo

    wfr = wfr_ref[0]
    wb = wb_ref[0]
    kf = _dot3(cc_ref[0, 0], wfr)
    kb = _dot3(cc_ref[0, 1], wb)
    lane = lax.broadcasted_iota(jnp.int32, (S5_GROUP_CH, S5_W), 1)
    blocks = []
    for i in range(S5_CHUNK):
        sf = ((i + 1 - S5_CHUNK) * S5_GROUP_CH) % S5_W
        fwd = jnp.where(lane < (i + 1) * S5_GROUP_CH, pltpu.roll(kf, sf, 1) if sf else kf, 0.0)
        bwd = jnp.where(lane >= i * S5_GROUP_CH, pltpu.roll(kb, i * S5_GROUP_CH, 1) if i else kb, 0.0)
        blocks.append(fwd + bwd)
    swapped = lambda t: jnp.concatenate([t[S5_STATE:], t[:S5_STATE]], axis=0)
    put(te_hi_ref, te_lo_ref, slice(0, S5_TE_ROWS),
        jnp.concatenate(blocks + [wfr, wb, swapped(wfr), swapped(wb)], axis=0))
    put(ff_hi_ref, ff_lo_ref, slice(0, 2 * S5_P2), ff_ref[0])


def _s5_toeplitz(cc, wfr, wb, ff):
    n = cc.shape[0]
    tbl = pl.BlockSpec((1, S5_P2, S5_W), lambda i: (i, 0, 0))
    te_spec = pl.BlockSpec((1, S5_TE_ROWS, S5_W), lambda i: (i, 0, 0))
    ff_spec = pl.BlockSpec((1, 2 * S5_P2, S5_W), lambda i: (i, 0, 0))
    te_shape = jax.ShapeDtypeStruct((n, S5_TE_ROWS, S5_W), BF16)
    ff_shape = jax.ShapeDtypeStruct((n, 2 * S5_P2, S5_W), BF16)
    return pl.pallas_call(
        _s5_toeplitz_kernel,
        grid=(n,),
        in_specs=[pl.BlockSpec((1, 2, S5_GROUP_CH, S5_P2), lambda i: (i, 0, 0, 0)), tbl, tbl, ff_spec],
        out_specs=[te_spec, te_spec, ff_spec, ff_spec],
        out_shape=[te_shape, te_shape, ff_shape, ff_shape],
        compiler_params=_cparams(("parallel",)),
        name="s5_toeplitz",
    )(cc, wfr, wb, ff)


def _piece_transpose(blocks, piece):
    x = list(blocks)
    n = len(x)
    d = n // 2
    while d >= 1:
        low = (piece & d) == 0
        for v in range(n):
            if v & d:
                continue
            a, b = x[v], x[v + d]
            x[v] = jnp.where(low, a, pltpu.roll(b, d * S5_GROUP_CH, 1))
            x[v + d] = jnp.where(low, pltpu.roll(a, 128 - d * S5_GROUP_CH, 1), b)
        d //= 2
    return x


def _s5_kernel(u_ref, te_hi_ref, te_lo_ref, ff_hi_ref, ff_lo_ref, lam_ref, x0_ref, y_ref, fin_ref,
               ug_s, xs_s, ps_s):
    C = S5_CHUNK
    ng = S5_LANE_GROUPS
    rc_rows = S5_RELAYOUT_ROWS
    piece = lax.broadcasted_iota(jnp.int32, (rc_rows, 128), 1) // S5_GROUP_CH

    def gather_body(rc, carry):
        r0 = pl.multiple_of(rc * rc_rows, rc_rows)
        for m in range(C // ng):
            out = _piece_transpose(
                [u_ref[pl.ds(r0 * C + m * ng + jl, rc_rows, stride=C), :] for jl in range(ng)], piece)
            for g in range(ng):
                col = g * S5_W + m * 128
                ug_s[pl.ds(r0, rc_rows), col:col + 128] = out[g]
        return carry

    lax.fori_loop(0, S5_ROWS // rc_rows, gather_body, 0)

    nt = lambda a, b: lax.dot_general(a, b, (((1,), (1,)), ((), ())), preferred_element_type=F32)
    for g in range(ng):
        uh, ul = _split(ug_s[:, g * S5_W:(g + 1) * S5_W])
        r = nt(uh, te_hi_ref[0, g]) + nt(ul, te_hi_ref[0, g]) + nt(uh, te_lo_ref[0, g])
        ug_s[:, g * S5_W:(g + 1) * S5_W] = r[:, 0:S5_W]
        for t in range(4):
            xs_s[t * ng + g] = r[:, S5_W + t * S5_P2:S5_W + (t + 1) * S5_P2]

    w = ng * S5_P2

    def carry(base, nseq, nchunks, init_f, init_b):
        a_f, b_f = lam_ref[0, 0, 0:1], lam_ref[0, 0, 1:2]
        a_b, b_b = lam_ref[0, 1, 0:1], lam_ref[0, 1, 1:2]
        load = lambda t, rows: jnp.concatenate([xs_s[t * ng + g, rows, :] for g in range(ng)], axis=1)

        def body(i, st):
            sf, tf, sb, tb = st
            rf = pl.ds(base + i, nseq, stride=nchunks)
            rb = pl.ds(base + (nchunks - 1 - i), nseq, stride=nchunks)
            for g in range(ng):
                ps_s[g, rf, :] = sf[:, g * S5_P2:(g + 1) * S5_P2]
                ps_s[ng + g, rb, :] = sb[:, g * S5_P2:(g + 1) * S5_P2]
            return (a_f * sf + b_f * tf + load(0, rf), a_f * tf - b_f * sf + load(2, rf),
                    a_b * sb + b_b * tb + load(1, rb), a_b * tb - b_b * sb + load(3, rb))

        first = (lax.broadcasted_iota(jnp.int32, (1, w), 1) % S5_P2) < S5_STATE
        swap = lambda s: jnp.where(first, pltpu.roll(s, w - S5_STATE, 1), pltpu.roll(s, S5_STATE, 1))
        return lax.fori_loop(0, nchunks, body, (init_f, swap(init_f), init_b, swap(init_b)))

    zeros = jnp.zeros((BATCH, w), F32)
    fin = carry(0, BATCH, SEQ // C, zeros, zeros)
    fin_ref[0] = fin[0]
    fin_ref[1] = fin[2]
    carry(S5_ROWS_CTX, DEC_BATCH, DEC_SEQ // C, x0_ref[0, 0], x0_ref[0, 1])

    for g in range(ng):
        ph, pl_ = _split(jnp.concatenate([ps_s[g], ps_s[ng + g]], axis=1))
        d = functools.partial(jnp.dot, preferred_element_type=F32)
        ug_s[:, g * S5_W:(g + 1) * S5_W] += (d(ph, ff_hi_ref[0, g]) + d(pl_, ff_hi_ref[0, g])
                                             + d(ph, ff_lo_ref[0, g]))

    def scatter_body(rc, carry):
        r0 = pl.multiple_of(rc * rc_rows, rc_rows)
        for m in range(C // ng):
            out = _piece_transpose(
                [ug_s[pl.ds(r0, rc_rows), g * S5_W + m * 128:g * S5_W + (m + 1) * 128] for g in range(ng)],
                piece)
            for il in range(ng):
                y_ref[pl.ds(r0 * C + m * ng + il, rc_rows, stride=C), :] = out[il]
        return carry

    lax.fori_loop(0, S5_ROWS // rc_rows, scatter_body, 0)


def _s5(l, su, te_hi, te_lo, ff_hi, ff_lo, lam, x0):
    ng = S5_LANE_GROUPS
    w = ng * S5_P2
    te_spec = pl.BlockSpec((1, ng, S5_TE_ROWS, S5_W), lambda i: (l, i, 0, 0))
    ff_spec = pl.BlockSpec((1, ng, 2 * S5_P2, S5_W), lambda i: (l, i, 0, 0))
    return pl.pallas_call(
        _s5_kernel,
        grid=(S5_GROUPS // ng,),
        in_specs=[pl.BlockSpec((T_ALL, 128), lambda i: (0, i)), te_spec, te_spec, ff_spec, ff_spec,
                  pl.BlockSpec((1, 2, 2, w), lambda i: (l, 0, 0, i)),
                  pl.BlockSpec((1, 2, DEC_BATCH, w), lambda i: (l, 0, 0, i))],
        out_specs=[pl.BlockSpec((T_ALL, 128), lambda i: (0, i)),
                   pl.BlockSpec((2, BATCH, w), lambda i: (0, 0, i))],
        out_shape=[jax.ShapeDtypeStruct((T_ALL, S5_GROUPS * S5_GROUP_CH), F32),
                   jax.ShapeDtypeStruct((2, BATCH, S5_GROUPS * S5_P2), F32)],
        scratch_shapes=[pltpu.VMEM((S5_ROWS, ng * S5_W), F32),
                        pltpu.VMEM((4 * ng, S5_ROWS, S5_P2), F32), pltpu.VMEM((2 * ng, S5_ROWS, S5_P2), F32)],
        compiler_params=_cparams(("parallel",)),
        name="s5_scan",
    )(su, te_hi, te_lo, ff_hi, ff_lo, lam, x0)


def _s5_tables(lam_re, lam_im, log_dt, b_re, b_im, c_re, c_im):
    C, G, P, H = S5_CHUNK, S5_GROUPS, S5_STATE, S5_GROUP_CH
    L = lam_re.shape[0]
    dt = jnp.exp(log_dt)[..., None]
    ar, ai = lam_re * dt, lam_im * dt

    steps = jnp.arange(C + 1, dtype=F32)
    mag = jnp.exp(ar[..., None] * steps)
    pw_re, pw_im = mag * jnp.cos(ai[..., None] * steps), mag * jnp.sin(ai[..., None] * steps)
    exact = functools.partial(jnp.einsum, precision=lax.Precision.HIGHEST)
    tau_np = np.arange(C * H) // H

    def power(d, t):
        sel = jnp.asarray((np.arange(C + 1)[:, None] == np.asarray(t)[None, :]).astype(np.float32))
        return exact('lgpt,tn->lgpn', pw_re[:, d], sel), exact('lgpt,tn->lgpn', pw_im[:, d], sel)

    lr, li = pw_re[..., 1], pw_im[..., 1]
    den = lam_re * lam_re + lam_im * lam_im
    qr = ((lr - 1.0) * lam_re + li * lam_im) / den
    qi = (li * lam_re - (lr - 1.0) * lam_im) / den
    bbr = qr[..., None] * b_re - qi[..., None] * b_im
    bbi = qr[..., None] * b_im + qi[..., None] * b_re
    chan = jnp.asarray((np.arange(H)[:, None] == (np.arange(C * H) % H)[None, :]).astype(np.float32))
    lanes = lambda a: exact('ldgph,hn->ldgpn', a, chan)
    bbr, bbi = lanes(bbr), lanes(bbi)
    c_t = lambda a: exact('ldghp,hn->ldgpn', a, chan)
    ctr, cti = c_t(c_re), c_t(c_im)

    def w_of(d, t):
        pr, pi = power(d, t)
        return jnp.concatenate([pr * bbr[:, d] - pi * bbi[:, d], pr * bbi[:, d] + pi * bbr[:, d]], axis=-2)

    def f_of(d, t):
        pr, pi = power(d, t)
        return jnp.concatenate([ctr[:, d] * pr - cti[:, d] * pi, -(ctr[:, d] * pi + cti[:, d] * pr)], axis=-2)

    wfr = w_of(0, (C - 1) - tau_np).reshape(L * G, 2 * P, C * H)
    wb = w_of(1, tau_np).reshape(L * G, 2 * P, C * H)
    cc = jnp.concatenate([c_re, -c_im], axis=-1).transpose(0, 2, 1, 3, 4).reshape(L * G, 2, H, 2 * P)
    ff = jnp.concatenate([f_of(0, tau_np + 1), f_of(1, C - tau_np)], axis=-2)
    tables = _s5_toeplitz(cc, wfr, wb, ff.reshape(L * G, 4 * P, C * H))
    tables = [t.reshape((L, G) + t.shape[1:]) for t in tables]
    cr, ci = pw_re[..., C], pw_im[..., C]
    a = jnp.concatenate([cr, cr], axis=-1).reshape(L, 2, 1, G * 2 * P)
    b = jnp.concatenate([-ci, ci], axis=-1).reshape(L, 2, 1, G * 2 * P)
    return tables, jnp.concatenate([a, b], axis=2)


def _softmax_pv(s_parts, v_parts):
    m = s_parts[0].max(axis=-1, keepdims=True)
    for s in s_parts[1:]:
        m = jnp.maximum(m, s.max(axis=-1, keepdims=True))
    o = None
    l = None
    for s, v in zip(s_parts, v_parts):
        p = jnp.exp(s - m)
        pl_ = p.sum(axis=-1, keepdims=True)
        po = _bdot(p, v)
        o = po if o is None else o + po
        l = pl_ if l is None else l + pl_
    return o / l


def _attn_ctx_kernel(q_ref, k_ref, v_ref, o_ref):
    scale = NA_DH ** -0.5
    for h in range(NA_HEADS):
        s = _bdot_nt(q_ref[h], k_ref[h]) * scale
        o_ref[h] = _softmax_pv([s], [v_ref[h]])


def _attn_ctx(nq, nk, nv):
    spec = pl.BlockSpec((NA_HEADS, SEQ, NA_DH), lambda b: (0, b, 0))
    return pl.pallas_call(
        _attn_ctx_kernel,
        grid=(BATCH,),
        in_specs=[spec, spec, spec],
        out_specs=spec,
        out_shape=jax.ShapeDtypeStruct((NA_HEADS, T_ALL, NA_DH), F32),
        compiler_params=_cparams(("parallel",)),
        name="attn_ctx",
    )(nq, nk, nv)


def _attn_lat_kernel(q_ref, k_ref, v_ref, kc_ref, vc_ref, tb_ref, _, o_ref, bias_s):
    @pl.when(pl.program_id(1) == 0)
    def _build_bias():
        bias_s[...] = jnp.full((DEC_SEQ, DEC_SEQ), -jnp.inf, F32)
        for r in range(GRID_ROWS):
            rs = min(max(r - NA_KH // 2, 0), GRID_ROWS - NA_KH)
            dr0 = rs - r + NA_WIN_H - 1
            bias_s[r * GRID_W:(r + 1) * GRID_W, rs * GRID_W:(rs + NA_KH) * GRID_W] = (
                tb_ref[0, 0, :, dr0 * GRID_W:(dr0 + NA_KH) * GRID_W])

    scale = NA_DH ** -0.5
    kb = k_ref[0].astype(BF16)
    vb = v_ref[0].astype(BF16)
    kc = kc_ref[0, 0, 0].astype(BF16)
    vc = vc_ref[0, 0, 0].astype(BF16)
    tq = NA_QBLOCK
    for qb in range(DEC_SEQ // tq):
        rows = slice(qb * tq, (qb + 1) * tq)
        qh = q_ref[0, rows, :].astype(BF16)
        s_loc = _bdot_nt(qh, kb) * scale + bias_s[rows, :]
        s_ctx = _bdot_nt(qh, kc) * scale
        o_ref[0, rows, :] = _softmax_pv([s_loc, s_ctx], [vb, vc])


def _attn_lat(l, nq, nk, nv, kc, vc, tb, prev):
    tok = pl.BlockSpec((1, DEC_SEQ, NA_DH), lambda h, b: (h, b + LAT_BLOCK0, 0))
    cache = pl.BlockSpec((1, 1, 1, PAST_LEN, NA_DH), lambda h, b: (b, l, h, 0, 0))
    return pl.pallas_call(
        _attn_lat_kernel,
        grid=(NA_HEADS, DEC_BATCH),
        in_specs=[tok, tok, tok, cache, cache,
                  pl.BlockSpec((1, 1, GRID_W, NA_REL_ROWS * GRID_W), lambda h, b: (l, h, 0, 0)),
                  _ANY],
        out_specs=tok,
        out_shape=jax.ShapeDtypeStruct((NA_HEADS, T_ALL, NA_DH), F32),
        input_output_aliases={6: 0},
        scratch_shapes=[pltpu.VMEM((DEC_SEQ, DEC_SEQ), F32)],
        compiler_params=_cparams(("arbitrary", "arbitrary")),
        name="attn_lat",
    )(nq, nk, nv, kc, vc, tb, prev)


def _na_tables(rpb):
    col = np.arange(GRID_W)
    col_start = np.clip(col - NA_WIN_W // 2, 0, GRID_W - NA_WIN_W)
    col_in = (col[None, :] >= col_start[:, None]) & (col[None, :] < col_start[:, None] + NA_WIN_W)
    col_idx = np.clip(col[None, :] - col[:, None] + NA_WIN_W - 1, 0, 2 * NA_WIN_W - 2)
    onehot = (col_idx[:, :, None] == np.arange(2 * NA_WIN_W - 1)[None, None, :]).astype(np.float32)
    tb = jnp.einsum('lhrd,qkd->lhqrk', rpb, jnp.asarray(onehot), precision=lax.Precision.HIGHEST)
    tb = jnp.where(jnp.asarray(col_in)[None, None, :, None, :], tb, -jnp.inf)
    return tb.reshape(rpb.shape[0], NA_HEADS, GRID_W, NA_REL_ROWS * GRID_W)


def _merge_kernel(x_ref, mod_ref, g_ref, ret_ref, s5y_ref, s5u_ref, gla_ref, na_ref,
                  s5d_ref, wglu_ref, bglu_ref, wbr_ref, wmg_ref, bmg_ref, wout_ref, o_ref):
    x = x_ref[...]
    mod = mod_ref[0, 0]
    hb = (_rms(x, g_ref[0, 0:1]) * (1.0 + mod[1:2]) + mod[0:1]).astype(BF16)

    y = s5y_ref[...] + s5d_ref[0] * s5u_ref[...]
    y = 0.5 * y * (1.0 + jnp.tanh(math.sqrt(2.0 / math.pi) * (y + 0.044715 * (y * y * y))))
    z = _bdot(y, wglu_ref[0]) + bglu_ref[0]
    s5_out = z[:, 0:BRANCH_W] * _sigmoid(z[:, BRANCH_W:2 * BRANCH_W])

    def gate(n):
        return _sigmoid(_bdot(hb, wmg_ref[0, :, n * D_MODEL:(n + 1) * D_MODEL])
                        + bmg_ref[0, :, n * D_MODEL:(n + 1) * D_MODEL])

    acc = gate(0) * _bdot(ret_ref[...], wbr_ref[0, 0])
    acc += gate(1) * _bdot(s5_out, wbr_ref[0, 1])
    acc += gate(2) * _bdot(gla_ref[...], wbr_ref[0, 2])
    up = _bdot(na_ref[0], wbr_ref[0, 3, 0:NA_DH, :])
    for hh in range(1, NA_HEADS):
        up += _bdot(na_ref[hh], wbr_ref[0, 3, hh * NA_DH:(hh + 1) * NA_DH, :])
    acc += gate(3) * up
    m = _bdot(acc, wout_ref[0])
    o_ref[...] = x + mod[2:3] * _rms(m, g_ref[0, 1:2])


def _merge(l, x, mod, g_norm, ret_o, s5_y, s5_u, gla_o, na_o, s5d, wglu, bglu, wbr, wmg, bmg, wout):
    tm = TOKEN_TILE
    tok = lambda w: pl.BlockSpec((tm, w), lambda i: (i, 0))
    return pl.pallas_call(
        _merge_kernel,
        grid=(T_ALL // tm,),
        in_specs=[tok(D_MODEL), _mod_spec(l), _layer_spec(l, 4, D_MODEL),
                  tok(256), tok(256), tok(256), tok(256),
                  pl.BlockSpec((NA_HEADS, tm, NA_DH), lambda i: (0, i, 0)),
                  _layer_spec(l, 1, 256), _layer_spec(l, 256, 512), _layer_spec(l, 1, 512),
                  _layer_spec(l, N_BRANCH, BRANCH_W, D_MODEL, single_buffer=True),
                  _layer_spec(l, D_MODEL, N_BRANCH * D_MODEL, single_buffer=True),
                  _layer_spec(l, 1, N_BRANCH * D_MODEL),
                  _layer_spec(l, D_MODEL, D_MODEL, single_buffer=True)],
        out_specs=tok(D_MODEL),
        out_shape=jax.ShapeDtypeStruct((T_ALL, D_MODEL), F32),
        compiler_params=_cparams(("parallel",)),
        name="merge",
    )(x, mod, g_norm, ret_o, s5_y, s5_u, gla_o, na_o, s5d, wglu, bglu, wbr, wmg, bmg, wout)


FF_TILE = 1024


def _mlp_kernel(x_ref, mod_ref, g_ref, w1_ref, w2_ref, *o_refs):
    x = x_ref[...]
    mod = mod_ref[0, 0]
    hb = (_rms(x, g_ref[0, 2:3]) * (1.0 + mod[4:5]) + mod[3:4]).astype(BF16)
    f = None
    for j in range(D_FF // FF_TILE):
        a = jnp.maximum(_bdot(hb, w1_ref[0, :, j * FF_TILE:(j + 1) * FF_TILE]), 0.0)
        part = _bdot(a * a, w2_ref[0, j * FF_TILE:(j + 1) * FF_TILE, :])
        f = part if f is None else f + part
    y = x + mod[5:6] * _rms(f, g_ref[0, 3:4])
    if len(o_refs) == 1:
        o_refs[0][...] = y
    else:
        ctx_tiles = T_CTX // TOKEN_TILE

        @pl.when(pl.program_id(0) < ctx_tiles)
        def _store_ctx():
            o_refs[0][...] = y

        @pl.when(pl.program_id(0) >= ctx_tiles)
        def _store_lat():
            o_refs[1][...] = y


def _mlp(l, x, mod, g_norm, w1, w2, *, split_out):
    tm = TOKEN_TILE
    tok = pl.BlockSpec((tm, D_MODEL), lambda i: (i, 0))
    if split_out:
        ctx_tiles = T_CTX // tm
        out_specs = [pl.BlockSpec((tm, D_MODEL), lambda i: (jnp.minimum(i, ctx_tiles - 1), 0)),
                     pl.BlockSpec((tm, D_MODEL), lambda i: (jnp.maximum(i - ctx_tiles, 0), 0))]
        out_shape = [jax.ShapeDtypeStruct((T_CTX, D_MODEL), F32), jax.ShapeDtypeStruct((T_LAT, D_MODEL), F32)]
    else:
        out_specs = tok
        out_shape = jax.ShapeDtypeStruct((T_ALL, D_MODEL), F32)
    return pl.pallas_call(
        _mlp_kernel,
        grid=(T_ALL // tm,),
        in_specs=[tok, _mod_spec(l), _layer_spec(l, 4, D_MODEL),
                  _layer_spec(l, D_MODEL, D_FF, single_buffer=True),
                  _layer_spec(l, D_FF, D_MODEL, single_buffer=True)],
        out_specs=out_specs,
        out_shape=out_shape,
        compiler_params=_cparams(("arbitrary",)),
        name="mlp",
    )(x, mod, g_norm, w1, w2)


def _rope_tables():
    half = RET_DK // 2
    nf = half // 2
    t = jnp.arange(DEC_SEQ)
    row = (t // GRID_W).astype(F32)
    col = (t % GRID_W).astype(F32)
    inv = ROPE_BASE ** (-jnp.arange(nf, dtype=F32) / nf)
    ang_r = row[:, None] * inv[None, :]
    ang_c = col[:, None] * inv[None, :]
    cos = jnp.concatenate([jnp.cos(ang_r)] * 2 + [jnp.cos(ang_c)] * 2, axis=1)
    sin = jnp.concatenate([-jnp.sin(ang_r), jnp.sin(ang_r), -jnp.sin(ang_c), jnp.sin(ang_c)], axis=1)
    return jnp.tile(cos, (1, RET_HEADS)), jnp.tile(sin, (1, RET_HEADS))


def _pack_w_in(w):
    offs = [0, 256, 512, 768, 1024, 1280, 1408, 1536, 1792, 2048, 2080, 2336, 2592, 2848]
    seg = lambda i: w[:, :, offs[i]:offs[i + 1]]
    order = [0, 1, 2, 3, 4, 5, 6, 7, 8, 10, 11, 12, 9]
    packed = jnp.concatenate([seg(i) for i in order], axis=2)
    return jnp.pad(packed, ((0, 0), (0, 0), (0, W_IN_PACKED - packed.shape[2]))).astype(BF16)


def _gla_state_in(st):
    eye = jnp.eye(GLA_HEADS, dtype=st.dtype)
    t = jnp.einsum('bldhkv,hg->bldhvgk', st, eye)
    return t.reshape(st.shape[0], st.shape[1], 2, GLA_HEADS * GLA_DV, GLA_HEADS * GLA_DK)


def kernel(x_prompt, x_sample, c, cache_na_k, cache_na_v, state_ret, state_s5, state_gla, c_ctx, w_ada, b_ada, g_norm, w_in, ret_log_decay, ret_gn, s5_lambda_re, s5_lambda_im, s5_log_dt, s5_b_re, s5_b_im, s5_c_re, s5_c_im, s5_d, s5_w_glu, s5_b_glu, gla_w_gate, gla_b_gate, gla_gn, na_rpb, w_branch, w_merge, b_merge, w_out, w_mlp1, w_mlp2):
    depth = w_in.shape[0]
    x = (x_prompt.reshape(T_CTX, D_MODEL), x_sample.reshape(T_LAT, D_MODEL))
    cc = jnp.concatenate([c_ctx[None], c, jnp.zeros((N_MOD_ROWS - 1 - DEC_BATCH, D_MODEL), F32)], axis=0)
    mod = _ada(cc, w_ada, b_ada).reshape(depth, N_MOD_ROWS, 6, D_MODEL)

    cos, sin = _rope_tables()
    w_in_p = _pack_w_in(w_in)
    ret_gn3, gla_gn3, s5_d3 = (a.reshape(depth, 1, BRANCH_W) for a in (ret_gn, gla_gn, s5_d))
    b_glu3 = s5_b_glu.reshape(depth, 1, 2 * BRANCH_W)
    b_mg3 = b_merge.reshape(depth, 1, N_BRANCH * D_MODEL)
    cache_k = cache_na_k.transpose(0, 1, 3, 2, 4)
    cache_v = cache_na_v.transpose(0, 1, 3, 2, 4)
    na_tb = _na_tables(na_rpb)
    gla_s0 = _gla_state_in(state_gla)
    s5_tables, s5_lam = _s5_tables(s5_lambda_re, s5_lambda_im, s5_log_dt, s5_b_re, s5_b_im,
                                   s5_c_re, s5_c_im)
    s5_x0 = state_s5.transpose(1, 2, 0, 3, 5, 4).reshape(depth, 2, DEC_BATCH, S5_GROUPS * S5_P2)

    ks_l, vs_l, ret_l, s5_l, gla_l = [], [], [], [], []
    for l in range(depth):
        proj = _inproj(l, x, mod, g_norm, w_in_p)
        if l == 0:
            x, proj = proj[0], proj[1:]
        ret, su, gqk, gv, gg, glr, nq, nk, nv = proj

        ret_o, st_ret = _retention(l, ret, ret_log_decay, ret_gn3, latent=False)
        ret_o, = _retention(l, ret, ret_log_decay, ret_gn3, latent=True, cos=cos, sin=sin, s0=state_ret,
                            prev=ret_o)

        s5_y, s5_fin = _s5(l, su, *s5_tables, s5_lam, s5_x0)

        gla_o, st_gla = _gla(l, gqk, gv, gg, glr, gla_w_gate, gla_b_gate, gla_gn3, latent=False)
        gla_o, = _gla(l, gqk, gv, gg, glr, gla_w_gate, gla_b_gate, gla_gn3, latent=True, s0=gla_s0,
                      prev=gla_o)

        na_o = _attn_ctx(nq, nk, nv)
        na_o = _attn_lat(l, nq, nk, nv, cache_k, cache_v, na_tb, na_o)

        x = _merge(l, x, mod, g_norm, ret_o, s5_y, su, gla_o, na_o,
                   s5_d3, s5_w_glu, b_glu3, w_branch, w_merge, b_mg3, w_out)
        x = _mlp(l, x, mod, g_norm, w_mlp1, w_mlp2, split_out=(l == depth - 1))

        ks_l.append(nk)
        vs_l.append(nv)
        ret_l.append(st_ret)
        s5_l.append(s5_fin)
        gla_l.append(st_gla)

    y_prompt = x[0].reshape(BATCH, SEQ, D_MODEL)
    y_sample = x[1].reshape(DEC_BATCH, DEC_SEQ, D_MODEL)

    def cache_out(per_layer):
        a = jnp.stack(per_layer, axis=0)[:, :, :T_CTX].reshape(depth, NA_HEADS, BATCH, SEQ, NA_DH)
        return a.transpose(2, 0, 3, 1, 4)

    s5_out = jnp.stack(s5_l, axis=0).reshape(depth, 2, BATCH, S5_GROUPS, 2, S5_STATE)
    return (y_prompt, y_sample, cache_out(ks_l), cache_out(vs_l), jnp.stack(ret_l, axis=1),
            s5_out.transpose(2, 0, 1, 3, 5, 4), jnp.stack(gla_l, axis=1))
```

```python
import functools
import math

import numpy as np
import jax
import jax.numpy as jnp
from jax import lax
from jax.experimental import pallas as pl
from jax.experimental.pallas import tpu as pltpu

F32 = jnp.float32
BF16 = jnp.bfloat16

D_MODEL = 1024
BATCH = 16
SEQ = 256
DEPTH = 4
DEC_BATCH = 4
DEC_SEQ = 1024
PAST_LEN = 256
GRID_W = 64
N_BRANCH = 4
BRANCH_W = 256
RET_HEADS = 4
RET_DK = 64
RET_DV = 64
S5_GROUPS = 16
S5_GROUP_CH = 16
S5_STATE = 64
GLA_HEADS = 4
GLA_DK = 32
GLA_DV = 64
GLA_RANK = 16
GLA_TAU = 16.0
NA_HEADS = 4
NA_DH = 64
NA_WIN_H = 8
NA_WIN_W = 16
D_FF = 4 * D_MODEL
ROPE_BASE = 10000.0
EPS = 1e-6

T_CTX = BATCH * SEQ
T_LAT = DEC_BATCH * DEC_SEQ
T_ALL = T_CTX + T_LAT
LAT_BLOCK0 = T_CTX // DEC_SEQ
N_MOD_ROWS = 8
TOKEN_TILE = 512
GLA_CHUNK = 64
GLA_BLOCK_CHUNKS = 4
S5_CHUNK = 16
RET_QBLOCK = 256
NA_QBLOCK = 256
GRID_ROWS = DEC_SEQ // GRID_W
NA_KH = min(NA_WIN_H, GRID_ROWS)
NA_REL_ROWS = 2 * NA_WIN_H - 1
VMEM_LIMIT = 56 * 1024 * 1024
W_IN_PACKED = 2944


def _cparams(sem):
    return pltpu.CompilerParams(dimension_semantics=sem, vmem_limit_bytes=VMEM_LIMIT)


def _bdot(a, b):
    return jnp.dot(a.astype(BF16), b.astype(BF16), preferred_element_type=F32)


def _bdot_nt(a, b):
    return lax.dot_general(a.astype(BF16), b.astype(BF16), (((1,), (1,)), ((), ())),
                           preferred_element_type=F32)


def _bdot_tn(a, b):
    return lax.dot_general(a.astype(BF16), b.astype(BF16), (((0,), (0,)), ((), ())),
                           preferred_element_type=F32)


def _split(a):
    hi = a.astype(BF16)
    lo = (a - hi.astype(F32)).astype(BF16)
    return hi, lo


def _dot3(a, b):
    ah, al = _split(a)
    bh, bl = _split(b)
    d = functools.partial(jnp.dot, preferred_element_type=F32)
    return d(ah, bh) + d(al, bh) + d(ah, bl)


def _sigmoid(x):
    return 1.0 / (1.0 + jnp.exp(-x))


def _silu(x):
    return x * _sigmoid(x)


def _rms(x, g):
    return x * lax.rsqrt(jnp.mean(x * x, axis=-1, keepdims=True) + EPS) * g


def _group_norm(o, g):
    mu = jnp.mean(o, axis=-1, keepdims=True)
    xc = o - mu
    return xc * lax.rsqrt(jnp.mean(xc * xc, axis=-1, keepdims=True) + EPS) * g


def _mod_row(i):
    ctx_tiles = T_CTX // TOKEN_TILE
    return jnp.where(i < ctx_tiles, 0, 1 + (i - ctx_tiles) // (DEC_SEQ // TOKEN_TILE))


def _mod_spec(l):
    return pl.BlockSpec((1, 1, 6, D_MODEL), lambda i: (l, _mod_row(i), 0, 0))


def _layer_spec(l, *shape, single_buffer=False):
    mode = pl.Buffered(1) if single_buffer else None
    return pl.BlockSpec((1,) + shape, lambda *_: (l,) + (0,) * len(shape), pipeline_mode=mode)


_ANY = pl.BlockSpec(memory_space=pl.ANY)


ADA_TILE = 1536


def _ada_kernel(c_ref, w_ref, b_ref, o_ref):
    a = _silu(c_ref[...])
    o_ref[0] = _bdot(a, w_ref[0]) + b_ref[0]


def _ada(cc, w_ada, b_ada):
    n = 6 * D_MODEL
    return pl.pallas_call(
        _ada_kernel,
        grid=(DEPTH, n // ADA_TILE),
        in_specs=[pl.BlockSpec((N_MOD_ROWS, D_MODEL), lambda l, j: (0, 0)),
                  pl.BlockSpec((1, D_MODEL, ADA_TILE), lambda l, j: (l, 0, j)),
                  pl.BlockSpec((1, 1, ADA_TILE), lambda l, j: (l, 0, j))],
        out_specs=pl.BlockSpec((1, N_MOD_ROWS, ADA_TILE), lambda l, j: (l, 0, j)),
        out_shape=jax.ShapeDtypeStruct((DEPTH, N_MOD_ROWS, n), F32),
        compiler_params=_cparams(("parallel", "parallel")),
        name="ada_mod",
    )(cc, w_ada, b_ada.reshape(DEPTH, 1, n))


def _piece_transpose(blocks, piece):
    x = list(blocks)
    n = len(x)
    d = n // 2
    while d >= 1:
        low = (piece & d) == 0
        for v in range(n):
            if v & d:
                continue
            a, b = x[v], x[v + d]
            x[v] = jnp.where(low, a, pltpu.roll(b, d * S5_GROUP_CH, 1))
            x[v + d] = jnp.where(low, pltpu.roll(a, 128 - d * S5_GROUP_CH, 1), b)
        d //= 2
    return x


def _inproj_kernel(*refs, first):
    if first:
        (xa_ref, xb_ref, mod_ref, g_ref, w_ref, x_out_ref,
         ret_ref, s5_ref, gqk_ref, gv_ref, gg_ref, glr_ref, nq_ref, nk_ref, nv_ref, su_s) = refs
        x = jnp.where(pl.program_id(0) < T_CTX // TOKEN_TILE, xa_ref[...], xb_ref[...])
        x_out_ref[...] = x
    else:
        (x_ref, mod_ref, g_ref, w_ref,
         ret_ref, s5_ref, gqk_ref, gv_ref, gg_ref, glr_ref, nq_ref, nk_ref, nv_ref, su_s) = refs
        x = x_ref[...]
    mod = mod_ref[0, 0]
    h = _rms(x, g_ref[0, 0:1]) * (1.0 + mod[1:2]) + mod[0:1]
    hb = h.astype(BF16)

    def proj(lo, hi):
        return jnp.dot(hb, w_ref[0, :, lo:hi], preferred_element_type=F32)

    ret_ref[...] = proj(0, 1024)
    su = proj(1024, 1280)
    rows = TOKEN_TILE // S5_CHUNK
    ng = S5_LANE_GROUPS
    piece = lax.broadcasted_iota(jnp.int32, (rows, 128), 1) // S5_GROUP_CH
    for lb in range(S5_GROUPS // ng):
        su_s[lb] = su[:, lb * 128:(lb + 1) * 128]
        for m in range(S5_CHUNK // ng):
            out = _piece_transpose(
                [su_s[lb, pl.ds(m * ng + jl, rows, stride=S5_CHUNK), :] for jl in range(ng)], piece)
            for g in range(ng):
                col = (lb * ng + g) * S5_W + m * 128
                s5_ref[:, col:col + 128] = out[g]
    gqk_ref[...] = proj(1280, 1536)
    gv_ref[...] = proj(1536, 1792)
    gg_ref[...] = proj(1792, 2048)
    for ref, lo in ((nq_ref, 2048), (nk_ref, 2304), (nv_ref, 2560)):
        r = proj(lo, lo + 256)
        for hh in range(NA_HEADS):
            ref[hh] = r[:, hh * NA_DH:(hh + 1) * NA_DH]
    glr_ref[...] = proj(2816, 2944)


def _inproj(l, xs, mod, g_norm, w_in_p):
    tm = TOKEN_TILE
    first = isinstance(xs, tuple)
    tok = lambda w: pl.BlockSpec((tm, w), lambda i: (i, 0))
    head = pl.BlockSpec((NA_HEADS, tm, NA_DH), lambda i: (0, i, 0))
    tshape = lambda w: jax.ShapeDtypeStruct((T_ALL, w), F32)
    hshape = jax.ShapeDtypeStruct((NA_HEADS, T_ALL, NA_DH), F32)
    ctx_tiles = T_CTX // tm
    if first:
        x_specs = [pl.BlockSpec((tm, D_MODEL), lambda i: (jnp.minimum(i, ctx_tiles - 1), 0)),
                   pl.BlockSpec((tm, D_MODEL), lambda i: (jnp.maximum(i - ctx_tiles, 0), 0))]
        x_args = list(xs)
    else:
        x_specs, x_args = [tok(D_MODEL)], [xs]
    return pl.pallas_call(
        functools.partial(_inproj_kernel, first=first),
        grid=(T_ALL // tm,),
        in_specs=x_specs + [_mod_spec(l), _layer_spec(l, 4, D_MODEL), _layer_spec(l, D_MODEL, W_IN_PACKED)],
        out_specs=([tok(D_MODEL)] if first else [])
        + [tok(1024), pl.BlockSpec((tm // S5_CHUNK, S5_GROUPS * S5_W), lambda i: (i, 0)),
           tok(256), tok(256), tok(256), tok(128), head, head, head],
        out_shape=([tshape(D_MODEL)] if first else [])
        + [tshape(1024), jax.ShapeDtypeStruct((S5_ROWS, S5_GROUPS * S5_W), F32),
           tshape(256), tshape(256), tshape(256), tshape(128), hshape, hshape, hshape],
        scratch_shapes=[pltpu.VMEM((S5_GROUPS // S5_LANE_GROUPS, tm, 128), F32)],
        compiler_params=_cparams(("parallel",)),
        name="in_proj",
    )(*x_args, mod, g_norm, w_in_p)


def _rope_rotate(x, lane):
    first = (lane % 32) < 16
    w = x.shape[-1]
    return jnp.where(first, pltpu.roll(x, w - 16, 1), pltpu.roll(x, 16, 1))


def _ret_kernel(ld_ref, ret_ref, gn_ref, *rest, layer, seq, latent):
    if latent:
        cos_ref, sin_ref, s0_ref, _, out_ref, dec_s = rest
    else:
        out_ref, st_ref, dec_s = rest
    tq = RET_QBLOCK
    nq = seq // tq
    width = dec_s.shape[-1]

    @pl.when(pl.program_id(0) == 0)
    def _build_decay():
        rel = (lax.broadcasted_iota(jnp.int32, (tq, width), 0) + (nq - 1) * tq
               - lax.broadcasted_iota(jnp.int32, (tq, width), 1)).astype(F32)
        for h in range(RET_HEADS):
            dec_s[h] = (jnp.where(rel >= 0, jnp.exp(ld_ref[layer, 0, h] * jnp.maximum(rel, 0.0)), 0.0)
                        + jnp.where(rel <= 0, jnp.exp(ld_ref[layer, 1, h] * jnp.maximum(-rel, 0.0)), 0.0))

    q = ret_ref[:, 0:256]
    k = ret_ref[:, 256:512]
    if latent:
        lane = lax.broadcasted_iota(jnp.int32, (seq, 256), 1)
        cos = cos_ref[...]
        sin = sin_ref[...]
        q = q * cos + _rope_rotate(q, lane) * sin
        k = k * cos + _rope_rotate(k, lane) * sin
    k = k * (RET_DK ** -0.5)
    pos_c = lax.broadcasted_iota(jnp.int32, (seq, 1), 0).astype(F32)
    for h in range(RET_HEADS):
        lgf = ld_ref[layer, 0, h]
        lgb = ld_ref[layer, 1, h]
        sl = slice(h * RET_DK, (h + 1) * RET_DK)
        qh = q[:, sl]
        kh = k[:, sl]
        vh = ret_ref[:, 512 + h * RET_DV:512 + (h + 1) * RET_DV]
        kb = kh.astype(BF16)
        vb = vh.astype(BF16)
        if latent:
            q_init = jnp.concatenate([qh * jnp.exp(lgf * (pos_c + 1.0)),
                                      qh * jnp.exp(lgb * (seq - pos_c))], axis=1)
            s_init = jnp.concatenate([s0_ref[0, 0, 0, h], s0_ref[0, 0, 1, h]], axis=0)
        for qb in range(nq):
            w0 = (nq - 1 - qb) * tq
            rows = slice(qb * tq, (qb + 1) * tq)
            s = _bdot_nt(qh[rows], kb) * dec_s[h, :, w0:w0 + seq]
            o = _bdot(s, vb)
            if latent:
                o = o + _bdot(q_init[rows], s_init)
            g = ret_ref[rows, 768 + h * RET_DV:768 + (h + 1) * RET_DV]
            out_ref[rows, sl] = _group_norm(o, gn_ref[0, :, sl]) * _silu(g)
        if not latent:
            st_ref[0, 0, h] = _bdot_tn(kh * jnp.exp(lgf * (seq - 1.0 - pos_c)), vb)
            st_ref[0, 1, h] = _bdot_tn(kh * jnp.exp(lgb * pos_c), vb)


def _retention(l, ret, ld, gn, *, latent, cos=None, sin=None, s0=None, prev=None):
    seq = DEC_SEQ if latent else SEQ
    nb = DEC_BATCH if latent else BATCH
    off = LAT_BLOCK0 if latent else 0
    in_specs = [pl.BlockSpec(memory_space=pltpu.SMEM),
                pl.BlockSpec((seq, 1024), lambda b: (b + off, 0)),
                _layer_spec(l, 1, 256)]
    args = [ld, ret, gn]
    out_specs = [pl.BlockSpec((seq, 256), lambda b: (b + off, 0))]
    out_shape = [jax.ShapeDtypeStruct((T_ALL, 256), F32)]
    aliases = {}
    if latent:
        in_specs += [pl.BlockSpec((seq, 256), lambda b: (0, 0)),
                     pl.BlockSpec((seq, 256), lambda b: (0, 0)),
                     pl.BlockSpec((1, 1, 2, RET_HEADS, RET_DK, RET_DV), lambda b: (b, l, 0, 0, 0, 0)),
                     _ANY]
        args += [cos, sin, s0, prev]
        aliases = {6: 0}
    else:
        out_specs.append(pl.BlockSpec((1, 2, RET_HEADS, RET_DK, RET_DV), lambda b: (b, 0, 0, 0, 0)))
        out_shape.append(jax.ShapeDtypeStruct((nb, 2, RET_HEADS, RET_DK, RET_DV), F32))
    return pl.pallas_call(
        functools.partial(_ret_kernel, layer=l, seq=seq, latent=latent),
        grid=(nb,),
        in_specs=in_specs, out_specs=out_specs, out_shape=out_shape,
        input_output_aliases=aliases,
        scratch_shapes=[pltpu.VMEM((RET_HEADS, RET_QBLOCK, 2 * seq - RET_QBLOCK), F32)],
        compiler_params=_cparams(("arbitrary",)),
        name="retention_lat" if latent else "retention_ctx",
    )(*args)


def _gla_kernel(gqk_ref, gv_ref, gg_ref, glr_ref, wg_ref, bg_ref, gn_ref, *rest, seq, latent):
    if latent:
        s0_ref, _, out_ref, gate_s, o_s, st_s, qst_s, ds_s, e_s = rest
    else:
        out_ref, st_ref, gate_s, o_s, st_s, qst_s, ds_s, e_s = rest
    c = GLA_CHUNK
    n = seq // c
    hk = GLA_HEADS * GLA_DK
    lr = glr_ref[...]
    for d in range(2):
        pre = _bdot(lr[:, d * GLA_RANK:(d + 1) * GLA_RANK], wg_ref[0, d]) + bg_ref[0, d:d + 1]
        gate_s[d] = (jnp.minimum(pre, 0.0) - jnp.log(1.0 + jnp.exp(-jnp.abs(pre)))) / GLA_TAU
        st_s[d] = s0_ref[0, 0, d] if latent else jnp.zeros((GLA_HEADS * GLA_DV, hk), F32)

    nc = GLA_BLOCK_CHUNKS
    rb = nc * c
    ti = lax.broadcasted_iota(jnp.int32, (rb, rb), 0)
    tj = lax.broadcasted_iota(jnp.int32, (rb, rb), 1)
    same = (ti // c) == (tj // c)
    ones = lambda m: (same & m).astype(BF16)
    tri = [ones(tj <= ti), ones(tj >= ti)]
    mid = [ones((tj % c) < c // 2), ones((tj % c) >= c // 2)]
    tot = ones(tj == tj)
    lane_k = lax.broadcasted_iota(jnp.int32, (c, hk), 1)
    head_mask = [(lane_k // GLA_DK) == h for h in range(GLA_HEADS)]
    ai = lax.broadcasted_iota(jnp.int32, (GLA_HEADS * c, c), 0) % c
    aj = lax.broadcasted_iota(jnp.int32, (GLA_HEADS * c, c), 1)
    keep = [aj <= ai, aj >= ai]
    sr = lax.broadcasted_iota(jnp.int32, (GLA_HEADS * GLA_DV, hk), 0) // GLA_DV
    sc = lax.broadcasted_iota(jnp.int32, (GLA_HEADS * GLA_DV, hk), 1) // GLA_DK
    diag = sr == sc
    scale = GLA_DK ** -0.5
    d32 = functools.partial(jnp.dot, preferred_element_type=F32)

    def rows_of(i, size):
        return pl.ds(i * size, size) if isinstance(i, int) else pl.ds(pl.multiple_of(i * size, size), size)

    def local(bi):
        rows = rows_of(bi, rb)
        q = gqk_ref[rows, 0:hk] * scale
        k = gqk_ref[rows, hk:2 * hk]
        v = gv_ref[rows, :].astype(BF16)
        q_att, k_att, k_st = [], [], []
        for d in range(2):
            gh, gl = _split(gate_s[d, rows, :])
            b = d32(tri[d], gh) + d32(tri[d], gl)
            b_mid = d32(mid[d], gh) + d32(mid[d], gl)
            b_end = d32(tot, gh) + d32(tot, gl)
            q_att.append(q * jnp.exp(b - b_mid))
            k_att.append(k * jnp.exp(b_mid - b))
            k_st.append((k * jnp.exp(b_end - b)).astype(BF16))
            qst_s[d, rows, :] = (q * jnp.exp(b)).astype(BF16)
            decay = jnp.exp(b_end)
            for cc in range(nc):
                e_s[d, bi * nc + cc] = decay[cc * c:cc * c + 8]
        pairs = [(d, cc) for d in range(2) for cc in range(nc)]
        att = {}
        for d, cc in pairs:
            r = slice(cc * c, (cc + 1) * c)
            qa = q_att[d][r]
            q_stack = jnp.concatenate([jnp.where(head_mask[h], qa, 0.0) for h in range(GLA_HEADS)], axis=0)
            att[d, cc] = jnp.where(keep[d], _bdot_nt(q_stack, k_att[d][r]), 0.0).astype(BF16)
        for d, cc in pairs:
            r = slice(cc * c, (cc + 1) * c)
            o = jnp.concatenate([d32(att[d, cc][h * c:(h + 1) * c], v[r, h * GLA_DV:(h + 1) * GLA_DV])
                                 for h in range(GLA_HEADS)], axis=1)
            o_s[d, rows_of(bi * nc + cc, c), :] = o
        for d, cc in pairs:
            r = slice(cc * c, (cc + 1) * c)
            ds_s[d, bi * nc + cc] = jnp.where(diag, lax.dot_general(
                v[r], k_st[d][r], (((0,), (0,)), ((), ())), preferred_element_type=F32), 0.0)

    def recur(ci, d):
        rows = rows_of(ci, c)
        st = st_s[d]
        o_s[d, rows, :] += _bdot_nt(qst_s[d, rows, :], st)
        st_s[d] = st * e_s[d, ci, 0:1] + ds_s[d, ci]

    def recur_body(i, carry):
        recur(i, 0)
        recur(n - 1 - i, 1)
        return carry

    if n == nc:
        local(0)
        for i in range(n):
            recur_body(i, 0)
    else:
        lax.fori_loop(0, n // nc, lambda i, carry: (local(i), carry)[1], 0)
        lax.fori_loop(0, n, recur_body, 0, unroll=2)

    o = o_s[0] + o_s[1]
    for h in range(GLA_HEADS):
        sl = slice(h * GLA_DV, (h + 1) * GLA_DV)
        out_ref[:, sl] = _group_norm(o[:, sl], gn_ref[0, :, sl]) * _silu(gg_ref[:, sl])
    if not latent:
        hv = GLA_HEADS * GLA_DV
        eye = (lax.broadcasted_iota(jnp.int32, (hv, hv), 0)
               == lax.broadcasted_iota(jnp.int32, (hv, hv), 1)).astype(BF16)
        tn = lambda a: lax.dot_general(a, eye, (((0,), (0,)), ((), ())), preferred_element_type=F32)
        for d in range(2):
            hi, lo = _split(st_s[d])
            lo2 = (st_s[d] - hi.astype(F32) - lo.astype(F32)).astype(BF16)
            s_all = tn(hi) + tn(lo) + tn(lo2)
            for h in range(GLA_HEADS):
                st_ref[0, d, h] = s_all[h * GLA_DK:(h + 1) * GLA_DK, h * GLA_DV:(h + 1) * GLA_DV]


def _gla(l, gqk, gv, gg, glr, wg, bg, gn, *, latent, s0=None, prev=None):
    seq = DEC_SEQ if latent else SEQ
    nb = DEC_BATCH if latent else BATCH
    off = LAT_BLOCK0 if latent else 0
    hk = GLA_HEADS * GLA_DK
    hv = GLA_HEADS * GLA_DV
    tok = lambda w: pl.BlockSpec((seq, w), lambda b: (b + off, 0))
    in_specs = [tok(256), tok(256), tok(256), tok(128),
                _layer_spec(l, 2, GLA_RANK, hk), _layer_spec(l, 2, hk), _layer_spec(l, 1, 256)]
    args = [gqk, gv, gg, glr, wg, bg, gn]
    out_specs = [tok(256)]
    out_shape = [jax.ShapeDtypeStruct((T_ALL, 256), F32)]
    aliases = {}
    if latent:
        in_specs += [pl.BlockSpec((1, 1, 2, hv, hk), lambda b: (b, l, 0, 0, 0)), _ANY]
        args += [s0, prev]
        aliases = {8: 0}
    else:
        out_specs.append(pl.BlockSpec((1, 2, GLA_HEADS, GLA_DK, GLA_DV), lambda b: (b, 0, 0, 0, 0)))
        out_shape.append(jax.ShapeDtypeStruct((nb, 2, GLA_HEADS, GLA_DK, GLA_DV), F32))
    return pl.pallas_call(
        functools.partial(_gla_kernel, seq=seq, latent=latent),
        grid=(nb,),
        in_specs=in_specs, out_specs=out_specs, out_shape=out_shape,
        input_output_aliases=aliases,
        scratch_shapes=[pltpu.VMEM((2, seq, hk), F32), pltpu.VMEM((2, seq, hv), F32),
                        pltpu.VMEM((2, hv, hk), F32), pltpu.VMEM((2, seq, hk), BF16),
                        pltpu.VMEM((2, seq // GLA_CHUNK, hv, hk), F32),
                        pltpu.VMEM((2, seq // GLA_CHUNK, 8, hk), F32)],
        compiler_params=_cparams(("parallel",)),
        name="gla_lat" if latent else "gla_ctx",
    )(*args)


S5_W = S5_CHUNK * S5_GROUP_CH
S5_P2 = 2 * S5_STATE
S5_ROWS = T_ALL // S5_CHUNK
S5_ROWS_CTX = T_CTX // S5_CHUNK
S5_LANE_GROUPS = 128 // S5_GROUP_CH


S5_TE_ROWS = S5_W + 4 * S5_P2


def _s5_toeplitz_kernel(cc_ref, wfr_ref, wb_ref, ff_ref, te_ref, ffb_ref):
    wfr = wfr_ref[0]
    wb = wb_ref[0]
    kf = _dot3(cc_ref[0, 0], wfr)
    kb = _dot3(cc_ref[0, 1], wb)
    lane = lax.broadcasted_iota(jnp.int32, (S5_GROUP_CH, S5_W), 1)
    blocks = []
    for i in range(S5_CHUNK):
        sf = ((i + 1 - S5_CHUNK) * S5_GROUP_CH) % S5_W
        fwd = jnp.where(lane < (i + 1) * S5_GROUP_CH, pltpu.roll(kf, sf, 1) if sf else kf, 0.0)
        bwd = jnp.where(lane >= i * S5_GROUP_CH, pltpu.roll(kb, i * S5_GROUP_CH, 1) if i else kb, 0.0)
        blocks.append(fwd + bwd)
    swapped = lambda t: jnp.concatenate([t[S5_STATE:], t[:S5_STATE]], axis=0)
    te_ref[0] = jnp.concatenate(blocks + [wfr, wb, swapped(wfr), swapped(wb)], axis=0).astype(BF16)
    ffb_ref[0] = ff_ref[0].astype(BF16)


def _s5_toeplitz(cc, wfr, wb, ff):
    n = cc.shape[0]
    tbl = pl.BlockSpec((1, S5_P2, S5_W), lambda i: (i, 0, 0))
    te_spec = pl.BlockSpec((1, S5_TE_ROWS, S5_W), lambda i: (i, 0, 0))
    ff_spec = pl.BlockSpec((1, 2 * S5_P2, S5_W), lambda i: (i, 0, 0))
    return pl.pallas_call(
        _s5_toeplitz_kernel,
        grid=(n,),
        in_specs=[pl.BlockSpec((1, 2, S5_GROUP_CH, S5_P2), lambda i: (i, 0, 0, 0)), tbl, tbl, ff_spec],
        out_specs=[te_spec, ff_spec],
        out_shape=[jax.ShapeDtypeStruct((n, S5_TE_ROWS, S5_W), BF16),
                   jax.ShapeDtypeStruct((n, 2 * S5_P2, S5_W), BF16)],
        compiler_params=_cparams(("parallel",)),
        name="s5_toeplitz",
    )(cc, wfr, wb, ff)


def _s5_kernel(u_ref, te_ref, ff_ref, d_ref, lam_ref, x0_ref, y_ref, fin_ref, xs_s, ps_s):
    C = S5_CHUNK
    ng = S5_LANE_GROUPS

    for g in range(ng):
        u = u_ref[:, g * S5_W:(g + 1) * S5_W]
        r = _bdot_nt(u, te_ref[0, g])
        y_ref[:, g * S5_W:(g + 1) * S5_W] = r[:, 0:S5_W] + d_ref[0, :, g * S5_W:(g + 1) * S5_W] * u
        for t in range(4):
            xs_s[t * ng + g] = r[:, S5_W + t * S5_P2:S5_W + (t + 1) * S5_P2]

    w = ng * S5_P2

    def carry(base, nseq, nchunks, init_f, init_b):
        a_f, b_f = lam_ref[0, 0, 0:1], lam_ref[0, 0, 1:2]
        a_b, b_b = lam_ref[0, 1, 0:1], lam_ref[0, 1, 1:2]
        load = lambda t, rows: jnp.concatenate([xs_s[t * ng + g, rows, :] for g in range(ng)], axis=1)

        def body(i, st):
            sf, tf, sb, tb = st
            rf = pl.ds(base + i, nseq, stride=nchunks)
            rb = pl.ds(base + (nchunks - 1 - i), nseq, stride=nchunks)
            for g in range(ng):
                ps_s[g, rf, :] = sf[:, g * S5_P2:(g + 1) * S5_P2]
                ps_s[ng + g, rb, :] = sb[:, g * S5_P2:(g + 1) * S5_P2]
            return (a_f * sf + b_f * tf + load(0, rf), a_f * tf - b_f * sf + load(2, rf),
                    a_b * sb + b_b * tb + load(1, rb), a_b * tb - b_b * sb + load(3, rb))

        first = (lax.broadcasted_iota(jnp.int32, (1, w), 1) % S5_P2) < S5_STATE
        swap = lambda s: jnp.where(first, pltpu.roll(s, w - S5_STATE, 1), pltpu.roll(s, S5_STATE, 1))
        return lax.fori_loop(0, nchunks, body, (init_f, swap(init_f), init_b, swap(init_b)))

    zeros = jnp.zeros((BATCH, w), F32)
    fin = carry(0, BATCH, SEQ // C, zeros, zeros)
    fin_ref[0] = fin[0]
    fin_ref[1] = fin[2]
    carry(S5_ROWS_CTX, DEC_BATCH, DEC_SEQ // C, x0_ref[0, 0], x0_ref[0, 1])

    for g in range(ng):
        p = jnp.concatenate([ps_s[g], ps_s[ng + g]], axis=1)
        y_ref[:, g * S5_W:(g + 1) * S5_W] += _bdot(p, ff_ref[0, g])


def _s5(l, su_rows, te, ff, d_rows, lam, x0):
    ng = S5_LANE_GROUPS
    w = ng * S5_P2
    rows = pl.BlockSpec((S5_ROWS, ng * S5_W), lambda i: (0, i))
    return pl.pallas_call(
        _s5_kernel,
        grid=(S5_GROUPS // ng,),
        in_specs=[rows,
                  pl.BlockSpec((1, ng, S5_TE_ROWS, S5_W), lambda i: (l, i, 0, 0)),
                  pl.BlockSpec((1, ng, 2 * S5_P2, S5_W), lambda i: (l, i, 0, 0)),
                  pl.BlockSpec((1, 1, ng * S5_W), lambda i: (l, 0, i)),
                  pl.BlockSpec((1, 2, 2, w), lambda i: (l, 0, 0, i)),
                  pl.BlockSpec((1, 2, DEC_BATCH, w), lambda i: (l, 0, 0, i))],
        out_specs=[rows, pl.BlockSpec((2, BATCH, w), lambda i: (0, 0, i))],
        out_shape=[jax.ShapeDtypeStruct((S5_ROWS, S5_GROUPS * S5_W), F32),
                   jax.ShapeDtypeStruct((2, BATCH, S5_GROUPS * S5_P2), F32)],
        scratch_shapes=[pltpu.VMEM((4 * ng, S5_ROWS, S5_P2), F32), pltpu.VMEM((2 * ng, S5_ROWS, S5_P2), F32)],
        compiler_params=_cparams(("parallel",)),
        name="s5_scan",
    )(su_rows, te, ff, d_rows, lam, x0)


def _s5_tables(lam_re, lam_im, log_dt, b_re, b_im, c_re, c_im):
    C, G, P, H = S5_CHUNK, S5_GROUPS, S5_STATE, S5_GROUP_CH
    L = lam_re.shape[0]
    dt = jnp.exp(log_dt)[..., None]
    ar, ai = lam_re * dt, lam_im * dt

    steps = jnp.arange(C + 1, dtype=F32)
    mag = jnp.exp(ar[..., None] * steps)
    pw_re, pw_im = mag * jnp.cos(ai[..., None] * steps), mag * jnp.sin(ai[..., None] * steps)
    exact = functools.partial(jnp.einsum, precision=lax.Precision.HIGHEST)
    tau_np = np.arange(C * H) // H

    def power(d, t):
        sel = jnp.asarray((np.arange(C + 1)[:, None] == np.asarray(t)[None, :]).astype(np.float32))
        return exact('lgpt,tn->lgpn', pw_re[:, d], sel), exact('lgpt,tn->lgpn', pw_im[:, d], sel)

    lr, li = pw_re[..., 1], pw_im[..., 1]
    den = lam_re * lam_re + lam_im * lam_im
    qr = ((lr - 1.0) * lam_re + li * lam_im) / den
    qi = (li * lam_re - (lr - 1.0) * lam_im) / den
    bbr = qr[..., None] * b_re - qi[..., None] * b_im
    bbi = qr[..., None] * b_im + qi[..., None] * b_re
    chan = jnp.asarray((np.arange(H)[:, None] == (np.arange(C * H) % H)[None, :]).astype(np.float32))
    lanes = lambda a: exact('ldgph,hn->ldgpn', a, chan)
    bbr, bbi = lanes(bbr), lanes(bbi)
    c_t = lambda a: exact('ldghp,hn->ldgpn', a, chan)
    ctr, cti = c_t(c_re), c_t(c_im)

    def w_of(d, t):
        pr, pi = power(d, t)
        return jnp.concatenate([pr * bbr[:, d] - pi * bbi[:, d], pr * bbi[:, d] + pi * bbr[:, d]], axis=-2)

    def f_of(d, t):
        pr, pi = power(d, t)
        return jnp.concatenate([ctr[:, d] * pr - cti[:, d] * pi, -(ctr[:, d] * pi + cti[:, d] * pr)], axis=-2)

    wfr = w_of(0, (C - 1) - tau_np).reshape(L * G, 2 * P, C * H)
    wb = w_of(1, tau_np).reshape(L * G, 2 * P, C * H)
    cc = jnp.concatenate([c_re, -c_im], axis=-1).transpose(0, 2, 1, 3, 4).reshape(L * G, 2, H, 2 * P)
    ff = jnp.concatenate([f_of(0, tau_np + 1), f_of(1, C - tau_np)], axis=-2)
    tables = _s5_toeplitz(cc, wfr, wb, ff.reshape(L * G, 4 * P, C * H))
    tables = [t.reshape((L, G) + t.shape[1:]) for t in tables]
    cr, ci = pw_re[..., C], pw_im[..., C]
    a = jnp.concatenate([cr, cr], axis=-1).reshape(L, 2, 1, G * 2 * P)
    b = jnp.concatenate([-ci, ci], axis=-1).reshape(L, 2, 1, G * 2 * P)
    return tables, jnp.concatenate([a, b], axis=2)


def _softmax_pv(s_parts, v_parts):
    m = s_parts[0].max(axis=-1, keepdims=True)
    for s in s_parts[1:]:
        m = jnp.maximum(m, s.max(axis=-1, keepdims=True))
    o = None
    l = None
    for s, v in zip(s_parts, v_parts):
        p = jnp.exp(s - m)
        pl_ = p.sum(axis=-1, keepdims=True)
        po = _bdot(p, v)
        o = po if o is None else o + po
        l = pl_ if l is None else l + pl_
    return o / l


def _attn_ctx_kernel(q_ref, k_ref, v_ref, o_ref):
    scale = NA_DH ** -0.5
    for h in range(NA_HEADS):
        s = _bdot_nt(q_ref[h], k_ref[h]) * scale
        o_ref[h] = _softmax_pv([s], [v_ref[h]])


def _attn_ctx(nq, nk, nv):
    spec = pl.BlockSpec((NA_HEADS, SEQ, NA_DH), lambda b: (0, b, 0))
    return pl.pallas_call(
        _attn_ctx_kernel,
        grid=(BATCH,),
        in_specs=[spec, spec, spec],
        out_specs=spec,
        out_shape=jax.ShapeDtypeStruct((NA_HEADS, T_ALL, NA_DH), F32),
        compiler_params=_cparams(("parallel",)),
        name="attn_ctx",
    )(nq, nk, nv)


def _attn_lat_kernel(q_ref, k_ref, v_ref, kc_ref, vc_ref, tb_ref, _, o_ref, bias_s):
    @pl.when(pl.program_id(1) == 0)
    def _build_bias():
        bias_s[...] = jnp.full((DEC_SEQ, DEC_SEQ), -jnp.inf, F32)
        for r in range(GRID_ROWS):
            rs = min(max(r - NA_KH // 2, 0), GRID_ROWS - NA_KH)
            dr0 = rs - r + NA_WIN_H - 1
            bias_s[r * GRID_W:(r + 1) * GRID_W, rs * GRID_W:(rs + NA_KH) * GRID_W] = (
                tb_ref[0, 0, :, dr0 * GRID_W:(dr0 + NA_KH) * GRID_W])

    scale = NA_DH ** -0.5
    kb = k_ref[0].astype(BF16)
    vb = v_ref[0].astype(BF16)
    kc = kc_ref[0, 0, 0].astype(BF16)
    vc = vc_ref[0, 0, 0].astype(BF16)
    tq = NA_QBLOCK
    for qb in range(DEC_SEQ // tq):
        rows = slice(qb * tq, (qb + 1) * tq)
        qh = q_ref[0, rows, :].astype(BF16)
        s_loc = _bdot_nt(qh, kb) * scale + bias_s[rows, :]
        s_ctx = _bdot_nt(qh, kc) * scale
        o_ref[0, rows, :] = _softmax_pv([s_loc, s_ctx], [vb, vc])


def _attn_lat(l, nq, nk, nv, kc, vc, tb, prev):
    tok = pl.BlockSpec((1, DEC_SEQ, NA_DH), lambda h, b: (h, b + LAT_BLOCK0, 0))
    cache = pl.BlockSpec((1, 1, 1, PAST_LEN, NA_DH), lambda h, b: (b, l, h, 0, 0))
    return pl.pallas_call(
        _attn_lat_kernel,
        grid=(NA_HEADS, DEC_BATCH),
        in_specs=[tok, tok, tok, cache, cache,
                  pl.BlockSpec((1, 1, GRID_W, NA_REL_ROWS * GRID_W), lambda h, b: (l, h, 0, 0)),
                  _ANY],
        out_specs=tok,
        out_shape=jax.ShapeDtypeStruct((NA_HEADS, T_ALL, NA_DH), F32),
        input_output_aliases={6: 0},
        scratch_shapes=[pltpu.VMEM((DEC_SEQ, DEC_SEQ), F32)],
        compiler_params=_cparams(("arbitrary", "arbitrary")),
        name="attn_lat",
    )(nq, nk, nv, kc, vc, tb, prev)


def _na_tables(rpb):
    col = np.arange(GRID_W)
    col_start = np.clip(col - NA_WIN_W // 2, 0, GRID_W - NA_WIN_W)
    col_in = (col[None, :] >= col_start[:, None]) & (col[None, :] < col_start[:, None] + NA_WIN_W)
    col_idx = np.clip(col[None, :] - col[:, None] + NA_WIN_W - 1, 0, 2 * NA_WIN_W - 2)
    onehot = (col_idx[:, :, None] == np.arange(2 * NA_WIN_W - 1)[None, None, :]).astype(np.float32)
    tb = jnp.einsum('lhrd,qkd->lhqrk', rpb, jnp.asarray(onehot), precision=lax.Precision.HIGHEST)
    tb = jnp.where(jnp.asarray(col_in)[None, None, :, None, :], tb, -jnp.inf)
    return tb.reshape(rpb.shape[0], NA_HEADS, GRID_W, NA_REL_ROWS * GRID_W)


def _merge_kernel(x_ref, mod_ref, g_ref, ret_ref, s5y_ref, gla_ref, na_ref,
                  wglu_ref, bglu_ref, wbr_ref, wmg_ref, bmg_ref, wout_ref, o_ref, y_s):
    x = x_ref[...]
    mod = mod_ref[0, 0]
    hb = (_rms(x, g_ref[0, 0:1]) * (1.0 + mod[1:2]) + mod[0:1]).astype(BF16)

    rows = TOKEN_TILE // S5_CHUNK
    ng = S5_LANE_GROUPS
    piece = lax.broadcasted_iota(jnp.int32, (rows, 128), 1) // S5_GROUP_CH
    for lb in range(S5_GROUPS // ng):
        for m in range(S5_CHUNK // ng):
            cols = [(lb * ng + g) * S5_W + m * 128 for g in range(ng)]
            out = _piece_transpose([s5y_ref[:, c0:c0 + 128] for c0 in cols], piece)
            for il in range(ng):
                y_s[lb, pl.ds(m * ng + il, rows, stride=S5_CHUNK), :] = out[il]
    y = jnp.concatenate([y_s[lb] for lb in range(S5_GROUPS // ng)], axis=1)
    y = 0.5 * y * (1.0 + jnp.tanh(math.sqrt(2.0 / math.pi) * (y + 0.044715 * (y * y * y))))
    z = _bdot(y, wglu_ref[0]) + bglu_ref[0]
    s5_out = z[:, 0:BRANCH_W] * _sigmoid(z[:, BRANCH_W:2 * BRANCH_W])

    def gate(n):
        return _sigmoid(_bdot(hb, wmg_ref[0, :, n * D_MODEL:(n + 1) * D_MODEL])
                        + bmg_ref[0, :, n * D_MODEL:(n + 1) * D_MODEL])

    acc = gate(0) * _bdot(ret_ref[...], wbr_ref[0, 0])
    acc += gate(1) * _bdot(s5_out, wbr_ref[0, 1])
    acc += gate(2) * _bdot(gla_ref[...], wbr_ref[0, 2])
    na = jnp.concatenate([na_ref[hh].astype(BF16) for hh in range(NA_HEADS)], axis=1)
    acc += gate(3) * _bdot(na, wbr_ref[0, 3])
    m = _bdot(acc, wout_ref[0])
    o_ref[...] = x + mod[2:3] * _rms(m, g_ref[0, 1:2])


def _merge(l, x, mod, g_norm, ret_o, s5_y, gla_o, na_o, wglu, bglu, wbr, wmg, bmg, wout):
    tm = TOKEN_TILE
    tok = lambda w: pl.BlockSpec((tm, w), lambda i: (i, 0))
    return pl.pallas_call(
        _merge_kernel,
        grid=(T_ALL // tm,),
        in_specs=[tok(D_MODEL), _mod_spec(l), _layer_spec(l, 4, D_MODEL),
                  tok(256), pl.BlockSpec((tm // S5_CHUNK, S5_GROUPS * S5_W), lambda i: (i, 0)), tok(256),
                  pl.BlockSpec((NA_HEADS, tm, NA_DH), lambda i: (0, i, 0)),
                  _layer_spec(l, 256, 512), _layer_spec(l, 1, 512),
                  _layer_spec(l, N_BRANCH, BRANCH_W, D_MODEL, single_buffer=True),
                  _layer_spec(l, D_MODEL, N_BRANCH * D_MODEL, single_buffer=True),
                  _layer_spec(l, 1, N_BRANCH * D_MODEL),
                  _layer_spec(l, D_MODEL, D_MODEL, single_buffer=True)],
        out_specs=tok(D_MODEL),
        out_shape=jax.ShapeDtypeStruct((T_ALL, D_MODEL), F32),
        scratch_shapes=[pltpu.VMEM((S5_GROUPS // S5_LANE_GROUPS, tm, 128), F32)],
        compiler_params=_cparams(("parallel",)),
        name="merge",
    )(x, mod, g_norm, ret_o, s5_y, gla_o, na_o, wglu, bglu, wbr, wmg, bmg, wout)


FF_TILE = 1024


def _mlp_kernel(x_ref, mod_ref, g_ref, w1_ref, w2_ref, *o_refs):
    x = x_ref[...]
    mod = mod_ref[0, 0]
    hb = (_rms(x, g_ref[0, 2:3]) * (1.0 + mod[4:5]) + mod[3:4]).astype(BF16)
    f = None
    for j in range(D_FF // FF_TILE):
        a = jnp.maximum(_bdot(hb, w1_ref[0, :, j * FF_TILE:(j + 1) * FF_TILE]), 0.0)
        part = _bdot(a * a, w2_ref[0, j * FF_TILE:(j + 1) * FF_TILE, :])
        f = part if f is None else f + part
    y = x + mod[5:6] * _rms(f, g_ref[0, 3:4])
    if len(o_refs) == 1:
        o_refs[0][...] = y
    else:
        ctx_tiles = T_CTX // TOKEN_TILE

        @pl.when(pl.program_id(0) < ctx_tiles)
        def _store_ctx():
            o_refs[0][...] = y

        @pl.when(pl.program_id(0) >= ctx_tiles)
        def _store_lat():
            o_refs[1][...] = y


def _mlp(l, x, mod, g_norm, w1, w2, *, split_out):
    tm = TOKEN_TILE
    tok = pl.BlockSpec((tm, D_MODEL), lambda i: (i, 0))
    if split_out:
        ctx_tiles = T_CTX // tm
        out_specs = [pl.BlockSpec((tm, D_MODEL), lambda i: (jnp.minimum(i, ctx_tiles - 1), 0)),
                     pl.BlockSpec((tm, D_MODEL), lambda i: (jnp.maximum(i - ctx_tiles, 0), 0))]
        out_shape = [jax.ShapeDtypeStruct((T_CTX, D_MODEL), F32), jax.ShapeDtypeStruct((T_LAT, D_MODEL), F32)]
    else:
        out_specs = tok
        out_shape = jax.ShapeDtypeStruct((T_ALL, D_MODEL), F32)
    return pl.pallas_call(
        _mlp_kernel,
        grid=(T_ALL // tm,),
        in_specs=[tok, _mod_spec(l), _layer_spec(l, 4, D_MODEL),
                  _layer_spec(l, D_MODEL, D_FF, single_buffer=True),
                  _layer_spec(l, D_FF, D_MODEL, single_buffer=True)],
        out_specs=out_specs,
        out_shape=out_shape,
        compiler_params=_cparams(("arbitrary",)),
        name="mlp",
    )(x, mod, g_norm, w1, w2)


def _rope_tables():
    half = RET_DK // 2
    nf = half // 2
    t = jnp.arange(DEC_SEQ)
    row = (t // GRID_W).astype(F32)
    col = (t % GRID_W).astype(F32)
    inv = ROPE_BASE ** (-jnp.arange(nf, dtype=F32) / nf)
    ang_r = row[:, None] * inv[None, :]
    ang_c = col[:, None] * inv[None, :]
    cos = jnp.concatenate([jnp.cos(ang_r)] * 2 + [jnp.cos(ang_c)] * 2, axis=1)
    sin = jnp.concatenate([-jnp.sin(ang_r), jnp.sin(ang_r), -jnp.sin(ang_c), jnp.sin(ang_c)], axis=1)
    return jnp.tile(cos, (1, RET_HEADS)), jnp.tile(sin, (1, RET_HEADS))


def _pack_w_in(w):
    offs = [0, 256, 512, 768, 1024, 1280, 1408, 1536, 1792, 2048, 2080, 2336, 2592, 2848]
    seg = lambda i: w[:, :, offs[i]:offs[i + 1]]
    order = [0, 1, 2, 3, 4, 5, 6, 7, 8, 10, 11, 12, 9]
    packed = jnp.concatenate([seg(i) for i in order], axis=2)
    return jnp.pad(packed, ((0, 0), (0, 0), (0, W_IN_PACKED - packed.shape[2]))).astype(BF16)


def _gla_state_in(st):
    eye = jnp.eye(GLA_HEADS, dtype=st.dtype)
    t = jnp.einsum('bldhkv,hg->bldhvgk', st, eye)
    return t.reshape(st.shape[0], st.shape[1], 2, GLA_HEADS * GLA_DV, GLA_HEADS * GLA_DK)


def kernel(x_prompt, x_sample, c, cache_na_k, cache_na_v, state_ret, state_s5, state_gla, c_ctx, w_ada, b_ada, g_norm, w_in, ret_log_decay, ret_gn, s5_lambda_re, s5_lambda_im, s5_log_dt, s5_b_re, s5_b_im, s5_c_re, s5_c_im, s5_d, s5_w_glu, s5_b_glu, gla_w_gate, gla_b_gate, gla_gn, na_rpb, w_branch, w_merge, b_merge, w_out, w_mlp1, w_mlp2):
    depth = w_in.shape[0]
    x = (x_prompt.reshape(T_CTX, D_MODEL), x_sample.reshape(T_LAT, D_MODEL))
    cc = jnp.concatenate([c_ctx[None], c, jnp.zeros((N_MOD_ROWS - 1 - DEC_BATCH, D_MODEL), F32)], axis=0)
    mod = _ada(cc, w_ada, b_ada).reshape(depth, N_MOD_ROWS, 6, D_MODEL)

    cos, sin = _rope_tables()
    w_in_p = _pack_w_in(w_in)
    ret_gn3, gla_gn3 = (a.reshape(depth, 1, BRANCH_W) for a in (ret_gn, gla_gn))
    s5_d_rows = jnp.broadcast_to(s5_d.reshape(depth, S5_GROUPS, 1, S5_GROUP_CH),
                                 (depth, S5_GROUPS, S5_CHUNK, S5_GROUP_CH)).reshape(depth, 1, S5_GROUPS * S5_W)
    b_glu3 = s5_b_glu.reshape(depth, 1, 2 * BRANCH_W)
    b_mg3 = b_merge.reshape(depth, 1, N_BRANCH * D_MODEL)
    cache_k = cache_na_k.transpose(0, 1, 3, 2, 4)
    cache_v = cache_na_v.transpose(0, 1, 3, 2, 4)
    na_tb = _na_tables(na_rpb)
    gla_s0 = _gla_state_in(state_gla)
    s5_tables, s5_lam = _s5_tables(s5_lambda_re, s5_lambda_im, s5_log_dt, s5_b_re, s5_b_im,
                                   s5_c_re, s5_c_im)
    s5_x0 = state_s5.transpose(1, 2, 0, 3, 5, 4).reshape(depth, 2, DEC_BATCH, S5_GROUPS * S5_P2)

    ks_l, vs_l, ret_l, s5_l, gla_l = [], [], [], [], []
    for l in range(depth):
        proj = _inproj(l, x, mod, g_norm, w_in_p)
        if l == 0:
            x, proj = proj[0], proj[1:]
        ret, su, gqk, gv, gg, glr, nq, nk, nv = proj

        ret_o, st_ret = _retention(l, ret, ret_log_decay, ret_gn3, latent=False)
        ret_o, = _retention(l, ret, ret_log_decay, ret_gn3, latent=True, cos=cos, sin=sin, s0=state_ret,
                            prev=ret_o)

        s5_y, s5_fin = _s5(l, su, *s5_tables, s5_d_rows, s5_lam, s5_x0)

        gla_o, st_gla = _gla(l, gqk, gv, gg, glr, gla_w_gate, gla_b_gate, gla_gn3, latent=False)
        gla_o, = _gla(l, gqk, gv, gg, glr, gla_w_gate, gla_b_gate, gla_gn3, latent=True, s0=gla_s0,
                      prev=gla_o)

        na_o = _attn_ctx(nq, nk, nv)
        na_o = _attn_lat(l, nq, nk, nv, cache_k, cache_v, na_tb, na_o)

        x = _merge(l, x, mod, g_norm, ret_o, s5_y, gla_o, na_o,
                   s5_w_glu, b_glu3, w_branch, w_merge, b_mg3, w_out)
        x = _mlp(l, x, mod, g_norm, w_mlp1, w_mlp2, split_out=(l == depth - 1))

        ks_l.append(nk)
        vs_l.append(nv)
        ret_l.append(st_ret)
        s5_l.append(s5_fin)
        gla_l.append(st_gla)

    y_prompt = x[0].reshape(BATCH, SEQ, D_MODEL)
    y_sample = x[1].reshape(DEC_BATCH, DEC_SEQ, D_MODEL)

    def cache_out(per_layer):
        a = jnp.stack(per_layer, axis=0)[:, :, :T_CTX].reshape(depth, NA_HEADS, BATCH, SEQ, NA_DH)
        return a.transpose(2, 0, 3, 1, 4)

    s5_out = jnp.stack(s5_l, axis=0).reshape(depth, 2, BATCH, S5_GROUPS, 2, S5_STATE)
    return (y_prompt, y_sample, cache_out(ks_l), cache_out(vs_l), jnp.stack(ret_l, axis=1),
            s5_out.transpose(2, 0, 1, 3, 5, 4), jnp.stack(gla_l, axis=1))
```

```python
import functools
import math

import numpy as np
import jax
import jax.numpy as jnp
from jax import lax
from jax.experimental import pallas as pl
from jax.experimental.pallas import tpu as pltpu

F32 = jnp.float32
BF16 = jnp.bfloat16

D_MODEL = 1024
BATCH = 16
SEQ = 256
DEPTH = 4
DEC_BATCH = 4
DEC_SEQ = 1024
PAST_LEN = 256
GRID_W = 64
N_BRANCH = 4
BRANCH_W = 256
RET_HEADS = 4
RET_DK = 64
RET_DV = 64
S5_GROUPS = 16
S5_GROUP_CH = 16
S5_STATE = 64
GLA_HEADS = 4
GLA_DK = 32
GLA_DV = 64
GLA_RANK = 16
GLA_TAU = 16.0
NA_HEADS = 4
NA_DH = 64
NA_WIN_H = 8
NA_WIN_W = 16
D_FF = 4 * D_MODEL
ROPE_BASE = 10000.0
EPS = 1e-6

T_CTX = BATCH * SEQ
T_LAT = DEC_BATCH * DEC_SEQ
T_ALL = T_CTX + T_LAT
LAT_BLOCK0 = T_CTX // DEC_SEQ
N_MOD_ROWS = 8
TOKEN_TILE = 512
CTX_SEQS_PER_STEP = DEC_SEQ // SEQ
GLA_CHUNK = 64
GLA_BLOCK_CHUNKS = 4
S5_CHUNK = 16
RET_QBLOCK = 256
NA_QBLOCK = 256
GRID_ROWS = DEC_SEQ // GRID_W
NA_KH = min(NA_WIN_H, GRID_ROWS)
NA_REL_ROWS = 2 * NA_WIN_H - 1
VMEM_LIMIT = 56 * 1024 * 1024
W_IN_PACKED = 2944


def _cparams(sem):
    return pltpu.CompilerParams(dimension_semantics=sem, vmem_limit_bytes=VMEM_LIMIT)


def _bdot(a, b):
    return jnp.dot(a.astype(BF16), b.astype(BF16), preferred_element_type=F32)


def _bdot_nt(a, b):
    return lax.dot_general(a.astype(BF16), b.astype(BF16), (((1,), (1,)), ((), ())),
                           preferred_element_type=F32)


def _bdot_tn(a, b):
    return lax.dot_general(a.astype(BF16), b.astype(BF16), (((0,), (0,)), ((), ())),
                           preferred_element_type=F32)


def _split(a):
    hi = a.astype(BF16)
    lo = (a - hi.astype(F32)).astype(BF16)
    return hi, lo


def _dot3(a, b):
    ah, al = _split(a)
    bh, bl = _split(b)
    d = functools.partial(jnp.dot, preferred_element_type=F32)
    return d(ah, bh) + d(al, bh) + d(ah, bl)


def _sigmoid(x):
    return 0.5 * jnp.tanh(0.5 * x) + 0.5


def _silu(x):
    return x * _sigmoid(x)


def _rms(x, g):
    return x * lax.rsqrt(jnp.mean(x * x, axis=-1, keepdims=True) + EPS) * g


def _group_norm(o, g):
    mu = jnp.mean(o, axis=-1, keepdims=True)
    xc = o - mu
    return xc * lax.rsqrt(jnp.mean(xc * xc, axis=-1, keepdims=True) + EPS) * g


def _mod_row(i):
    ctx_tiles = T_CTX // TOKEN_TILE
    return jnp.where(i < ctx_tiles, 0, 1 + (i - ctx_tiles) // (DEC_SEQ // TOKEN_TILE))


def _mod_spec(l):
    return pl.BlockSpec((1, 1, 6, D_MODEL), lambda i: (l, _mod_row(i), 0, 0))


def _layer_spec(l, *shape, single_buffer=False):
    mode = pl.Buffered(1) if single_buffer else None
    return pl.BlockSpec((1,) + shape, lambda *_: (l,) + (0,) * len(shape), pipeline_mode=mode)


_ANY = pl.BlockSpec(memory_space=pl.ANY)


ADA_TILE = 1536


def _ada_kernel(c_ref, w_ref, b_ref, o_ref):
    a = _silu(c_ref[...])
    o_ref[0] = _bdot(a, w_ref[0]) + b_ref[0]


def _ada(cc, w_ada, b_ada):
    n = 6 * D_MODEL
    return pl.pallas_call(
        _ada_kernel,
        grid=(DEPTH, n // ADA_TILE),
        in_specs=[pl.BlockSpec((N_MOD_ROWS, D_MODEL), lambda l, j: (0, 0)),
                  pl.BlockSpec((1, D_MODEL, ADA_TILE), lambda l, j: (l, 0, j)),
                  pl.BlockSpec((1, 1, ADA_TILE), lambda l, j: (l, 0, j))],
        out_specs=pl.BlockSpec((1, N_MOD_ROWS, ADA_TILE), lambda l, j: (l, 0, j)),
        out_shape=jax.ShapeDtypeStruct((DEPTH, N_MOD_ROWS, n), F32),
        compiler_params=_cparams(("parallel", "parallel")),
        name="ada_mod",
    )(cc, w_ada, b_ada.reshape(DEPTH, 1, n))


def _piece_transpose(blocks, piece):
    x = list(blocks)
    n = len(x)
    d = n // 2
    while d >= 1:
        low = (piece & d) == 0
        for v in range(n):
            if v & d:
                continue
            a, b = x[v], x[v + d]
            x[v] = jnp.where(low, a, pltpu.roll(b, d * S5_GROUP_CH, 1))
            x[v + d] = jnp.where(low, pltpu.roll(a, 128 - d * S5_GROUP_CH, 1), b)
        d //= 2
    return x


def _inproj_kernel(*refs, first):
    if first:
        (xa_ref, xb_ref, mod_ref, g_ref, w_ref, x_out_ref,
         ret_ref, s5_ref, gqk_ref, gv_ref, gg_ref, glr_ref, nq_ref, nk_ref, nv_ref, su_s) = refs
        x = jnp.where(pl.program_id(0) < T_CTX // TOKEN_TILE, xa_ref[...], xb_ref[...])
        x_out_ref[...] = x
    else:
        (x_ref, mod_ref, g_ref, w_ref,
         ret_ref, s5_ref, gqk_ref, gv_ref, gg_ref, glr_ref, nq_ref, nk_ref, nv_ref, su_s) = refs
        x = x_ref[...]
    mod = mod_ref[0, 0]
    h = _rms(x, g_ref[0, 0:1]) * (1.0 + mod[1:2]) + mod[0:1]
    hb = h.astype(BF16)

    def proj(lo, hi):
        return jnp.dot(hb, w_ref[0, :, lo:hi], preferred_element_type=F32)

    ret_ref[...] = proj(0, 1024)
    su = proj(1024, 1280)
    rows = TOKEN_TILE // S5_CHUNK
    ng = S5_LANE_GROUPS
    piece = lax.broadcasted_iota(jnp.int32, (rows, 128), 1) // S5_GROUP_CH
    for lb in range(S5_GROUPS // ng):
        su_s[lb] = su[:, lb * 128:(lb + 1) * 128]
        for m in range(S5_CHUNK // ng):
            out = _piece_transpose(
                [su_s[lb, pl.ds(m * ng + jl, rows, stride=S5_CHUNK), :] for jl in range(ng)], piece)
            for g in range(ng):
                col = (lb * ng + g) * S5_W + m * 128
                s5_ref[:, col:col + 128] = out[g]
    gqk_ref[...] = proj(1280, 1536)
    gv_ref[...] = proj(1536, 1792)
    gg_ref[...] = proj(1792, 2048)
    for ref, lo in ((nq_ref, 2048), (nk_ref, 2304), (nv_ref, 2560)):
        r = proj(lo, lo + 256)
        for hh in range(NA_HEADS):
            ref[hh] = r[:, hh * NA_DH:(hh + 1) * NA_DH]
    glr_ref[...] = proj(2816, 2944)


def _inproj(l, xs, mod, g_norm, w_in_p):
    tm = TOKEN_TILE
    first = isinstance(xs, tuple)
    tok = lambda w: pl.BlockSpec((tm, w), lambda i: (i, 0))
    head = pl.BlockSpec((NA_HEADS, tm, NA_DH), lambda i: (0, i, 0))
    tshape = lambda w: jax.ShapeDtypeStruct((T_ALL, w), F32)
    hshape = jax.ShapeDtypeStruct((NA_HEADS, T_ALL, NA_DH), F32)
    ctx_tiles = T_CTX // tm
    if first:
        x_specs = [pl.BlockSpec((tm, D_MODEL), lambda i: (jnp.minimum(i, ctx_tiles - 1), 0)),
                   pl.BlockSpec((tm, D_MODEL), lambda i: (jnp.maximum(i - ctx_tiles, 0), 0))]
        x_args = list(xs)
    else:
        x_specs, x_args = [tok(D_MODEL)], [xs]
    return pl.pallas_call(
        functools.partial(_inproj_kernel, first=first),
        grid=(T_ALL // tm,),
        in_specs=x_specs + [_mod_spec(l), _layer_spec(l, 4, D_MODEL), _layer_spec(l, D_MODEL, W_IN_PACKED)],
        out_specs=([tok(D_MODEL)] if first else [])
        + [tok(1024), pl.BlockSpec((tm // S5_CHUNK, S5_GROUPS * S5_W), lambda i: (i, 0)),
           tok(256), tok(256), tok(256), tok(128), head, head, head],
        out_shape=([tshape(D_MODEL)] if first else [])
        + [tshape(1024), jax.ShapeDtypeStruct((S5_ROWS, S5_GROUPS * S5_W), F32),
           tshape(256), tshape(256), tshape(256), tshape(128), hshape, hshape, hshape],
        scratch_shapes=[pltpu.VMEM((S5_GROUPS // S5_LANE_GROUPS, tm, 128), F32)],
        compiler_params=_cparams(("parallel",)),
        name="in_proj",
    )(*x_args, mod, g_norm, w_in_p)


def _rope_rotate(x, lane):
    first = (lane % 32) < 16
    w = x.shape[-1]
    return jnp.where(first, pltpu.roll(x, w - 16, 1), pltpu.roll(x, 16, 1))


def _ret_kernel(ld_ref, ret_ref, gn_ref, *rest, layer, seq, latent):
    if latent:
        cos_ref, sin_ref, s0_ref, _, out_ref, dec_s = rest
    else:
        out_ref, st_ref, dec_s = rest
    tq = RET_QBLOCK
    nq = seq // tq
    width = dec_s.shape[-1]

    @pl.when(pl.program_id(0) == 0)
    def _build_decay():
        rel = (lax.broadcasted_iota(jnp.int32, (tq, width), 0) + (nq - 1) * tq
               - lax.broadcasted_iota(jnp.int32, (tq, width), 1)).astype(F32)
        for h in range(RET_HEADS):
            dec_s[h] = (jnp.where(rel >= 0, jnp.exp(ld_ref[layer, 0, h] * jnp.maximum(rel, 0.0)), 0.0)
                        + jnp.where(rel <= 0, jnp.exp(ld_ref[layer, 1, h] * jnp.maximum(-rel, 0.0)), 0.0))

    nrows = ret_ref.shape[0]
    nsub = nrows // seq
    q = ret_ref[:, 0:256]
    k = ret_ref[:, 256:512]
    if latent:
        lane = lax.broadcasted_iota(jnp.int32, (nrows, 256), 1)
        cos = cos_ref[...]
        sin = sin_ref[...]
        q = q * cos + _rope_rotate(q, lane) * sin
        k = k * cos + _rope_rotate(k, lane) * sin
    k = k * (RET_DK ** -0.5)
    pos_c = lax.broadcasted_iota(jnp.int32, (seq, 1), 0).astype(F32)
    tiles = [(s, qb) for s in range(nsub) for qb in range(nq)]
    for h in range(RET_HEADS):
        lgf = ld_ref[layer, 0, h]
        lgb = ld_ref[layer, 1, h]
        sl = slice(h * RET_DK, (h + 1) * RET_DK)
        qh = q[:, sl]
        kh = k[:, sl]
        kb = kh.astype(BF16)
        vb = ret_ref[:, 512 + h * RET_DV:512 + (h + 1) * RET_DV].astype(BF16)
        if latent:
            q_init = jnp.concatenate([qh * jnp.exp(lgf * (pos_c + 1.0)),
                                      qh * jnp.exp(lgb * (seq - pos_c))], axis=1)
            s_init = jnp.concatenate([s0_ref[0, 0, 0, h], s0_ref[0, 0, 1, h]], axis=0)

        def score(s, qb):
            w0 = (nq - 1 - qb) * tq
            rows = slice(s * seq + qb * tq, s * seq + (qb + 1) * tq)
            keys = slice(s * seq, (s + 1) * seq)
            return (_bdot_nt(qh[rows], kb[keys]) * dec_s[h, :, w0:w0 + seq]).astype(BF16)

        def values(sc, s, qb):
            o = jnp.dot(sc, vb[s * seq:(s + 1) * seq], preferred_element_type=F32)
            if latent:
                o = o + _bdot(q_init[qb * tq:(qb + 1) * tq], s_init)
            return o

        def finish(o, s, qb):
            rows = slice(s * seq + qb * tq, s * seq + (qb + 1) * tq)
            g = ret_ref[rows, 768 + h * RET_DV:768 + (h + 1) * RET_DV]
            out_ref[rows, sl] = _group_norm(o, gn_ref[0, :, sl]) * _silu(g)

        if nsub > 1:
            scores = [score(s, qb) for s, qb in tiles]
            outs = [values(sc, s, qb) for sc, (s, qb) in zip(scores, tiles)]
            for o, (s, qb) in zip(outs, tiles):
                finish(o, s, qb)
        else:
            for s, qb in tiles:
                finish(values(score(s, qb), s, qb), s, qb)
        if not latent:
            for s in range(nsub):
                keys = slice(s * seq, (s + 1) * seq)
                st_ref[s, 0, h] = _bdot_tn(kh[keys] * jnp.exp(lgf * (seq - 1.0 - pos_c)), vb[keys])
                st_ref[s, 1, h] = _bdot_tn(kh[keys] * jnp.exp(lgb * pos_c), vb[keys])


def _retention(l, ret, ld, gn, *, latent, cos=None, sin=None, s0=None, prev=None):
    seq = DEC_SEQ if latent else SEQ
    nsub = 1 if latent else CTX_SEQS_PER_STEP
    nb = DEC_BATCH if latent else BATCH // nsub
    off = LAT_BLOCK0 if latent else 0
    rows = nsub * seq
    in_specs = [pl.BlockSpec(memory_space=pltpu.SMEM),
                pl.BlockSpec((rows, 1024), lambda b: (b + off, 0)),
                _layer_spec(l, 1, 256)]
    args = [ld, ret, gn]
    out_specs = [pl.BlockSpec((rows, 256), lambda b: (b + off, 0))]
    out_shape = [jax.ShapeDtypeStruct((T_ALL, 256), F32)]
    aliases = {}
    if latent:
        in_specs += [pl.BlockSpec((seq, 256), lambda b: (0, 0)),
                     pl.BlockSpec((seq, 256), lambda b: (0, 0)),
                     pl.BlockSpec((1, 1, 2, RET_HEADS, RET_DK, RET_DV), lambda b: (b, l, 0, 0, 0, 0)),
                     _ANY]
        args += [cos, sin, s0, prev]
        aliases = {6: 0}
    else:
        out_specs.append(pl.BlockSpec((nsub, 2, RET_HEADS, RET_DK, RET_DV), lambda b: (b, 0, 0, 0, 0)))
        out_shape.append(jax.ShapeDtypeStruct((BATCH, 2, RET_HEADS, RET_DK, RET_DV), F32))
    return pl.pallas_call(
        functools.partial(_ret_kernel, layer=l, seq=seq, latent=latent),
        grid=(nb,),
        in_specs=in_specs, out_specs=out_specs, out_shape=out_shape,
        input_output_aliases=aliases,
        scratch_shapes=[pltpu.VMEM((RET_HEADS, RET_QBLOCK, 2 * seq - RET_QBLOCK), F32)],
        compiler_params=_cparams(("arbitrary",)),
        name="retention_lat" if latent else "retention_ctx",
    )(*args)


def _gla_kernel(gqk_ref, gv_ref, gg_ref, glr_ref, wg_ref, bg_ref, gn_ref, *rest, seq, latent):
    if latent:
        s0_ref, _, out_ref, gate_s, o_s, st_s, qst_s, ds_s, e_s = rest
    else:
        out_ref, st_ref, gate_s, o_s, st_s, qst_s, ds_s, e_s = rest
    c = GLA_CHUNK
    n = seq // c
    nsub = gqk_ref.shape[0] // seq
    hk = GLA_HEADS * GLA_DK
    lr = glr_ref[...]
    for d in range(2):
        pre = _bdot(lr[:, d * GLA_RANK:(d + 1) * GLA_RANK], wg_ref[0, d]) + bg_ref[0, d:d + 1]
        gate_s[d] = (jnp.minimum(pre, 0.0) - jnp.log(1.0 + jnp.exp(-jnp.abs(pre)))) / GLA_TAU
        for s in range(nsub):
            st_s[2 * s + d] = s0_ref[0, 0, d] if latent else jnp.zeros((GLA_HEADS * GLA_DV, hk), F32)

    nc = GLA_BLOCK_CHUNKS
    rb = nc * c
    ti = lax.broadcasted_iota(jnp.int32, (rb, rb), 0)
    tj = lax.broadcasted_iota(jnp.int32, (rb, rb), 1)
    same = (ti // c) == (tj // c)
    ones = lambda m: (same & m).astype(BF16)
    tri = [ones(tj <= ti), ones(tj >= ti)]
    mid = [ones((tj % c) < c // 2), ones((tj % c) >= c // 2)]
    tot = ones(tj == tj)
    lane_k = lax.broadcasted_iota(jnp.int32, (c, hk), 1)
    head_mask = [(lane_k // GLA_DK) == h for h in range(GLA_HEADS)]
    ai = lax.broadcasted_iota(jnp.int32, (GLA_HEADS * c, c), 0) % c
    aj = lax.broadcasted_iota(jnp.int32, (GLA_HEADS * c, c), 1)
    keep = [aj <= ai, aj >= ai]
    sr = lax.broadcasted_iota(jnp.int32, (GLA_HEADS * GLA_DV, hk), 0) // GLA_DV
    sc = lax.broadcasted_iota(jnp.int32, (GLA_HEADS * GLA_DV, hk), 1) // GLA_DK
    diag = sr == sc
    scale = GLA_DK ** -0.5
    d32 = functools.partial(jnp.dot, preferred_element_type=F32)

    def rows_of(i, size):
        return pl.ds(i * size, size) if isinstance(i, int) else pl.ds(pl.multiple_of(i * size, size), size)

    def local(block_ids):
        vs, q_att, k_att, k_st = {}, {}, {}, {}
        for bi in block_ids:
            rows = rows_of(bi, rb)
            q = gqk_ref[rows, 0:hk] * scale
            k = gqk_ref[rows, hk:2 * hk]
            vs[bi] = gv_ref[rows, :].astype(BF16)
            for d in range(2):
                gh, gl = _split(gate_s[d, rows, :])
                b = d32(tri[d], gh) + d32(tri[d], gl)
                b_mid = d32(mid[d], gh) + d32(mid[d], gl)
                b_end = d32(tot, gh) + d32(tot, gl)
                q_att[bi, d] = q * jnp.exp(b - b_mid)
                k_att[bi, d] = k * jnp.exp(b_mid - b)
                k_st[bi, d] = (k * jnp.exp(b_end - b)).astype(BF16)
                qst_s[d, rows, :] = (q * jnp.exp(b)).astype(BF16)
                decay = jnp.exp(b_end)
                for cc in range(nc):
                    e_s[d, bi * nc + cc] = decay[cc * c:cc * c + 8]
        tiles = [(bi, d, cc) for bi in block_ids for d in range(2) for cc in range(nc)]
        att = {}
        for bi, d, cc in tiles:
            r = slice(cc * c, (cc + 1) * c)
            qa = q_att[bi, d][r]
            q_stack = jnp.concatenate([jnp.where(head_mask[h], qa, 0.0) for h in range(GLA_HEADS)], axis=0)
            att[bi, d, cc] = jnp.where(keep[d], _bdot_nt(q_stack, k_att[bi, d][r]), 0.0).astype(BF16)
        for bi, d, cc in tiles:
            r = slice(cc * c, (cc + 1) * c)
            o = jnp.concatenate([d32(att[bi, d, cc][h * c:(h + 1) * c], vs[bi][r, h * GLA_DV:(h + 1) * GLA_DV])
                                 for h in range(GLA_HEADS)], axis=1)
            o_s[d, rows_of(bi * nc + cc, c), :] = o
        for bi, d, cc in tiles:
            r = slice(cc * c, (cc + 1) * c)
            ds_s[d, bi * nc + cc] = jnp.where(diag, lax.dot_general(
                vs[bi][r], k_st[bi, d][r], (((0,), (0,)), ((), ())), preferred_element_type=F32), 0.0)

    def recur(s, ci, d):
        g = s * n + ci
        rows = rows_of(g, c)
        st = st_s[2 * s + d]
        o_s[d, rows, :] += _bdot_nt(qst_s[d, rows, :], st)
        st_s[2 * s + d] = st * e_s[d, g, 0:1] + ds_s[d, g]

    def recur_body(i, carry):
        for s in range(nsub):
            recur(s, i, 0)
            recur(s, n - 1 - i, 1)
        return carry

    local(list(range(nsub * n // nc)))
    if n <= 4:
        for i in range(n):
            recur_body(i, 0)
    else:
        lax.fori_loop(0, n, recur_body, 0, unroll=2)

    o = o_s[0] + o_s[1]
    for h in range(GLA_HEADS):
        sl = slice(h * GLA_DV, (h + 1) * GLA_DV)
        out_ref[:, sl] = _group_norm(o[:, sl], gn_ref[0, :, sl]) * _silu(gg_ref[:, sl])
    if not latent:
        hv = GLA_HEADS * GLA_DV
        eye = (lax.broadcasted_iota(jnp.int32, (hv, hv), 0)
               == lax.broadcasted_iota(jnp.int32, (hv, hv), 1)).astype(BF16)
        tn = lambda a: lax.dot_general(a, eye, (((0,), (0,)), ((), ())), preferred_element_type=F32)
        for s in range(nsub):
            for d in range(2):
                st = st_s[2 * s + d]
                hi, lo = _split(st)
                lo2 = (st - hi.astype(F32) - lo.astype(F32)).astype(BF16)
                s_all = tn(hi) + tn(lo) + tn(lo2)
                for h in range(GLA_HEADS):
                    st_ref[s, d, h] = s_all[h * GLA_DK:(h + 1) * GLA_DK, h * GLA_DV:(h + 1) * GLA_DV]


def _gla(l, gqk, gv, gg, glr, wg, bg, gn, *, latent, s0=None, prev=None):
    seq = DEC_SEQ if latent else SEQ
    nsub = 1 if latent else CTX_SEQS_PER_STEP
    nb = DEC_BATCH if latent else BATCH // nsub
    off = LAT_BLOCK0 if latent else 0
    rows = nsub * seq
    hk = GLA_HEADS * GLA_DK
    hv = GLA_HEADS * GLA_DV
    tok = lambda w: pl.BlockSpec((rows, w), lambda b: (b + off, 0))
    in_specs = [tok(256), tok(256), tok(256), tok(128),
                _layer_spec(l, 2, GLA_RANK, hk), _layer_spec(l, 2, hk), _layer_spec(l, 1, 256)]
    args = [gqk, gv, gg, glr, wg, bg, gn]
    out_specs = [tok(256)]
    out_shape = [jax.ShapeDtypeStruct((T_ALL, 256), F32)]
    aliases = {}
    if latent:
        in_specs += [pl.BlockSpec((1, 1, 2, hv, hk), lambda b: (b, l, 0, 0, 0)), _ANY]
        args += [s0, prev]
        aliases = {8: 0}
    else:
        out_specs.append(pl.BlockSpec((nsub, 2, GLA_HEADS, GLA_DK, GLA_DV), lambda b: (b, 0, 0, 0, 0)))
        out_shape.append(jax.ShapeDtypeStruct((BATCH, 2, GLA_HEADS, GLA_DK, GLA_DV), F32))
    return pl.pallas_call(
        functools.partial(_gla_kernel, seq=seq, latent=latent),
        grid=(nb,),
        in_specs=in_specs, out_specs=out_specs, out_shape=out_shape,
        input_output_aliases=aliases,
        scratch_shapes=[pltpu.VMEM((2, rows, hk), F32), pltpu.VMEM((2, rows, hv), F32),
                        pltpu.VMEM((2 * nsub, hv, hk), F32), pltpu.VMEM((2, rows, hk), BF16),
                        pltpu.VMEM((2, rows // GLA_CHUNK, hv, hk), F32),
                        pltpu.VMEM((2, rows // GLA_CHUNK, 8, hk), F32)],
        compiler_params=_cparams(("parallel",)),
        name="gla_lat" if latent else "gla_ctx",
    )(*args)


S5_W = S5_CHUNK * S5_GROUP_CH
S5_P2 = 2 * S5_STATE
S5_ROWS = T_ALL // S5_CHUNK
S5_ROWS_CTX = T_CTX // S5_CHUNK
S5_LANE_GROUPS = 128 // S5_GROUP_CH


S5_TE_ROWS = S5_W + 4 * S5_P2


def _s5_toeplitz_kernel(cc_ref, wfr_ref, wb_ref, ff_ref, te_ref, ffb_ref):
    wfr = wfr_ref[0]
    wb = wb_ref[0]
    kf = _dot3(cc_ref[0, 0], wfr)
    kb = _dot3(cc_ref[0, 1], wb)
    lane = lax.broadcasted_iota(jnp.int32, (S5_GROUP_CH, S5_W), 1)
    blocks = []
    for i in range(S5_CHUNK):
        sf = ((i + 1 - S5_CHUNK) * S5_GROUP_CH) % S5_W
        fwd = jnp.where(lane < (i + 1) * S5_GROUP_CH, pltpu.roll(kf, sf, 1) if sf else kf, 0.0)
        bwd = jnp.where(lane >= i * S5_GROUP_CH, pltpu.roll(kb, i * S5_GROUP_CH, 1) if i else kb, 0.0)
        blocks.append(fwd + bwd)
    swapped = lambda t: jnp.concatenate([t[S5_STATE:], t[:S5_STATE]], axis=0)
    te_ref[0] = jnp.concatenate(blocks + [wfr, wb, swapped(wfr), swapped(wb)], axis=0).astype(BF16)
    ffb_ref[0] = ff_ref[0].astype(BF16)


def _s5_toeplitz(cc, wfr, wb, ff):
    n = cc.shape[0]
    tbl = pl.BlockSpec((1, S5_P2, S5_W), lambda i: (i, 0, 0))
    te_spec = pl.BlockSpec((1, S5_TE_ROWS, S5_W), lambda i: (i, 0, 0))
    ff_spec = pl.BlockSpec((1, 2 * S5_P2, S5_W), lambda i: (i, 0, 0))
    return pl.pallas_call(
        _s5_toeplitz_kernel,
        grid=(n,),
        in_specs=[pl.BlockSpec((1, 2, S5_GROUP_CH, S5_P2), lambda i: (i, 0, 0, 0)), tbl, tbl, ff_spec],
        out_specs=[te_spec, ff_spec],
        out_shape=[jax.ShapeDtypeStruct((n, S5_TE_ROWS, S5_W), BF16),
                   jax.ShapeDtypeStruct((n, 2 * S5_P2, S5_W), BF16)],
        compiler_params=_cparams(("parallel",)),
        name="s5_toeplitz",
    )(cc, wfr, wb, ff)


def _s5_kernel(u_ref, te_ref, ff_ref, d_ref, lam_ref, x0_ref, y_ref, fin_ref, xs_s, ps_s):
    C = S5_CHUNK
    ng = S5_LANE_GROUPS

    for g in range(ng):
        u = u_ref[:, g * S5_W:(g + 1) * S5_W]
        r = _bdot_nt(u, te_ref[0, g])
        y_ref[:, g * S5_W:(g + 1) * S5_W] = r[:, 0:S5_W] + d_ref[0, :, g * S5_W:(g + 1) * S5_W] * u
        for t in range(4):
            xs_s[t * ng + g] = r[:, S5_W + t * S5_P2:S5_W + (t + 1) * S5_P2]

    w = ng * S5_P2

    def carry(base, nseq, nchunks, init_f, init_b):
        a_f, b_f = lam_ref[0, 0, 0:1], lam_ref[0, 0, 1:2]
        a_b, b_b = lam_ref[0, 1, 0:1], lam_ref[0, 1, 1:2]
        load = lambda t, rows: jnp.concatenate([xs_s[t * ng + g, rows, :] for g in range(ng)], axis=1)

        def body(i, st):
            sf, tf, sb, tb = st
            rf = pl.ds(base + i, nseq, stride=nchunks)
            rb = pl.ds(base + (nchunks - 1 - i), nseq, stride=nchunks)
            for g in range(ng):
                ps_s[g, rf, :] = sf[:, g * S5_P2:(g + 1) * S5_P2]
                ps_s[ng + g, rb, :] = sb[:, g * S5_P2:(g + 1) * S5_P2]
            return (a_f * sf + b_f * tf + load(0, rf), a_f * tf - b_f * sf + load(2, rf),
                    a_b * sb + b_b * tb + load(1, rb), a_b * tb - b_b * sb + load(3, rb))

        first = (lax.broadcasted_iota(jnp.int32, (1, w), 1) % S5_P2) < S5_STATE
        swap = lambda s: jnp.where(first, pltpu.roll(s, w - S5_STATE, 1), pltpu.roll(s, S5_STATE, 1))
        return lax.fori_loop(0, nchunks, body, (init_f, swap(init_f), init_b, swap(init_b)))

    zeros = jnp.zeros((BATCH, w), F32)
    fin = carry(0, BATCH, SEQ // C, zeros, zeros)
    fin_ref[0] = fin[0]
    fin_ref[1] = fin[2]
    carry(S5_ROWS_CTX, DEC_BATCH, DEC_SEQ // C, x0_ref[0, 0], x0_ref[0, 1])

    for g in range(ng):
        p = jnp.concatenate([ps_s[g], ps_s[ng + g]], axis=1)
        y_ref[:, g * S5_W:(g + 1) * S5_W] += _bdot(p, ff_ref[0, g])


def _s5(l, su_rows, te, ff, d_rows, lam, x0):
    ng = S5_LANE_GROUPS
    w = ng * S5_P2
    rows = pl.BlockSpec((S5_ROWS, ng * S5_W), lambda i: (0, i))
    return pl.pallas_call(
        _s5_kernel,
        grid=(S5_GROUPS // ng,),
        in_specs=[rows,
                  pl.BlockSpec((1, ng, S5_TE_ROWS, S5_W), lambda i: (l, i, 0, 0)),
                  pl.BlockSpec((1, ng, 2 * S5_P2, S5_W), lambda i: (l, i, 0, 0)),
                  pl.BlockSpec((1, 1, ng * S5_W), lambda i: (l, 0, i)),
                  pl.BlockSpec((1, 2, 2, w), lambda i: (l, 0, 0, i)),
                  pl.BlockSpec((1, 2, DEC_BATCH, w), lambda i: (l, 0, 0, i))],
        out_specs=[rows, pl.BlockSpec((2, BATCH, w), lambda i: (0, 0, i))],
        out_shape=[jax.ShapeDtypeStruct((S5_ROWS, S5_GROUPS * S5_W), F32),
                   jax.ShapeDtypeStruct((2, BATCH, S5_GROUPS * S5_P2), F32)],
        scratch_shapes=[pltpu.VMEM((4 * ng, S5_ROWS, S5_P2), F32), pltpu.VMEM((2 * ng, S5_ROWS, S5_P2), F32)],
        compiler_params=_cparams(("parallel",)),
        name="s5_scan",
    )(su_rows, te, ff, d_rows, lam, x0)


def _s5_tables(lam_re, lam_im, log_dt, b_re, b_im, c_re, c_im):
    C, G, P, H = S5_CHUNK, S5_GROUPS, S5_STATE, S5_GROUP_CH
    L = lam_re.shape[0]
    dt = jnp.exp(log_dt)[..., None]
    ar, ai = lam_re * dt, lam_im * dt

    steps = jnp.arange(C + 1, dtype=F32)
    mag = jnp.exp(ar[..., None] * steps)
    pw_re, pw_im = mag * jnp.cos(ai[..., None] * steps), mag * jnp.sin(ai[..., None] * steps)
    exact = functools.partial(jnp.einsum, precision=lax.Precision.HIGHEST)
    tau_np = np.arange(C * H) // H

    def power(d, t):
        sel = jnp.asarray((np.arange(C + 1)[:, None] == np.asarray(t)[None, :]).astype(np.float32))
        return exact('lgpt,tn->lgpn', pw_re[:, d], sel), exact('lgpt,tn->lgpn', pw_im[:, d], sel)

    lr, li = pw_re[..., 1], pw_im[..., 1]
    den = lam_re * lam_re + lam_im * lam_im
    qr = ((lr - 1.0) * lam_re + li * lam_im) / den
    qi = (li * lam_re - (lr - 1.0) * lam_im) / den
    bbr = qr[..., None] * b_re - qi[..., None] * b_im
    bbi = qr[..., None] * b_im + qi[..., None] * b_re
    chan = jnp.asarray((np.arange(H)[:, None] == (np.arange(C * H) % H)[None, :]).astype(np.float32))
    lanes = lambda a: exact('ldgph,hn->ldgpn', a, chan)
    bbr, bbi = lanes(bbr), lanes(bbi)
    c_t = lambda a: exact('ldghp,hn->ldgpn', a, chan)
    ctr, cti = c_t(c_re), c_t(c_im)

    def w_of(d, t):
        pr, pi = power(d, t)
        return jnp.concatenate([pr * bbr[:, d] - pi * bbi[:, d], pr * bbi[:, d] + pi * bbr[:, d]], axis=-2)

    def f_of(d, t):
        pr, pi = power(d, t)
        return jnp.concatenate([ctr[:, d] * pr - cti[:, d] * pi, -(ctr[:, d] * pi + cti[:, d] * pr)], axis=-2)

    wfr = w_of(0, (C - 1) - tau_np).reshape(L * G, 2 * P, C * H)
    wb = w_of(1, tau_np).reshape(L * G, 2 * P, C * H)
    cc = jnp.concatenate([c_re, -c_im], axis=-1).transpose(0, 2, 1, 3, 4).reshape(L * G, 2, H, 2 * P)
    ff = jnp.concatenate([f_of(0, tau_np + 1), f_of(1, C - tau_np)], axis=-2)
    tables = _s5_toeplitz(cc, wfr, wb, ff.reshape(L * G, 4 * P, C * H))
    tables = [t.reshape((L, G) + t.shape[1:]) for t in tables]
    cr, ci = pw_re[..., C], pw_im[..., C]
    a = jnp.concatenate([cr, cr], axis=-1).reshape(L, 2, 1, G * 2 * P)
    b = jnp.concatenate([-ci, ci], axis=-1).reshape(L, 2, 1, G * 2 * P)
    return tables, jnp.concatenate([a, b], axis=2)


def _softmax_pv(s_parts, v_parts):
    m = s_parts[0].max(axis=-1, keepdims=True)
    for s in s_parts[1:]:
        m = jnp.maximum(m, s.max(axis=-1, keepdims=True))
    o = None
    l = None
    for s, v in zip(s_parts, v_parts):
        p = jnp.exp(s - m)
        pl_ = p.sum(axis=-1, keepdims=True)
        po = _bdot(p, v)
        o = po if o is None else o + po
        l = pl_ if l is None else l + pl_
    return o / l


def _attn_ctx_kernel(q_ref, k_ref, v_ref, o_ref):
    scale = NA_DH ** -0.5
    tiles = [(h, slice(s * SEQ, (s + 1) * SEQ)) for h in range(NA_HEADS) for s in range(CTX_SEQS_PER_STEP)]
    scores = [_bdot_nt(q_ref[h, rows, :], k_ref[h, rows, :]) * scale for h, rows in tiles]
    for s, (h, rows) in zip(scores, tiles):
        o_ref[h, rows, :] = _softmax_pv([s], [v_ref[h, rows, :]])


def _attn_ctx(nq, nk, nv):
    spec = pl.BlockSpec((NA_HEADS, CTX_SEQS_PER_STEP * SEQ, NA_DH), lambda b: (0, b, 0))
    return pl.pallas_call(
        _attn_ctx_kernel,
        grid=(BATCH // CTX_SEQS_PER_STEP,),
        in_specs=[spec, spec, spec],
        out_specs=spec,
        out_shape=jax.ShapeDtypeStruct((NA_HEADS, T_ALL, NA_DH), F32),
        compiler_params=_cparams(("parallel",)),
        name="attn_ctx",
    )(nq, nk, nv)


def _attn_lat_kernel(q_ref, k_ref, v_ref, kc_ref, vc_ref, tb_ref, _, o_ref, bias_s):
    @pl.when(pl.program_id(1) == 0)
    def _build_bias():
        bias_s[...] = jnp.full((DEC_SEQ, DEC_SEQ), -jnp.inf, F32)
        for r in range(GRID_ROWS):
            rs = min(max(r - NA_KH // 2, 0), GRID_ROWS - NA_KH)
            dr0 = rs - r + NA_WIN_H - 1
            bias_s[r * GRID_W:(r + 1) * GRID_W, rs * GRID_W:(rs + NA_KH) * GRID_W] = (
                tb_ref[0, 0, :, dr0 * GRID_W:(dr0 + NA_KH) * GRID_W])

    scale = NA_DH ** -0.5
    kb = k_ref[0].astype(BF16)
    vb = v_ref[0].astype(BF16)
    kc = kc_ref[0, 0, 0].astype(BF16)
    vc = vc_ref[0, 0, 0].astype(BF16)
    tq = NA_QBLOCK
    q_rows = tq // GRID_W
    for qb in range(DEC_SEQ // tq):
        rows = slice(qb * tq, (qb + 1) * tq)
        starts = [min(max(r - NA_KH // 2, 0), GRID_ROWS - NA_KH) for r in range(qb * q_rows, (qb + 1) * q_rows)]
        keys = slice(min(starts) * GRID_W // 128 * 128, -(-(max(starts) + NA_KH) * GRID_W // 128) * 128)
        qh = q_ref[0, rows, :].astype(BF16)
        s_loc = _bdot_nt(qh, kb[keys]) * scale + bias_s[rows, keys]
        s_ctx = _bdot_nt(qh, kc) * scale
        o_ref[0, rows, :] = _softmax_pv([s_loc, s_ctx], [vb[keys], vc])


def _attn_lat(l, nq, nk, nv, kc, vc, tb, prev):
    tok = pl.BlockSpec((1, DEC_SEQ, NA_DH), lambda h, b: (h, b + LAT_BLOCK0, 0))
    cache = pl.BlockSpec((1, 1, 1, PAST_LEN, NA_DH), lambda h, b: (b, l, h, 0, 0))
    return pl.pallas_call(
        _attn_lat_kernel,
        grid=(NA_HEADS, DEC_BATCH),
        in_specs=[tok, tok, tok, cache, cache,
                  pl.BlockSpec((1, 1, GRID_W, NA_REL_ROWS * GRID_W), lambda h, b: (l, h, 0, 0)),
                  _ANY],
        out_specs=tok,
        out_shape=jax.ShapeDtypeStruct((NA_HEADS, T_ALL, NA_DH), F32),
        input_output_aliases={6: 0},
        scratch_shapes=[pltpu.VMEM((DEC_SEQ, DEC_SEQ), F32)],
        compiler_params=_cparams(("arbitrary", "arbitrary")),
        name="attn_lat",
    )(nq, nk, nv, kc, vc, tb, prev)


def _na_tables(rpb):
    col = np.arange(GRID_W)
    col_start = np.clip(col - NA_WIN_W // 2, 0, GRID_W - NA_WIN_W)
    col_in = (col[None, :] >= col_start[:, None]) & (col[None, :] < col_start[:, None] + NA_WIN_W)
    col_idx = np.clip(col[None, :] - col[:, None] + NA_WIN_W - 1, 0, 2 * NA_WIN_W - 2)
    onehot = (col_idx[:, :, None] == np.arange(2 * NA_WIN_W - 1)[None, None, :]).astype(np.float32)
    tb = jnp.einsum('lhrd,qkd->lhqrk', rpb, jnp.asarray(onehot), precision=lax.Precision.HIGHEST)
    tb = jnp.where(jnp.asarray(col_in)[None, None, :, None, :], tb, -jnp.inf)
    return tb.reshape(rpb.shape[0], NA_HEADS, GRID_W, NA_REL_ROWS * GRID_W)


def _merge_kernel(x_ref, mod_ref, g_ref, ret_ref, s5y_ref, gla_ref, na_ref,
                  wglu_ref, bglu_ref, wbr_ref, wmg_ref, bmg_ref, wout_ref, o_ref, y_s):
    x = x_ref[...]
    mod = mod_ref[0, 0]
    hb = (_rms(x, g_ref[0, 0:1]) * (1.0 + mod[1:2]) + mod[0:1]).astype(BF16)

    rows = TOKEN_TILE // S5_CHUNK
    ng = S5_LANE_GROUPS
    piece = lax.broadcasted_iota(jnp.int32, (rows, 128), 1) // S5_GROUP_CH
    for lb in range(S5_GROUPS // ng):
        for m in range(S5_CHUNK // ng):
            cols = [(lb * ng + g) * S5_W + m * 128 for g in range(ng)]
            out = _piece_transpose([s5y_ref[:, c0:c0 + 128] for c0 in cols], piece)
            for il in range(ng):
                y_s[lb, pl.ds(m * ng + il, rows, stride=S5_CHUNK), :] = out[il]
    y = jnp.concatenate([y_s[lb] for lb in range(S5_GROUPS // ng)], axis=1)
    y = 0.5 * y * (1.0 + jnp.tanh(math.sqrt(2.0 / math.pi) * (y + 0.044715 * (y * y * y))))
    z = _bdot(y, wglu_ref[0]) + bglu_ref[0]
    s5_out = z[:, 0:BRANCH_W] * _sigmoid(z[:, BRANCH_W:2 * BRANCH_W])

    def gate(n):
        return _sigmoid(_bdot(hb, wmg_ref[0, :, n * D_MODEL:(n + 1) * D_MODEL])
                        + bmg_ref[0, :, n * D_MODEL:(n + 1) * D_MODEL])

    acc = gate(0) * _bdot(ret_ref[...], wbr_ref[0, 0])
    acc += gate(1) * _bdot(s5_out, wbr_ref[0, 1])
    acc += gate(2) * _bdot(gla_ref[...], wbr_ref[0, 2])
    na = jnp.concatenate([na_ref[hh].astype(BF16) for hh in range(NA_HEADS)], axis=1)
    acc += gate(3) * _bdot(na, wbr_ref[0, 3])
    m = _bdot(acc, wout_ref[0])
    o_ref[...] = x + mod[2:3] * _rms(m, g_ref[0, 1:2])


def _merge(l, x, mod, g_norm, ret_o, s5_y, gla_o, na_o, wglu, bglu, wbr, wmg, bmg, wout):
    tm = TOKEN_TILE
    tok = lambda w: pl.BlockSpec((tm, w), lambda i: (i, 0))
    return pl.pallas_call(
        _merge_kernel,
        grid=(T_ALL // tm,),
        in_specs=[tok(D_MODEL), _mod_spec(l), _layer_spec(l, 4, D_MODEL),
                  tok(256), pl.BlockSpec((tm // S5_CHUNK, S5_GROUPS * S5_W), lambda i: (i, 0)), tok(256),
                  pl.BlockSpec((NA_HEADS, tm, NA_DH), lambda i: (0, i, 0)),
                  _layer_spec(l, 256, 512), _layer_spec(l, 1, 512),
                  _layer_spec(l, N_BRANCH, BRANCH_W, D_MODEL, single_buffer=True),
                  _layer_spec(l, D_MODEL, N_BRANCH * D_MODEL, single_buffer=True),
                  _layer_spec(l, 1, N_BRANCH * D_MODEL),
                  _layer_spec(l, D_MODEL, D_MODEL, single_buffer=True)],
        out_specs=tok(D_MODEL),
        out_shape=jax.ShapeDtypeStruct((T_ALL, D_MODEL), F32),
        scratch_shapes=[pltpu.VMEM((S5_GROUPS // S5_LANE_GROUPS, tm, 128), F32)],
        compiler_params=_cparams(("parallel",)),
        name="merge",
    )(x, mod, g_norm, ret_o, s5_y, gla_o, na_o, wglu, bglu, wbr, wmg, bmg, wout)


FF_TILE = 1024


def _mlp_kernel(x_ref, mod_ref, g_ref, w1_ref, w2_ref, *o_refs):
    x = x_ref[...]
    mod = mod_ref[0, 0]
    hb = (_rms(x, g_ref[0, 2:3]) * (1.0 + mod[4:5]) + mod[3:4]).astype(BF16)
    f = None
    for j in range(D_FF // FF_TILE):
        a = jnp.maximum(_bdot(hb, w1_ref[0, :, j * FF_TILE:(j + 1) * FF_TILE]), 0.0)
        part = _bdot(a * a, w2_ref[0, j * FF_TILE:(j + 1) * FF_TILE, :])
        f = part if f is None else f + part
    y = x + mod[5:6] * _rms(f, g_ref[0, 3:4])
    if len(o_refs) == 1:
        o_refs[0][...] = y
    else:
        ctx_tiles = T_CTX // TOKEN_TILE

        @pl.when(pl.program_id(0) < ctx_tiles)
        def _store_ctx():
            o_refs[0][...] = y

        @pl.when(pl.program_id(0) >= ctx_tiles)
        def _store_lat():
            o_refs[1][...] = y


def _mlp(l, x, mod, g_norm, w1, w2, *, split_out):
    tm = TOKEN_TILE
    tok = pl.BlockSpec((tm, D_MODEL), lambda i: (i, 0))
    if split_out:
        ctx_tiles = T_CTX // tm
        out_specs = [pl.BlockSpec((tm, D_MODEL), lambda i: (jnp.minimum(i, ctx_tiles - 1), 0)),
                     pl.BlockSpec((tm, D_MODEL), lambda i: (jnp.maximum(i - ctx_tiles, 0), 0))]
        out_shape = [jax.ShapeDtypeStruct((T_CTX, D_MODEL), F32), jax.ShapeDtypeStruct((T_LAT, D_MODEL), F32)]
    else:
        out_specs = tok
        out_shape = jax.ShapeDtypeStruct((T_ALL, D_MODEL), F32)
    return pl.pallas_call(
        _mlp_kernel,
        grid=(T_ALL // tm,),
        in_specs=[tok, _mod_spec(l), _layer_spec(l, 4, D_MODEL),
                  _layer_spec(l, D_MODEL, D_FF, single_buffer=True),
                  _layer_spec(l, D_FF, D_MODEL, single_buffer=True)],
        out_specs=out_specs,
        out_shape=out_shape,
        compiler_params=_cparams(("arbitrary",)),
        name="mlp",
    )(x, mod, g_norm, w1, w2)


def _rope_tables():
    half = RET_DK // 2
    nf = half // 2
    t = jnp.arange(DEC_SEQ)
    row = (t // GRID_W).astype(F32)
    col = (t % GRID_W).astype(F32)
    inv = ROPE_BASE ** (-jnp.arange(nf, dtype=F32) / nf)
    ang_r = row[:, None] * inv[None, :]
    ang_c = col[:, None] * inv[None, :]
    cos = jnp.concatenate([jnp.cos(ang_r)] * 2 + [jnp.cos(ang_c)] * 2, axis=1)
    sin = jnp.concatenate([-jnp.sin(ang_r), jnp.sin(ang_r), -jnp.sin(ang_c), jnp.sin(ang_c)], axis=1)
    return jnp.tile(cos, (1, RET_HEADS)), jnp.tile(sin, (1, RET_HEADS))


def _pack_w_in(w):
    offs = [0, 256, 512, 768, 1024, 1280, 1408, 1536, 1792, 2048, 2080, 2336, 2592, 2848]
    seg = lambda i: w[:, :, offs[i]:offs[i + 1]]
    order = [0, 1, 2, 3, 4, 5, 6, 7, 8, 10, 11, 12, 9]
    packed = jnp.concatenate([seg(i) for i in order], axis=2)
    return jnp.pad(packed, ((0, 0), (0, 0), (0, W_IN_PACKED - packed.shape[2]))).astype(BF16)


def _gla_state_in(st):
    eye = jnp.eye(GLA_HEADS, dtype=st.dtype)
    t = jnp.einsum('bldhkv,hg->bldhvgk', st, eye)
    return t.reshape(st.shape[0], st.shape[1], 2, GLA_HEADS * GLA_DV, GLA_HEADS * GLA_DK)


def kernel(x_prompt, x_sample, c, cache_na_k, cache_na_v, state_ret, state_s5, state_gla, c_ctx, w_ada, b_ada, g_norm, w_in, ret_log_decay, ret_gn, s5_lambda_re, s5_lambda_im, s5_log_dt, s5_b_re, s5_b_im, s5_c_re, s5_c_im, s5_d, s5_w_glu, s5_b_glu, gla_w_gate, gla_b_gate, gla_gn, na_rpb, w_branch, w_merge, b_merge, w_out, w_mlp1, w_mlp2):
    depth = w_in.shape[0]
    x = (x_prompt.reshape(T_CTX, D_MODEL), x_sample.reshape(T_LAT, D_MODEL))
    cc = jnp.concatenate([c_ctx[None], c, jnp.zeros((N_MOD_ROWS - 1 - DEC_BATCH, D_MODEL), F32)], axis=0)
    mod = _ada(cc, w_ada, b_ada).reshape(depth, N_MOD_ROWS, 6, D_MODEL)

    cos, sin = _rope_tables()
    w_in_p = _pack_w_in(w_in)
    ret_gn3, gla_gn3 = (a.reshape(depth, 1, BRANCH_W) for a in (ret_gn, gla_gn))
    s5_d_rows = jnp.broadcast_to(s5_d.reshape(depth, S5_GROUPS, 1, S5_GROUP_CH),
                                 (depth, S5_GROUPS, S5_CHUNK, S5_GROUP_CH)).reshape(depth, 1, S5_GROUPS * S5_W)
    b_glu3 = s5_b_glu.reshape(depth, 1, 2 * BRANCH_W)
    b_mg3 = b_merge.reshape(depth, 1, N_BRANCH * D_MODEL)
    cache_k = cache_na_k.transpose(0, 1, 3, 2, 4)
    cache_v = cache_na_v.transpose(0, 1, 3, 2, 4)
    na_tb = _na_tables(na_rpb)
    gla_s0 = _gla_state_in(state_gla)
    s5_tables, s5_lam = _s5_tables(s5_lambda_re, s5_lambda_im, s5_log_dt, s5_b_re, s5_b_im,
                                   s5_c_re, s5_c_im)
    s5_x0 = state_s5.transpose(1, 2, 0, 3, 5, 4).reshape(depth, 2, DEC_BATCH, S5_GROUPS * S5_P2)

    ks_l, vs_l, ret_l, s5_l, gla_l = [], [], [], [], []
    for l in range(depth):
        proj = _inproj(l, x, mod, g_norm, w_in_p)
        if l == 0:
            x, proj = proj[0], proj[1:]
        ret, su, gqk, gv, gg, glr, nq, nk, nv = proj

        ret_o, st_ret = _retention(l, ret, ret_log_decay, ret_gn3, latent=False)
        ret_o, = _retention(l, ret, ret_log_decay, ret_gn3, latent=True, cos=cos, sin=sin, s0=state_ret,
                            prev=ret_o)

        s5_y, s5_fin = _s5(l, su, *s5_tables, s5_d_rows, s5_lam, s5_x0)

        gla_o, st_gla = _gla(l, gqk, gv, gg, glr, gla_w_gate, gla_b_gate, gla_gn3, latent=False)
        gla_o, = _gla(l, gqk, gv, gg, glr, gla_w_gate, gla_b_gate, gla_gn3, latent=True, s0=gla_s0,
                      prev=gla_o)

        na_o = _attn_ctx(nq, nk, nv)
        na_o = _attn_lat(l, nq, nk, nv, cache_k, cache_v, na_tb, na_o)

        x = _merge(l, x, mod, g_norm, ret_o, s5_y, gla_o, na_o,
                   s5_w_glu, b_glu3, w_branch, w_merge, b_mg3, w_out)
        x = _mlp(l, x, mod, g_norm, w_mlp1, w_mlp2, split_out=(l == depth - 1))

        ks_l.append(nk)
        vs_l.append(nv)
        ret_l.append(st_ret)
        s5_l.append(s5_fin)
        gla_l.append(st_gla)

    y_prompt = x[0].reshape(BATCH, SEQ, D_MODEL)
    y_sample = x[1].reshape(DEC_BATCH, DEC_SEQ, D_MODEL)

    def cache_out(per_layer):
        a = jnp.stack(per_layer, axis=0)[:, :, :T_CTX].reshape(depth, NA_HEADS, BATCH, SEQ, NA_DH)
        return a.transpose(2, 0, 3, 1, 4)

    s5_out = jnp.stack(s5_l, axis=0).reshape(depth, 2, BATCH, S5_GROUPS, 2, S5_STATE)
    return (y_prompt, y_sample, cache_out(ks_l), cache_out(vs_l), jnp.stack(ret_l, axis=1),
            s5_out.transpose(2, 0, 1, 3, 5, 4), jnp.stack(gla_l, axis=1))
```

```python
import functools
import math

import numpy as np
import jax
import jax.numpy as jnp
from jax import lax
from jax.experimental import pallas as pl
from jax.experimental.pallas import tpu as pltpu

F32 = jnp.float32
BF16 = jnp.bfloat16

D_MODEL = 1024
BATCH = 16
SEQ = 256
DEPTH = 4
DEC_BATCH = 4
DEC_SEQ = 1024
PAST_LEN = 256
GRID_W = 64
N_BRANCH = 4
BRANCH_W = 256
RET_HEADS = 4
RET_DK = 64
RET_DV = 64
S5_GROUPS = 16
S5_GROUP_CH = 16
S5_STATE = 64
GLA_HEADS = 4
GLA_DK = 32
GLA_DV = 64
GLA_RANK = 16
GLA_TAU = 16.0
NA_HEADS = 4
NA_DH = 64
NA_WIN_H = 8
NA_WIN_W = 16
D_FF = 4 * D_MODEL
ROPE_BASE = 10000.0
EPS = 1e-6

T_CTX = BATCH * SEQ
T_LAT = DEC_BATCH * DEC_SEQ
T_ALL = T_CTX + T_LAT
LAT_BLOCK0 = T_CTX // DEC_SEQ
N_MOD_ROWS = 8
TOKEN_TILE = 512
CTX_SEQS_PER_STEP = DEC_SEQ // SEQ
GLA_CHUNK = 64
GLA_BLOCK_CHUNKS = 4
S5_CHUNK = 16
RET_QBLOCK = 256
NA_QBLOCK = 256
GRID_ROWS = DEC_SEQ // GRID_W
NA_KH = min(NA_WIN_H, GRID_ROWS)
NA_REL_ROWS = 2 * NA_WIN_H - 1
VMEM_LIMIT = 56 * 1024 * 1024
D_IN = 2848
IN_TAIL_COL = 2048


def _cparams(sem):
    return pltpu.CompilerParams(dimension_semantics=sem, vmem_limit_bytes=VMEM_LIMIT)


def _bdot(a, b):
    return jnp.dot(a.astype(BF16), b.astype(BF16), preferred_element_type=F32)


def _bdot_nt(a, b):
    return lax.dot_general(a.astype(BF16), b.astype(BF16), (((1,), (1,)), ((), ())),
                           preferred_element_type=F32)


def _bdot_tn(a, b):
    return lax.dot_general(a.astype(BF16), b.astype(BF16), (((0,), (0,)), ((), ())),
                           preferred_element_type=F32)


def _split(a):
    hi = a.astype(BF16)
    lo = (a - hi.astype(F32)).astype(BF16)
    return hi, lo


def _dot3(a, b):
    ah, al = _split(a)
    bh, bl = _split(b)
    d = functools.partial(jnp.dot, preferred_element_type=F32)
    return d(ah, bh) + d(al, bh) + d(ah, bl)


def _sigmoid(x):
    return 0.5 * jnp.tanh(0.5 * x) + 0.5


def _silu(x):
    return x * _sigmoid(x)


def _rms(x, g):
    return x * lax.rsqrt(jnp.mean(x * x, axis=-1, keepdims=True) + EPS) * g


def _group_norm(o, g):
    mu = jnp.mean(o, axis=-1, keepdims=True)
    xc = o - mu
    return xc * lax.rsqrt(jnp.mean(xc * xc, axis=-1, keepdims=True) + EPS) * g


def _mod_row(i):
    ctx_tiles = T_CTX // TOKEN_TILE
    return jnp.where(i < ctx_tiles, 0, 1 + (i - ctx_tiles) // (DEC_SEQ // TOKEN_TILE))


def _mod_spec(l):
    return pl.BlockSpec((1, 1, 6, D_MODEL), lambda i: (l, _mod_row(i), 0, 0))


def _layer_spec(l, *shape, single_buffer=False):
    mode = pl.Buffered(1) if single_buffer else None
    return pl.BlockSpec((1,) + shape, lambda *_: (l,) + (0,) * len(shape), pipeline_mode=mode)


_ANY = pl.BlockSpec(memory_space=pl.ANY)


ADA_TILE = 1536


def _ada_kernel(c_ref, w_ref, b_ref, o_ref):
    a = _silu(c_ref[...])
    o_ref[0] = _bdot(a, w_ref[0]) + b_ref[0]


def _ada(cc, w_ada, b_ada):
    n = 6 * D_MODEL
    return pl.pallas_call(
        _ada_kernel,
        grid=(DEPTH, n // ADA_TILE),
        in_specs=[pl.BlockSpec((N_MOD_ROWS, D_MODEL), lambda l, j: (0, 0)),
                  pl.BlockSpec((1, D_MODEL, ADA_TILE), lambda l, j: (l, 0, j)),
                  pl.BlockSpec((1, 1, ADA_TILE), lambda l, j: (l, 0, j))],
        out_specs=pl.BlockSpec((1, N_MOD_ROWS, ADA_TILE), lambda l, j: (l, 0, j)),
        out_shape=jax.ShapeDtypeStruct((DEPTH, N_MOD_ROWS, n), F32),
        compiler_params=_cparams(("parallel", "parallel")),
        name="ada_mod",
    )(cc, w_ada, b_ada.reshape(DEPTH, 1, n))


def _piece_transpose(blocks, piece):
    x = list(blocks)
    n = len(x)
    d = n // 2
    while d >= 1:
        low = (piece & d) == 0
        for v in range(n):
            if v & d:
                continue
            a, b = x[v], x[v + d]
            x[v] = jnp.where(low, a, pltpu.roll(b, d * S5_GROUP_CH, 1))
            x[v + d] = jnp.where(low, pltpu.roll(a, 128 - d * S5_GROUP_CH, 1), b)
        d //= 2
    return x


def _inproj_kernel(*refs, first):
    if first:
        (xa_ref, xb_ref, mod_ref, g_ref, w_ref, x_out_ref,
         ret_ref, s5_ref, gqk_ref, gv_ref, gg_ref, glr_ref, nq_ref, nk_ref, nv_ref, su_s) = refs
        x = jnp.where(pl.program_id(0) < T_CTX // TOKEN_TILE, xa_ref[...], xb_ref[...])
        x_out_ref[...] = x
    else:
        (x_ref, mod_ref, g_ref, w_ref,
         ret_ref, s5_ref, gqk_ref, gv_ref, gg_ref, glr_ref, nq_ref, nk_ref, nv_ref, su_s) = refs
        x = x_ref[...]
    mod = mod_ref[0, 0]
    h = _rms(x, g_ref[0, 0:1]) * (1.0 + mod[1:2]) + mod[0:1]
    hb = h.astype(BF16)

    def proj(lo, hi):
        return _bdot(hb, w_ref[0, :, lo:hi])

    ret_ref[...] = proj(0, 1024)
    su = proj(1024, 1280)
    rows = TOKEN_TILE // S5_CHUNK
    ng = S5_LANE_GROUPS
    piece = lax.broadcasted_iota(jnp.int32, (rows, 128), 1) // S5_GROUP_CH
    for lb in range(S5_GROUPS // ng):
        su_s[lb] = su[:, lb * 128:(lb + 1) * 128]
        for m in range(S5_CHUNK // ng):
            out = _piece_transpose(
                [su_s[lb, pl.ds(m * ng + jl, rows, stride=S5_CHUNK), :] for jl in range(ng)], piece)
            for g in range(ng):
                col = (lb * ng + g) * S5_W + m * 128
                s5_ref[:, col:col + 128] = out[g]
    gqk_ref[...] = proj(1280, 1536)
    gv_ref[...] = proj(1536, 1792)
    gg_ref[...] = proj(1792, 2048)
    lo, hi = IN_TAIL_COL, IN_TAIL_COL + (D_IN - IN_TAIL_COL) // 128 * 128
    tail = jnp.concatenate([proj(lo, hi), proj(hi, D_IN)], axis=1)
    rank = 2 * GLA_RANK
    glr_ref[...] = jnp.concatenate([tail[:, 0:rank], jnp.zeros((TOKEN_TILE, 128 - rank), F32)], axis=1)
    for i, ref in enumerate((nq_ref, nk_ref, nv_ref)):
        for hh in range(NA_HEADS):
            c0 = rank + (i * NA_HEADS + hh) * NA_DH
            ref[hh] = tail[:, c0:c0 + NA_DH]


def _inproj(l, xs, mod, g_norm, w_in):
    tm = TOKEN_TILE
    first = isinstance(xs, tuple)
    tok = lambda w: pl.BlockSpec((tm, w), lambda i: (i, 0))
    head = pl.BlockSpec((NA_HEADS, tm, NA_DH), lambda i: (0, i, 0))
    tshape = lambda w: jax.ShapeDtypeStruct((T_ALL, w), F32)
    hshape = jax.ShapeDtypeStruct((NA_HEADS, T_ALL, NA_DH), F32)
    ctx_tiles = T_CTX // tm
    if first:
        x_specs = [pl.BlockSpec((tm, D_MODEL), lambda i: (jnp.minimum(i, ctx_tiles - 1), 0)),
                   pl.BlockSpec((tm, D_MODEL), lambda i: (jnp.maximum(i - ctx_tiles, 0), 0))]
        x_args = list(xs)
    else:
        x_specs, x_args = [tok(D_MODEL)], [xs]
    return pl.pallas_call(
        functools.partial(_inproj_kernel, first=first),
        grid=(T_ALL // tm,),
        in_specs=x_specs + [_mod_spec(l), _layer_spec(l, 4, D_MODEL),
                            _layer_spec(l, D_MODEL, D_IN, single_buffer=True)],
        out_specs=([tok(D_MODEL)] if first else [])
        + [tok(1024), pl.BlockSpec((tm // S5_CHUNK, S5_GROUPS * S5_W), lambda i: (i, 0)),
           tok(256), tok(256), tok(256), tok(128), head, head, head],
        out_shape=([tshape(D_MODEL)] if first else [])
        + [tshape(1024), jax.ShapeDtypeStruct((S5_ROWS, S5_GROUPS * S5_W), F32),
           tshape(256), tshape(256), tshape(256), tshape(128), hshape, hshape, hshape],
        scratch_shapes=[pltpu.VMEM((S5_GROUPS // S5_LANE_GROUPS, tm, 128), F32)],
        compiler_params=_cparams(("parallel",)),
        name="in_proj",
    )(*x_args, mod, g_norm, w_in)


def _rope_rotate(x, lane):
    first = (lane % 32) < 16
    w = x.shape[-1]
    return jnp.where(first, pltpu.roll(x, w - 16, 1), pltpu.roll(x, 16, 1))


def _ret_kernel(ld_ref, ret_ref, gn_ref, *rest, layer, seq, latent):
    if latent:
        cos_ref, sin_ref, s0_ref, _, out_ref, dec_s = rest
    else:
        out_ref, st_ref, dec_s = rest
    tq = RET_QBLOCK
    nq = seq // tq
    width = dec_s.shape[-1]

    @pl.when(pl.program_id(0) == 0)
    def _build_decay():
        rel = (lax.broadcasted_iota(jnp.int32, (tq, width), 0) + (nq - 1) * tq
               - lax.broadcasted_iota(jnp.int32, (tq, width), 1)).astype(F32)
        for h in range(RET_HEADS):
            dec_s[h] = (jnp.where(rel >= 0, jnp.exp(ld_ref[layer, 0, h] * jnp.maximum(rel, 0.0)), 0.0)
                        + jnp.where(rel <= 0, jnp.exp(ld_ref[layer, 1, h] * jnp.maximum(-rel, 0.0)), 0.0))

    nrows = ret_ref.shape[0]
    nsub = nrows // seq
    q = ret_ref[:, 0:256]
    k = ret_ref[:, 256:512]
    if latent:
        lane = lax.broadcasted_iota(jnp.int32, (nrows, 256), 1)
        cos = cos_ref[...]
        sin = sin_ref[...]
        q = q * cos + _rope_rotate(q, lane) * sin
        k = k * cos + _rope_rotate(k, lane) * sin
    k = k * (RET_DK ** -0.5)
    pos_c = lax.broadcasted_iota(jnp.int32, (seq, 1), 0).astype(F32)
    tiles = [(s, qb) for s in range(nsub) for qb in range(nq)]
    for h in range(RET_HEADS):
        lgf = ld_ref[layer, 0, h]
        lgb = ld_ref[layer, 1, h]
        sl = slice(h * RET_DK, (h + 1) * RET_DK)
        qh = q[:, sl]
        kh = k[:, sl]
        kb = kh.astype(BF16)
        vb = ret_ref[:, 512 + h * RET_DV:512 + (h + 1) * RET_DV].astype(BF16)
        if latent:
            q_init = jnp.concatenate([qh * jnp.exp(lgf * (pos_c + 1.0)),
                                      qh * jnp.exp(lgb * (seq - pos_c))], axis=1)
            s_init = jnp.concatenate([s0_ref[0, 0, 0, h], s0_ref[0, 0, 1, h]], axis=0)

        def score(s, qb):
            w0 = (nq - 1 - qb) * tq
            rows = slice(s * seq + qb * tq, s * seq + (qb + 1) * tq)
            keys = slice(s * seq, (s + 1) * seq)
            return (_bdot_nt(qh[rows], kb[keys]) * dec_s[h, :, w0:w0 + seq]).astype(BF16)

        def values(sc, s, qb):
            o = jnp.dot(sc, vb[s * seq:(s + 1) * seq], preferred_element_type=F32)
            if latent:
                o = o + _bdot(q_init[qb * tq:(qb + 1) * tq], s_init)
            return o

        def finish(o, s, qb):
            rows = slice(s * seq + qb * tq, s * seq + (qb + 1) * tq)
            g = ret_ref[rows, 768 + h * RET_DV:768 + (h + 1) * RET_DV]
            out_ref[rows, sl] = _group_norm(o, gn_ref[0, :, sl]) * _silu(g)

        if nsub > 1:
            scores = [score(s, qb) for s, qb in tiles]
            outs = [values(sc, s, qb) for sc, (s, qb) in zip(scores, tiles)]
            for o, (s, qb) in zip(outs, tiles):
                finish(o, s, qb)
        else:
            for s, qb in tiles:
                finish(values(score(s, qb), s, qb), s, qb)
        if not latent:
            for s in range(nsub):
                keys = slice(s * seq, (s + 1) * seq)
                st_ref[s, 0, h] = _bdot_tn(kh[keys] * jnp.exp(lgf * (seq - 1.0 - pos_c)), vb[keys])
                st_ref[s, 1, h] = _bdot_tn(kh[keys] * jnp.exp(lgb * pos_c), vb[keys])


def _retention(l, ret, ld, gn, *, latent, cos=None, sin=None, s0=None, prev=None):
    seq = DEC_SEQ if latent else SEQ
    nsub = 1 if latent else CTX_SEQS_PER_STEP
    nb = DEC_BATCH if latent else BATCH // nsub
    off = LAT_BLOCK0 if latent else 0
    rows = nsub * seq
    in_specs = [pl.BlockSpec(memory_space=pltpu.SMEM),
                pl.BlockSpec((rows, 1024), lambda b: (b + off, 0)),
                _layer_spec(l, 1, 256)]
    args = [ld, ret, gn]
    out_specs = [pl.BlockSpec((rows, 256), lambda b: (b + off, 0))]
    out_shape = [jax.ShapeDtypeStruct((T_ALL, 256), F32)]
    aliases = {}
    if latent:
        in_specs += [pl.BlockSpec((seq, 256), lambda b: (0, 0)),
                     pl.BlockSpec((seq, 256), lambda b: (0, 0)),
                     pl.BlockSpec((1, 1, 2, RET_HEADS, RET_DK, RET_DV), lambda b: (b, l, 0, 0, 0, 0)),
                     _ANY]
        args += [cos, sin, s0, prev]
        aliases = {6: 0}
    else:
        out_specs.append(pl.BlockSpec((nsub, 2, RET_HEADS, RET_DK, RET_DV), lambda b: (b, 0, 0, 0, 0)))
        out_shape.append(jax.ShapeDtypeStruct((BATCH, 2, RET_HEADS, RET_DK, RET_DV), F32))
    return pl.pallas_call(
        functools.partial(_ret_kernel, layer=l, seq=seq, latent=latent),
        grid=(nb,),
        in_specs=in_specs, out_specs=out_specs, out_shape=out_shape,
        input_output_aliases=aliases,
        scratch_shapes=[pltpu.VMEM((RET_HEADS, RET_QBLOCK, 2 * seq - RET_QBLOCK), F32)],
        compiler_params=_cparams(("arbitrary",)),
        name="retention_lat" if latent else "retention_ctx",
    )(*args)


def _gla_kernel(gqk_ref, gv_ref, gg_ref, glr_ref, wg_ref, bg_ref, gn_ref, *rest, seq, latent):
    if latent:
        s0_ref, _, out_ref, gate_s, o_s, st_s, qst_s, ds_s, e_s = rest
    else:
        out_ref, st_ref, gate_s, o_s, st_s, qst_s, ds_s, e_s = rest
    c = GLA_CHUNK
    n = seq // c
    nsub = gqk_ref.shape[0] // seq
    hk = GLA_HEADS * GLA_DK
    lr = glr_ref[...]
    for d in range(2):
        pre = _bdot(lr[:, d * GLA_RANK:(d + 1) * GLA_RANK], wg_ref[0, d]) + bg_ref[0, d:d + 1]
        gate_s[d] = (jnp.minimum(pre, 0.0) - jnp.log(1.0 + jnp.exp(-jnp.abs(pre)))) / GLA_TAU
        for s in range(nsub):
            st_s[2 * s + d] = s0_ref[0, 0, d] if latent else jnp.zeros((GLA_HEADS * GLA_DV, hk), F32)

    nc = GLA_BLOCK_CHUNKS
    rb = nc * c
    ti = lax.broadcasted_iota(jnp.int32, (rb, rb), 0)
    tj = lax.broadcasted_iota(jnp.int32, (rb, rb), 1)
    same = (ti // c) == (tj // c)
    ones = lambda m: (same & m).astype(BF16)
    tri = [ones(tj <= ti), ones(tj >= ti)]
    mid = [ones((tj % c) < c // 2), ones((tj % c) >= c // 2)]
    tot = ones(tj == tj)
    lane_k = lax.broadcasted_iota(jnp.int32, (c, hk), 1)
    head_mask = [(lane_k // GLA_DK) == h for h in range(GLA_HEADS)]
    ai = lax.broadcasted_iota(jnp.int32, (GLA_HEADS * c, c), 0) % c
    aj = lax.broadcasted_iota(jnp.int32, (GLA_HEADS * c, c), 1)
    keep = [aj <= ai, aj >= ai]
    sr = lax.broadcasted_iota(jnp.int32, (GLA_HEADS * GLA_DV, hk), 0) // GLA_DV
    sc = lax.broadcasted_iota(jnp.int32, (GLA_HEADS * GLA_DV, hk), 1) // GLA_DK
    diag = sr == sc
    scale = GLA_DK ** -0.5
    d32 = functools.partial(jnp.dot, preferred_element_type=F32)

    def rows_of(i, size):
        return pl.ds(i * size, size) if isinstance(i, int) else pl.ds(pl.multiple_of(i * size, size), size)

    def local(block_ids):
        vs, q_att, k_att, k_st = {}, {}, {}, {}
        for bi in block_ids:
            rows = rows_of(bi, rb)
            q = gqk_ref[rows, 0:hk] * scale
            k = gqk_ref[rows, hk:2 * hk]
            vs[bi] = gv_ref[rows, :].astype(BF16)
            for d in range(2):
                gh, gl = _split(gate_s[d, rows, :])
                b = d32(tri[d], gh) + d32(tri[d], gl)
                b_mid = d32(mid[d], gh) + d32(mid[d], gl)
                b_end = d32(tot, gh) + d32(tot, gl)
                q_att[bi, d] = q * jnp.exp(b - b_mid)
                k_att[bi, d] = k * jnp.exp(b_mid - b)
                k_st[bi, d] = (k * jnp.exp(b_end - b)).astype(BF16)
                qst_s[d, rows, :] = (q * jnp.exp(b)).astype(BF16)
                decay = jnp.exp(b_end)
                for cc in range(nc):
                    e_s[d, bi * nc + cc] = decay[cc * c:cc * c + 8]
        tiles = [(bi, d, cc) for bi in block_ids for d in range(2) for cc in range(nc)]
        att = {}
        for bi, d, cc in tiles:
            r = slice(cc * c, (cc + 1) * c)
            qa = q_att[bi, d][r]
            q_stack = jnp.concatenate([jnp.where(head_mask[h], qa, 0.0) for h in range(GLA_HEADS)], axis=0)
            att[bi, d, cc] = jnp.where(keep[d], _bdot_nt(q_stack, k_att[bi, d][r]), 0.0).astype(BF16)
        for bi, d, cc in tiles:
            r = slice(cc * c, (cc + 1) * c)
            o = jnp.concatenate([d32(att[bi, d, cc][h * c:(h + 1) * c], vs[bi][r, h * GLA_DV:(h + 1) * GLA_DV])
                                 for h in range(GLA_HEADS)], axis=1)
            o_s[d, rows_of(bi * nc + cc, c), :] = o
        for bi, d, cc in tiles:
            r = slice(cc * c, (cc + 1) * c)
            ds_s[d, bi * nc + cc] = jnp.where(diag, lax.dot_general(
                vs[bi][r], k_st[bi, d][r], (((0,), (0,)), ((), ())), preferred_element_type=F32), 0.0)

    def recur(s, ci, d):
        g = s * n + ci
        rows = rows_of(g, c)
        st = st_s[2 * s + d]
        o_s[d, rows, :] += _bdot_nt(qst_s[d, rows, :], st)
        st_s[2 * s + d] = st * e_s[d, g, 0:1] + ds_s[d, g]

    def recur_body(i, carry):
        for s in range(nsub):
            recur(s, i, 0)
            recur(s, n - 1 - i, 1)
        return carry

    local(list(range(nsub * n // nc)))
    if n <= 4:
        for i in range(n):
            recur_body(i, 0)
    else:
        lax.fori_loop(0, n, recur_body, 0, unroll=2)

    o = o_s[0] + o_s[1]
    for h in range(GLA_HEADS):
        sl = slice(h * GLA_DV, (h + 1) * GLA_DV)
        out_ref[:, sl] = _group_norm(o[:, sl], gn_ref[0, :, sl]) * _silu(gg_ref[:, sl])
    if not latent:
        hv = GLA_HEADS * GLA_DV
        eye = (lax.broadcasted_iota(jnp.int32, (hv, hv), 0)
               == lax.broadcasted_iota(jnp.int32, (hv, hv), 1)).astype(BF16)
        tn = lambda a: lax.dot_general(a, eye, (((0,), (0,)), ((), ())), preferred_element_type=F32)
        for s in range(nsub):
            for d in range(2):
                st = st_s[2 * s + d]
                hi, lo = _split(st)
                lo2 = (st - hi.astype(F32) - lo.astype(F32)).astype(BF16)
                s_all = tn(hi) + tn(lo) + tn(lo2)
                for h in range(GLA_HEADS):
                    st_ref[s, d, h] = s_all[h * GLA_DK:(h + 1) * GLA_DK, h * GLA_DV:(h + 1) * GLA_DV]


def _gla(l, gqk, gv, gg, glr, wg, bg, gn, *, latent, s0=None, prev=None):
    seq = DEC_SEQ if latent else SEQ
    nsub = 1 if latent else CTX_SEQS_PER_STEP
    nb = DEC_BATCH if latent else BATCH // nsub
    off = LAT_BLOCK0 if latent else 0
    rows = nsub * seq
    hk = GLA_HEADS * GLA_DK
    hv = GLA_HEADS * GLA_DV
    tok = lambda w: pl.BlockSpec((rows, w), lambda b: (b + off, 0))
    in_specs = [tok(256), tok(256), tok(256), tok(128),
                _layer_spec(l, 2, GLA_RANK, hk), _layer_spec(l, 2, hk), _layer_spec(l, 1, 256)]
    args = [gqk, gv, gg, glr, wg, bg, gn]
    out_specs = [tok(256)]
    out_shape = [jax.ShapeDtypeStruct((T_ALL, 256), F32)]
    aliases = {}
    if latent:
        in_specs += [pl.BlockSpec((1, 1, 2, hv, hk), lambda b: (b, l, 0, 0, 0)), _ANY]
        args += [s0, prev]
        aliases = {8: 0}
    else:
        out_specs.append(pl.BlockSpec((nsub, 2, GLA_HEADS, GLA_DK, GLA_DV), lambda b: (b, 0, 0, 0, 0)))
        out_shape.append(jax.ShapeDtypeStruct((BATCH, 2, GLA_HEADS, GLA_DK, GLA_DV), F32))
    return pl.pallas_call(
        functools.partial(_gla_kernel, seq=seq, latent=latent),
        grid=(nb,),
        in_specs=in_specs, out_specs=out_specs, out_shape=out_shape,
        input_output_aliases=aliases,
        scratch_shapes=[pltpu.VMEM((2, rows, hk), F32), pltpu.VMEM((2, rows, hv), F32),
                        pltpu.VMEM((2 * nsub, hv, hk), F32), pltpu.VMEM((2, rows, hk), BF16),
                        pltpu.VMEM((2, rows // GLA_CHUNK, hv, hk), F32),
                        pltpu.VMEM((2, rows // GLA_CHUNK, 8, hk), F32)],
        compiler_params=_cparams(("parallel",)),
        name="gla_lat" if latent else "gla_ctx",
    )(*args)


S5_W = S5_CHUNK * S5_GROUP_CH
S5_P2 = 2 * S5_STATE
S5_ROWS = T_ALL // S5_CHUNK
S5_ROWS_CTX = T_CTX // S5_CHUNK
S5_LANE_GROUPS = 128 // S5_GROUP_CH


S5_TE_ROWS = S5_W + 4 * S5_P2


def _s5_toeplitz_kernel(cc_ref, wfr_ref, wb_ref, ff_ref, te_ref, ffb_ref):
    lane = lax.broadcasted_iota(jnp.int32, (S5_GROUP_CH, S5_W), 1)
    swapped = lambda t: jnp.concatenate([t[S5_STATE:], t[:S5_STATE]], axis=0)
    for t in range(S5_TABLE_GROUPS):
        wfr = wfr_ref[t]
        wb = wb_ref[t]
        kf = _dot3(cc_ref[t, 0], wfr)
        kb = _dot3(cc_ref[t, 1], wb)
        blocks = []
        for i in range(S5_CHUNK):
            sf = ((i + 1 - S5_CHUNK) * S5_GROUP_CH) % S5_W
            fwd = jnp.where(lane < (i + 1) * S5_GROUP_CH, pltpu.roll(kf, sf, 1) if sf else kf, 0.0)
            bwd = jnp.where(lane >= i * S5_GROUP_CH, pltpu.roll(kb, i * S5_GROUP_CH, 1) if i else kb, 0.0)
            blocks.append(fwd + bwd)
        te_ref[t] = jnp.concatenate(blocks + [wfr, wb, swapped(wfr), swapped(wb)], axis=0).astype(BF16)
        ffb_ref[t] = ff_ref[t].astype(BF16)


S5_TABLE_GROUPS = 4


def _s5_toeplitz(cc, wfr, wb, ff):
    tg = S5_TABLE_GROUPS
    n = cc.shape[0]
    tbl = pl.BlockSpec((tg, S5_P2, S5_W), lambda i: (i, 0, 0))
    te_spec = pl.BlockSpec((tg, S5_TE_ROWS, S5_W), lambda i: (i, 0, 0))
    ff_spec = pl.BlockSpec((tg, 2 * S5_P2, S5_W), lambda i: (i, 0, 0))
    return pl.pallas_call(
        _s5_toeplitz_kernel,
        grid=(n // tg,),
        in_specs=[pl.BlockSpec((tg, 2, S5_GROUP_CH, S5_P2), lambda i: (i, 0, 0, 0)), tbl, tbl, ff_spec],
        out_specs=[te_spec, ff_spec],
        out_shape=[jax.ShapeDtypeStruct((n, S5_TE_ROWS, S5_W), BF16),
                   jax.ShapeDtypeStruct((n, 2 * S5_P2, S5_W), BF16)],
        compiler_params=_cparams(("parallel",)),
        name="s5_toeplitz",
    )(cc, wfr, wb, ff)


def _s5_kernel(u_ref, te_ref, ff_ref, d_ref, lam_ref, x0_ref, y_ref, fin_ref, xs_s, ps_s):
    C = S5_CHUNK
    ng = S5_LANE_GROUPS

    for g in range(ng):
        u = u_ref[:, g * S5_W:(g + 1) * S5_W]
        r = _bdot_nt(u, te_ref[0, g])
        y_ref[:, g * S5_W:(g + 1) * S5_W] = r[:, 0:S5_W] + d_ref[0, :, g * S5_W:(g + 1) * S5_W] * u
        for t in range(4):
            xs_s[t * ng + g] = r[:, S5_W + t * S5_P2:S5_W + (t + 1) * S5_P2]

    w = ng * S5_P2

    def carry(base, nseq, nchunks, init_f, init_b):
        a_f, b_f = lam_ref[0, 0, 0:1], lam_ref[0, 0, 1:2]
        a_b, b_b = lam_ref[0, 1, 0:1], lam_ref[0, 1, 1:2]
        load = lambda t, rows: jnp.concatenate([xs_s[t * ng + g, rows, :] for g in range(ng)], axis=1)

        def body(i, st):
            sf, tf, sb, tb = st
            rf = pl.ds(base + i, nseq, stride=nchunks)
            rb = pl.ds(base + (nchunks - 1 - i), nseq, stride=nchunks)
            for g in range(ng):
                ps_s[g, rf, :] = sf[:, g * S5_P2:(g + 1) * S5_P2]
                ps_s[ng + g, rb, :] = sb[:, g * S5_P2:(g + 1) * S5_P2]
            return (a_f * sf + b_f * tf + load(0, rf), a_f * tf - b_f * sf + load(2, rf),
                    a_b * sb + b_b * tb + load(1, rb), a_b * tb - b_b * sb + load(3, rb))

        first = (lax.broadcasted_iota(jnp.int32, (1, w), 1) % S5_P2) < S5_STATE
        swap = lambda s: jnp.where(first, pltpu.roll(s, w - S5_STATE, 1), pltpu.roll(s, S5_STATE, 1))
        return lax.fori_loop(0, nchunks, body, (init_f, swap(init_f), init_b, swap(init_b)))

    zeros = jnp.zeros((BATCH, w), F32)
    fin = carry(0, BATCH, SEQ // C, zeros, zeros)
    fin_ref[0] = fin[0]
    fin_ref[1] = fin[2]
    carry(S5_ROWS_CTX, DEC_BATCH, DEC_SEQ // C, x0_ref[0, 0], x0_ref[0, 1])

    for g in range(ng):
        p = jnp.concatenate([ps_s[g], ps_s[ng + g]], axis=1)
        y_ref[:, g * S5_W:(g + 1) * S5_W] += _bdot(p, ff_ref[0, g])


def _s5(l, su_rows, te, ff, d_rows, lam, x0):
    ng = S5_LANE_GROUPS
    w = ng * S5_P2
    rows = pl.BlockSpec((S5_ROWS, ng * S5_W), lambda i: (0, i))
    return pl.pallas_call(
        _s5_kernel,
        grid=(S5_GROUPS // ng,),
        in_specs=[rows,
                  pl.BlockSpec((1, ng, S5_TE_ROWS, S5_W), lambda i: (l, i, 0, 0)),
                  pl.BlockSpec((1, ng, 2 * S5_P2, S5_W), lambda i: (l, i, 0, 0)),
                  pl.BlockSpec((1, 1, ng * S5_W), lambda i: (l, 0, i)),
                  pl.BlockSpec((1, 2, 2, w), lambda i: (l, 0, 0, i)),
                  pl.BlockSpec((1, 2, DEC_BATCH, w), lambda i: (l, 0, 0, i))],
        out_specs=[rows, pl.BlockSpec((2, BATCH, w), lambda i: (0, 0, i))],
        out_shape=[jax.ShapeDtypeStruct((S5_ROWS, S5_GROUPS * S5_W), F32),
                   jax.ShapeDtypeStruct((2, BATCH, S5_GROUPS * S5_P2), F32)],
        scratch_shapes=[pltpu.VMEM((4 * ng, S5_ROWS, S5_P2), F32), pltpu.VMEM((2 * ng, S5_ROWS, S5_P2), F32)],
        compiler_params=_cparams(("parallel",)),
        name="s5_scan",
    )(su_rows, te, ff, d_rows, lam, x0)


def _s5_tables(lam_re, lam_im, log_dt, b_re, b_im, c_re, c_im):
    C, G, P, H = S5_CHUNK, S5_GROUPS, S5_STATE, S5_GROUP_CH
    L = lam_re.shape[0]
    dt = jnp.exp(log_dt)[..., None]
    ar, ai = lam_re * dt, lam_im * dt

    steps = jnp.arange(C + 1, dtype=F32)
    mag = jnp.exp(ar[..., None] * steps)
    pw_re, pw_im = mag * jnp.cos(ai[..., None] * steps), mag * jnp.sin(ai[..., None] * steps)
    exact = functools.partial(jnp.einsum, precision=lax.Precision.HIGHEST)
    tau_np = np.arange(C * H) // H

    def power(d, t):
        sel = jnp.asarray((np.arange(C + 1)[:, None] == np.asarray(t)[None, :]).astype(np.float32))
        return exact('lgpt,tn->lgpn', pw_re[:, d], sel), exact('lgpt,tn->lgpn', pw_im[:, d], sel)

    lr, li = pw_re[..., 1], pw_im[..., 1]
    den = lam_re * lam_re + lam_im * lam_im
    qr = ((lr - 1.0) * lam_re + li * lam_im) / den
    qi = (li * lam_re - (lr - 1.0) * lam_im) / den
    bbr = qr[..., None] * b_re - qi[..., None] * b_im
    bbi = qr[..., None] * b_im + qi[..., None] * b_re
    chan = jnp.asarray((np.arange(H)[:, None] == (np.arange(C * H) % H)[None, :]).astype(np.float32))
    lanes = lambda a: exact('ldgph,hn->ldgpn', a, chan)
    bbr, bbi = lanes(bbr), lanes(bbi)
    c_t = lambda a: exact('ldghp,hn->ldgpn', a, chan)
    ctr, cti = c_t(c_re), c_t(c_im)

    def w_of(d, t):
        pr, pi = power(d, t)
        return jnp.concatenate([pr * bbr[:, d] - pi * bbi[:, d], pr * bbi[:, d] + pi * bbr[:, d]], axis=-2)

    def f_of(d, t):
        pr, pi = power(d, t)
        return jnp.concatenate([ctr[:, d] * pr - cti[:, d] * pi, -(ctr[:, d] * pi + cti[:, d] * pr)], axis=-2)

    wfr = w_of(0, (C - 1) - tau_np).reshape(L * G, 2 * P, C * H)
    wb = w_of(1, tau_np).reshape(L * G, 2 * P, C * H)
    cc = jnp.concatenate([c_re, -c_im], axis=-1).transpose(0, 2, 1, 3, 4).reshape(L * G, 2, H, 2 * P)
    ff = jnp.concatenate([f_of(0, tau_np + 1), f_of(1, C - tau_np)], axis=-2)
    tables = _s5_toeplitz(cc, wfr, wb, ff.reshape(L * G, 4 * P, C * H))
    tables = [t.reshape((L, G) + t.shape[1:]) for t in tables]
    cr, ci = pw_re[..., C], pw_im[..., C]
    a = jnp.concatenate([cr, cr], axis=-1).reshape(L, 2, 1, G * 2 * P)
    b = jnp.concatenate([-ci, ci], axis=-1).reshape(L, 2, 1, G * 2 * P)
    return tables, jnp.concatenate([a, b], axis=2)


def _softmax_pv(s_parts, v_parts):
    m = s_parts[0].max(axis=-1, keepdims=True)
    for s in s_parts[1:]:
        m = jnp.maximum(m, s.max(axis=-1, keepdims=True))
    o = None
    l = None
    for s, v in zip(s_parts, v_parts):
        p = jnp.exp(s - m)
        pl_ = p.sum(axis=-1, keepdims=True)
        po = _bdot(p, v)
        o = po if o is None else o + po
        l = pl_ if l is None else l + pl_
    return o / l


def _attn_ctx_kernel(q_ref, k_ref, v_ref, o_ref):
    scale = NA_DH ** -0.5
    tiles = [(h, slice(s * SEQ, (s + 1) * SEQ)) for h in range(NA_HEADS) for s in range(CTX_SEQS_PER_STEP)]
    scores = [_bdot_nt(q_ref[h, rows, :], k_ref[h, rows, :]) * scale for h, rows in tiles]
    for s, (h, rows) in zip(scores, tiles):
        o_ref[h, rows, :] = _softmax_pv([s], [v_ref[h, rows, :]])


def _attn_ctx(nq, nk, nv):
    spec = pl.BlockSpec((NA_HEADS, CTX_SEQS_PER_STEP * SEQ, NA_DH), lambda b: (0, b, 0))
    return pl.pallas_call(
        _attn_ctx_kernel,
        grid=(BATCH // CTX_SEQS_PER_STEP,),
        in_specs=[spec, spec, spec],
        out_specs=spec,
        out_shape=jax.ShapeDtypeStruct((NA_HEADS, T_ALL, NA_DH), F32),
        compiler_params=_cparams(("parallel",)),
        name="attn_ctx",
    )(nq, nk, nv)


def _attn_lat_kernel(q_ref, k_ref, v_ref, kc_ref, vc_ref, tb_ref, _, o_ref, bias_s):
    @pl.when(pl.program_id(1) == 0)
    def _build_bias():
        bias_s[...] = jnp.full((DEC_SEQ, DEC_SEQ), -jnp.inf, F32)
        for r in range(GRID_ROWS):
            rs = min(max(r - NA_KH // 2, 0), GRID_ROWS - NA_KH)
            dr0 = rs - r + NA_WIN_H - 1
            bias_s[r * GRID_W:(r + 1) * GRID_W, rs * GRID_W:(rs + NA_KH) * GRID_W] = (
                tb_ref[0, 0, :, dr0 * GRID_W:(dr0 + NA_KH) * GRID_W])

    scale = NA_DH ** -0.5
    kb = k_ref[0].astype(BF16)
    vb = v_ref[0].astype(BF16)
    kc = kc_ref[0, 0, 0].astype(BF16)
    vc = vc_ref[0, 0, 0].astype(BF16)
    tq = NA_QBLOCK
    q_rows = tq // GRID_W
    for qb in range(DEC_SEQ // tq):
        rows = slice(qb * tq, (qb + 1) * tq)
        starts = [min(max(r - NA_KH // 2, 0), GRID_ROWS - NA_KH) for r in range(qb * q_rows, (qb + 1) * q_rows)]
        keys = slice(min(starts) * GRID_W // 128 * 128, -(-(max(starts) + NA_KH) * GRID_W // 128) * 128)
        qh = q_ref[0, rows, :].astype(BF16)
        s_loc = _bdot_nt(qh, kb[keys]) * scale + bias_s[rows, keys]
        s_ctx = _bdot_nt(qh, kc) * scale
        o_ref[0, rows, :] = _softmax_pv([s_loc, s_ctx], [vb[keys], vc])


def _attn_lat(l, nq, nk, nv, kc, vc, tb, prev):
    tok = pl.BlockSpec((1, DEC_SEQ, NA_DH), lambda h, b: (h, b + LAT_BLOCK0, 0))
    cache = pl.BlockSpec((1, 1, 1, PAST_LEN, NA_DH), lambda h, b: (b, l, h, 0, 0))
    return pl.pallas_call(
        _attn_lat_kernel,
        grid=(NA_HEADS, DEC_BATCH),
        in_specs=[tok, tok, tok, cache, cache,
                  pl.BlockSpec((1, 1, GRID_W, NA_REL_ROWS * GRID_W), lambda h, b: (l, h, 0, 0)),
                  _ANY],
        out_specs=tok,
        out_shape=jax.ShapeDtypeStruct((NA_HEADS, T_ALL, NA_DH), F32),
        input_output_aliases={6: 0},
        scratch_shapes=[pltpu.VMEM((DEC_SEQ, DEC_SEQ), F32)],
        compiler_params=_cparams(("arbitrary", "arbitrary")),
        name="attn_lat",
    )(nq, nk, nv, kc, vc, tb, prev)


def _na_tables(rpb):
    col = np.arange(GRID_W)
    col_start = np.clip(col - NA_WIN_W // 2, 0, GRID_W - NA_WIN_W)
    col_in = (col[None, :] >= col_start[:, None]) & (col[None, :] < col_start[:, None] + NA_WIN_W)
    col_idx = np.clip(col[None, :] - col[:, None] + NA_WIN_W - 1, 0, 2 * NA_WIN_W - 2)
    onehot = (col_idx[:, :, None] == np.arange(2 * NA_WIN_W - 1)[None, None, :]).astype(np.float32)
    tb = jnp.einsum('lhrd,qkd->lhqrk', rpb, jnp.asarray(onehot), precision=lax.Precision.HIGHEST)
    tb = jnp.where(jnp.asarray(col_in)[None, None, :, None, :], tb, -jnp.inf)
    return tb.reshape(rpb.shape[0], NA_HEADS, GRID_W, NA_REL_ROWS * GRID_W)


def _merge_kernel(x_ref, mod_ref, g_ref, ret_ref, s5y_ref, gla_ref, na_ref,
                  wglu_ref, bglu_ref, wbr_ref, wmg_ref, bmg_ref, wout_ref, o_ref, y_s):
    x = x_ref[...]
    mod = mod_ref[0, 0]
    hb = (_rms(x, g_ref[0, 0:1]) * (1.0 + mod[1:2]) + mod[0:1]).astype(BF16)

    def gate_pre(n):
        return _bdot(hb, wmg_ref[0, :, n * D_MODEL:(n + 1) * D_MODEL]) + bmg_ref[0, :, n * D_MODEL:(n + 1) * D_MODEL]

    acc = _sigmoid(gate_pre(0)) * _bdot(ret_ref[...], wbr_ref[0, 0])
    acc += _sigmoid(gate_pre(2)) * _bdot(gla_ref[...], wbr_ref[0, 2])
    na = jnp.concatenate([na_ref[hh].astype(BF16) for hh in range(NA_HEADS)], axis=1)
    acc += _sigmoid(gate_pre(3)) * _bdot(na, wbr_ref[0, 3])
    s5_gate = gate_pre(1)

    rows = TOKEN_TILE // S5_CHUNK
    ng = S5_LANE_GROUPS
    piece = lax.broadcasted_iota(jnp.int32, (rows, 128), 1) // S5_GROUP_CH
    for lb in range(S5_GROUPS // ng):
        for m in range(S5_CHUNK // ng):
            cols = [(lb * ng + g) * S5_W + m * 128 for g in range(ng)]
            out = _piece_transpose([s5y_ref[:, c0:c0 + 128] for c0 in cols], piece)
            for il in range(ng):
                y_s[lb, pl.ds(m * ng + il, rows, stride=S5_CHUNK), :] = out[il]
    y = jnp.concatenate([y_s[lb] for lb in range(S5_GROUPS // ng)], axis=1)
    y = 0.5 * y * (1.0 + jnp.tanh(math.sqrt(2.0 / math.pi) * (y + 0.044715 * (y * y * y))))
    z = _bdot(y, wglu_ref[0]) + bglu_ref[0]
    s5_out = z[:, 0:BRANCH_W] * _sigmoid(z[:, BRANCH_W:2 * BRANCH_W])
    acc += _sigmoid(s5_gate) * _bdot(s5_out, wbr_ref[0, 1])
    m = _bdot(acc, wout_ref[0])
    o_ref[...] = x + mod[2:3] * _rms(m, g_ref[0, 1:2])


def _merge(l, x, mod, g_norm, ret_o, s5_y, gla_o, na_o, wglu, bglu, wbr, wmg, bmg, wout):
    tm = TOKEN_TILE
    tok = lambda w: pl.BlockSpec((tm, w), lambda i: (i, 0))
    return pl.pallas_call(
        _merge_kernel,
        grid=(T_ALL // tm,),
        in_specs=[tok(D_MODEL), _mod_spec(l), _layer_spec(l, 4, D_MODEL),
                  tok(256), pl.BlockSpec((tm // S5_CHUNK, S5_GROUPS * S5_W), lambda i: (i, 0)), tok(256),
                  pl.BlockSpec((NA_HEADS, tm, NA_DH), lambda i: (0, i, 0)),
                  _layer_spec(l, 256, 512), _layer_spec(l, 1, 512),
                  _layer_spec(l, N_BRANCH, BRANCH_W, D_MODEL, single_buffer=True),
                  _layer_spec(l, D_MODEL, N_BRANCH * D_MODEL, single_buffer=True),
                  _layer_spec(l, 1, N_BRANCH * D_MODEL),
                  _layer_spec(l, D_MODEL, D_MODEL, single_buffer=True)],
        out_specs=tok(D_MODEL),
        out_shape=jax.ShapeDtypeStruct((T_ALL, D_MODEL), F32),
        scratch_shapes=[pltpu.VMEM((S5_GROUPS // S5_LANE_GROUPS, tm, 128), F32)],
        compiler_params=_cparams(("parallel",)),
        name="merge",
    )(x, mod, g_norm, ret_o, s5_y, gla_o, na_o, wglu, bglu, wbr, wmg, bmg, wout)


FF_TILE = 1024


def _mlp_kernel(x_ref, mod_ref, g_ref, w1_ref, w2_ref, *o_refs):
    x = x_ref[...]
    mod = mod_ref[0, 0]
    hb = (_rms(x, g_ref[0, 2:3]) * (1.0 + mod[4:5]) + mod[3:4]).astype(BF16)
    nj = D_FF // FF_TILE
    up = lambda j: _bdot(hb, w1_ref[0, :, j * FF_TILE:(j + 1) * FF_TILE])
    f = None
    pre = up(0)
    for j in range(nj):
        nxt = up(j + 1) if j + 1 < nj else None
        a = jnp.maximum(pre, 0.0)
        part = _bdot(a * a, w2_ref[0, j * FF_TILE:(j + 1) * FF_TILE, :])
        f = part if f is None else f + part
        pre = nxt
    y = x + mod[5:6] * _rms(f, g_ref[0, 3:4])
    if len(o_refs) == 1:
        o_refs[0][...] = y
    else:
        ctx_tiles = T_CTX // TOKEN_TILE

        @pl.when(pl.program_id(0) < ctx_tiles)
        def _store_ctx():
            o_refs[0][...] = y

        @pl.when(pl.program_id(0) >= ctx_tiles)
        def _store_lat():
            o_refs[1][...] = y


def _mlp(l, x, mod, g_norm, w1, w2, *, split_out):
    tm = TOKEN_TILE
    tok = pl.BlockSpec((tm, D_MODEL), lambda i: (i, 0))
    if split_out:
        ctx_tiles = T_CTX // tm
        out_specs = [pl.BlockSpec((tm, D_MODEL), lambda i: (jnp.minimum(i, ctx_tiles - 1), 0)),
                     pl.BlockSpec((tm, D_MODEL), lambda i: (jnp.maximum(i - ctx_tiles, 0), 0))]
        out_shape = [jax.ShapeDtypeStruct((T_CTX, D_MODEL), F32), jax.ShapeDtypeStruct((T_LAT, D_MODEL), F32)]
    else:
        out_specs = tok
        out_shape = jax.ShapeDtypeStruct((T_ALL, D_MODEL), F32)
    return pl.pallas_call(
        _mlp_kernel,
        grid=(T_ALL // tm,),
        in_specs=[tok, _mod_spec(l), _layer_spec(l, 4, D_MODEL),
                  _layer_spec(l, D_MODEL, D_FF, single_buffer=True),
                  _layer_spec(l, D_FF, D_MODEL, single_buffer=True)],
        out_specs=out_specs,
        out_shape=out_shape,
        compiler_params=_cparams(("arbitrary",)),
        name="mlp",
    )(x, mod, g_norm, w1, w2)


def _rope_tables():
    half = RET_DK // 2
    nf = half // 2
    t = jnp.arange(DEC_SEQ)
    row = (t // GRID_W).astype(F32)
    col = (t % GRID_W).astype(F32)
    inv = ROPE_BASE ** (-jnp.arange(nf, dtype=F32) / nf)
    ang_r = row[:, None] * inv[None, :]
    ang_c = col[:, None] * inv[None, :]
    cos = jnp.concatenate([jnp.cos(ang_r)] * 2 + [jnp.cos(ang_c)] * 2, axis=1)
    sin = jnp.concatenate([-jnp.sin(ang_r), jnp.sin(ang_r), -jnp.sin(ang_c), jnp.sin(ang_c)], axis=1)
    return jnp.tile(cos, (1, RET_HEADS)), jnp.tile(sin, (1, RET_HEADS))


def _gla_state_in(st):
    eye = jnp.eye(GLA_HEADS, dtype=st.dtype)
    t = jnp.einsum('bldhkv,hg->bldhvgk', st, eye)
    return t.reshape(st.shape[0], st.shape[1], 2, GLA_HEADS * GLA_DV, GLA_HEADS * GLA_DK)


def kernel(x_prompt, x_sample, c, cache_na_k, cache_na_v, state_ret, state_s5, state_gla, c_ctx, w_ada, b_ada, g_norm, w_in, ret_log_decay, ret_gn, s5_lambda_re, s5_lambda_im, s5_log_dt, s5_b_re, s5_b_im, s5_c_re, s5_c_im, s5_d, s5_w_glu, s5_b_glu, gla_w_gate, gla_b_gate, gla_gn, na_rpb, w_branch, w_merge, b_merge, w_out, w_mlp1, w_mlp2):
    depth = w_in.shape[0]
    x = (x_prompt.reshape(T_CTX, D_MODEL), x_sample.reshape(T_LAT, D_MODEL))
    cc = jnp.concatenate([c_ctx[None], c, jnp.zeros((N_MOD_ROWS - 1 - DEC_BATCH, D_MODEL), F32)], axis=0)
    mod = _ada(cc, w_ada, b_ada).reshape(depth, N_MOD_ROWS, 6, D_MODEL)

    cos, sin = _rope_tables()
    ret_gn3, gla_gn3 = (a.reshape(depth, 1, BRANCH_W) for a in (ret_gn, gla_gn))
    s5_d_rows = jnp.broadcast_to(s5_d.reshape(depth, S5_GROUPS, 1, S5_GROUP_CH),
                                 (depth, S5_GROUPS, S5_CHUNK, S5_GROUP_CH)).reshape(depth, 1, S5_GROUPS * S5_W)
    b_glu3 = s5_b_glu.reshape(depth, 1, 2 * BRANCH_W)
    b_mg3 = b_merge.reshape(depth, 1, N_BRANCH * D_MODEL)
    cache_k = cache_na_k.transpose(0, 1, 3, 2, 4)
    cache_v = cache_na_v.transpose(0, 1, 3, 2, 4)
    na_tb = _na_tables(na_rpb)
    gla_s0 = _gla_state_in(state_gla)
    s5_tables, s5_lam = _s5_tables(s5_lambda_re, s5_lambda_im, s5_log_dt, s5_b_re, s5_b_im,
                                   s5_c_re, s5_c_im)
    s5_x0 = state_s5.transpose(1, 2, 0, 3, 5, 4).reshape(depth, 2, DEC_BATCH, S5_GROUPS * S5_P2)

    ks_l, vs_l, ret_l, s5_l, gla_l = [], [], [], [], []
    for l in range(depth):
        proj = _inproj(l, x, mod, g_norm, w_in)
        if l == 0:
            x, proj = proj[0], proj[1:]
        ret, su, gqk, gv, gg, glr, nq, nk, nv = proj

        ret_o, st_ret = _retention(l, ret, ret_log_decay, ret_gn3, latent=False)
        ret_o, = _retention(l, ret, ret_log_decay, ret_gn3, latent=True, cos=cos, sin=sin, s0=state_ret,
                            prev=ret_o)

        s5_y, s5_fin = _s5(l, su, *s5_tables, s5_d_rows, s5_lam, s5_x0)

        gla_o, st_gla = _gla(l, gqk, gv, gg, glr, gla_w_gate, gla_b_gate, gla_gn3, latent=False)
        gla_o, = _gla(l, gqk, gv, gg, glr, gla_w_gate, gla_b_gate, gla_gn3, latent=True, s0=gla_s0,
                      prev=gla_o)

        na_o = _attn_ctx(nq, nk, nv)
        na_o = _attn_lat(l, nq, nk, nv, cache_k, cache_v, na_tb, na_o)

        x = _merge(l, x, mod, g_norm, ret_o, s5_y, gla_o, na_o,
                   s5_w_glu, b_glu3, w_branch, w_merge, b_mg3, w_out)
        x = _mlp(l, x, mod, g_norm, w_mlp1, w_mlp2, split_out=(l == depth - 1))

        ks_l.append(nk)
        vs_l.append(nv)
        ret_l.append(st_ret)
        s5_l.append(s5_fin)
        gla_l.append(st_gla)

    y_prompt = x[0].reshape(BATCH, SEQ, D_MODEL)
    y_sample = x[1].reshape(DEC_BATCH, DEC_SEQ, D_MODEL)

    def cache_out(per_layer):
        a = jnp.stack(per_layer, axis=0)[:, :, :T_CTX].reshape(depth, NA_HEADS, BATCH, SEQ, NA_DH)
        return a.transpose(2, 0, 3, 1, 4)

    s5_out = jnp.stack(s5_l, axis=0).reshape(depth, 2, BATCH, S5_GROUPS, 2, S5_STATE)
    return (y_prompt, y_sample, cache_out(ks_l), cache_out(vs_l), jnp.stack(ret_l, axis=1),
            s5_out.transpose(2, 0, 1, 3, 5, 4), jnp.stack(gla_l, axis=1))
```

```python
import functools
import math

import numpy as np
import jax
import jax.numpy as jnp
from jax import lax
from jax.experimental import pallas as pl
from jax.experimental.pallas import tpu as pltpu

F32 = jnp.float32
BF16 = jnp.bfloat16

D_MODEL = 1024
BATCH = 16
SEQ = 256
DEPTH = 4
DEC_BATCH = 4
DEC_SEQ = 1024
PAST_LEN = 256
GRID_W = 64
N_BRANCH = 4
BRANCH_W = 256
RET_HEADS = 4
RET_DK = 64
RET_DV = 64
S5_GROUPS = 16
S5_GROUP_CH = 16
S5_STATE = 64
GLA_HEADS = 4
GLA_DK = 32
GLA_DV = 64
GLA_RANK = 16
GLA_TAU = 16.0
NA_HEADS = 4
NA_DH = 64
NA_WIN_H = 8
NA_WIN_W = 16
D_FF = 4 * D_MODEL
ROPE_BASE = 10000.0
EPS = 1e-6

T_CTX = BATCH * SEQ
T_LAT = DEC_BATCH * DEC_SEQ
T_ALL = T_CTX + T_LAT
LAT_BLOCK0 = T_CTX // DEC_SEQ
N_MOD_ROWS = 8
TOKEN_TILE = 512
CTX_SEQS_PER_STEP = DEC_SEQ // SEQ
GLA_CHUNK = 64
GLA_BLOCK_CHUNKS = 4
S5_CHUNK = 16
RET_QBLOCK = 256
NA_QBLOCK = 256
GRID_ROWS = DEC_SEQ // GRID_W
NA_KH = min(NA_WIN_H, GRID_ROWS)
NA_REL_ROWS = 2 * NA_WIN_H - 1
VMEM_LIMIT = 56 * 1024 * 1024
D_IN = 2848
IN_TAIL_COL = 2048


def _cparams(sem):
    return pltpu.CompilerParams(dimension_semantics=sem, vmem_limit_bytes=VMEM_LIMIT)


def _bdot(a, b):
    return jnp.dot(a.astype(BF16), b.astype(BF16), preferred_element_type=F32)


def _bdot_nt(a, b):
    return lax.dot_general(a.astype(BF16), b.astype(BF16), (((1,), (1,)), ((), ())),
                           preferred_element_type=F32)


def _bdot_tn(a, b):
    return lax.dot_general(a.astype(BF16), b.astype(BF16), (((0,), (0,)), ((), ())),
                           preferred_element_type=F32)


def _split(a):
    hi = a.astype(BF16)
    lo = (a - hi.astype(F32)).astype(BF16)
    return hi, lo


def _dot3(a, b):
    ah, al = _split(a)
    bh, bl = _split(b)
    d = functools.partial(jnp.dot, preferred_element_type=F32)
    return d(ah, bh) + d(al, bh) + d(ah, bl)


def _sigmoid(x):
    return 0.5 * jnp.tanh(0.5 * x) + 0.5


def _silu(x):
    return x * _sigmoid(x)


def _rms(x, g):
    return x * lax.rsqrt(jnp.mean(x * x, axis=-1, keepdims=True) + EPS) * g


def _group_norm(o, g):
    mu = jnp.mean(o, axis=-1, keepdims=True)
    xc = o - mu
    return xc * lax.rsqrt(jnp.mean(xc * xc, axis=-1, keepdims=True) + EPS) * g


def _mod_row(i):
    ctx_tiles = T_CTX // TOKEN_TILE
    return jnp.where(i < ctx_tiles, 0, 1 + (i - ctx_tiles) // (DEC_SEQ // TOKEN_TILE))


def _mod_spec(l):
    return pl.BlockSpec((1, 1, 6, D_MODEL), lambda i: (l, _mod_row(i), 0, 0))


def _layer_spec(l, *shape, single_buffer=False):
    mode = pl.Buffered(1) if single_buffer else None
    return pl.BlockSpec((1,) + shape, lambda *_: (l,) + (0,) * len(shape), pipeline_mode=mode)


_ANY = pl.BlockSpec(memory_space=pl.ANY)


ADA_TILE = 1536


def _ada_kernel(c_ref, w_ref, b_ref, o_ref):
    a = _silu(c_ref[...])
    o_ref[0] = _bdot(a, w_ref[0]) + b_ref[0]


def _ada(cc, w_ada, b_ada):
    n = 6 * D_MODEL
    return pl.pallas_call(
        _ada_kernel,
        grid=(DEPTH, n // ADA_TILE),
        in_specs=[pl.BlockSpec((N_MOD_ROWS, D_MODEL), lambda l, j: (0, 0)),
                  pl.BlockSpec((1, D_MODEL, ADA_TILE), lambda l, j: (l, 0, j)),
                  pl.BlockSpec((1, 1, ADA_TILE), lambda l, j: (l, 0, j))],
        out_specs=pl.BlockSpec((1, N_MOD_ROWS, ADA_TILE), lambda l, j: (l, 0, j)),
        out_shape=jax.ShapeDtypeStruct((DEPTH, N_MOD_ROWS, n), F32),
        compiler_params=_cparams(("parallel", "parallel")),
        name="ada_mod",
    )(cc, w_ada, b_ada.reshape(DEPTH, 1, n))


def _piece_transpose(blocks, piece):
    x = list(blocks)
    n = len(x)
    d = n // 2
    while d >= 1:
        low = (piece & d) == 0
        for v in range(n):
            if v & d:
                continue
            a, b = x[v], x[v + d]
            x[v] = jnp.where(low, a, pltpu.roll(b, d * S5_GROUP_CH, 1))
            x[v + d] = jnp.where(low, pltpu.roll(a, 128 - d * S5_GROUP_CH, 1), b)
        d //= 2
    return x


def _inproj_kernel(*refs, first):
    if first:
        (xa_ref, xb_ref, mod_ref, g_ref, w_ref, x_out_ref,
         ret_ref, s5_ref, gqk_ref, gv_ref, gg_ref, glr_ref, nq_ref, nk_ref, nv_ref, su_s) = refs
        x = jnp.where(pl.program_id(0) < T_CTX // TOKEN_TILE, xa_ref[...], xb_ref[...])
        x_out_ref[...] = x
    else:
        (x_ref, mod_ref, g_ref, w_ref,
         ret_ref, s5_ref, gqk_ref, gv_ref, gg_ref, glr_ref, nq_ref, nk_ref, nv_ref, su_s) = refs
        x = x_ref[...]
    mod = mod_ref[0, 0]
    h = _rms(x, g_ref[0, 0:1]) * (1.0 + mod[1:2]) + mod[0:1]
    hb = h.astype(BF16)

    def proj(lo, hi):
        return _bdot_nt(hb, w_ref[0, lo:hi, :])

    ret_ref[...] = proj(0, 1024)
    su = proj(1024, 1280)
    rows = TOKEN_TILE // S5_CHUNK
    ng = S5_LANE_GROUPS
    piece = lax.broadcasted_iota(jnp.int32, (rows, 128), 1) // S5_GROUP_CH
    for lb in range(S5_GROUPS // ng):
        su_s[lb] = su[:, lb * 128:(lb + 1) * 128]
        for m in range(S5_CHUNK // ng):
            out = _piece_transpose(
                [su_s[lb, pl.ds(m * ng + jl, rows, stride=S5_CHUNK), :] for jl in range(ng)], piece)
            for g in range(ng):
                col = (lb * ng + g) * S5_W + m * 128
                s5_ref[:, col:col + 128] = out[g]
    gqk_ref[...] = proj(1280, 1536)
    gv_ref[...] = proj(1536, 1792)
    gg_ref[...] = proj(1792, 2048)
    glr_ref[...] = proj(IN_TAIL_COL, IN_TAIL_COL + 128)
    lo = IN_TAIL_COL + 2 * GLA_RANK
    for ref in (nq_ref, nk_ref, nv_ref):
        r = proj(lo, lo + NA_HEADS * NA_DH)
        for hh in range(NA_HEADS):
            ref[hh] = r[:, hh * NA_DH:(hh + 1) * NA_DH]
        lo += NA_HEADS * NA_DH


def _inproj(l, xs, mod, g_norm, w_in):
    tm = TOKEN_TILE
    first = isinstance(xs, tuple)
    tok = lambda w: pl.BlockSpec((tm, w), lambda i: (i, 0))
    head = pl.BlockSpec((NA_HEADS, tm, NA_DH), lambda i: (0, i, 0))
    tshape = lambda w: jax.ShapeDtypeStruct((T_ALL, w), F32)
    hshape = jax.ShapeDtypeStruct((NA_HEADS, T_ALL, NA_DH), F32)
    ctx_tiles = T_CTX // tm
    if first:
        x_specs = [pl.BlockSpec((tm, D_MODEL), lambda i: (jnp.minimum(i, ctx_tiles - 1), 0)),
                   pl.BlockSpec((tm, D_MODEL), lambda i: (jnp.maximum(i - ctx_tiles, 0), 0))]
        x_args = list(xs)
    else:
        x_specs, x_args = [tok(D_MODEL)], [xs]
    return pl.pallas_call(
        functools.partial(_inproj_kernel, first=first),
        grid=(T_ALL // tm,),
        in_specs=x_specs + [_mod_spec(l), _layer_spec(l, 4, D_MODEL),
                            _layer_spec(l, D_IN, D_MODEL, single_buffer=True)],
        out_specs=([tok(D_MODEL)] if first else [])
        + [tok(1024), pl.BlockSpec((tm // S5_CHUNK, S5_GROUPS * S5_W), lambda i: (i, 0)),
           tok(256), tok(256), tok(256), tok(128), head, head, head],
        out_shape=([tshape(D_MODEL)] if first else [])
        + [tshape(1024), jax.ShapeDtypeStruct((S5_ROWS, S5_GROUPS * S5_W), F32),
           tshape(256), tshape(256), tshape(256), tshape(128), hshape, hshape, hshape],
        scratch_shapes=[pltpu.VMEM((S5_GROUPS // S5_LANE_GROUPS, tm, 128), F32)],
        compiler_params=_cparams(("parallel",)),
        name="in_proj",
    )(*x_args, mod, g_norm, w_in)


def _rope_rotate(x, lane):
    first = (lane % 32) < 16
    w = x.shape[-1]
    return jnp.where(first, pltpu.roll(x, w - 16, 1), pltpu.roll(x, 16, 1))


def _ret_kernel(ld_ref, ret_ref, gn_ref, *rest, layer, seq, latent):
    if latent:
        cos_ref, sin_ref, s0_ref, _, out_ref, dec_s = rest
    else:
        out_ref, st_ref, dec_s = rest
    tq = RET_QBLOCK
    nq = seq // tq
    width = dec_s.shape[-1]

    @pl.when(pl.program_id(0) == 0)
    def _build_decay():
        rel = (lax.broadcasted_iota(jnp.int32, (tq, width), 0) + (nq - 1) * tq
               - lax.broadcasted_iota(jnp.int32, (tq, width), 1)).astype(F32)
        for h in range(RET_HEADS):
            dec_s[h] = (jnp.where(rel >= 0, jnp.exp(ld_ref[layer, 0, h] * jnp.maximum(rel, 0.0)), 0.0)
                        + jnp.where(rel <= 0, jnp.exp(ld_ref[layer, 1, h] * jnp.maximum(-rel, 0.0)), 0.0))

    nrows = ret_ref.shape[0]
    nsub = nrows // seq
    q = ret_ref[:, 0:256]
    k = ret_ref[:, 256:512]
    if latent:
        lane = lax.broadcasted_iota(jnp.int32, (nrows, 256), 1)
        cos = cos_ref[...]
        sin = sin_ref[...]
        q = q * cos + _rope_rotate(q, lane) * sin
        k = k * cos + _rope_rotate(k, lane) * sin
    k = k * (RET_DK ** -0.5)
    pos_c = lax.broadcasted_iota(jnp.int32, (seq, 1), 0).astype(F32)
    tiles = [(s, qb) for s in range(nsub) for qb in range(nq)]
    for h in range(RET_HEADS):
        lgf = ld_ref[layer, 0, h]
        lgb = ld_ref[layer, 1, h]
        sl = slice(h * RET_DK, (h + 1) * RET_DK)
        qh = q[:, sl]
        kh = k[:, sl]
        kb = kh.astype(BF16)
        vb = ret_ref[:, 512 + h * RET_DV:512 + (h + 1) * RET_DV].astype(BF16)
        if latent:
            q_init = jnp.concatenate([qh * jnp.exp(lgf * (pos_c + 1.0)),
                                      qh * jnp.exp(lgb * (seq - pos_c))], axis=1)
            s_init = jnp.concatenate([s0_ref[0, 0, 0, h], s0_ref[0, 0, 1, h]], axis=0)

        def score(s, qb):
            w0 = (nq - 1 - qb) * tq
            rows = slice(s * seq + qb * tq, s * seq + (qb + 1) * tq)
            keys = slice(s * seq, (s + 1) * seq)
            return (_bdot_nt(qh[rows], kb[keys]) * dec_s[h, :, w0:w0 + seq]).astype(BF16)

        def values(sc, s, qb):
            o = jnp.dot(sc, vb[s * seq:(s + 1) * seq], preferred_element_type=F32)
            if latent:
                o = o + _bdot(q_init[qb * tq:(qb + 1) * tq], s_init)
            return o

        def finish(o, s, qb):
            rows = slice(s * seq + qb * tq, s * seq + (qb + 1) * tq)
            g = ret_ref[rows, 768 + h * RET_DV:768 + (h + 1) * RET_DV]
            out_ref[rows, sl] = _group_norm(o, gn_ref[0, :, sl]) * _silu(g)

        if nsub > 1:
            scores = [score(s, qb) for s, qb in tiles]
            outs = [values(sc, s, qb) for sc, (s, qb) in zip(scores, tiles)]
            for o, (s, qb) in zip(outs, tiles):
                finish(o, s, qb)
        else:
            for s, qb in tiles:
                finish(values(score(s, qb), s, qb), s, qb)
        if not latent:
            for s in range(nsub):
                keys = slice(s * seq, (s + 1) * seq)
                st_ref[s, 0, h] = _bdot_tn(kh[keys] * jnp.exp(lgf * (seq - 1.0 - pos_c)), vb[keys])
                st_ref[s, 1, h] = _bdot_tn(kh[keys] * jnp.exp(lgb * pos_c), vb[keys])


def _retention(l, ret, ld, gn, *, latent, cos=None, sin=None, s0=None, prev=None):
    seq = DEC_SEQ if latent else SEQ
    nsub = 1 if latent else CTX_SEQS_PER_STEP
    nb = DEC_BATCH if latent else BATCH // nsub
    off = LAT_BLOCK0 if latent else 0
    rows = nsub * seq
    in_specs = [pl.BlockSpec(memory_space=pltpu.SMEM),
                pl.BlockSpec((rows, 1024), lambda b: (b + off, 0)),
                _layer_spec(l, 1, 256)]
    args = [ld, ret, gn]
    out_specs = [pl.BlockSpec((rows, 256), lambda b: (b + off, 0))]
    out_shape = [jax.ShapeDtypeStruct((T_ALL, 256), F32)]
    aliases = {}
    if latent:
        in_specs += [pl.BlockSpec((seq, 256), lambda b: (0, 0)),
                     pl.BlockSpec((seq, 256), lambda b: (0, 0)),
                     pl.BlockSpec((1, 1, 2, RET_HEADS, RET_DK, RET_DV), lambda b: (b, l, 0, 0, 0, 0)),
                     _ANY]
        args += [cos, sin, s0, prev]
        aliases = {6: 0}
    else:
        out_specs.append(pl.BlockSpec((nsub, 2, RET_HEADS, RET_DK, RET_DV), lambda b: (b, 0, 0, 0, 0)))
        out_shape.append(jax.ShapeDtypeStruct((BATCH, 2, RET_HEADS, RET_DK, RET_DV), F32))
    return pl.pallas_call(
        functools.partial(_ret_kernel, layer=l, seq=seq, latent=latent),
        grid=(nb,),
        in_specs=in_specs, out_specs=out_specs, out_shape=out_shape,
        input_output_aliases=aliases,
        scratch_shapes=[pltpu.VMEM((RET_HEADS, RET_QBLOCK, 2 * seq - RET_QBLOCK), F32)],
        compiler_params=_cparams(("arbitrary",)),
        name="retention_lat" if latent else "retention_ctx",
    )(*args)


def _gla_kernel(gqk_ref, gv_ref, gg_ref, glr_ref, wg_ref, bg_ref, gn_ref, *rest, seq, latent):
    if latent:
        s0_ref, _, out_ref, gate_s, o_s, st_s, qst_s, ds_s, e_s = rest
    else:
        out_ref, st_ref, gate_s, o_s, st_s, qst_s, ds_s, e_s = rest
    c = GLA_CHUNK
    n = seq // c
    nsub = gqk_ref.shape[0] // seq
    hk = GLA_HEADS * GLA_DK
    lr = glr_ref[...]
    for d in range(2):
        pre = _bdot(lr[:, d * GLA_RANK:(d + 1) * GLA_RANK], wg_ref[0, d]) + bg_ref[0, d:d + 1]
        gate_s[d] = (jnp.minimum(pre, 0.0) - jnp.log(1.0 + jnp.exp(-jnp.abs(pre)))) / GLA_TAU
        for s in range(nsub):
            st_s[2 * s + d] = s0_ref[0, 0, d] if latent else jnp.zeros((GLA_HEADS * GLA_DV, hk), F32)

    nc = GLA_BLOCK_CHUNKS
    rb = nc * c
    ti = lax.broadcasted_iota(jnp.int32, (rb, rb), 0)
    tj = lax.broadcasted_iota(jnp.int32, (rb, rb), 1)
    same = (ti // c) == (tj // c)
    ones = lambda m: (same & m).astype(BF16)
    tri = [ones(tj <= ti), ones(tj >= ti)]
    mid = [ones((tj % c) < c // 2), ones((tj % c) >= c // 2)]
    tot = ones(tj == tj)
    lane_k = lax.broadcasted_iota(jnp.int32, (c, hk), 1)
    head_mask = [(lane_k // GLA_DK) == h for h in range(GLA_HEADS)]
    ai = lax.broadcasted_iota(jnp.int32, (GLA_HEADS * c, c), 0) % c
    aj = lax.broadcasted_iota(jnp.int32, (GLA_HEADS * c, c), 1)
    keep = [aj <= ai, aj >= ai]
    sr = lax.broadcasted_iota(jnp.int32, (GLA_HEADS * GLA_DV, hk), 0) // GLA_DV
    sc = lax.broadcasted_iota(jnp.int32, (GLA_HEADS * GLA_DV, hk), 1) // GLA_DK
    diag = sr == sc
    scale = GLA_DK ** -0.5
    d32 = functools.partial(jnp.dot, preferred_element_type=F32)

    def rows_of(i, size):
        return pl.ds(i * size, size) if isinstance(i, int) else pl.ds(pl.multiple_of(i * size, size), size)

    def local(block_ids):
        vs, q_att, k_att, k_st = {}, {}, {}, {}
        for bi in block_ids:
            rows = rows_of(bi, rb)
            q = gqk_ref[rows, 0:hk] * scale
            k = gqk_ref[rows, hk:2 * hk]
            vs[bi] = gv_ref[rows, :].astype(BF16)
            for d in range(2):
                gh, gl = _split(gate_s[d, rows, :])
                b = d32(tri[d], gh) + d32(tri[d], gl)
                b_mid = d32(mid[d], gh) + d32(mid[d], gl)
                b_end = d32(tot, gh) + d32(tot, gl)
                q_att[bi, d] = q * jnp.exp(b - b_mid)
                k_att[bi, d] = k * jnp.exp(b_mid - b)
                k_st[bi, d] = (k * jnp.exp(b_end - b)).astype(BF16)
                qst_s[d, rows, :] = (q * jnp.exp(b)).astype(BF16)
                decay = jnp.exp(b_end)
                for cc in range(nc):
                    e_s[d, bi * nc + cc] = decay[cc * c:cc * c + 8]
        tiles = [(bi, d, cc) for bi in block_ids for d in range(2) for cc in range(nc)]
        att = {}
        for bi, d, cc in tiles:
            r = slice(cc * c, (cc + 1) * c)
            qa = q_att[bi, d][r]
            q_stack = jnp.concatenate([jnp.where(head_mask[h], qa, 0.0) for h in range(GLA_HEADS)], axis=0)
            att[bi, d, cc] = jnp.where(keep[d], _bdot_nt(q_stack, k_att[bi, d][r]), 0.0).astype(BF16)
        for bi, d, cc in tiles:
            r = slice(cc * c, (cc + 1) * c)
            o = jnp.concatenate([d32(att[bi, d, cc][h * c:(h + 1) * c], vs[bi][r, h * GLA_DV:(h + 1) * GLA_DV])
                                 for h in range(GLA_HEADS)], axis=1)
            o_s[d, rows_of(bi * nc + cc, c), :] = o
        for bi, d, cc in tiles:
            r = slice(cc * c, (cc + 1) * c)
            ds_s[d, bi * nc + cc] = jnp.where(diag, lax.dot_general(
                vs[bi][r], k_st[bi, d][r], (((0,), (0,)), ((), ())), preferred_element_type=F32), 0.0)

    def recur(s, ci, d):
        g = s * n + ci
        rows = rows_of(g, c)
        st = st_s[2 * s + d]
        o_s[d, rows, :] += _bdot_nt(qst_s[d, rows, :], st)
        st_s[2 * s + d] = st * e_s[d, g, 0:1] + ds_s[d, g]

    def recur_body(i, carry):
        for s in range(nsub):
            recur(s, i, 0)
            recur(s, n - 1 - i, 1)
        return carry

    local(list(range(nsub * n // nc)))
    if n <= 4:
        for i in range(n):
            recur_body(i, 0)
    else:
        lax.fori_loop(0, n, recur_body, 0, unroll=2)

    o = o_s[0] + o_s[1]
    for h in range(GLA_HEADS):
        sl = slice(h * GLA_DV, (h + 1) * GLA_DV)
        out_ref[:, sl] = _group_norm(o[:, sl], gn_ref[0, :, sl]) * _silu(gg_ref[:, sl])
    if not latent:
        hv = GLA_HEADS * GLA_DV
        eye = (lax.broadcasted_iota(jnp.int32, (hv, hv), 0)
               == lax.broadcasted_iota(jnp.int32, (hv, hv), 1)).astype(BF16)
        tn = lambda a: lax.dot_general(a, eye, (((0,), (0,)), ((), ())), preferred_element_type=F32)
        for s in range(nsub):
            for d in range(2):
                st = st_s[2 * s + d]
                hi, lo = _split(st)
                lo2 = (st - hi.astype(F32) - lo.astype(F32)).astype(BF16)
                s_all = tn(hi) + tn(lo) + tn(lo2)
                for h in range(GLA_HEADS):
                    st_ref[s, d, h] = s_all[h * GLA_DK:(h + 1) * GLA_DK, h * GLA_DV:(h + 1) * GLA_DV]


def _gla(l, gqk, gv, gg, glr, wg, bg, gn, *, latent, s0=None, prev=None):
    seq = DEC_SEQ if latent else SEQ
    nsub = 1 if latent else CTX_SEQS_PER_STEP
    nb = DEC_BATCH if latent else BATCH // nsub
    off = LAT_BLOCK0 if latent else 0
    rows = nsub * seq
    hk = GLA_HEADS * GLA_DK
    hv = GLA_HEADS * GLA_DV
    tok = lambda w: pl.BlockSpec((rows, w), lambda b: (b + off, 0))
    in_specs = [tok(256), tok(256), tok(256), tok(128),
                _layer_spec(l, 2, GLA_RANK, hk), _layer_spec(l, 2, hk), _layer_spec(l, 1, 256)]
    args = [gqk, gv, gg, glr, wg, bg, gn]
    out_specs = [tok(256)]
    out_shape = [jax.ShapeDtypeStruct((T_ALL, 256), F32)]
    aliases = {}
    if latent:
        in_specs += [pl.BlockSpec((1, 1, 2, hv, hk), lambda b: (b, l, 0, 0, 0)), _ANY]
        args += [s0, prev]
        aliases = {8: 0}
    else:
        out_specs.append(pl.BlockSpec((nsub, 2, GLA_HEADS, GLA_DK, GLA_DV), lambda b: (b, 0, 0, 0, 0)))
        out_shape.append(jax.ShapeDtypeStruct((BATCH, 2, GLA_HEADS, GLA_DK, GLA_DV), F32))
    return pl.pallas_call(
        functools.partial(_gla_kernel, seq=seq, latent=latent),
        grid=(nb,),
        in_specs=in_specs, out_specs=out_specs, out_shape=out_shape,
        input_output_aliases=aliases,
        scratch_shapes=[pltpu.VMEM((2, rows, hk), F32), pltpu.VMEM((2, rows, hv), F32),
                        pltpu.VMEM((2 * nsub, hv, hk), F32), pltpu.VMEM((2, rows, hk), BF16),
                        pltpu.VMEM((2, rows // GLA_CHUNK, hv, hk), F32),
                        pltpu.VMEM((2, rows // GLA_CHUNK, 8, hk), F32)],
        compiler_params=_cparams(("parallel",)),
        name="gla_lat" if latent else "gla_ctx",
    )(*args)


S5_W = S5_CHUNK * S5_GROUP_CH
S5_P2 = 2 * S5_STATE
S5_ROWS = T_ALL // S5_CHUNK
S5_ROWS_CTX = T_CTX // S5_CHUNK
S5_LANE_GROUPS = 128 // S5_GROUP_CH


S5_TE_ROWS = S5_W + 4 * S5_P2


def _s5_toeplitz_kernel(cc_ref, pwr_ref, pwi_ref, bbr_ref, bbi_ref, ctr_ref, cti_ref, te_ref, ffb_ref):
    lane = lax.broadcasted_iota(jnp.int32, (S5_GROUP_CH, S5_W), 1)
    swapped = lambda t: jnp.concatenate([t[S5_STATE:], t[:S5_STATE]], axis=0)
    for t in range(S5_TABLE_GROUPS):
        def times_b(v, d):
            pr, pi, br, bi = pwr_ref[0, t, v], pwi_ref[0, t, v], bbr_ref[0, d, t], bbi_ref[0, d, t]
            return jnp.concatenate([pr * br - pi * bi, pr * bi + pi * br], axis=0)

        def times_c(v, d):
            pr, pi, cr, ci = pwr_ref[0, t, v], pwi_ref[0, t, v], ctr_ref[0, d, t], cti_ref[0, d, t]
            return jnp.concatenate([cr * pr - ci * pi, -(cr * pi + ci * pr)], axis=0)

        wfr = times_b(0, 0)
        wb = times_b(1, 1)
        kf = _dot3(cc_ref[0, 0, t], wfr)
        kb = _dot3(cc_ref[0, 1, t], wb)
        blocks = []
        for i in range(S5_CHUNK):
            sf = ((i + 1 - S5_CHUNK) * S5_GROUP_CH) % S5_W
            fwd = jnp.where(lane < (i + 1) * S5_GROUP_CH, pltpu.roll(kf, sf, 1) if sf else kf, 0.0)
            bwd = jnp.where(lane >= i * S5_GROUP_CH, pltpu.roll(kb, i * S5_GROUP_CH, 1) if i else kb, 0.0)
            blocks.append(fwd + bwd)
        te_ref[0, t] = jnp.concatenate(blocks + [wfr, wb, swapped(wfr), swapped(wb)], axis=0).astype(BF16)
        ffb_ref[0, t] = jnp.concatenate([times_c(2, 0), times_c(3, 1)], axis=0).astype(BF16)


S5_TABLE_GROUPS = 4


def _s5_toeplitz(cc, pwr, pwi, bbr, bbi, ctr, cti):
    tg = S5_TABLE_GROUPS
    nl, _, ngroups = cc.shape[:3]
    by_dir = lambda *tail: pl.BlockSpec((1, 2, tg) + tail, lambda l, i: (l, 0, i) + (0,) * len(tail))
    by_group = lambda *tail: pl.BlockSpec((1, tg) + tail, lambda l, i: (l, i) + (0,) * len(tail))
    lanes = by_dir(S5_STATE, S5_W)
    powers = by_group(4, S5_STATE, S5_W)
    return pl.pallas_call(
        _s5_toeplitz_kernel,
        grid=(nl, ngroups // tg),
        in_specs=[by_dir(S5_GROUP_CH, S5_P2), powers, powers, lanes, lanes, lanes, lanes],
        out_specs=[by_group(S5_TE_ROWS, S5_W), by_group(2 * S5_P2, S5_W)],
        out_shape=[jax.ShapeDtypeStruct((nl, ngroups, S5_TE_ROWS, S5_W), BF16),
                   jax.ShapeDtypeStruct((nl, ngroups, 2 * S5_P2, S5_W), BF16)],
        compiler_params=_cparams(("parallel", "parallel")),
        name="s5_toeplitz",
    )(cc, pwr, pwi, bbr, bbi, ctr, cti)


def _s5_kernel(u_ref, te_ref, ff_ref, d_ref, lam_ref, x0_ref, y_ref, fin_ref, xs_s, ps_s):
    C = S5_CHUNK
    ng = S5_LANE_GROUPS

    for g in range(ng):
        u = u_ref[:, g * S5_W:(g + 1) * S5_W]
        r = _bdot_nt(u, te_ref[0, g])
        y_ref[:, g * S5_W:(g + 1) * S5_W] = r[:, 0:S5_W] + d_ref[0, :, g * S5_W:(g + 1) * S5_W] * u
        for t in range(4):
            xs_s[t * ng + g] = r[:, S5_W + t * S5_P2:S5_W + (t + 1) * S5_P2]

    w = ng * S5_P2

    def carry(base, nseq, nchunks, init_f, init_b):
        a_f, b_f = lam_ref[0, 0, 0:1], lam_ref[0, 0, 1:2]
        a_b, b_b = lam_ref[0, 1, 0:1], lam_ref[0, 1, 1:2]
        load = lambda t, rows: jnp.concatenate([xs_s[t * ng + g, rows, :] for g in range(ng)], axis=1)

        def body(i, st):
            sf, tf, sb, tb = st
            rf = pl.ds(base + i, nseq, stride=nchunks)
            rb = pl.ds(base + (nchunks - 1 - i), nseq, stride=nchunks)
            for g in range(ng):
                ps_s[g, rf, :] = sf[:, g * S5_P2:(g + 1) * S5_P2]
                ps_s[ng + g, rb, :] = sb[:, g * S5_P2:(g + 1) * S5_P2]
            return (a_f * sf + b_f * tf + load(0, rf), a_f * tf - b_f * sf + load(2, rf),
                    a_b * sb + b_b * tb + load(1, rb), a_b * tb - b_b * sb + load(3, rb))

        first = (lax.broadcasted_iota(jnp.int32, (1, w), 1) % S5_P2) < S5_STATE
        swap = lambda s: jnp.where(first, pltpu.roll(s, w - S5_STATE, 1), pltpu.roll(s, S5_STATE, 1))
        return lax.fori_loop(0, nchunks, body, (init_f, swap(init_f), init_b, swap(init_b)))

    zeros = jnp.zeros((BATCH, w), F32)
    fin = carry(0, BATCH, SEQ // C, zeros, zeros)
    fin_ref[0] = fin[0]
    fin_ref[1] = fin[2]
    carry(S5_ROWS_CTX, DEC_BATCH, DEC_SEQ // C, x0_ref[0, 0], x0_ref[0, 1])

    for g in range(ng):
        p = jnp.concatenate([ps_s[g], ps_s[ng + g]], axis=1)
        y_ref[:, g * S5_W:(g + 1) * S5_W] += _bdot(p, ff_ref[0, g])


def _s5(l, su_rows, te, ff, d_rows, lam, x0):
    ng = S5_LANE_GROUPS
    w = ng * S5_P2
    rows = pl.BlockSpec((S5_ROWS, ng * S5_W), lambda i: (0, i))
    return pl.pallas_call(
        _s5_kernel,
        grid=(S5_GROUPS // ng,),
        in_specs=[rows,
                  pl.BlockSpec((1, ng, S5_TE_ROWS, S5_W), lambda i: (l, i, 0, 0)),
                  pl.BlockSpec((1, ng, 2 * S5_P2, S5_W), lambda i: (l, i, 0, 0)),
                  pl.BlockSpec((1, 1, ng * S5_W), lambda i: (l, 0, i)),
                  pl.BlockSpec((1, 2, 2, w), lambda i: (l, 0, 0, i)),
                  pl.BlockSpec((1, 2, DEC_BATCH, w), lambda i: (l, 0, 0, i))],
        out_specs=[rows, pl.BlockSpec((2, BATCH, w), lambda i: (0, 0, i))],
        out_shape=[jax.ShapeDtypeStruct((S5_ROWS, S5_GROUPS * S5_W), F32),
                   jax.ShapeDtypeStruct((2, BATCH, S5_GROUPS * S5_P2), F32)],
        scratch_shapes=[pltpu.VMEM((4 * ng, S5_ROWS, S5_P2), F32), pltpu.VMEM((2 * ng, S5_ROWS, S5_P2), F32)],
        compiler_params=_cparams(("parallel",)),
        name="s5_scan",
    )(su_rows, te, ff, d_rows, lam, x0)


def _s5_tables(lam_re, lam_im, log_dt, b_re, b_im, c_re, c_im):
    C, G, P, H = S5_CHUNK, S5_GROUPS, S5_STATE, S5_GROUP_CH
    L = lam_re.shape[0]
    dt = jnp.exp(log_dt)[..., None]
    ar, ai = lam_re * dt, lam_im * dt

    steps = jnp.arange(C + 1, dtype=F32)
    mag = jnp.exp(ar[..., None] * steps)
    pw_re, pw_im = mag * jnp.cos(ai[..., None] * steps), mag * jnp.sin(ai[..., None] * steps)
    exact = functools.partial(jnp.einsum, precision=lax.Precision.HIGHEST)
    tau_np = np.arange(C * H) // H

    patterns = [(0, (C - 1) - tau_np), (1, tau_np), (0, tau_np + 1), (1, C - tau_np)]
    sel = np.zeros((2, len(patterns), C + 1, C * H), np.float32)
    for v, (d, t) in enumerate(patterns):
        sel[d, v, t, np.arange(C * H)] = 1.0
    sel = jnp.asarray(sel)
    pwr = exact('ldgpt,dvtn->lgvpn', pw_re, sel)
    pwi = exact('ldgpt,dvtn->lgvpn', pw_im, sel)

    lr, li = pw_re[..., 1], pw_im[..., 1]
    den = lam_re * lam_re + lam_im * lam_im
    qr = ((lr - 1.0) * lam_re + li * lam_im) / den
    qi = (li * lam_re - (lr - 1.0) * lam_im) / den
    bbr = qr[..., None] * b_re - qi[..., None] * b_im
    bbi = qr[..., None] * b_im + qi[..., None] * b_re
    chan = jnp.asarray((np.arange(H)[:, None] == (np.arange(C * H) % H)[None, :]).astype(np.float32))
    lanes = lambda a: exact('ldgph,hn->ldgpn', a, chan)
    bbr, bbi = lanes(bbr), lanes(bbi)
    c_t = lambda a: exact('ldghp,hn->ldgpn', a, chan)
    ctr, cti = c_t(c_re), c_t(c_im)

    cc = jnp.concatenate([c_re, -c_im], axis=-1)
    tables = _s5_toeplitz(cc, pwr, pwi, bbr, bbi, ctr, cti)
    cr, ci = pw_re[..., C], pw_im[..., C]
    a = jnp.concatenate([cr, cr], axis=-1).reshape(L, 2, 1, G * 2 * P)
    b = jnp.concatenate([-ci, ci], axis=-1).reshape(L, 2, 1, G * 2 * P)
    return tables, jnp.concatenate([a, b], axis=2)


def _softmax_pv(s_parts, v_parts):
    m = s_parts[0].max(axis=-1, keepdims=True)
    for s in s_parts[1:]:
        m = jnp.maximum(m, s.max(axis=-1, keepdims=True))
    o = None
    l = None
    for s, v in zip(s_parts, v_parts):
        p = jnp.exp(s - m)
        pl_ = p.sum(axis=-1, keepdims=True)
        po = _bdot(p, v)
        o = po if o is None else o + po
        l = pl_ if l is None else l + pl_
    return o / l


def _attn_ctx_kernel(q_ref, k_ref, v_ref, o_ref):
    scale = NA_DH ** -0.5
    tiles = [(h, slice(s * SEQ, (s + 1) * SEQ)) for h in range(NA_HEADS) for s in range(CTX_SEQS_PER_STEP)]
    scores = [_bdot_nt(q_ref[h, rows, :], k_ref[h, rows, :]) * scale for h, rows in tiles]
    for s, (h, rows) in zip(scores, tiles):
        o_ref[h, rows, :] = _softmax_pv([s], [v_ref[h, rows, :]])


def _attn_ctx(nq, nk, nv):
    spec = pl.BlockSpec((NA_HEADS, CTX_SEQS_PER_STEP * SEQ, NA_DH), lambda b: (0, b, 0))
    return pl.pallas_call(
        _attn_ctx_kernel,
        grid=(BATCH // CTX_SEQS_PER_STEP,),
        in_specs=[spec, spec, spec],
        out_specs=spec,
        out_shape=jax.ShapeDtypeStruct((NA_HEADS, T_ALL, NA_DH), F32),
        compiler_params=_cparams(("parallel",)),
        name="attn_ctx",
    )(nq, nk, nv)


def _attn_lat_kernel(q_ref, k_ref, v_ref, kc_ref, vc_ref, tb_ref, _, o_ref, bias_s):
    @pl.when(pl.program_id(1) == 0)
    def _build_bias():
        bias_s[...] = jnp.full((DEC_SEQ, DEC_SEQ), -jnp.inf, F32)
        for r in range(GRID_ROWS):
            rs = min(max(r - NA_KH // 2, 0), GRID_ROWS - NA_KH)
            dr0 = rs - r + NA_WIN_H - 1
            bias_s[r * GRID_W:(r + 1) * GRID_W, rs * GRID_W:(rs + NA_KH) * GRID_W] = (
                tb_ref[0, 0, :, dr0 * GRID_W:(dr0 + NA_KH) * GRID_W])

    scale = NA_DH ** -0.5
    kb = k_ref[0].astype(BF16)
    vb = v_ref[0].astype(BF16)
    kc = kc_ref[0, 0, 0].astype(BF16)
    vc = vc_ref[0, 0, 0].astype(BF16)
    tq = NA_QBLOCK
    q_rows = tq // GRID_W
    for qb in range(DEC_SEQ // tq):
        rows = slice(qb * tq, (qb + 1) * tq)
        starts = [min(max(r - NA_KH // 2, 0), GRID_ROWS - NA_KH) for r in range(qb * q_rows, (qb + 1) * q_rows)]
        keys = slice(min(starts) * GRID_W // 128 * 128, -(-(max(starts) + NA_KH) * GRID_W // 128) * 128)
        qh = q_ref[0, rows, :].astype(BF16)
        s_loc = _bdot_nt(qh, kb[keys]) * scale + bias_s[rows, keys]
        s_ctx = _bdot_nt(qh, kc) * scale
        o_ref[0, rows, :] = _softmax_pv([s_loc, s_ctx], [vb[keys], vc])


def _attn_lat(l, nq, nk, nv, kc, vc, tb, prev):
    tok = pl.BlockSpec((1, DEC_SEQ, NA_DH), lambda h, b: (h, b + LAT_BLOCK0, 0))
    cache = pl.BlockSpec((1, 1, 1, PAST_LEN, NA_DH), lambda h, b: (b, l, h, 0, 0))
    return pl.pallas_call(
        _attn_lat_kernel,
        grid=(NA_HEADS, DEC_BATCH),
        in_specs=[tok, tok, tok, cache, cache,
                  pl.BlockSpec((1, 1, GRID_W, NA_REL_ROWS * GRID_W), lambda h, b: (l, h, 0, 0)),
                  _ANY],
        out_specs=tok,
        out_shape=jax.ShapeDtypeStruct((NA_HEADS, T_ALL, NA_DH), F32),
        input_output_aliases={6: 0},
        scratch_shapes=[pltpu.VMEM((DEC_SEQ, DEC_SEQ), F32)],
        compiler_params=_cparams(("arbitrary", "arbitrary")),
        name="attn_lat",
    )(nq, nk, nv, kc, vc, tb, prev)


def _na_tables(rpb):
    col = np.arange(GRID_W)
    col_start = np.clip(col - NA_WIN_W // 2, 0, GRID_W - NA_WIN_W)
    col_in = (col[None, :] >= col_start[:, None]) & (col[None, :] < col_start[:, None] + NA_WIN_W)
    col_idx = np.clip(col[None, :] - col[:, None] + NA_WIN_W - 1, 0, 2 * NA_WIN_W - 2)
    onehot = (col_idx[:, :, None] == np.arange(2 * NA_WIN_W - 1)[None, None, :]).astype(np.float32)
    tb = jnp.einsum('lhrd,qkd->lhqrk', rpb, jnp.asarray(onehot), precision=lax.Precision.HIGHEST)
    tb = jnp.where(jnp.asarray(col_in)[None, None, :, None, :], tb, -jnp.inf)
    return tb.reshape(rpb.shape[0], NA_HEADS, GRID_W, NA_REL_ROWS * GRID_W)


def _merge_kernel(x_ref, mod_ref, g_ref, ret_ref, s5y_ref, gla_ref, na_ref,
                  wglu_ref, bglu_ref, wbr_ref, wmg_ref, bmg_ref, wout_ref, o_ref, y_s):
    x = x_ref[...]
    mod = mod_ref[0, 0]
    hb = (_rms(x, g_ref[0, 0:1]) * (1.0 + mod[1:2]) + mod[0:1]).astype(BF16)

    def gate_pre(n):
        return _bdot(hb, wmg_ref[0, :, n * D_MODEL:(n + 1) * D_MODEL]) + bmg_ref[0, :, n * D_MODEL:(n + 1) * D_MODEL]

    acc = _sigmoid(gate_pre(0)) * _bdot(ret_ref[...], wbr_ref[0, 0])
    acc += _sigmoid(gate_pre(2)) * _bdot(gla_ref[...], wbr_ref[0, 2])
    na = jnp.concatenate([na_ref[hh].astype(BF16) for hh in range(NA_HEADS)], axis=1)
    acc += _sigmoid(gate_pre(3)) * _bdot(na, wbr_ref[0, 3])
    s5_gate = gate_pre(1)

    rows = TOKEN_TILE // S5_CHUNK
    ng = S5_LANE_GROUPS
    piece = lax.broadcasted_iota(jnp.int32, (rows, 128), 1) // S5_GROUP_CH
    for lb in range(S5_GROUPS // ng):
        for m in range(S5_CHUNK // ng):
            cols = [(lb * ng + g) * S5_W + m * 128 for g in range(ng)]
            out = _piece_transpose([s5y_ref[:, c0:c0 + 128] for c0 in cols], piece)
            for il in range(ng):
                y_s[lb, pl.ds(m * ng + il, rows, stride=S5_CHUNK), :] = out[il]
    y = jnp.concatenate([y_s[lb] for lb in range(S5_GROUPS // ng)], axis=1)
    y = 0.5 * y * (1.0 + jnp.tanh(math.sqrt(2.0 / math.pi) * (y + 0.044715 * (y * y * y))))
    z = _bdot(y, wglu_ref[0]) + bglu_ref[0]
    s5_out = z[:, 0:BRANCH_W] * _sigmoid(z[:, BRANCH_W:2 * BRANCH_W])
    acc += _sigmoid(s5_gate) * _bdot(s5_out, wbr_ref[0, 1])
    m = _bdot(acc, wout_ref[0])
    o_ref[...] = x + mod[2:3] * _rms(m, g_ref[0, 1:2])


def _merge(l, x, mod, g_norm, ret_o, s5_y, gla_o, na_o, wglu, bglu, wbr, wmg, bmg, wout):
    tm = TOKEN_TILE
    tok = lambda w: pl.BlockSpec((tm, w), lambda i: (i, 0))
    return pl.pallas_call(
        _merge_kernel,
        grid=(T_ALL // tm,),
        in_specs=[tok(D_MODEL), _mod_spec(l), _layer_spec(l, 4, D_MODEL),
                  tok(256), pl.BlockSpec((tm // S5_CHUNK, S5_GROUPS * S5_W), lambda i: (i, 0)), tok(256),
                  pl.BlockSpec((NA_HEADS, tm, NA_DH), lambda i: (0, i, 0)),
                  _layer_spec(l, 256, 512), _layer_spec(l, 1, 512),
                  _layer_spec(l, N_BRANCH, BRANCH_W, D_MODEL, single_buffer=True),
                  _layer_spec(l, D_MODEL, N_BRANCH * D_MODEL, single_buffer=True),
                  _layer_spec(l, 1, N_BRANCH * D_MODEL),
                  _layer_spec(l, D_MODEL, D_MODEL, single_buffer=True)],
        out_specs=tok(D_MODEL),
        out_shape=jax.ShapeDtypeStruct((T_ALL, D_MODEL), F32),
        scratch_shapes=[pltpu.VMEM((S5_GROUPS // S5_LANE_GROUPS, tm, 128), F32)],
        compiler_params=_cparams(("parallel",)),
        name="merge",
    )(x, mod, g_norm, ret_o, s5_y, gla_o, na_o, wglu, bglu, wbr, wmg, bmg, wout)


FF_TILE = 1024


def _mlp_kernel(x_ref, mod_ref, g_ref, w1_ref, w2_ref, *o_refs):
    x = x_ref[...]
    mod = mod_ref[0, 0]
    hb = (_rms(x, g_ref[0, 2:3]) * (1.0 + mod[4:5]) + mod[3:4]).astype(BF16)
    nj = D_FF // FF_TILE
    up = lambda j: _bdot(hb, w1_ref[0, :, j * FF_TILE:(j + 1) * FF_TILE])
    f = None
    pre = up(0)
    for j in range(nj):
        nxt = up(j + 1) if j + 1 < nj else None
        a = jnp.maximum(pre, 0.0)
        part = _bdot(a * a, w2_ref[0, j * FF_TILE:(j + 1) * FF_TILE, :])
        f = part if f is None else f + part
        pre = nxt
    y = x + mod[5:6] * _rms(f, g_ref[0, 3:4])
    if len(o_refs) == 1:
        o_refs[0][...] = y
    else:
        ctx_tiles = T_CTX // TOKEN_TILE

        @pl.when(pl.program_id(0) < ctx_tiles)
        def _store_ctx():
            o_refs[0][...] = y

        @pl.when(pl.program_id(0) >= ctx_tiles)
        def _store_lat():
            o_refs[1][...] = y


def _mlp(l, x, mod, g_norm, w1, w2, *, split_out):
    tm = TOKEN_TILE
    tok = pl.BlockSpec((tm, D_MODEL), lambda i: (i, 0))
    if split_out:
        ctx_tiles = T_CTX // tm
        out_specs = [pl.BlockSpec((tm, D_MODEL), lambda i: (jnp.minimum(i, ctx_tiles - 1), 0)),
                     pl.BlockSpec((tm, D_MODEL), lambda i: (jnp.maximum(i - ctx_tiles, 0), 0))]
        out_shape = [jax.ShapeDtypeStruct((T_CTX, D_MODEL), F32), jax.ShapeDtypeStruct((T_LAT, D_MODEL), F32)]
    else:
        out_specs = tok
        out_shape = jax.ShapeDtypeStruct((T_ALL, D_MODEL), F32)
    return pl.pallas_call(
        _mlp_kernel,
        grid=(T_ALL // tm,),
        in_specs=[tok, _mod_spec(l), _layer_spec(l, 4, D_MODEL),
                  _layer_spec(l, D_MODEL, D_FF, single_buffer=True),
                  _layer_spec(l, D_FF, D_MODEL, single_buffer=True)],
        out_specs=out_specs,
        out_shape=out_shape,
        compiler_params=_cparams(("arbitrary",)),
        name="mlp",
    )(x, mod, g_norm, w1, w2)


def _rope_tables():
    half = RET_DK // 2
    nf = half // 2
    t = jnp.arange(DEC_SEQ)
    row = (t // GRID_W).astype(F32)
    col = (t % GRID_W).astype(F32)
    inv = ROPE_BASE ** (-jnp.arange(nf, dtype=F32) / nf)
    ang_r = row[:, None] * inv[None, :]
    ang_c = col[:, None] * inv[None, :]
    cos = jnp.concatenate([jnp.cos(ang_r)] * 2 + [jnp.cos(ang_c)] * 2, axis=1)
    sin = jnp.concatenate([-jnp.sin(ang_r), jnp.sin(ang_r), -jnp.sin(ang_c), jnp.sin(ang_c)], axis=1)
    return jnp.tile(cos, (1, RET_HEADS)), jnp.tile(sin, (1, RET_HEADS))


def _gla_state_in(st):
    eye = jnp.eye(GLA_HEADS, dtype=st.dtype)
    t = jnp.einsum('bldhkv,hg->bldhvgk', st, eye)
    return t.reshape(st.shape[0], st.shape[1], 2, GLA_HEADS * GLA_DV, GLA_HEADS * GLA_DK)


def kernel(x_prompt, x_sample, c, cache_na_k, cache_na_v, state_ret, state_s5, state_gla, c_ctx, w_ada, b_ada, g_norm, w_in, ret_log_decay, ret_gn, s5_lambda_re, s5_lambda_im, s5_log_dt, s5_b_re, s5_b_im, s5_c_re, s5_c_im, s5_d, s5_w_glu, s5_b_glu, gla_w_gate, gla_b_gate, gla_gn, na_rpb, w_branch, w_merge, b_merge, w_out, w_mlp1, w_mlp2):
    depth = w_in.shape[0]
    x = (x_prompt.reshape(T_CTX, D_MODEL), x_sample.reshape(T_LAT, D_MODEL))
    cc = jnp.concatenate([c_ctx[None], c, jnp.zeros((N_MOD_ROWS - 1 - DEC_BATCH, D_MODEL), F32)], axis=0)
    mod = _ada(cc, w_ada, b_ada).reshape(depth, N_MOD_ROWS, 6, D_MODEL)

    cos, sin = _rope_tables()
    w_in_t = w_in.transpose(0, 2, 1)
    ret_gn3, gla_gn3 = (a.reshape(depth, 1, BRANCH_W) for a in (ret_gn, gla_gn))
    s5_d_rows = jnp.broadcast_to(s5_d.reshape(depth, S5_GROUPS, 1, S5_GROUP_CH),
                                 (depth, S5_GROUPS, S5_CHUNK, S5_GROUP_CH)).reshape(depth, 1, S5_GROUPS * S5_W)
    b_glu3 = s5_b_glu.reshape(depth, 1, 2 * BRANCH_W)
    b_mg3 = b_merge.reshape(depth, 1, N_BRANCH * D_MODEL)
    cache_k = cache_na_k.transpose(0, 1, 3, 2, 4)
    cache_v = cache_na_v.transpose(0, 1, 3, 2, 4)
    na_tb = _na_tables(na_rpb)
    gla_s0 = _gla_state_in(state_gla)
    s5_tables, s5_lam = _s5_tables(s5_lambda_re, s5_lambda_im, s5_log_dt, s5_b_re, s5_b_im,
                                   s5_c_re, s5_c_im)
    s5_x0 = state_s5.transpose(1, 2, 0, 3, 5, 4).reshape(depth, 2, DEC_BATCH, S5_GROUPS * S5_P2)

    ks_l, vs_l, ret_l, s5_l, gla_l = [], [], [], [], []
    for l in range(depth):
        proj = _inproj(l, x, mod, g_norm, w_in_t)
        if l == 0:
            x, proj = proj[0], proj[1:]
        ret, su, gqk, gv, gg, glr, nq, nk, nv = proj

        ret_o, st_ret = _retention(l, ret, ret_log_decay, ret_gn3, latent=False)
        ret_o, = _retention(l, ret, ret_log_decay, ret_gn3, latent=True, cos=cos, sin=sin, s0=state_ret,
                            prev=ret_o)

        s5_y, s5_fin = _s5(l, su, *s5_tables, s5_d_rows, s5_lam, s5_x0)

        gla_o, st_gla = _gla(l, gqk, gv, gg, glr, gla_w_gate, gla_b_gate, gla_gn3, latent=False)
        gla_o, = _gla(l, gqk, gv, gg, glr, gla_w_gate, gla_b_gate, gla_gn3, latent=True, s0=gla_s0,
                      prev=gla_o)

        na_o = _attn_ctx(nq, nk, nv)
        na_o = _attn_lat(l, nq, nk, nv, cache_k, cache_v, na_tb, na_o)

        x = _merge(l, x, mod, g_norm, ret_o, s5_y, gla_o, na_o,
                   s5_w_glu, b_glu3, w_branch, w_merge, b_mg3, w_out)
        x = _mlp(l, x, mod, g_norm, w_mlp1, w_mlp2, split_out=(l == depth - 1))

        ks_l.append(nk)
        vs_l.append(nv)
        ret_l.append(st_ret)
        s5_l.append(s5_fin)
        gla_l.append(st_gla)

    y_prompt = x[0].reshape(BATCH, SEQ, D_MODEL)
    y_sample = x[1].reshape(DEC_BATCH, DEC_SEQ, D_MODEL)

    def cache_out(per_layer):
        a = jnp.stack(per_layer, axis=0)[:, :, :T_CTX].reshape(depth, NA_HEADS, BATCH, SEQ, NA_DH)
        return a.transpose(2, 0, 3, 1, 4)

    s5_out = jnp.stack(s5_l, axis=0).reshape(depth, 2, BATCH, S5_GROUPS, 2, S5_STATE)
    return (y_prompt, y_sample, cache_out(ks_l), cache_out(vs_l), jnp.stack(ret_l, axis=1),
            s5_out.transpose(2, 0, 1, 3, 5, 4), jnp.stack(gla_l, axis=1))
```

```python
import functools
import math

import numpy as np
import jax
import jax.numpy as jnp
from jax import lax
from jax.experimental import pallas as pl
from jax.experimental.pallas import tpu as pltpu

F32 = jnp.float32
BF16 = jnp.bfloat16

D_MODEL = 1024
BATCH = 16
SEQ = 256
DEPTH = 4
DEC_BATCH = 4
DEC_SEQ = 1024
PAST_LEN = 256
GRID_W = 64
N_BRANCH = 4
BRANCH_W = 256
RET_HEADS = 4
RET_DK = 64
RET_DV = 64
S5_GROUPS = 16
S5_GROUP_CH = 16
S5_STATE = 64
GLA_HEADS = 4
GLA_DK = 32
GLA_DV = 64
GLA_RANK = 16
GLA_TAU = 16.0
NA_HEADS = 4
NA_DH = 64
NA_WIN_H = 8
NA_WIN_W = 16
D_FF = 4 * D_MODEL
ROPE_BASE = 10000.0
EPS = 1e-6

T_CTX = BATCH * SEQ
T_LAT = DEC_BATCH * DEC_SEQ
T_ALL = T_CTX + T_LAT
LAT_BLOCK0 = T_CTX // DEC_SEQ
N_MOD_ROWS = 8
TOKEN_TILE = 512
CTX_SEQS_PER_STEP = DEC_SEQ // SEQ
GLA_CHUNK = 64
GLA_BLOCK_CHUNKS = 4
S5_CHUNK = 16
RET_QBLOCK = 256
NA_QBLOCK = 256
GRID_ROWS = DEC_SEQ // GRID_W
NA_KH = min(NA_WIN_H, GRID_ROWS)
NA_REL_ROWS = 2 * NA_WIN_H - 1
VMEM_LIMIT = 56 * 1024 * 1024
D_IN = 2848
IN_TAIL_COL = 2048


def _cparams(sem):
    return pltpu.CompilerParams(dimension_semantics=sem, vmem_limit_bytes=VMEM_LIMIT)


def _bdot(a, b):
    return jnp.dot(a.astype(BF16), b.astype(BF16), preferred_element_type=F32)


def _bdot_nt(a, b):
    return lax.dot_general(a.astype(BF16), b.astype(BF16), (((1,), (1,)), ((), ())),
                           preferred_element_type=F32)


def _bdot_tn(a, b):
    return lax.dot_general(a.astype(BF16), b.astype(BF16), (((0,), (0,)), ((), ())),
                           preferred_element_type=F32)


def _split(a):
    hi = a.astype(BF16)
    lo = (a - hi.astype(F32)).astype(BF16)
    return hi, lo


def _dot3(a, b):
    ah, al = _split(a)
    bh, bl = _split(b)
    d = functools.partial(jnp.dot, preferred_element_type=F32)
    return d(ah, bh) + d(al, bh) + d(ah, bl)


def _sigmoid(x):
    return 0.5 * jnp.tanh(0.5 * x) + 0.5


def _silu(x):
    return x * _sigmoid(x)


def _rms(x, g):
    return x * lax.rsqrt(jnp.mean(x * x, axis=-1, keepdims=True) + EPS) * g


def _head_norm(o, g, width):
    n = o.shape[-1]
    hi = lax.broadcasted_iota(jnp.int32, (n, n), 0) // width
    hj = lax.broadcasted_iota(jnp.int32, (n, n), 1) // width
    avg = jnp.where(hi == hj, 1.0 / width, 0.0).astype(BF16)

    def head_mean(a):
        ah, al = _split(a)
        return (jnp.dot(ah, avg, preferred_element_type=F32) + jnp.dot(al, avg, preferred_element_type=F32))

    xc = o - head_mean(o)
    return xc * lax.rsqrt(head_mean(xc * xc) + EPS) * g


def _mod_row(i):
    ctx_tiles = T_CTX // TOKEN_TILE
    return jnp.where(i < ctx_tiles, 0, 1 + (i - ctx_tiles) // (DEC_SEQ // TOKEN_TILE))


def _mod_spec(l):
    return pl.BlockSpec((1, 1, 6, D_MODEL), lambda i: (l, _mod_row(i), 0, 0))


def _layer_spec(l, *shape, single_buffer=False):
    mode = pl.Buffered(1) if single_buffer else None
    return pl.BlockSpec((1,) + shape, lambda *_: (l,) + (0,) * len(shape), pipeline_mode=mode)


_ANY = pl.BlockSpec(memory_space=pl.ANY)


ADA_TILE = 1536


def _ada_kernel(c_ref, w_ref, b_ref, o_ref):
    a = _silu(c_ref[...])
    o_ref[0] = _bdot(a, w_ref[0]) + b_ref[0]


def _ada(cc, w_ada, b_ada):
    n = 6 * D_MODEL
    return pl.pallas_call(
        _ada_kernel,
        grid=(DEPTH, n // ADA_TILE),
        in_specs=[pl.BlockSpec((N_MOD_ROWS, D_MODEL), lambda l, j: (0, 0)),
                  pl.BlockSpec((1, D_MODEL, ADA_TILE), lambda l, j: (l, 0, j)),
                  pl.BlockSpec((1, 1, ADA_TILE), lambda l, j: (l, 0, j))],
        out_specs=pl.BlockSpec((1, N_MOD_ROWS, ADA_TILE), lambda l, j: (l, 0, j)),
        out_shape=jax.ShapeDtypeStruct((DEPTH, N_MOD_ROWS, n), F32),
        compiler_params=_cparams(("parallel", "parallel")),
        name="ada_mod",
    )(cc, w_ada, b_ada.reshape(DEPTH, 1, n))


def _piece_transpose(blocks, piece):
    x = list(blocks)
    n = len(x)
    d = n // 2
    while d >= 1:
        low = (piece & d) == 0
        for v in range(n):
            if v & d:
                continue
            a, b = x[v], x[v + d]
            x[v] = jnp.where(low, a, pltpu.roll(b, d * S5_GROUP_CH, 1))
            x[v + d] = jnp.where(low, pltpu.roll(a, 128 - d * S5_GROUP_CH, 1), b)
        d //= 2
    return x


def _inproj_kernel(*refs, first):
    if first:
        (xa_ref, xb_ref, mod_ref, g_ref, w_ref, x_out_ref,
         ret_ref, s5_ref, gqk_ref, gv_ref, gg_ref, glr_ref, nq_ref, nk_ref, nv_ref, su_s) = refs
        x = jnp.where(pl.program_id(0) < T_CTX // TOKEN_TILE, xa_ref[...], xb_ref[...])
        x_out_ref[...] = x
    else:
        (x_ref, mod_ref, g_ref, w_ref,
         ret_ref, s5_ref, gqk_ref, gv_ref, gg_ref, glr_ref, nq_ref, nk_ref, nv_ref, su_s) = refs
        x = x_ref[...]
    mod = mod_ref[0, 0]
    h = _rms(x, g_ref[0, 0:1]) * (1.0 + mod[1:2]) + mod[0:1]
    hb = h.astype(BF16)

    def proj(lo, hi):
        return _bdot_nt(hb, w_ref[0, lo:hi, :])

    ret_ref[...] = proj(0, 1024)
    su = proj(1024, 1280)
    rows = TOKEN_TILE // S5_CHUNK
    ng = S5_LANE_GROUPS
    piece = lax.broadcasted_iota(jnp.int32, (rows, 128), 1) // S5_GROUP_CH
    for lb in range(S5_GROUPS // ng):
        su_s[lb] = su[:, lb * 128:(lb + 1) * 128]
        for m in range(S5_CHUNK // ng):
            out = _piece_transpose(
                [su_s[lb, pl.ds(m * ng + jl, rows, stride=S5_CHUNK), :] for jl in range(ng)], piece)
            for g in range(ng):
                col = (lb * ng + g) * S5_W + m * 128
                s5_ref[:, col:col + 128] = out[g]
    gqk_ref[...] = proj(1280, 1536)
    gv_ref[...] = proj(1536, 1792)
    gg_ref[...] = proj(1792, 2048)
    glr_ref[...] = proj(IN_TAIL_COL, IN_TAIL_COL + 128)
    lo = IN_TAIL_COL + 2 * GLA_RANK
    for ref in (nq_ref, nk_ref, nv_ref):
        r = proj(lo, lo + NA_HEADS * NA_DH)
        for hh in range(NA_HEADS):
            ref[hh] = r[:, hh * NA_DH:(hh + 1) * NA_DH]
        lo += NA_HEADS * NA_DH


def _inproj(l, xs, mod, g_norm, w_in):
    tm = TOKEN_TILE
    first = isinstance(xs, tuple)
    tok = lambda w: pl.BlockSpec((tm, w), lambda i: (i, 0))
    head = pl.BlockSpec((NA_HEADS, tm, NA_DH), lambda i: (0, i, 0))
    tshape = lambda w: jax.ShapeDtypeStruct((T_ALL, w), F32)
    hshape = jax.ShapeDtypeStruct((NA_HEADS, T_ALL, NA_DH), F32)
    ctx_tiles = T_CTX // tm
    if first:
        x_specs = [pl.BlockSpec((tm, D_MODEL), lambda i: (jnp.minimum(i, ctx_tiles - 1), 0)),
                   pl.BlockSpec((tm, D_MODEL), lambda i: (jnp.maximum(i - ctx_tiles, 0), 0))]
        x_args = list(xs)
    else:
        x_specs, x_args = [tok(D_MODEL)], [xs]
    return pl.pallas_call(
        functools.partial(_inproj_kernel, first=first),
        grid=(T_ALL // tm,),
        in_specs=x_specs + [_mod_spec(l), _layer_spec(l, 4, D_MODEL),
                            _layer_spec(l, D_IN, D_MODEL, single_buffer=True)],
        out_specs=([tok(D_MODEL)] if first else [])
        + [tok(1024), pl.BlockSpec((tm // S5_CHUNK, S5_GROUPS * S5_W), lambda i: (i, 0)),
           tok(256), tok(256), tok(256), tok(128), head, head, head],
        out_shape=([tshape(D_MODEL)] if first else [])
        + [tshape(1024), jax.ShapeDtypeStruct((S5_ROWS, S5_GROUPS * S5_W), F32),
           tshape(256), tshape(256), tshape(256), tshape(128), hshape, hshape, hshape],
        scratch_shapes=[pltpu.VMEM((S5_GROUPS // S5_LANE_GROUPS, tm, 128), F32)],
        compiler_params=_cparams(("parallel",)),
        name="in_proj",
    )(*x_args, mod, g_norm, w_in)


def _rope_rotate(x, lane):
    first = (lane % 32) < 16
    w = x.shape[-1]
    return jnp.where(first, pltpu.roll(x, w - 16, 1), pltpu.roll(x, 16, 1))


def _ret_kernel(ld_ref, ret_ref, gn_ref, *rest, layer, seq, latent):
    if latent:
        cos_ref, sin_ref, s0_ref, _, out_ref, dec_s = rest
    else:
        out_ref, st_ref, dec_s = rest
    tq = RET_QBLOCK
    nq = seq // tq
    width = dec_s.shape[-1]

    @pl.when(pl.program_id(0) == 0)
    def _build_decay():
        rel = (lax.broadcasted_iota(jnp.int32, (tq, width), 0) + (nq - 1) * tq
               - lax.broadcasted_iota(jnp.int32, (tq, width), 1)).astype(F32)
        for h in range(RET_HEADS):
            dec_s[h] = (jnp.where(rel >= 0, jnp.exp(ld_ref[layer, 0, h] * jnp.maximum(rel, 0.0)), 0.0)
                        + jnp.where(rel <= 0, jnp.exp(ld_ref[layer, 1, h] * jnp.maximum(-rel, 0.0)), 0.0))

    nrows = ret_ref.shape[0]
    nsub = nrows // seq
    q = ret_ref[:, 0:256]
    k = ret_ref[:, 256:512]
    if latent:
        lane = lax.broadcasted_iota(jnp.int32, (nrows, 256), 1)
        cos = cos_ref[...]
        sin = sin_ref[...]
        q = q * cos + _rope_rotate(q, lane) * sin
        k = k * cos + _rope_rotate(k, lane) * sin
    k = k * (RET_DK ** -0.5)
    pos_c = lax.broadcasted_iota(jnp.int32, (seq, 1), 0).astype(F32)
    tiles = [(s, qb) for s in range(nsub) for qb in range(nq)]
    for h in range(RET_HEADS):
        lgf = ld_ref[layer, 0, h]
        lgb = ld_ref[layer, 1, h]
        sl = slice(h * RET_DK, (h + 1) * RET_DK)
        qh = q[:, sl]
        kh = k[:, sl]
        kb = kh.astype(BF16)
        vb = ret_ref[:, 512 + h * RET_DV:512 + (h + 1) * RET_DV].astype(BF16)
        if latent:
            q_init = jnp.concatenate([qh * jnp.exp(lgf * (pos_c + 1.0)),
                                      qh * jnp.exp(lgb * (seq - pos_c))], axis=1)
            s_init = jnp.concatenate([s0_ref[0, 0, 0, h], s0_ref[0, 0, 1, h]], axis=0)

        def score(s, qb):
            w0 = (nq - 1 - qb) * tq
            rows = slice(s * seq + qb * tq, s * seq + (qb + 1) * tq)
            keys = slice(s * seq, (s + 1) * seq)
            return (_bdot_nt(qh[rows], kb[keys]) * dec_s[h, :, w0:w0 + seq]).astype(BF16)

        def values(sc, s, qb):
            o = jnp.dot(sc, vb[s * seq:(s + 1) * seq], preferred_element_type=F32)
            if latent:
                o = o + _bdot(q_init[qb * tq:(qb + 1) * tq], s_init)
            return o

        def finish(o, s, qb):
            out_ref[s * seq + qb * tq:s * seq + (qb + 1) * tq, sl] = o

        if nsub > 1:
            scores = [score(s, qb) for s, qb in tiles]
            outs = [values(sc, s, qb) for sc, (s, qb) in zip(scores, tiles)]
            for o, (s, qb) in zip(outs, tiles):
                finish(o, s, qb)
        else:
            for s, qb in tiles:
                finish(values(score(s, qb), s, qb), s, qb)
        if not latent:
            for s in range(nsub):
                keys = slice(s * seq, (s + 1) * seq)
                st_ref[s, 0, h] = _bdot_tn(kh[keys] * jnp.exp(lgf * (seq - 1.0 - pos_c)), vb[keys])
                st_ref[s, 1, h] = _bdot_tn(kh[keys] * jnp.exp(lgb * pos_c), vb[keys])
    for r0 in range(0, nrows, tq):
        rows = slice(r0, r0 + tq)
        out_ref[rows, :] = _head_norm(out_ref[rows, :], gn_ref[0], RET_DV) * _silu(ret_ref[rows, 768:1024])


def _retention(l, ret, ld, gn, *, latent, cos=None, sin=None, s0=None, prev=None):
    seq = DEC_SEQ if latent else SEQ
    nsub = 1 if latent else CTX_SEQS_PER_STEP
    nb = DEC_BATCH if latent else BATCH // nsub
    off = LAT_BLOCK0 if latent else 0
    rows = nsub * seq
    in_specs = [pl.BlockSpec(memory_space=pltpu.SMEM),
                pl.BlockSpec((rows, 1024), lambda b: (b + off, 0)),
                _layer_spec(l, 1, 256)]
    args = [ld, ret, gn]
    out_specs = [pl.BlockSpec((rows, 256), lambda b: (b + off, 0))]
    out_shape = [jax.ShapeDtypeStruct((T_ALL, 256), F32)]
    aliases = {}
    if latent:
        in_specs += [pl.BlockSpec((seq, 256), lambda b: (0, 0)),
                     pl.BlockSpec((seq, 256), lambda b: (0, 0)),
                     pl.BlockSpec((1, 1, 2, RET_HEADS, RET_DK, RET_DV), lambda b: (b, l, 0, 0, 0, 0)),
                     _ANY]
        args += [cos, sin, s0, prev]
        aliases = {6: 0}
    else:
        out_specs.append(pl.BlockSpec((nsub, 2, RET_HEADS, RET_DK, RET_DV), lambda b: (b, 0, 0, 0, 0)))
        out_shape.append(jax.ShapeDtypeStruct((BATCH, 2, RET_HEADS, RET_DK, RET_DV), F32))
    return pl.pallas_call(
        functools.partial(_ret_kernel, layer=l, seq=seq, latent=latent),
        grid=(nb,),
        in_specs=in_specs, out_specs=out_specs, out_shape=out_shape,
        input_output_aliases=aliases,
        scratch_shapes=[pltpu.VMEM((RET_HEADS, RET_QBLOCK, 2 * seq - RET_QBLOCK), F32)],
        compiler_params=_cparams(("arbitrary",)),
        name="retention_lat" if latent else "retention_ctx",
    )(*args)


def _gla_kernel(gqk_ref, gv_ref, gg_ref, glr_ref, wg_ref, bg_ref, gn_ref, *rest, seq, latent):
    if latent:
        s0_ref, _, out_ref, gate_s, o_s, st_s, qst_s, ds_s, e_s = rest
    else:
        out_ref, st_ref, gate_s, o_s, st_s, qst_s, ds_s, e_s = rest
    c = GLA_CHUNK
    n = seq // c
    nsub = gqk_ref.shape[0] // seq
    hk = GLA_HEADS * GLA_DK
    lr = glr_ref[...]
    for d in range(2):
        pre = _bdot(lr[:, d * GLA_RANK:(d + 1) * GLA_RANK], wg_ref[0, d]) + bg_ref[0, d:d + 1]
        gate_s[d] = (jnp.minimum(pre, 0.0) - jnp.log(1.0 + jnp.exp(-jnp.abs(pre)))) / GLA_TAU
        for s in range(nsub):
            st_s[2 * s + d] = s0_ref[0, 0, d] if latent else jnp.zeros((GLA_HEADS * GLA_DV, hk), F32)

    nc = GLA_BLOCK_CHUNKS
    rb = nc * c
    ti = lax.broadcasted_iota(jnp.int32, (rb, rb), 0)
    tj = lax.broadcasted_iota(jnp.int32, (rb, rb), 1)
    same = (ti // c) == (tj // c)
    ones = lambda m: (same & m).astype(BF16)
    tri = [ones(tj <= ti), ones(tj >= ti)]
    mid = [ones((tj % c) < c // 2), ones((tj % c) >= c // 2)]
    tot = ones(tj == tj)
    lane_k = lax.broadcasted_iota(jnp.int32, (c, hk), 1)
    head_mask = [(lane_k // GLA_DK) == h for h in range(GLA_HEADS)]
    ai = lax.broadcasted_iota(jnp.int32, (GLA_HEADS * c, c), 0) % c
    aj = lax.broadcasted_iota(jnp.int32, (GLA_HEADS * c, c), 1)
    keep = [aj <= ai, aj >= ai]
    sr = lax.broadcasted_iota(jnp.int32, (GLA_HEADS * GLA_DV, hk), 0) // GLA_DV
    sc = lax.broadcasted_iota(jnp.int32, (GLA_HEADS * GLA_DV, hk), 1) // GLA_DK
    diag = sr == sc
    scale = GLA_DK ** -0.5
    d32 = functools.partial(jnp.dot, preferred_element_type=F32)

    def rows_of(i, size):
        return pl.ds(i * size, size) if isinstance(i, int) else pl.ds(pl.multiple_of(i * size, size), size)

    def local(block_ids):
        vs, q_att, k_att, k_st = {}, {}, {}, {}
        for bi in block_ids:
            rows = rows_of(bi, rb)
            q = gqk_ref[rows, 0:hk] * scale
            k = gqk_ref[rows, hk:2 * hk]
            vs[bi] = gv_ref[rows, :].astype(BF16)
            for d in range(2):
                gh, gl = _split(gate_s[d, rows, :])
                b = d32(tri[d], gh) + d32(tri[d], gl)
                b_mid = d32(mid[d], gh) + d32(mid[d], gl)
                b_end = d32(tot, gh) + d32(tot, gl)
                q_att[bi, d] = q * jnp.exp(b - b_mid)
                k_att[bi, d] = k * jnp.exp(b_mid - b)
                k_st[bi, d] = (k * jnp.exp(b_end - b)).astype(BF16)
                qst_s[d, rows, :] = (q * jnp.exp(b)).astype(BF16)
                decay = jnp.exp(b_end)
                for cc in range(nc):
                    e_s[d, bi * nc + cc] = decay[cc * c:cc * c + 8]
        tiles = [(bi, d, cc) for bi in block_ids for d in range(2) for cc in range(nc)]
        att = {}
        for bi, d, cc in tiles:
            r = slice(cc * c, (cc + 1) * c)
            qa = q_att[bi, d][r]
            q_stack = jnp.concatenate([jnp.where(head_mask[h], qa, 0.0) for h in range(GLA_HEADS)], axis=0)
            att[bi, d, cc] = jnp.where(keep[d], _bdot_nt(q_stack, k_att[bi, d][r]), 0.0).astype(BF16)
        for bi, d, cc in tiles:
            r = slice(cc * c, (cc + 1) * c)
            o = jnp.concatenate([d32(att[bi, d, cc][h * c:(h + 1) * c], vs[bi][r, h * GLA_DV:(h + 1) * GLA_DV])
                                 for h in range(GLA_HEADS)], axis=1)
            o_s[d, rows_of(bi * nc + cc, c), :] = o
        for bi, d, cc in tiles:
            r = slice(cc * c, (cc + 1) * c)
            ds_s[d, bi * nc + cc] = jnp.where(diag, lax.dot_general(
                vs[bi][r], k_st[bi, d][r], (((0,), (0,)), ((), ())), preferred_element_type=F32), 0.0)

    def recur(s, ci, d):
        g = s * n + ci
        rows = rows_of(g, c)
        st = st_s[2 * s + d]
        o_s[d, rows, :] += _bdot_nt(qst_s[d, rows, :], st)
        st_s[2 * s + d] = st * e_s[d, g, 0:1] + ds_s[d, g]

    def recur_body(i, carry):
        for s in range(nsub):
            recur(s, i, 0)
            recur(s, n - 1 - i, 1)
        return carry

    local(list(range(nsub * n // nc)))
    if n <= 4:
        for i in range(n):
            recur_body(i, 0)
    else:
        lax.fori_loop(0, n, recur_body, 0, unroll=2)

    for r0 in range(0, nsub * seq, rb):
        rows = slice(r0, r0 + rb)
        out_ref[rows, :] = (_head_norm(o_s[0, rows, :] + o_s[1, rows, :], gn_ref[0], GLA_DV)
                            * _silu(gg_ref[rows, :]))
    if not latent:
        hv = GLA_HEADS * GLA_DV
        eye = (lax.broadcasted_iota(jnp.int32, (hv, hv), 0)
               == lax.broadcasted_iota(jnp.int32, (hv, hv), 1)).astype(BF16)
        tn = lambda a: lax.dot_general(a, eye, (((0,), (0,)), ((), ())), preferred_element_type=F32)
        for s in range(nsub):
            for d in range(2):
                st = st_s[2 * s + d]
                hi, lo = _split(st)
                lo2 = (st - hi.astype(F32) - lo.astype(F32)).astype(BF16)
                s_all = tn(hi) + tn(lo) + tn(lo2)
                for h in range(GLA_HEADS):
                    st_ref[s, d, h] = s_all[h * GLA_DK:(h + 1) * GLA_DK, h * GLA_DV:(h + 1) * GLA_DV]


def _gla(l, gqk, gv, gg, glr, wg, bg, gn, *, latent, s0=None, prev=None):
    seq = DEC_SEQ if latent else SEQ
    nsub = 1 if latent else CTX_SEQS_PER_STEP
    nb = DEC_BATCH if latent else BATCH // nsub
    off = LAT_BLOCK0 if latent else 0
    rows = nsub * seq
    hk = GLA_HEADS * GLA_DK
    hv = GLA_HEADS * GLA_DV
    tok = lambda w: pl.BlockSpec((rows, w), lambda b: (b + off, 0))
    in_specs = [tok(256), tok(256), tok(256), tok(128),
                _layer_spec(l, 2, GLA_RANK, hk), _layer_spec(l, 2, hk), _layer_spec(l, 1, 256)]
    args = [gqk, gv, gg, glr, wg, bg, gn]
    out_specs = [tok(256)]
    out_shape = [jax.ShapeDtypeStruct((T_ALL, 256), F32)]
    aliases = {}
    if latent:
        in_specs += [pl.BlockSpec((1, 1, 2, hv, hk), lambda b: (b, l, 0, 0, 0)), _ANY]
        args += [s0, prev]
        aliases = {8: 0}
    else:
        out_specs.append(pl.BlockSpec((nsub, 2, GLA_HEADS, GLA_DK, GLA_DV), lambda b: (b, 0, 0, 0, 0)))
        out_shape.append(jax.ShapeDtypeStruct((BATCH, 2, GLA_HEADS, GLA_DK, GLA_DV), F32))
    return pl.pallas_call(
        functools.partial(_gla_kernel, seq=seq, latent=latent),
        grid=(nb,),
        in_specs=in_specs, out_specs=out_specs, out_shape=out_shape,
        input_output_aliases=aliases,
        scratch_shapes=[pltpu.VMEM((2, rows, hk), F32), pltpu.VMEM((2, rows, hv), F32),
                        pltpu.VMEM((2 * nsub, hv, hk), F32), pltpu.VMEM((2, rows, hk), BF16),
                        pltpu.VMEM((2, rows // GLA_CHUNK, hv, hk), F32),
                        pltpu.VMEM((2, rows // GLA_CHUNK, 8, hk), F32)],
        compiler_params=_cparams(("parallel",)),
        name="gla_lat" if latent else "gla_ctx",
    )(*args)


S5_W = S5_CHUNK * S5_GROUP_CH
S5_P2 = 2 * S5_STATE
S5_ROWS = T_ALL // S5_CHUNK
S5_ROWS_CTX = T_CTX // S5_CHUNK
S5_LANE_GROUPS = 128 // S5_GROUP_CH


S5_TE_ROWS = S5_W + 4 * S5_P2


def _s5_toeplitz_kernel(cc_ref, pwr_ref, pwi_ref, bbr_ref, bbi_ref, ctr_ref, cti_ref, te_ref, ffb_ref):
    lane = lax.broadcasted_iota(jnp.int32, (S5_GROUP_CH, S5_W), 1)
    swapped = lambda t: jnp.concatenate([t[S5_STATE:], t[:S5_STATE]], axis=0)
    for t in range(S5_TABLE_GROUPS):
        def times_b(v, d):
            pr, pi, br, bi = pwr_ref[0, t, v], pwi_ref[0, t, v], bbr_ref[0, d, t], bbi_ref[0, d, t]
            return jnp.concatenate([pr * br - pi * bi, pr * bi + pi * br], axis=0)

        def times_c(v, d):
            pr, pi, cr, ci = pwr_ref[0, t, v], pwi_ref[0, t, v], ctr_ref[0, d, t], cti_ref[0, d, t]
            return jnp.concatenate([cr * pr - ci * pi, -(cr * pi + ci * pr)], axis=0)

        wfr = times_b(0, 0)
        wb = times_b(1, 1)
        kf = _dot3(cc_ref[0, 0, t], wfr)
        kb = _dot3(cc_ref[0, 1, t], wb)
        blocks = []
        for i in range(S5_CHUNK):
            sf = ((i + 1 - S5_CHUNK) * S5_GROUP_CH) % S5_W
            fwd = jnp.where(lane < (i + 1) * S5_GROUP_CH, pltpu.roll(kf, sf, 1) if sf else kf, 0.0)
            bwd = jnp.where(lane >= i * S5_GROUP_CH, pltpu.roll(kb, i * S5_GROUP_CH, 1) if i else kb, 0.0)
            blocks.append(fwd + bwd)
        te_ref[0, t] = jnp.concatenate(blocks + [wfr, wb, swapped(wfr), swapped(wb)], axis=0).astype(BF16)
        ffb_ref[0, t] = jnp.concatenate([times_c(2, 0), times_c(3, 1)], axis=0).astype(BF16)


S5_TABLE_GROUPS = 4


def _s5_toeplitz(cc, pwr, pwi, bbr, bbi, ctr, cti):
    tg = S5_TABLE_GROUPS
    nl, _, ngroups = cc.shape[:3]
    by_dir = lambda *tail: pl.BlockSpec((1, 2, tg) + tail, lambda l, i: (l, 0, i) + (0,) * len(tail))
    by_group = lambda *tail: pl.BlockSpec((1, tg) + tail, lambda l, i: (l, i) + (0,) * len(tail))
    lanes = by_dir(S5_STATE, S5_W)
    powers = by_group(4, S5_STATE, S5_W)
    return pl.pallas_call(
        _s5_toeplitz_kernel,
        grid=(nl, ngroups // tg),
        in_specs=[by_dir(S5_GROUP_CH, S5_P2), powers, powers, lanes, lanes, lanes, lanes],
        out_specs=[by_group(S5_TE_ROWS, S5_W), by_group(2 * S5_P2, S5_W)],
        out_shape=[jax.ShapeDtypeStruct((nl, ngroups, S5_TE_ROWS, S5_W), BF16),
                   jax.ShapeDtypeStruct((nl, ngroups, 2 * S5_P2, S5_W), BF16)],
        compiler_params=_cparams(("parallel", "parallel")),
        name="s5_toeplitz",
    )(cc, pwr, pwi, bbr, bbi, ctr, cti)


def _s5_kernel(u_ref, te_ref, ff_ref, d_ref, lam_ref, x0_ref, y_ref, fin_ref, xs_s, ps_s):
    C = S5_CHUNK
    ng = S5_LANE_GROUPS

    for g in range(ng):
        u = u_ref[:, g * S5_W:(g + 1) * S5_W]
        r = _bdot_nt(u, te_ref[0, g])
        y_ref[:, g * S5_W:(g + 1) * S5_W] = r[:, 0:S5_W] + d_ref[0, :, g * S5_W:(g + 1) * S5_W] * u
        for t in range(4):
            xs_s[t * ng + g] = r[:, S5_W + t * S5_P2:S5_W + (t + 1) * S5_P2]

    w = ng * S5_P2

    def carry(base, nseq, nchunks, init_f, init_b):
        a_f, b_f = lam_ref[0, 0, 0:1], lam_ref[0, 0, 1:2]
        a_b, b_b = lam_ref[0, 1, 0:1], lam_ref[0, 1, 1:2]
        load = lambda t, rows: jnp.concatenate([xs_s[t * ng + g, rows, :] for g in range(ng)], axis=1)

        def body(i, st):
            sf, tf, sb, tb = st
            rf = pl.ds(base + i, nseq, stride=nchunks)
            rb = pl.ds(base + (nchunks - 1 - i), nseq, stride=nchunks)
            for g in range(ng):
                ps_s[g, rf, :] = sf[:, g * S5_P2:(g + 1) * S5_P2]
                ps_s[ng + g, rb, :] = sb[:, g * S5_P2:(g + 1) * S5_P2]
            return (a_f * sf + b_f * tf + load(0, rf), a_f * tf - b_f * sf + load(2, rf),
                    a_b * sb + b_b * tb + load(1, rb), a_b * tb - b_b * sb + load(3, rb))

        first = (lax.broadcasted_iota(jnp.int32, (1, w), 1) % S5_P2) < S5_STATE
        swap = lambda s: jnp.where(first, pltpu.roll(s, w - S5_STATE, 1), pltpu.roll(s, S5_STATE, 1))
        return lax.fori_loop(0, nchunks, body, (init_f, swap(init_f), init_b, swap(init_b)))

    zeros = jnp.zeros((BATCH, w), F32)
    fin = carry(0, BATCH, SEQ // C, zeros, zeros)
    fin_ref[0] = fin[0]
    fin_ref[1] = fin[2]
    carry(S5_ROWS_CTX, DEC_BATCH, DEC_SEQ // C, x0_ref[0, 0], x0_ref[0, 1])

    for g in range(ng):
        p = jnp.concatenate([ps_s[g], ps_s[ng + g]], axis=1)
        y_ref[:, g * S5_W:(g + 1) * S5_W] += _bdot(p, ff_ref[0, g])


def _s5(l, su_rows, te, ff, d_rows, lam, x0):
    ng = S5_LANE_GROUPS
    w = ng * S5_P2
    rows = pl.BlockSpec((S5_ROWS, ng * S5_W), lambda i: (0, i))
    return pl.pallas_call(
        _s5_kernel,
        grid=(S5_GROUPS // ng,),
        in_specs=[rows,
                  pl.BlockSpec((1, ng, S5_TE_ROWS, S5_W), lambda i: (l, i, 0, 0)),
                  pl.BlockSpec((1, ng, 2 * S5_P2, S5_W), lambda i: (l, i, 0, 0)),
                  pl.BlockSpec((1, 1, ng * S5_W), lambda i: (l, 0, i)),
                  pl.BlockSpec((1, 2, 2, w), lambda i: (l, 0, 0, i)),
                  pl.BlockSpec((1, 2, DEC_BATCH, w), lambda i: (l, 0, 0, i))],
        out_specs=[rows, pl.BlockSpec((2, BATCH, w), lambda i: (0, 0, i))],
        out_shape=[jax.ShapeDtypeStruct((S5_ROWS, S5_GROUPS * S5_W), F32),
                   jax.ShapeDtypeStruct((2, BATCH, S5_GROUPS * S5_P2), F32)],
        scratch_shapes=[pltpu.VMEM((4 * ng, S5_ROWS, S5_P2), F32), pltpu.VMEM((2 * ng, S5_ROWS, S5_P2), F32)],
        compiler_params=_cparams(("parallel",)),
        name="s5_scan",
    )(su_rows, te, ff, d_rows, lam, x0)


def _s5_tables(lam_re, lam_im, log_dt, b_re, b_im, c_re, c_im):
    C, G, P, H = S5_CHUNK, S5_GROUPS, S5_STATE, S5_GROUP_CH
    L = lam_re.shape[0]
    dt = jnp.exp(log_dt)[..., None]
    ar, ai = lam_re * dt, lam_im * dt

    steps = jnp.arange(C + 1, dtype=F32)
    mag = jnp.exp(ar[..., None] * steps)
    pw_re, pw_im = mag * jnp.cos(ai[..., None] * steps), mag * jnp.sin(ai[..., None] * steps)
    exact = functools.partial(jnp.einsum, precision=lax.Precision.HIGHEST)
    tau_np = np.arange(C * H) // H

    patterns = [(0, (C - 1) - tau_np), (1, tau_np), (0, tau_np + 1), (1, C - tau_np)]
    sel = np.zeros((2, len(patterns), C + 1, C * H), np.float32)
    for v, (d, t) in enumerate(patterns):
        sel[d, v, t, np.arange(C * H)] = 1.0
    sel = jnp.asarray(sel)
    pwr = exact('ldgpt,dvtn->lgvpn', pw_re, sel)
    pwi = exact('ldgpt,dvtn->lgvpn', pw_im, sel)

    lr, li = pw_re[..., 1], pw_im[..., 1]
    den = lam_re * lam_re + lam_im * lam_im
    qr = ((lr - 1.0) * lam_re + li * lam_im) / den
    qi = (li * lam_re - (lr - 1.0) * lam_im) / den
    bbr = qr[..., None] * b_re - qi[..., None] * b_im
    bbi = qr[..., None] * b_im + qi[..., None] * b_re
    chan = jnp.asarray((np.arange(H)[:, None] == (np.arange(C * H) % H)[None, :]).astype(np.float32))
    lanes = lambda a: exact('ldgph,hn->ldgpn', a, chan)
    bbr, bbi = lanes(bbr), lanes(bbi)
    c_t = lambda a: exact('ldghp,hn->ldgpn', a, chan)
    ctr, cti = c_t(c_re), c_t(c_im)

    cc = jnp.concatenate([c_re, -c_im], axis=-1)
    tables = _s5_toeplitz(cc, pwr, pwi, bbr, bbi, ctr, cti)
    cr, ci = pw_re[..., C], pw_im[..., C]
    a = jnp.concatenate([cr, cr], axis=-1).reshape(L, 2, 1, G * 2 * P)
    b = jnp.concatenate([-ci, ci], axis=-1).reshape(L, 2, 1, G * 2 * P)
    return tables, jnp.concatenate([a, b], axis=2)


def _softmax_pv(s_parts, v_parts):
    m = s_parts[0].max(axis=-1, keepdims=True)
    for s in s_parts[1:]:
        m = jnp.maximum(m, s.max(axis=-1, keepdims=True))
    o = None
    l = None
    for s, v in zip(s_parts, v_parts):
        p = jnp.exp(s - m)
        pl_ = p.sum(axis=-1, keepdims=True)
        po = _bdot(p, v)
        o = po if o is None else o + po
        l = pl_ if l is None else l + pl_
    return o / l


def _attn_ctx_kernel(q_ref, k_ref, v_ref, o_ref):
    scale = NA_DH ** -0.5
    tiles = [(h, slice(s * SEQ, (s + 1) * SEQ)) for h in range(NA_HEADS) for s in range(CTX_SEQS_PER_STEP)]
    scores = [_bdot_nt(q_ref[h, rows, :], k_ref[h, rows, :]) * scale for h, rows in tiles]
    for s, (h, rows) in zip(scores, tiles):
        o_ref[h, rows, :] = _softmax_pv([s], [v_ref[h, rows, :]])


def _attn_ctx(nq, nk, nv):
    spec = pl.BlockSpec((NA_HEADS, CTX_SEQS_PER_STEP * SEQ, NA_DH), lambda b: (0, b, 0))
    return pl.pallas_call(
        _attn_ctx_kernel,
        grid=(BATCH // CTX_SEQS_PER_STEP,),
        in_specs=[spec, spec, spec],
        out_specs=spec,
        out_shape=jax.ShapeDtypeStruct((NA_HEADS, T_ALL, NA_DH), F32),
        compiler_params=_cparams(("parallel",)),
        name="attn_ctx",
    )(nq, nk, nv)


def _attn_lat_kernel(q_ref, k_ref, v_ref, kc_ref, vc_ref, tb_ref, _, o_ref, bias_s):
    @pl.when(pl.program_id(1) == 0)
    def _build_bias():
        bias_s[...] = jnp.full((DEC_SEQ, DEC_SEQ), -jnp.inf, F32)
        for r in range(GRID_ROWS):
            rs = min(max(r - NA_KH // 2, 0), GRID_ROWS - NA_KH)
            dr0 = rs - r + NA_WIN_H - 1
            bias_s[r * GRID_W:(r + 1) * GRID_W, rs * GRID_W:(rs + NA_KH) * GRID_W] = (
                tb_ref[0, 0, :, dr0 * GRID_W:(dr0 + NA_KH) * GRID_W])

    scale = NA_DH ** -0.5
    kb = k_ref[0].astype(BF16)
    vb = v_ref[0].astype(BF16)
    kc = kc_ref[0, 0, 0].astype(BF16)
    vc = vc_ref[0, 0, 0].astype(BF16)
    tq = NA_QBLOCK
    q_rows = tq // GRID_W
    tiles = []
    for qb in range(DEC_SEQ // tq):
        rows = slice(qb * tq, (qb + 1) * tq)
        starts = [min(max(r - NA_KH // 2, 0), GRID_ROWS - NA_KH) for r in range(qb * q_rows, (qb + 1) * q_rows)]
        keys = slice(min(starts) * GRID_W // 128 * 128, -(-(max(starts) + NA_KH) * GRID_W // 128) * 128)
        tiles.append((rows, keys))
    scores = []
    for rows, keys in tiles:
        qh = q_ref[0, rows, :].astype(BF16)
        scores.append((_bdot_nt(qh, kb[keys]) * scale + bias_s[rows, keys], _bdot_nt(qh, kc) * scale))
    for (s_loc, s_ctx), (rows, keys) in zip(scores, tiles):
        o_ref[0, rows, :] = _softmax_pv([s_loc, s_ctx], [vb[keys], vc])


def _attn_lat(l, nq, nk, nv, kc, vc, tb, prev):
    tok = pl.BlockSpec((1, DEC_SEQ, NA_DH), lambda h, b: (h, b + LAT_BLOCK0, 0))
    cache = pl.BlockSpec((1, 1, 1, PAST_LEN, NA_DH), lambda h, b: (b, l, h, 0, 0))
    return pl.pallas_call(
        _attn_lat_kernel,
        grid=(NA_HEADS, DEC_BATCH),
        in_specs=[tok, tok, tok, cache, cache,
                  pl.BlockSpec((1, 1, GRID_W, NA_REL_ROWS * GRID_W), lambda h, b: (l, h, 0, 0)),
                  _ANY],
        out_specs=tok,
        out_shape=jax.ShapeDtypeStruct((NA_HEADS, T_ALL, NA_DH), F32),
        input_output_aliases={6: 0},
        scratch_shapes=[pltpu.VMEM((DEC_SEQ, DEC_SEQ), F32)],
        compiler_params=_cparams(("arbitrary", "arbitrary")),
        name="attn_lat",
    )(nq, nk, nv, kc, vc, tb, prev)


def _na_tables(rpb):
    col = np.arange(GRID_W)
    col_start = np.clip(col - NA_WIN_W // 2, 0, GRID_W - NA_WIN_W)
    col_in = (col[None, :] >= col_start[:, None]) & (col[None, :] < col_start[:, None] + NA_WIN_W)
    col_idx = np.clip(col[None, :] - col[:, None] + NA_WIN_W - 1, 0, 2 * NA_WIN_W - 2)
    onehot = (col_idx[:, :, None] == np.arange(2 * NA_WIN_W - 1)[None, None, :]).astype(np.float32)
    tb = jnp.einsum('lhrd,qkd->lhqrk', rpb, jnp.asarray(onehot), precision=lax.Precision.HIGHEST)
    tb = jnp.where(jnp.asarray(col_in)[None, None, :, None, :], tb, -jnp.inf)
    return tb.reshape(rpb.shape[0], NA_HEADS, GRID_W, NA_REL_ROWS * GRID_W)


def _merge_kernel(x_ref, mod_ref, g_ref, ret_ref, s5y_ref, gla_ref, na_ref,
                  wglu_ref, bglu_ref, wbr_ref, wmg_ref, bmg_ref, wout_ref, o_ref, y_s):
    x = x_ref[...]
    mod = mod_ref[0, 0]
    hb = (_rms(x, g_ref[0, 0:1]) * (1.0 + mod[1:2]) + mod[0:1]).astype(BF16)

    def gate_pre(n):
        return _bdot(hb, wmg_ref[0, :, n * D_MODEL:(n + 1) * D_MODEL]) + bmg_ref[0, :, n * D_MODEL:(n + 1) * D_MODEL]

    acc = _sigmoid(gate_pre(0)) * _bdot(ret_ref[...], wbr_ref[0, 0])
    acc += _sigmoid(gate_pre(2)) * _bdot(gla_ref[...], wbr_ref[0, 2])
    na = jnp.concatenate([na_ref[hh].astype(BF16) for hh in range(NA_HEADS)], axis=1)
    acc += _sigmoid(gate_pre(3)) * _bdot(na, wbr_ref[0, 3])
    s5_gate = gate_pre(1)

    rows = TOKEN_TILE // S5_CHUNK
    ng = S5_LANE_GROUPS
    piece = lax.broadcasted_iota(jnp.int32, (rows, 128), 1) // S5_GROUP_CH
    for lb in range(S5_GROUPS // ng):
        for m in range(S5_CHUNK // ng):
            cols = [(lb * ng + g) * S5_W + m * 128 for g in range(ng)]
            out = _piece_transpose([s5y_ref[:, c0:c0 + 128] for c0 in cols], piece)
            for il in range(ng):
                y_s[lb, pl.ds(m * ng + il, rows, stride=S5_CHUNK), :] = out[il]
    y = jnp.concatenate([y_s[lb] for lb in range(S5_GROUPS // ng)], axis=1)
    y = 0.5 * y * (1.0 + jnp.tanh(math.sqrt(2.0 / math.pi) * (y + 0.044715 * (y * y * y))))
    z = _bdot(y, wglu_ref[0]) + bglu_ref[0]
    s5_out = z[:, 0:BRANCH_W] * _sigmoid(z[:, BRANCH_W:2 * BRANCH_W])
    acc += _sigmoid(s5_gate) * _bdot(s5_out, wbr_ref[0, 1])
    m = _bdot(acc, wout_ref[0])
    o_ref[...] = x + mod[2:3] * _rms(m, g_ref[0, 1:2])


def _merge(l, x, mod, g_norm, ret_o, s5_y, gla_o, na_o, wglu, bglu, wbr, wmg, bmg, wout):
    tm = TOKEN_TILE
    tok = lambda w: pl.BlockSpec((tm, w), lambda i: (i, 0))
    return pl.pallas_call(
        _merge_kernel,
        grid=(T_ALL // tm,),
        in_specs=[tok(D_MODEL), _mod_spec(l), _layer_spec(l, 4, D_MODEL),
                  tok(256), pl.BlockSpec((tm // S5_CHUNK, S5_GROUPS * S5_W), lambda i: (i, 0)), tok(256),
                  pl.BlockSpec((NA_HEADS, tm, NA_DH), lambda i: (0, i, 0)),
                  _layer_spec(l, 256, 512), _layer_spec(l, 1, 512),
                  _layer_spec(l, N_BRANCH, BRANCH_W, D_MODEL, single_buffer=True),
                  _layer_spec(l, D_MODEL, N_BRANCH * D_MODEL, single_buffer=True),
                  _layer_spec(l, 1, N_BRANCH * D_MODEL),
                  _layer_spec(l, D_MODEL, D_MODEL, single_buffer=True)],
        out_specs=tok(D_MODEL),
        out_shape=jax.ShapeDtypeStruct((T_ALL, D_MODEL), F32),
        scratch_shapes=[pltpu.VMEM((S5_GROUPS // S5_LANE_GROUPS, tm, 128), F32)],
        compiler_params=_cparams(("parallel",)),
        name="merge",
    )(x, mod, g_norm, ret_o, s5_y, gla_o, na_o, wglu, bglu, wbr, wmg, bmg, wout)


FF_TILE = 1024


def _mlp_kernel(x_ref, mod_ref, g_ref, w1_ref, w2_ref, *o_refs):
    x = x_ref[...]
    mod = mod_ref[0, 0]
    hb = (_rms(x, g_ref[0, 2:3]) * (1.0 + mod[4:5]) + mod[3:4]).astype(BF16)
    nj = D_FF // FF_TILE
    up = lambda j: _bdot(hb, w1_ref[0, :, j * FF_TILE:(j + 1) * FF_TILE])
    f = None
    pre = up(0)
    for j in range(nj):
        nxt = up(j + 1) if j + 1 < nj else None
        a = jnp.maximum(pre, 0.0)
        part = _bdot(a * a, w2_ref[0, j * FF_TILE:(j + 1) * FF_TILE, :])
        f = part if f is None else f + part
        pre = nxt
    y = x + mod[5:6] * _rms(f, g_ref[0, 3:4])
    if len(o_refs) == 1:
        o_refs[0][...] = y
    else:
        ctx_tiles = T_CTX // TOKEN_TILE

        @pl.when(pl.program_id(0) < ctx_tiles)
        def _store_ctx():
            o_refs[0][...] = y

        @pl.when(pl.program_id(0) >= ctx_tiles)
        def _store_lat():
            o_refs[1][...] = y


def _mlp(l, x, mod, g_norm, w1, w2, *, split_out):
    tm = TOKEN_TILE
    tok = pl.BlockSpec((tm, D_MODEL), lambda i: (i, 0))
    if split_out:
        ctx_tiles = T_CTX // tm
        out_specs = [pl.BlockSpec((tm, D_MODEL), lambda i: (jnp.minimum(i, ctx_tiles - 1), 0)),
                     pl.BlockSpec((tm, D_MODEL), lambda i: (jnp.maximum(i - ctx_tiles, 0), 0))]
        out_shape = [jax.ShapeDtypeStruct((T_CTX, D_MODEL), F32), jax.ShapeDtypeStruct((T_LAT, D_MODEL), F32)]
    else:
        out_specs = tok
        out_shape = jax.ShapeDtypeStruct((T_ALL, D_MODEL), F32)
    return pl.pallas_call(
        _mlp_kernel,
        grid=(T_ALL // tm,),
        in_specs=[tok, _mod_spec(l), _layer_spec(l, 4, D_MODEL),
                  _layer_spec(l, D_MODEL, D_FF, single_buffer=True),
                  _layer_spec(l, D_FF, D_MODEL, single_buffer=True)],
        out_specs=out_specs,
        out_shape=out_shape,
        compiler_params=_cparams(("arbitrary",)),
        name="mlp",
    )(x, mod, g_norm, w1, w2)


def _rope_tables():
    half = RET_DK // 2
    nf = half // 2
    t = jnp.arange(DEC_SEQ)
    row = (t // GRID_W).astype(F32)
    col = (t % GRID_W).astype(F32)
    inv = ROPE_BASE ** (-jnp.arange(nf, dtype=F32) / nf)
    ang_r = row[:, None] * inv[None, :]
    ang_c = col[:, None] * inv[None, :]
    cos = jnp.concatenate([jnp.cos(ang_r)] * 2 + [jnp.cos(ang_c)] * 2, axis=1)
    sin = jnp.concatenate([-jnp.sin(ang_r), jnp.sin(ang_r), -jnp.sin(ang_c), jnp.sin(ang_c)], axis=1)
    return jnp.tile(cos, (1, RET_HEADS)), jnp.tile(sin, (1, RET_HEADS))


def _gla_state_in(st):
    eye = jnp.eye(GLA_HEADS, dtype=st.dtype)
    t = jnp.einsum('bldhkv,hg->bldhvgk', st, eye)
    return t.reshape(st.shape[0], st.shape[1], 2, GLA_HEADS * GLA_DV, GLA_HEADS * GLA_DK)


def kernel(x_prompt, x_sample, c, cache_na_k, cache_na_v, state_ret, state_s5, state_gla, c_ctx, w_ada, b_ada, g_norm, w_in, ret_log_decay, ret_gn, s5_lambda_re, s5_lambda_im, s5_log_dt, s5_b_re, s5_b_im, s5_c_re, s5_c_im, s5_d, s5_w_glu, s5_b_glu, gla_w_gate, gla_b_gate, gla_gn, na_rpb, w_branch, w_merge, b_merge, w_out, w_mlp1, w_mlp2):
    depth = w_in.shape[0]
    x = (x_prompt.reshape(T_CTX, D_MODEL), x_sample.reshape(T_LAT, D_MODEL))
    cc = jnp.concatenate([c_ctx[None], c, jnp.zeros((N_MOD_ROWS - 1 - DEC_BATCH, D_MODEL), F32)], axis=0)
    mod = _ada(cc, w_ada, b_ada).reshape(depth, N_MOD_ROWS, 6, D_MODEL)

    cos, sin = _rope_tables()
    w_in_t = w_in.transpose(0, 2, 1)
    ret_gn3, gla_gn3 = (a.reshape(depth, 1, BRANCH_W) for a in (ret_gn, gla_gn))
    s5_d_rows = jnp.broadcast_to(s5_d.reshape(depth, S5_GROUPS, 1, S5_GROUP_CH),
                                 (depth, S5_GROUPS, S5_CHUNK, S5_GROUP_CH)).reshape(depth, 1, S5_GROUPS * S5_W)
    b_glu3 = s5_b_glu.reshape(depth, 1, 2 * BRANCH_W)
    b_mg3 = b_merge.reshape(depth, 1, N_BRANCH * D_MODEL)
    cache_k = cache_na_k.transpose(0, 1, 3, 2, 4)
    cache_v = cache_na_v.transpose(0, 1, 3, 2, 4)
    na_tb = _na_tables(na_rpb)
    gla_s0 = _gla_state_in(state_gla)
    s5_tables, s5_lam = _s5_tables(s5_lambda_re, s5_lambda_im, s5_log_dt, s5_b_re, s5_b_im,
                                   s5_c_re, s5_c_im)
    s5_x0 = state_s5.transpose(1, 2, 0, 3, 5, 4).reshape(depth, 2, DEC_BATCH, S5_GROUPS * S5_P2)

    ks_l, vs_l, ret_l, s5_l, gla_l = [], [], [], [], []
    for l in range(depth):
        proj = _inproj(l, x, mod, g_norm, w_in_t)
        if l == 0:
            x, proj = proj[0], proj[1:]
        ret, su, gqk, gv, gg, glr, nq, nk, nv = proj

        ret_o, st_ret = _retention(l, ret, ret_log_decay, ret_gn3, latent=False)
        ret_o, = _retention(l, ret, ret_log_decay, ret_gn3, latent=True, cos=cos, sin=sin, s0=state_ret,
                            prev=ret_o)

        s5_y, s5_fin = _s5(l, su, *s5_tables, s5_d_rows, s5_lam, s5_x0)

        gla_o, st_gla = _gla(l, gqk, gv, gg, glr, gla_w_gate, gla_b_gate, gla_gn3, latent=False)
        gla_o, = _gla(l, gqk, gv, gg, glr, gla_w_gate, gla_b_gate, gla_gn3, latent=True, s0=gla_s0,
                      prev=gla_o)

        na_o = _attn_ctx(nq, nk, nv)
        na_o = _attn_lat(l, nq, nk, nv, cache_k, cache_v, na_tb, na_o)

        x = _merge(l, x, mod, g_norm, ret_o, s5_y, gla_o, na_o,
                   s5_w_glu, b_glu3, w_branch, w_merge, b_mg3, w_out)
        x = _mlp(l, x, mod, g_norm, w_mlp1, w_mlp2, split_out=(l == depth - 1))

        ks_l.append(nk)
        vs_l.append(nv)
        ret_l.append(st_ret)
        s5_l.append(s5_fin)
        gla_l.append(st_gla)

    y_prompt = x[0].reshape(BATCH, SEQ, D_MODEL)
    y_sample = x[1].reshape(DEC_BATCH, DEC_SEQ, D_MODEL)

    def cache_out(per_layer):
        a = jnp.stack(per_layer, axis=0)[:, :, :T_CTX].reshape(depth, NA_HEADS, BATCH, SEQ, NA_DH)
        return a.transpose(2, 0, 3, 1, 4)

    s5_out = jnp.stack(s5_l, axis=0).reshape(depth, 2, BATCH, S5_GROUPS, 2, S5_STATE)
    return (y_prompt, y_sample, cache_out(ks_l), cache_out(vs_l), jnp.stack(ret_l, axis=1),
            s5_out.transpose(2, 0, 1, 3, 5, 4), jnp.stack(gla_l, axis=1))
```

```python
import functools
import math

import numpy as np
import jax
import jax.numpy as jnp
from jax import lax
from jax.experimental import pallas as pl
from jax.experimental.pallas import tpu as pltpu

F32 = jnp.float32
BF16 = jnp.bfloat16

D_MODEL = 1024
BATCH = 16
SEQ = 256
DEPTH = 4
DEC_BATCH = 4
DEC_SEQ = 1024
PAST_LEN = 256
GRID_W = 64
N_BRANCH = 4
BRANCH_W = 256
RET_HEADS = 4
RET_DK = 64
RET_DV = 64
S5_GROUPS = 16
S5_GROUP_CH = 16
S5_STATE = 64
GLA_HEADS = 4
GLA_DK = 32
GLA_DV = 64
GLA_RANK = 16
GLA_TAU = 16.0
NA_HEADS = 4
NA_DH = 64
NA_WIN_H = 8
NA_WIN_W = 16
D_FF = 4 * D_MODEL
ROPE_BASE = 10000.0
EPS = 1e-6

T_CTX = BATCH * SEQ
T_LAT = DEC_BATCH * DEC_SEQ
T_ALL = T_CTX + T_LAT
LAT_BLOCK0 = T_CTX // DEC_SEQ
N_MOD_ROWS = 8
TOKEN_TILE = 512
CTX_SEQS_PER_STEP = DEC_SEQ // SEQ
GLA_CHUNK = 64
GLA_BLOCK_CHUNKS = 4
S5_CHUNK = 16
RET_QBLOCK = 256
NA_QBLOCK = 256
GRID_ROWS = DEC_SEQ // GRID_W
NA_KH = min(NA_WIN_H, GRID_ROWS)
NA_REL_ROWS = 2 * NA_WIN_H - 1
VMEM_LIMIT = 56 * 1024 * 1024
D_IN = 2848
IN_TAIL_COL = 2048


def _cparams(sem):
    return pltpu.CompilerParams(dimension_semantics=sem, vmem_limit_bytes=VMEM_LIMIT)


def _bdot(a, b):
    return jnp.dot(a.astype(BF16), b.astype(BF16), preferred_element_type=F32)


def _bdot_nt(a, b):
    return lax.dot_general(a.astype(BF16), b.astype(BF16), (((1,), (1,)), ((), ())),
                           preferred_element_type=F32)


def _bdot_tn(a, b):
    return lax.dot_general(a.astype(BF16), b.astype(BF16), (((0,), (0,)), ((), ())),
                           preferred_element_type=F32)


def _split(a):
    hi = a.astype(BF16)
    lo = (a - hi.astype(F32)).astype(BF16)
    return hi, lo


def _dot3(a, b):
    ah, al = _split(a)
    bh, bl = _split(b)
    d = functools.partial(jnp.dot, preferred_element_type=F32)
    return d(ah, bh) + d(al, bh) + d(ah, bl)


def _sigmoid(x):
    return 0.5 * jnp.tanh(0.5 * x) + 0.5


def _silu(x):
    return x * _sigmoid(x)


def _rms(x, g):
    return x * lax.rsqrt(jnp.mean(x * x, axis=-1, keepdims=True) + EPS) * g


def _head_norm(o, g, width):
    n = o.shape[-1]
    hi = lax.broadcasted_iota(jnp.int32, (n, n), 0) // width
    hj = lax.broadcasted_iota(jnp.int32, (n, n), 1) // width
    avg = jnp.where(hi == hj, 1.0 / width, 0.0).astype(BF16)

    def head_mean(a):
        ah, al = _split(a)
        return (jnp.dot(ah, avg, preferred_element_type=F32) + jnp.dot(al, avg, preferred_element_type=F32))

    xc = o - head_mean(o)
    return xc * lax.rsqrt(head_mean(xc * xc) + EPS) * g


def _mod_row(i):
    ctx_tiles = T_CTX // TOKEN_TILE
    return jnp.where(i < ctx_tiles, 0, 1 + (i - ctx_tiles) // (DEC_SEQ // TOKEN_TILE))


def _mod_spec(l):
    return pl.BlockSpec((1, 1, 6, D_MODEL), lambda i: (l, _mod_row(i), 0, 0))


def _layer_spec(l, *shape, single_buffer=False):
    mode = pl.Buffered(1) if single_buffer else None
    return pl.BlockSpec((1,) + shape, lambda *_: (l,) + (0,) * len(shape), pipeline_mode=mode)


_ANY = pl.BlockSpec(memory_space=pl.ANY)


ADA_TILE = 1536


def _ada_kernel(c_ref, w_ref, b_ref, o_ref):
    a = _silu(c_ref[...])
    o_ref[0] = _bdot(a, w_ref[0]) + b_ref[0]


def _ada(cc, w_ada, b_ada):
    n = 6 * D_MODEL
    return pl.pallas_call(
        _ada_kernel,
        grid=(DEPTH, n // ADA_TILE),
        in_specs=[pl.BlockSpec((N_MOD_ROWS, D_MODEL), lambda l, j: (0, 0)),
                  pl.BlockSpec((1, D_MODEL, ADA_TILE), lambda l, j: (l, 0, j)),
                  pl.BlockSpec((1, 1, ADA_TILE), lambda l, j: (l, 0, j))],
        out_specs=pl.BlockSpec((1, N_MOD_ROWS, ADA_TILE), lambda l, j: (l, 0, j)),
        out_shape=jax.ShapeDtypeStruct((DEPTH, N_MOD_ROWS, n), F32),
        compiler_params=_cparams(("parallel", "parallel")),
        name="ada_mod",
    )(cc, w_ada, b_ada.reshape(DEPTH, 1, n))


def _piece_transpose(blocks, piece):
    x = list(blocks)
    n = len(x)
    d = n // 2
    while d >= 1:
        low = (piece & d) == 0
        for v in range(n):
            if v & d:
                continue
            a, b = x[v], x[v + d]
            x[v] = jnp.where(low, a, pltpu.roll(b, d * S5_GROUP_CH, 1))
            x[v + d] = jnp.where(low, pltpu.roll(a, 128 - d * S5_GROUP_CH, 1), b)
        d //= 2
    return x


def _inproj_kernel(*refs, first):
    if first:
        (xa_ref, xb_ref, mod_ref, g_ref, w_ref, x_out_ref,
         ret_ref, s5_ref, gqk_ref, gv_ref, gg_ref, glr_ref, nq_ref, nk_ref, nv_ref, su_s) = refs
        x = jnp.where(pl.program_id(0) < T_CTX // TOKEN_TILE, xa_ref[...], xb_ref[...])
        x_out_ref[...] = x
    else:
        (x_ref, mod_ref, g_ref, w_ref,
         ret_ref, s5_ref, gqk_ref, gv_ref, gg_ref, glr_ref, nq_ref, nk_ref, nv_ref, su_s) = refs
        x = x_ref[...]
    mod = mod_ref[0, 0]
    h = _rms(x, g_ref[0, 0:1]) * (1.0 + mod[1:2]) + mod[0:1]
    hb = h.astype(BF16)

    def proj(lo, hi):
        return _bdot_nt(hb, w_ref[0, lo:hi, :])

    ret_ref[...] = proj(0, 1024)
    su = proj(1024, 1280)
    rows = TOKEN_TILE // S5_CHUNK
    ng = S5_LANE_GROUPS
    piece = lax.broadcasted_iota(jnp.int32, (rows, 128), 1) // S5_GROUP_CH
    for lb in range(S5_GROUPS // ng):
        su_s[lb] = su[:, lb * 128:(lb + 1) * 128]
        for m in range(S5_CHUNK // ng):
            out = _piece_transpose(
                [su_s[lb, pl.ds(m * ng + jl, rows, stride=S5_CHUNK), :] for jl in range(ng)], piece)
            for g in range(ng):
                col = (lb * ng + g) * S5_W + m * 128
                s5_ref[:, col:col + 128] = out[g]
    gqk_ref[...] = proj(1280, 1536)
    gv_ref[...] = proj(1536, 1792)
    gg_ref[...] = proj(1792, 2048)
    glr_ref[...] = proj(IN_TAIL_COL, IN_TAIL_COL + 128)
    lo = IN_TAIL_COL + 2 * GLA_RANK
    for ref in (nq_ref, nk_ref, nv_ref):
        r = proj(lo, lo + NA_HEADS * NA_DH)
        for hh in range(NA_HEADS):
            ref[hh] = r[:, hh * NA_DH:(hh + 1) * NA_DH]
        lo += NA_HEADS * NA_DH


def _inproj(l, xs, mod, g_norm, w_in):
    tm = TOKEN_TILE
    first = isinstance(xs, tuple)
    tok = lambda w: pl.BlockSpec((tm, w), lambda i: (i, 0))
    head = pl.BlockSpec((NA_HEADS, tm, NA_DH), lambda i: (0, i, 0))
    tshape = lambda w: jax.ShapeDtypeStruct((T_ALL, w), F32)
    hshape = jax.ShapeDtypeStruct((NA_HEADS, T_ALL, NA_DH), F32)
    ctx_tiles = T_CTX // tm
    if first:
        x_specs = [pl.BlockSpec((tm, D_MODEL), lambda i: (jnp.minimum(i, ctx_tiles - 1), 0)),
                   pl.BlockSpec((tm, D_MODEL), lambda i: (jnp.maximum(i - ctx_tiles, 0), 0))]
        x_args = list(xs)
    else:
        x_specs, x_args = [tok(D_MODEL)], [xs]
    return pl.pallas_call(
        functools.partial(_inproj_kernel, first=first),
        grid=(T_ALL // tm,),
        in_specs=x_specs + [_mod_spec(l), _layer_spec(l, 4, D_MODEL),
                            _layer_spec(l, D_IN, D_MODEL, single_buffer=True)],
        out_specs=([tok(D_MODEL)] if first else [])
        + [tok(1024), pl.BlockSpec((tm // S5_CHUNK, S5_GROUPS * S5_W), lambda i: (i, 0)),
           tok(256), tok(256), tok(256), tok(128), head, head, head],
        out_shape=([tshape(D_MODEL)] if first else [])
        + [tshape(1024), jax.ShapeDtypeStruct((S5_ROWS, S5_GROUPS * S5_W), F32),
           tshape(256), tshape(256), tshape(256), tshape(128), hshape, hshape, hshape],
        scratch_shapes=[pltpu.VMEM((S5_GROUPS // S5_LANE_GROUPS, tm, 128), F32)],
        compiler_params=_cparams(("parallel",)),
        name="in_proj",
    )(*x_args, mod, g_norm, w_in)


def _rope_rotate(x, lane):
    first = (lane % 32) < 16
    w = x.shape[-1]
    return jnp.where(first, pltpu.roll(x, w - 16, 1), pltpu.roll(x, 16, 1))


def _ret_kernel(ld_ref, ret_ref, gn_ref, *rest, layer, seq, latent):
    if latent:
        cos_ref, sin_ref, s0_ref, _, out_ref, dec_s = rest
    else:
        out_ref, st_ref, dec_s = rest
    tq = RET_QBLOCK
    nq = seq // tq
    width = dec_s.shape[-1]

    @pl.when(pl.program_id(0) == 0)
    def _build_decay():
        rel = (lax.broadcasted_iota(jnp.int32, (tq, width), 0) + (nq - 1) * tq
               - lax.broadcasted_iota(jnp.int32, (tq, width), 1)).astype(F32)
        for h in range(RET_HEADS):
            dec_s[h] = (jnp.where(rel >= 0, jnp.exp(ld_ref[layer, 0, h] * jnp.maximum(rel, 0.0)), 0.0)
                        + jnp.where(rel <= 0, jnp.exp(ld_ref[layer, 1, h] * jnp.maximum(-rel, 0.0)), 0.0))

    nrows = ret_ref.shape[0]
    nsub = nrows // seq
    q = ret_ref[:, 0:256]
    k = ret_ref[:, 256:512]
    if latent:
        lane = lax.broadcasted_iota(jnp.int32, (nrows, 256), 1)
        cos = cos_ref[...]
        sin = sin_ref[...]
        q = q * cos + _rope_rotate(q, lane) * sin
        k = k * cos + _rope_rotate(k, lane) * sin
    k = k * (RET_DK ** -0.5)
    pos_c = lax.broadcasted_iota(jnp.int32, (seq, 1), 0).astype(F32)
    tiles = [(s, qb) for s in range(nsub) for qb in range(nq)]
    for h in range(RET_HEADS):
        lgf = ld_ref[layer, 0, h]
        lgb = ld_ref[layer, 1, h]
        sl = slice(h * RET_DK, (h + 1) * RET_DK)
        qh = q[:, sl]
        kh = k[:, sl]
        kb = kh.astype(BF16)
        vb = ret_ref[:, 512 + h * RET_DV:512 + (h + 1) * RET_DV].astype(BF16)
        if latent:
            q_init = jnp.concatenate([qh * jnp.exp(lgf * (pos_c + 1.0)),
                                      qh * jnp.exp(lgb * (seq - pos_c))], axis=1)
            s_init = jnp.concatenate([s0_ref[0, 0, 0, h], s0_ref[0, 0, 1, h]], axis=0)

        def score(s, qb):
            w0 = (nq - 1 - qb) * tq
            rows = slice(s * seq + qb * tq, s * seq + (qb + 1) * tq)
            keys = slice(s * seq, (s + 1) * seq)
            return (_bdot_nt(qh[rows], kb[keys]) * dec_s[h, :, w0:w0 + seq]).astype(BF16)

        def values(sc, s, qb):
            o = jnp.dot(sc, vb[s * seq:(s + 1) * seq], preferred_element_type=F32)
            if latent:
                o = o + _bdot(q_init[qb * tq:(qb + 1) * tq], s_init)
            return o

        def finish(o, s, qb):
            out_ref[s * seq + qb * tq:s * seq + (qb + 1) * tq, sl] = o

        if nsub > 1:
            scores = [score(s, qb) for s, qb in tiles]
            outs = [values(sc, s, qb) for sc, (s, qb) in zip(scores, tiles)]
            for o, (s, qb) in zip(outs, tiles):
                finish(o, s, qb)
        else:
            for s, qb in tiles:
                finish(values(score(s, qb), s, qb), s, qb)
        if not latent:
            for s in range(nsub):
                keys = slice(s * seq, (s + 1) * seq)
                st_ref[s, 0, h] = _bdot_tn(kh[keys] * jnp.exp(lgf * (seq - 1.0 - pos_c)), vb[keys])
                st_ref[s, 1, h] = _bdot_tn(kh[keys] * jnp.exp(lgb * pos_c), vb[keys])
    for r0 in range(0, nrows, tq):
        rows = slice(r0, r0 + tq)
        out_ref[rows, :] = _head_norm(out_ref[rows, :], gn_ref[0], RET_DV) * _silu(ret_ref[rows, 768:1024])


def _retention(l, ret, ld, gn, *, latent, cos=None, sin=None, s0=None, prev=None):
    seq = DEC_SEQ if latent else SEQ
    nsub = 1 if latent else CTX_SEQS_PER_STEP
    nb = DEC_BATCH if latent else BATCH // nsub
    off = LAT_BLOCK0 if latent else 0
    rows = nsub * seq
    in_specs = [pl.BlockSpec(memory_space=pltpu.SMEM),
                pl.BlockSpec((rows, 1024), lambda b: (b + off, 0)),
                _layer_spec(l, 1, 256)]
    args = [ld, ret, gn]
    out_specs = [pl.BlockSpec((rows, 256), lambda b: (b + off, 0))]
    out_shape = [jax.ShapeDtypeStruct((T_ALL, 256), F32)]
    aliases = {}
    if latent:
        in_specs += [pl.BlockSpec((seq, 256), lambda b: (0, 0)),
                     pl.BlockSpec((seq, 256), lambda b: (0, 0)),
                     pl.BlockSpec((1, 1, 2, RET_HEADS, RET_DK, RET_DV), lambda b: (b, l, 0, 0, 0, 0)),
                     _ANY]
        args += [cos, sin, s0, prev]
        aliases = {6: 0}
    else:
        out_specs.append(pl.BlockSpec((nsub, 2, RET_HEADS, RET_DK, RET_DV), lambda b: (b, 0, 0, 0, 0)))
        out_shape.append(jax.ShapeDtypeStruct((BATCH, 2, RET_HEADS, RET_DK, RET_DV), F32))
    return pl.pallas_call(
        functools.partial(_ret_kernel, layer=l, seq=seq, latent=latent),
        grid=(nb,),
        in_specs=in_specs, out_specs=out_specs, out_shape=out_shape,
        input_output_aliases=aliases,
        scratch_shapes=[pltpu.VMEM((RET_HEADS, RET_QBLOCK, 2 * seq - RET_QBLOCK), F32)],
        compiler_params=_cparams(("arbitrary",)),
        name="retention_lat" if latent else "retention_ctx",
    )(*args)


def _gla_kernel(gqk_ref, gv_ref, gg_ref, glr_ref, wg_ref, bg_ref, gn_ref, *rest, seq, latent):
    if latent:
        s0_ref, _, out_ref, gate_s, o_s, st_s, qst_s, ds_s, e_s = rest
    else:
        out_ref, st_ref, gate_s, o_s, st_s, qst_s, ds_s, e_s = rest
    c = GLA_CHUNK
    n = seq // c
    nsub = gqk_ref.shape[0] // seq
    hk = GLA_HEADS * GLA_DK
    lr = glr_ref[...]
    for d in range(2):
        pre = _bdot(lr[:, d * GLA_RANK:(d + 1) * GLA_RANK], wg_ref[0, d]) + bg_ref[0, d:d + 1]
        gate_s[d] = (jnp.minimum(pre, 0.0) - jnp.log(1.0 + jnp.exp(-jnp.abs(pre)))) / GLA_TAU
        for s in range(nsub):
            st_s[2 * s + d] = s0_ref[0, 0, d] if latent else jnp.zeros((GLA_HEADS * GLA_DV, hk), F32)

    nc = GLA_BLOCK_CHUNKS
    rb = nc * c
    ti = lax.broadcasted_iota(jnp.int32, (rb, rb), 0)
    tj = lax.broadcasted_iota(jnp.int32, (rb, rb), 1)
    same = (ti // c) == (tj // c)
    ones = lambda m: (same & m).astype(BF16)
    tri = [ones(tj <= ti), ones(tj >= ti)]
    mid = [ones((tj % c) < c // 2), ones((tj % c) >= c // 2)]
    tot = ones(tj == tj)
    lane_k = lax.broadcasted_iota(jnp.int32, (c, hk), 1)
    head_mask = [(lane_k // GLA_DK) == h for h in range(GLA_HEADS)]
    ai = lax.broadcasted_iota(jnp.int32, (GLA_HEADS * c, c), 0) % c
    aj = lax.broadcasted_iota(jnp.int32, (GLA_HEADS * c, c), 1)
    keep = [aj <= ai, aj >= ai]
    sr = lax.broadcasted_iota(jnp.int32, (GLA_HEADS * GLA_DV, hk), 0) // GLA_DV
    sc = lax.broadcasted_iota(jnp.int32, (GLA_HEADS * GLA_DV, hk), 1) // GLA_DK
    diag = sr == sc
    scale = GLA_DK ** -0.5
    d32 = functools.partial(jnp.dot, preferred_element_type=F32)

    def rows_of(i, size):
        return pl.ds(i * size, size) if isinstance(i, int) else pl.ds(pl.multiple_of(i * size, size), size)

    def local(block_ids):
        vs, q_att, k_att, k_st = {}, {}, {}, {}
        for bi in block_ids:
            rows = rows_of(bi, rb)
            q = gqk_ref[rows, 0:hk] * scale
            k = gqk_ref[rows, hk:2 * hk]
            vs[bi] = gv_ref[rows, :].astype(BF16)
            for d in range(2):
                parts = jnp.concatenate(_split(gate_s[d, rows, :]), axis=1)
                sums = lambda m: (lambda r: r[:, 0:hk] + r[:, hk:2 * hk])(d32(m, parts))
                b, b_mid, b_end = sums(tri[d]), sums(mid[d]), sums(tot)
                q_att[bi, d] = q * jnp.exp(b - b_mid)
                k_att[bi, d] = k * jnp.exp(b_mid - b)
                k_st[bi, d] = (k * jnp.exp(b_end - b)).astype(BF16)
                qst_s[d, rows, :] = (q * jnp.exp(b)).astype(BF16)
                decay = jnp.exp(b_end)
                for cc in range(nc):
                    e_s[d, bi * nc + cc] = decay[cc * c:cc * c + 8]
        tiles = [(bi, d, cc) for bi in block_ids for d in range(2) for cc in range(nc)]
        att = {}
        for bi, d, cc in tiles:
            r = slice(cc * c, (cc + 1) * c)
            qa = q_att[bi, d][r]
            q_stack = jnp.concatenate([jnp.where(head_mask[h], qa, 0.0) for h in range(GLA_HEADS)], axis=0)
            att[bi, d, cc] = jnp.where(keep[d], _bdot_nt(q_stack, k_att[bi, d][r]), 0.0).astype(BF16)
        for bi, d, cc in tiles:
            r = slice(cc * c, (cc + 1) * c)
            o = jnp.concatenate([d32(att[bi, d, cc][h * c:(h + 1) * c], vs[bi][r, h * GLA_DV:(h + 1) * GLA_DV])
                                 for h in range(GLA_HEADS)], axis=1)
            o_s[d, rows_of(bi * nc + cc, c), :] = o
        for bi, d, cc in tiles:
            r = slice(cc * c, (cc + 1) * c)
            ds_s[d, bi * nc + cc] = jnp.where(diag, lax.dot_general(
                vs[bi][r], k_st[bi, d][r], (((0,), (0,)), ((), ())), preferred_element_type=F32), 0.0)

    def recur(s, ci, d):
        g = s * n + ci
        rows = rows_of(g, c)
        st = st_s[2 * s + d]
        o_s[d, rows, :] += _bdot_nt(qst_s[d, rows, :], st)
        st_s[2 * s + d] = st * e_s[d, g, 0:1] + ds_s[d, g]

    def recur_body(i, carry):
        for s in range(nsub):
            recur(s, i, 0)
            recur(s, n - 1 - i, 1)
        return carry

    local(list(range(nsub * n // nc)))
    if n <= 4:
        for i in range(n):
            recur_body(i, 0)
    else:
        lax.fori_loop(0, n, recur_body, 0, unroll=2)

    for r0 in range(0, nsub * seq, rb):
        rows = slice(r0, r0 + rb)
        out_ref[rows, :] = (_head_norm(o_s[0, rows, :] + o_s[1, rows, :], gn_ref[0], GLA_DV)
                            * _silu(gg_ref[rows, :]))
    if not latent:
        hv = GLA_HEADS * GLA_DV
        eye = (lax.broadcasted_iota(jnp.int32, (hv, hv), 0)
               == lax.broadcasted_iota(jnp.int32, (hv, hv), 1)).astype(BF16)
        tn = lambda a: lax.dot_general(a, eye, (((0,), (0,)), ((), ())), preferred_element_type=F32)
        for s in range(nsub):
            for d in range(2):
                st = st_s[2 * s + d]
                hi, lo = _split(st)
                lo2 = (st - hi.astype(F32) - lo.astype(F32)).astype(BF16)
                s_all = tn(hi) + tn(lo) + tn(lo2)
                for h in range(GLA_HEADS):
                    st_ref[s, d, h] = s_all[h * GLA_DK:(h + 1) * GLA_DK, h * GLA_DV:(h + 1) * GLA_DV]


def _gla(l, gqk, gv, gg, glr, wg, bg, gn, *, latent, s0=None, prev=None):
    seq = DEC_SEQ if latent else SEQ
    nsub = 1 if latent else CTX_SEQS_PER_STEP
    nb = DEC_BATCH if latent else BATCH // nsub
    off = LAT_BLOCK0 if latent else 0
    rows = nsub * seq
    hk = GLA_HEADS * GLA_DK
    hv = GLA_HEADS * GLA_DV
    tok = lambda w: pl.BlockSpec((rows, w), lambda b: (b + off, 0))
    in_specs = [tok(256), tok(256), tok(256), tok(128),
                _layer_spec(l, 2, GLA_RANK, hk), _layer_spec(l, 2, hk), _layer_spec(l, 1, 256)]
    args = [gqk, gv, gg, glr, wg, bg, gn]
    out_specs = [tok(256)]
    out_shape = [jax.ShapeDtypeStruct((T_ALL, 256), F32)]
    aliases = {}
    if latent:
        in_specs += [pl.BlockSpec((1, 1, 2, hv, hk), lambda b: (b, l, 0, 0, 0)), _ANY]
        args += [s0, prev]
        aliases = {8: 0}
    else:
        out_specs.append(pl.BlockSpec((nsub, 2, GLA_HEADS, GLA_DK, GLA_DV), lambda b: (b, 0, 0, 0, 0)))
        out_shape.append(jax.ShapeDtypeStruct((BATCH, 2, GLA_HEADS, GLA_DK, GLA_DV), F32))
    return pl.pallas_call(
        functools.partial(_gla_kernel, seq=seq, latent=latent),
        grid=(nb,),
        in_specs=in_specs, out_specs=out_specs, out_shape=out_shape,
        input_output_aliases=aliases,
        scratch_shapes=[pltpu.VMEM((2, rows, hk), F32), pltpu.VMEM((2, rows, hv), F32),
                        pltpu.VMEM((2 * nsub, hv, hk), F32), pltpu.VMEM((2, rows, hk), BF16),
                        pltpu.VMEM((2, rows // GLA_CHUNK, hv, hk), F32),
                        pltpu.VMEM((2, rows // GLA_CHUNK, 8, hk), F32)],
        compiler_params=_cparams(("parallel",)),
        name="gla_lat" if latent else "gla_ctx",
    )(*args)


S5_W = S5_CHUNK * S5_GROUP_CH
S5_P2 = 2 * S5_STATE
S5_ROWS = T_ALL // S5_CHUNK
S5_ROWS_CTX = T_CTX // S5_CHUNK
S5_LANE_GROUPS = 128 // S5_GROUP_CH


S5_TE_ROWS = S5_W + 4 * S5_P2


def _s5_toeplitz_kernel(cc_ref, pwr_ref, pwi_ref, bbr_ref, bbi_ref, ctr_ref, cti_ref, te_ref, ffb_ref):
    lane = lax.broadcasted_iota(jnp.int32, (S5_GROUP_CH, S5_W), 1)
    swapped = lambda t: jnp.concatenate([t[S5_STATE:], t[:S5_STATE]], axis=0)
    for t in range(S5_TABLE_GROUPS):
        def times_b(v, d):
            pr, pi, br, bi = pwr_ref[0, t, v], pwi_ref[0, t, v], bbr_ref[0, d, t], bbi_ref[0, d, t]
            return jnp.concatenate([pr * br - pi * bi, pr * bi + pi * br], axis=0)

        def times_c(v, d):
            pr, pi, cr, ci = pwr_ref[0, t, v], pwi_ref[0, t, v], ctr_ref[0, d, t], cti_ref[0, d, t]
            return jnp.concatenate([cr * pr - ci * pi, -(cr * pi + ci * pr)], axis=0)

        wfr = times_b(0, 0)
        wb = times_b(1, 1)
        kf = _dot3(cc_ref[0, 0, t], wfr)
        kb = _dot3(cc_ref[0, 1, t], wb)
        blocks = []
        for i in range(S5_CHUNK):
            sf = ((i + 1 - S5_CHUNK) * S5_GROUP_CH) % S5_W
            fwd = jnp.where(lane < (i + 1) * S5_GROUP_CH, pltpu.roll(kf, sf, 1) if sf else kf, 0.0)
            bwd = jnp.where(lane >= i * S5_GROUP_CH, pltpu.roll(kb, i * S5_GROUP_CH, 1) if i else kb, 0.0)
            blocks.append(fwd + bwd)
        te_ref[0, t] = jnp.concatenate(blocks + [wfr, wb, swapped(wfr), swapped(wb)], axis=0).astype(BF16)
        ffb_ref[0, t] = jnp.concatenate([times_c(2, 0), times_c(3, 1)], axis=0).astype(BF16)


S5_TABLE_GROUPS = 4


def _s5_toeplitz(cc, pwr, pwi, bbr, bbi, ctr, cti):
    tg = S5_TABLE_GROUPS
    nl, _, ngroups = cc.shape[:3]
    by_dir = lambda *tail: pl.BlockSpec((1, 2, tg) + tail, lambda l, i: (l, 0, i) + (0,) * len(tail))
    by_group = lambda *tail: pl.BlockSpec((1, tg) + tail, lambda l, i: (l, i) + (0,) * len(tail))
    lanes = by_dir(S5_STATE, S5_W)
    powers = by_group(4, S5_STATE, S5_W)
    return pl.pallas_call(
        _s5_toeplitz_kernel,
        grid=(nl, ngroups // tg),
        in_specs=[by_dir(S5_GROUP_CH, S5_P2), powers, powers, lanes, lanes, lanes, lanes],
        out_specs=[by_group(S5_TE_ROWS, S5_W), by_group(2 * S5_P2, S5_W)],
        out_shape=[jax.ShapeDtypeStruct((nl, ngroups, S5_TE_ROWS, S5_W), BF16),
                   jax.ShapeDtypeStruct((nl, ngroups, 2 * S5_P2, S5_W), BF16)],
        compiler_params=_cparams(("parallel", "parallel")),
        name="s5_toeplitz",
    )(cc, pwr, pwi, bbr, bbi, ctr, cti)


def _s5_kernel(u_ref, te_ref, ff_ref, d_ref, lam_ref, x0_ref, y_ref, fin_ref, xs_s, ps_s):
    C = S5_CHUNK
    ng = S5_LANE_GROUPS

    for g in range(ng):
        u = u_ref[:, g * S5_W:(g + 1) * S5_W]
        r = _bdot_nt(u, te_ref[0, g])
        y_ref[:, g * S5_W:(g + 1) * S5_W] = r[:, 0:S5_W] + d_ref[0, :, g * S5_W:(g + 1) * S5_W] * u
        for t in range(4):
            xs_s[t * ng + g] = r[:, S5_W + t * S5_P2:S5_W + (t + 1) * S5_P2]

    w = ng * S5_P2

    def carry(base, nseq, nchunks, init_f, init_b):
        a_f, b_f = lam_ref[0, 0, 0:1], lam_ref[0, 0, 1:2]
        a_b, b_b = lam_ref[0, 1, 0:1], lam_ref[0, 1, 1:2]
        load = lambda t, rows: jnp.concatenate([xs_s[t * ng + g, rows, :] for g in range(ng)], axis=1)

        def body(i, st):
            sf, tf, sb, tb = st
            rf = pl.ds(base + i, nseq, stride=nchunks)
            rb = pl.ds(base + (nchunks - 1 - i), nseq, stride=nchunks)
            for g in range(ng):
                ps_s[g, rf, :] = sf[:, g * S5_P2:(g + 1) * S5_P2]
                ps_s[ng + g, rb, :] = sb[:, g * S5_P2:(g + 1) * S5_P2]
            return (a_f * sf + b_f * tf + load(0, rf), a_f * tf - b_f * sf + load(2, rf),
                    a_b * sb + b_b * tb + load(1, rb), a_b * tb - b_b * sb + load(3, rb))

        first = (lax.broadcasted_iota(jnp.int32, (1, w), 1) % S5_P2) < S5_STATE
        swap = lambda s: jnp.where(first, pltpu.roll(s, w - S5_STATE, 1), pltpu.roll(s, S5_STATE, 1))
        return lax.fori_loop(0, nchunks, body, (init_f, swap(init_f), init_b, swap(init_b)))

    zeros = jnp.zeros((BATCH, w), F32)
    fin = carry(0, BATCH, SEQ // C, zeros, zeros)
    fin_ref[0] = fin[0]
    fin_ref[1] = fin[2]
    carry(S5_ROWS_CTX, DEC_BATCH, DEC_SEQ // C, x0_ref[0, 0], x0_ref[0, 1])

    for g in range(ng):
        p = jnp.concatenate([ps_s[g], ps_s[ng + g]], axis=1)
        y_ref[:, g * S5_W:(g + 1) * S5_W] += _bdot(p, ff_ref[0, g])


def _s5(l, su_rows, te, ff, d_rows, lam, x0):
    ng = S5_LANE_GROUPS
    w = ng * S5_P2
    rows = pl.BlockSpec((S5_ROWS, ng * S5_W), lambda i: (0, i))
    return pl.pallas_call(
        _s5_kernel,
        grid=(S5_GROUPS // ng,),
        in_specs=[rows,
                  pl.BlockSpec((1, ng, S5_TE_ROWS, S5_W), lambda i: (l, i, 0, 0)),
                  pl.BlockSpec((1, ng, 2 * S5_P2, S5_W), lambda i: (l, i, 0, 0)),
                  pl.BlockSpec((1, 1, ng * S5_W), lambda i: (l, 0, i)),
                  pl.BlockSpec((1, 2, 2, w), lambda i: (l, 0, 0, i)),
                  pl.BlockSpec((1, 2, DEC_BATCH, w), lambda i: (l, 0, 0, i))],
        out_specs=[rows, pl.BlockSpec((2, BATCH, w), lambda i: (0, 0, i))],
        out_shape=[jax.ShapeDtypeStruct((S5_ROWS, S5_GROUPS * S5_W), F32),
                   jax.ShapeDtypeStruct((2, BATCH, S5_GROUPS * S5_P2), F32)],
        scratch_shapes=[pltpu.VMEM((4 * ng, S5_ROWS, S5_P2), F32), pltpu.VMEM((2 * ng, S5_ROWS, S5_P2), F32)],
        compiler_params=_cparams(("parallel",)),
        name="s5_scan",
    )(su_rows, te, ff, d_rows, lam, x0)


def _s5_tables(lam_re, lam_im, log_dt, b_re, b_im, c_re, c_im):
    C, G, P, H = S5_CHUNK, S5_GROUPS, S5_STATE, S5_GROUP_CH
    L = lam_re.shape[0]
    dt = jnp.exp(log_dt)[..., None]
    ar, ai = lam_re * dt, lam_im * dt

    steps = jnp.arange(C + 1, dtype=F32)
    mag = jnp.exp(ar[..., None] * steps)
    pw_re, pw_im = mag * jnp.cos(ai[..., None] * steps), mag * jnp.sin(ai[..., None] * steps)
    exact = functools.partial(jnp.einsum, precision=lax.Precision.HIGHEST)
    tau_np = np.arange(C * H) // H

    patterns = [(0, (C - 1) - tau_np), (1, tau_np), (0, tau_np + 1), (1, C - tau_np)]
    sel = np.zeros((2, len(patterns), C + 1, C * H), np.float32)
    for v, (d, t) in enumerate(patterns):
        sel[d, v, t, np.arange(C * H)] = 1.0
    sel = jnp.asarray(sel)
    pwr = exact('ldgpt,dvtn->lgvpn', pw_re, sel)
    pwi = exact('ldgpt,dvtn->lgvpn', pw_im, sel)

    lr, li = pw_re[..., 1], pw_im[..., 1]
    den = lam_re * lam_re + lam_im * lam_im
    qr = ((lr - 1.0) * lam_re + li * lam_im) / den
    qi = (li * lam_re - (lr - 1.0) * lam_im) / den
    bbr = qr[..., None] * b_re - qi[..., None] * b_im
    bbi = qr[..., None] * b_im + qi[..., None] * b_re
    chan = jnp.asarray((np.arange(H)[:, None] == (np.arange(C * H) % H)[None, :]).astype(np.float32))
    lanes = lambda a: exact('ldgph,hn->ldgpn', a, chan)
    bbr, bbi = lanes(bbr), lanes(bbi)
    c_t = lambda a: exact('ldghp,hn->ldgpn', a, chan)
    ctr, cti = c_t(c_re), c_t(c_im)

    cc = jnp.concatenate([c_re, -c_im], axis=-1)
    tables = _s5_toeplitz(cc, pwr, pwi, bbr, bbi, ctr, cti)
    cr, ci = pw_re[..., C], pw_im[..., C]
    a = jnp.concatenate([cr, cr], axis=-1).reshape(L, 2, 1, G * 2 * P)
    b = jnp.concatenate([-ci, ci], axis=-1).reshape(L, 2, 1, G * 2 * P)
    return tables, jnp.concatenate([a, b], axis=2)


def _softmax_pv(s_parts, v_parts):
    m = s_parts[0].max(axis=-1, keepdims=True)
    for s in s_parts[1:]:
        m = jnp.maximum(m, s.max(axis=-1, keepdims=True))
    o = None
    l = None
    for s, v in zip(s_parts, v_parts):
        p = jnp.exp(s - m)
        pl_ = p.sum(axis=-1, keepdims=True)
        po = _bdot(p, v)
        o = po if o is None else o + po
        l = pl_ if l is None else l + pl_
    return o / l


def _attn_ctx_kernel(q_ref, k_ref, v_ref, o_ref):
    scale = NA_DH ** -0.5
    tiles = [(h, slice(s * SEQ, (s + 1) * SEQ)) for h in range(NA_HEADS) for s in range(CTX_SEQS_PER_STEP)]
    scores = [_bdot_nt(q_ref[h, rows, :], k_ref[h, rows, :]) * scale for h, rows in tiles]
    for s, (h, rows) in zip(scores, tiles):
        o_ref[h, rows, :] = _softmax_pv([s], [v_ref[h, rows, :]])


def _attn_ctx(nq, nk, nv):
    spec = pl.BlockSpec((NA_HEADS, CTX_SEQS_PER_STEP * SEQ, NA_DH), lambda b: (0, b, 0))
    return pl.pallas_call(
        _attn_ctx_kernel,
        grid=(BATCH // CTX_SEQS_PER_STEP,),
        in_specs=[spec, spec, spec],
        out_specs=spec,
        out_shape=jax.ShapeDtypeStruct((NA_HEADS, T_ALL, NA_DH), F32),
        compiler_params=_cparams(("parallel",)),
        name="attn_ctx",
    )(nq, nk, nv)


def _attn_lat_kernel(q_ref, k_ref, v_ref, kc_ref, vc_ref, tb_ref, _, o_ref, bias_s):
    @pl.when(pl.program_id(1) == 0)
    def _build_bias():
        bias_s[...] = jnp.full((DEC_SEQ, DEC_SEQ), -jnp.inf, F32)
        for r in range(GRID_ROWS):
            rs = min(max(r - NA_KH // 2, 0), GRID_ROWS - NA_KH)
            dr0 = rs - r + NA_WIN_H - 1
            bias_s[r * GRID_W:(r + 1) * GRID_W, rs * GRID_W:(rs + NA_KH) * GRID_W] = (
                tb_ref[0, 0, :, dr0 * GRID_W:(dr0 + NA_KH) * GRID_W])

    scale = NA_DH ** -0.5
    kb = k_ref[0].astype(BF16)
    vb = v_ref[0].astype(BF16)
    kc = kc_ref[0, 0, 0].astype(BF16)
    vc = vc_ref[0, 0, 0].astype(BF16)
    tq = NA_QBLOCK
    q_rows = tq // GRID_W
    tiles = []
    for qb in range(DEC_SEQ // tq):
        rows = slice(qb * tq, (qb + 1) * tq)
        starts = [min(max(r - NA_KH // 2, 0), GRID_ROWS - NA_KH) for r in range(qb * q_rows, (qb + 1) * q_rows)]
        keys = slice(min(starts) * GRID_W // 128 * 128, -(-(max(starts) + NA_KH) * GRID_W // 128) * 128)
        tiles.append((rows, keys))
    scores = []
    for rows, keys in tiles:
        qh = q_ref[0, rows, :].astype(BF16)
        scores.append((_bdot_nt(qh, kb[keys]) * scale + bias_s[rows, keys], _bdot_nt(qh, kc) * scale))
    for (s_loc, s_ctx), (rows, keys) in zip(scores, tiles):
        o_ref[0, rows, :] = _softmax_pv([s_loc, s_ctx], [vb[keys], vc])


def _attn_lat(l, nq, nk, nv, kc, vc, tb, prev):
    tok = pl.BlockSpec((1, DEC_SEQ, NA_DH), lambda h, b: (h, b + LAT_BLOCK0, 0))
    cache = pl.BlockSpec((1, 1, 1, PAST_LEN, NA_DH), lambda h, b: (b, l, h, 0, 0))
    return pl.pallas_call(
        _attn_lat_kernel,
        grid=(NA_HEADS, DEC_BATCH),
        in_specs=[tok, tok, tok, cache, cache,
                  pl.BlockSpec((1, 1, GRID_W, NA_REL_ROWS * GRID_W), lambda h, b: (l, h, 0, 0)),
                  _ANY],
        out_specs=tok,
        out_shape=jax.ShapeDtypeStruct((NA_HEADS, T_ALL, NA_DH), F32),
        input_output_aliases={6: 0},
        scratch_shapes=[pltpu.VMEM((DEC_SEQ, DEC_SEQ), F32)],
        compiler_params=_cparams(("arbitrary", "arbitrary")),
        name="attn_lat",
    )(nq, nk, nv, kc, vc, tb, prev)


def _na_tables(rpb):
    col = np.arange(GRID_W)
    col_start = np.clip(col - NA_WIN_W // 2, 0, GRID_W - NA_WIN_W)
    col_in = (col[None, :] >= col_start[:, None]) & (col[None, :] < col_start[:, None] + NA_WIN_W)
    col_idx = np.clip(col[None, :] - col[:, None] + NA_WIN_W - 1, 0, 2 * NA_WIN_W - 2)
    onehot = (col_idx[:, :, None] == np.arange(2 * NA_WIN_W - 1)[None, None, :]).astype(np.float32)
    tb = jnp.einsum('lhrd,qkd->lhqrk', rpb, jnp.asarray(onehot), precision=lax.Precision.HIGHEST)
    tb = jnp.where(jnp.asarray(col_in)[None, None, :, None, :], tb, -jnp.inf)
    return tb.reshape(rpb.shape[0], NA_HEADS, GRID_W, NA_REL_ROWS * GRID_W)


def _merge_kernel(x_ref, mod_ref, g_ref, ret_ref, s5y_ref, gla_ref, na_ref,
                  wglu_ref, bglu_ref, wbr_ref, wmg_ref, bmg_ref, wout_ref, o_ref, y_s):
    x = x_ref[...]
    mod = mod_ref[0, 0]
    hb = (_rms(x, g_ref[0, 0:1]) * (1.0 + mod[1:2]) + mod[0:1]).astype(BF16)

    def gate_pre(n):
        return _bdot(hb, wmg_ref[0, :, n * D_MODEL:(n + 1) * D_MODEL]) + bmg_ref[0, :, n * D_MODEL:(n + 1) * D_MODEL]

    acc = _sigmoid(gate_pre(0)) * _bdot(ret_ref[...], wbr_ref[0, 0])
    acc += _sigmoid(gate_pre(2)) * _bdot(gla_ref[...], wbr_ref[0, 2])
    na = jnp.concatenate([na_ref[hh].astype(BF16) for hh in range(NA_HEADS)], axis=1)
    acc += _sigmoid(gate_pre(3)) * _bdot(na, wbr_ref[0, 3])
    s5_gate = gate_pre(1)

    rows = TOKEN_TILE // S5_CHUNK
    ng = S5_LANE_GROUPS
    piece = lax.broadcasted_iota(jnp.int32, (rows, 128), 1) // S5_GROUP_CH
    for lb in range(S5_GROUPS // ng):
        for m in range(S5_CHUNK // ng):
            cols = [(lb * ng + g) * S5_W + m * 128 for g in range(ng)]
            out = _piece_transpose([s5y_ref[:, c0:c0 + 128] for c0 in cols], piece)
            for il in range(ng):
                y_s[lb, pl.ds(m * ng + il, rows, stride=S5_CHUNK), :] = out[il]
    y = jnp.concatenate([y_s[lb] for lb in range(S5_GROUPS // ng)], axis=1)
    y = 0.5 * y * (1.0 + jnp.tanh(math.sqrt(2.0 / math.pi) * (y + 0.044715 * (y * y * y))))
    z = _bdot(y, wglu_ref[0]) + bglu_ref[0]
    s5_out = z[:, 0:BRANCH_W] * _sigmoid(z[:, BRANCH_W:2 * BRANCH_W])
    acc += _sigmoid(s5_gate) * _bdot(s5_out, wbr_ref[0, 1])
    m = _bdot(acc, wout_ref[0])
    o_ref[...] = x + mod[2:3] * _rms(m, g_ref[0, 1:2])


def _merge(l, x, mod, g_norm, ret_o, s5_y, gla_o, na_o, wglu, bglu, wbr, wmg, bmg, wout):
    tm = TOKEN_TILE
    tok = lambda w: pl.BlockSpec((tm, w), lambda i: (i, 0))
    return pl.pallas_call(
        _merge_kernel,
        grid=(T_ALL // tm,),
        in_specs=[tok(D_MODEL), _mod_spec(l), _layer_spec(l, 4, D_MODEL),
                  tok(256), pl.BlockSpec((tm // S5_CHUNK, S5_GROUPS * S5_W), lambda i: (i, 0)), tok(256),
                  pl.BlockSpec((NA_HEADS, tm, NA_DH), lambda i: (0, i, 0)),
                  _layer_spec(l, 256, 512), _layer_spec(l, 1, 512),
                  _layer_spec(l, N_BRANCH, BRANCH_W, D_MODEL, single_buffer=True),
                  _layer_spec(l, D_MODEL, N_BRANCH * D_MODEL, single_buffer=True),
                  _layer_spec(l, 1, N_BRANCH * D_MODEL),
                  _layer_spec(l, D_MODEL, D_MODEL, single_buffer=True)],
        out_specs=tok(D_MODEL),
        out_shape=jax.ShapeDtypeStruct((T_ALL, D_MODEL), F32),
        scratch_shapes=[pltpu.VMEM((S5_GROUPS // S5_LANE_GROUPS, tm, 128), F32)],
        compiler_params=_cparams(("parallel",)),
        name="merge",
    )(x, mod, g_norm, ret_o, s5_y, gla_o, na_o, wglu, bglu, wbr, wmg, bmg, wout)


FF_TILE = 1024


def _mlp_kernel(x_ref, mod_ref, g_ref, w1_ref, w2_ref, *o_refs):
    x = x_ref[...]
    mod = mod_ref[0, 0]
    hb = (_rms(x, g_ref[0, 2:3]) * (1.0 + mod[4:5]) + mod[3:4]).astype(BF16)
    nj = D_FF // FF_TILE
    up = lambda j: _bdot(hb, w1_ref[0, :, j * FF_TILE:(j + 1) * FF_TILE])
    f = None
    pre = up(0)
    for j in range(nj):
        nxt = up(j + 1) if j + 1 < nj else None
        a = jnp.maximum(pre, 0.0)
        part = _bdot(a * a, w2_ref[0, j * FF_TILE:(j + 1) * FF_TILE, :])
        f = part if f is None else f + part
        pre = nxt
    y = x + mod[5:6] * _rms(f, g_ref[0, 3:4])
    if len(o_refs) == 1:
        o_refs[0][...] = y
    else:
        ctx_tiles = T_CTX // TOKEN_TILE

        @pl.when(pl.program_id(0) < ctx_tiles)
        def _store_ctx():
            o_refs[0][...] = y

        @pl.when(pl.program_id(0) >= ctx_tiles)
        def _store_lat():
            o_refs[1][...] = y


def _mlp(l, x, mod, g_norm, w1, w2, *, split_out):
    tm = TOKEN_TILE
    tok = pl.BlockSpec((tm, D_MODEL), lambda i: (i, 0))
    if split_out:
        ctx_tiles = T_CTX // tm
        out_specs = [pl.BlockSpec((tm, D_MODEL), lambda i: (jnp.minimum(i, ctx_tiles - 1), 0)),
                     pl.BlockSpec((tm, D_MODEL), lambda i: (jnp.maximum(i - ctx_tiles, 0), 0))]
        out_shape = [jax.ShapeDtypeStruct((T_CTX, D_MODEL), F32), jax.ShapeDtypeStruct((T_LAT, D_MODEL), F32)]
    else:
        out_specs = tok
        out_shape = jax.ShapeDtypeStruct((T_ALL, D_MODEL), F32)
    return pl.pallas_call(
        _mlp_kernel,
        grid=(T_ALL // tm,),
        in_specs=[tok, _mod_spec(l), _layer_spec(l, 4, D_MODEL),
                  _layer_spec(l, D_MODEL, D_FF, single_buffer=True),
                  _layer_spec(l, D_FF, D_MODEL, single_buffer=True)],
        out_specs=out_specs,
        out_shape=out_shape,
        compiler_params=_cparams(("arbitrary",)),
        name="mlp",
    )(x, mod, g_norm, w1, w2)


def _rope_tables():
    half = RET_DK // 2
    nf = half // 2
    t = jnp.arange(DEC_SEQ)
    row = (t // GRID_W).astype(F32)
    col = (t % GRID_W).astype(F32)
    inv = ROPE_BASE ** (-jnp.arange(nf, dtype=F32) / nf)
    ang_r = row[:, None] * inv[None, :]
    ang_c = col[:, None] * inv[None, :]
    cos = jnp.concatenate([jnp.cos(ang_r)] * 2 + [jnp.cos(ang_c)] * 2, axis=1)
    sin = jnp.concatenate([-jnp.sin(ang_r), jnp.sin(ang_r), -jnp.sin(ang_c), jnp.sin(ang_c)], axis=1)
    return jnp.tile(cos, (1, RET_HEADS)), jnp.tile(sin, (1, RET_HEADS))


def _gla_state_in(st):
    eye = jnp.eye(GLA_HEADS, dtype=st.dtype)
    t = jnp.einsum('bldhkv,hg->bldhvgk', st, eye)
    return t.reshape(st.shape[0], st.shape[1], 2, GLA_HEADS * GLA_DV, GLA_HEADS * GLA_DK)


def kernel(x_prompt, x_sample, c, cache_na_k, cache_na_v, state_ret, state_s5, state_gla, c_ctx, w_ada, b_ada, g_norm, w_in, ret_log_decay, ret_gn, s5_lambda_re, s5_lambda_im, s5_log_dt, s5_b_re, s5_b_im, s5_c_re, s5_c_im, s5_d, s5_w_glu, s5_b_glu, gla_w_gate, gla_b_gate, gla_gn, na_rpb, w_branch, w_merge, b_merge, w_out, w_mlp1, w_mlp2):
    depth = w_in.shape[0]
    x = (x_prompt.reshape(T_CTX, D_MODEL), x_sample.reshape(T_LAT, D_MODEL))
    cc = jnp.concatenate([c_ctx[None], c, jnp.zeros((N_MOD_ROWS - 1 - DEC_BATCH, D_MODEL), F32)], axis=0)
    mod = _ada(cc, w_ada, b_ada).reshape(depth, N_MOD_ROWS, 6, D_MODEL)

    cos, sin = _rope_tables()
    w_in_t = w_in.transpose(0, 2, 1)
    ret_gn3, gla_gn3 = (a.reshape(depth, 1, BRANCH_W) for a in (ret_gn, gla_gn))
    s5_d_rows = jnp.broadcast_to(s5_d.reshape(depth, S5_GROUPS, 1, S5_GROUP_CH),
                                 (depth, S5_GROUPS, S5_CHUNK, S5_GROUP_CH)).reshape(depth, 1, S5_GROUPS * S5_W)
    b_glu3 = s5_b_glu.reshape(depth, 1, 2 * BRANCH_W)
    b_mg3 = b_merge.reshape(depth, 1, N_BRANCH * D_MODEL)
    cache_k = cache_na_k.transpose(0, 1, 3, 2, 4)
    cache_v = cache_na_v.transpose(0, 1, 3, 2, 4)
    na_tb = _na_tables(na_rpb)
    gla_s0 = _gla_state_in(state_gla)
    s5_tables, s5_lam = _s5_tables(s5_lambda_re, s5_lambda_im, s5_log_dt, s5_b_re, s5_b_im,
                                   s5_c_re, s5_c_im)
    s5_x0 = state_s5.transpose(1, 2, 0, 3, 5, 4).reshape(depth, 2, DEC_BATCH, S5_GROUPS * S5_P2)

    ks_l, vs_l, ret_l, s5_l, gla_l = [], [], [], [], []
    for l in range(depth):
        proj = _inproj(l, x, mod, g_norm, w_in_t)
        if l == 0:
            x, proj = proj[0], proj[1:]
        ret, su, gqk, gv, gg, glr, nq, nk, nv = proj

        ret_o, st_ret = _retention(l, ret, ret_log_decay, ret_gn3, latent=False)
        ret_o, = _retention(l, ret, ret_log_decay, ret_gn3, latent=True, cos=cos, sin=sin, s0=state_ret,
                            prev=ret_o)

        s5_y, s5_fin = _s5(l, su, *s5_tables, s5_d_rows, s5_lam, s5_x0)

        gla_o, st_gla = _gla(l, gqk, gv, gg, glr, gla_w_gate, gla_b_gate, gla_gn3, latent=False)
        gla_o, = _gla(l, gqk, gv, gg, glr, gla_w_gate, gla_b_gate, gla_gn3, latent=True, s0=gla_s0,
                      prev=gla_o)

        na_o = _attn_ctx(nq, nk, nv)
        na_o = _attn_lat(l, nq, nk, nv, cache_k, cache_v, na_tb, na_o)

        x = _merge(l, x, mod, g_norm, ret_o, s5_y, gla_o, na_o,
                   s5_w_glu, b_glu3, w_branch, w_merge, b_mg3, w_out)
        x = _mlp(l, x, mod, g_norm, w_mlp1, w_mlp2, split_out=(l == depth - 1))

        ks_l.append(nk)
        vs_l.append(nv)
        ret_l.append(st_ret)
        s5_l.append(s5_fin)
        gla_l.append(st_gla)

    y_prompt = x[0].reshape(BATCH, SEQ, D_MODEL)
    y_sample = x[1].reshape(DEC_BATCH, DEC_SEQ, D_MODEL)

    def cache_out(per_layer):
        a = jnp.stack(per_layer, axis=0)[:, :, :T_CTX].reshape(depth, NA_HEADS, BATCH, SEQ, NA_DH)
        return a.transpose(2, 0, 3, 1, 4)

    s5_out = jnp.stack(s5_l, axis=0).reshape(depth, 2, BATCH, S5_GROUPS, 2, S5_STATE)
    return (y_prompt, y_sample, cache_out(ks_l), cache_out(vs_l), jnp.stack(ret_l, axis=1),
            s5_out.transpose(2, 0, 1, 3, 5, 4), jnp.stack(gla_l, axis=1))
```

```python
import functools
import math

import numpy as np
import jax
import jax.numpy as jnp
from jax import lax
from jax.experimental import pallas as pl
from jax.experimental.pallas import tpu as pltpu

F32 = jnp.float32
BF16 = jnp.bfloat16

D_MODEL = 1024
BATCH = 16
SEQ = 256
DEPTH = 4
DEC_BATCH = 4
DEC_SEQ = 1024
PAST_LEN = 256
GRID_W = 64
N_BRANCH = 4
BRANCH_W = 256
RET_HEADS = 4
RET_DK = 64
RET_DV = 64
S5_GROUPS = 16
S5_GROUP_CH = 16
S5_STATE = 64
GLA_HEADS = 4
GLA_DK = 32
GLA_DV = 64
GLA_RANK = 16
GLA_TAU = 16.0
NA_HEADS = 4
NA_DH = 64
NA_WIN_H = 8
NA_WIN_W = 16
D_FF = 4 * D_MODEL
ROPE_BASE = 10000.0
EPS = 1e-6

T_CTX = BATCH * SEQ
T_LAT = DEC_BATCH * DEC_SEQ
T_ALL = T_CTX + T_LAT
LAT_BLOCK0 = T_CTX // DEC_SEQ
N_MOD_ROWS = 8
TOKEN_TILE = 512
CTX_SEQS_PER_STEP = DEC_SEQ // SEQ
GLA_CHUNK = 64
GLA_BLOCK_CHUNKS = 4
S5_CHUNK = 16
RET_QBLOCK = 256
NA_QBLOCK = 256
GRID_ROWS = DEC_SEQ // GRID_W
NA_KH = min(NA_WIN_H, GRID_ROWS)
NA_REL_ROWS = 2 * NA_WIN_H - 1
LANES = 128
VMEM_LIMIT = 56 * 1024 * 1024
IN_RET = 0
IN_S5 = IN_RET + 2 * RET_HEADS * RET_DK + 2 * RET_HEADS * RET_DV
IN_GLA_QK = IN_S5 + S5_GROUPS * S5_GROUP_CH
IN_GLA_V = IN_GLA_QK + 2 * GLA_HEADS * GLA_DK
IN_GLA_G = IN_GLA_V + GLA_HEADS * GLA_DV
IN_TAIL_COL = IN_GLA_G + GLA_HEADS * GLA_DV
IN_NA = IN_TAIL_COL + 2 * GLA_RANK
D_IN = IN_NA + 3 * NA_HEADS * NA_DH


def _cparams(sem):
    return pltpu.CompilerParams(dimension_semantics=sem, vmem_limit_bytes=VMEM_LIMIT)


def _bdot(a, b):
    return jnp.dot(a.astype(BF16), b.astype(BF16), preferred_element_type=F32)


def _bdot_nt(a, b):
    return lax.dot_general(a.astype(BF16), b.astype(BF16), (((1,), (1,)), ((), ())),
                           preferred_element_type=F32)


def _bdot_tn(a, b):
    return lax.dot_general(a.astype(BF16), b.astype(BF16), (((0,), (0,)), ((), ())),
                           preferred_element_type=F32)


def _split(a):
    hi = a.astype(BF16)
    lo = (a - hi.astype(F32)).astype(BF16)
    return hi, lo


def _dot3(a, b):
    ah, al = _split(a)
    bh, bl = _split(b)
    d = functools.partial(jnp.dot, preferred_element_type=F32)
    return d(ah, bh) + d(al, bh) + d(ah, bl)


def _sigmoid(x):
    return 0.5 * jnp.tanh(0.5 * x) + 0.5


def _silu(x):
    return x * _sigmoid(x)


def _rms(x, g):
    return x * lax.rsqrt(jnp.mean(x * x, axis=-1, keepdims=True) + EPS) * g


def _head_norm(o, g, width):
    n = o.shape[-1]
    hi = lax.broadcasted_iota(jnp.int32, (n, n), 0) // width
    hj = lax.broadcasted_iota(jnp.int32, (n, n), 1) // width
    avg = jnp.where(hi == hj, 1.0 / width, 0.0).astype(BF16)

    def head_mean(a):
        ah, al = _split(a)
        return (jnp.dot(ah, avg, preferred_element_type=F32) + jnp.dot(al, avg, preferred_element_type=F32))

    xc = o - head_mean(o)
    return xc * lax.rsqrt(head_mean(xc * xc) + EPS) * g


def _mod_row(i):
    ctx_tiles = T_CTX // TOKEN_TILE
    return jnp.where(i < ctx_tiles, 0, 1 + (i - ctx_tiles) // (DEC_SEQ // TOKEN_TILE))


def _mod_spec(l):
    return pl.BlockSpec((1, 1, 6, D_MODEL), lambda i: (l, _mod_row(i), 0, 0))


def _layer_spec(l, *shape, single_buffer=False):
    mode = pl.Buffered(1) if single_buffer else None
    return pl.BlockSpec((1,) + shape, lambda *_: (l,) + (0,) * len(shape), pipeline_mode=mode)


_ANY = pl.BlockSpec(memory_space=pl.ANY)


ADA_TILE = 1536


def _ada_kernel(c_ref, w_ref, b_ref, o_ref):
    a = _silu(c_ref[...])
    o_ref[0] = _bdot(a, w_ref[0]) + b_ref[0]


def _ada(cc, w_ada, b_ada):
    n = 6 * D_MODEL
    return pl.pallas_call(
        _ada_kernel,
        grid=(DEPTH, n // ADA_TILE),
        in_specs=[pl.BlockSpec((N_MOD_ROWS, D_MODEL), lambda l, j: (0, 0)),
                  pl.BlockSpec((1, D_MODEL, ADA_TILE), lambda l, j: (l, 0, j)),
                  pl.BlockSpec((1, 1, ADA_TILE), lambda l, j: (l, 0, j))],
        out_specs=pl.BlockSpec((1, N_MOD_ROWS, ADA_TILE), lambda l, j: (l, 0, j)),
        out_shape=jax.ShapeDtypeStruct((DEPTH, N_MOD_ROWS, n), F32),
        compiler_params=_cparams(("parallel", "parallel")),
        name="ada_mod",
    )(cc, w_ada, b_ada.reshape(DEPTH, 1, n))


def _piece_transpose(blocks, piece):
    x = list(blocks)
    n = len(x)
    d = n // 2
    while d >= 1:
        low = (piece & d) == 0
        for v in range(n):
            if v & d:
                continue
            a, b = x[v], x[v + d]
            x[v] = jnp.where(low, a, pltpu.roll(b, d * S5_GROUP_CH, 1))
            x[v + d] = jnp.where(low, pltpu.roll(a, LANES - d * S5_GROUP_CH, 1), b)
        d //= 2
    return x


def _inproj_kernel(*refs, first):
    if first:
        (xa_ref, xb_ref, mod_ref, g_ref, w_ref, x_out_ref,
         ret_ref, s5_ref, gqk_ref, gv_ref, gg_ref, glr_ref, nq_ref, nk_ref, nv_ref, su_s) = refs
        x = jnp.where(pl.program_id(0) < T_CTX // TOKEN_TILE, xa_ref[...], xb_ref[...])
        x_out_ref[...] = x
    else:
        (x_ref, mod_ref, g_ref, w_ref,
         ret_ref, s5_ref, gqk_ref, gv_ref, gg_ref, glr_ref, nq_ref, nk_ref, nv_ref, su_s) = refs
        x = x_ref[...]
    mod = mod_ref[0, 0]
    h = _rms(x, g_ref[0, 0:1]) * (1.0 + mod[1:2]) + mod[0:1]
    hb = h.astype(BF16)

    def proj(lo, hi):
        return _bdot_nt(hb, w_ref[0, lo:hi, :])

    ret_ref[...] = proj(IN_RET, IN_S5)
    su = proj(IN_S5, IN_GLA_QK)
    rows = TOKEN_TILE // S5_CHUNK
    ng = S5_LANE_GROUPS
    piece = lax.broadcasted_iota(jnp.int32, (rows, LANES), 1) // S5_GROUP_CH
    for lb in range(S5_GROUPS // ng):
        su_s[lb] = su[:, lb * LANES:(lb + 1) * LANES]
        for m in range(S5_CHUNK // ng):
            out = _piece_transpose(
                [su_s[lb, pl.ds(m * ng + jl, rows, stride=S5_CHUNK), :] for jl in range(ng)], piece)
            for g in range(ng):
                col = (lb * ng + g) * S5_W + m * LANES
                s5_ref[:, col:col + LANES] = out[g]
    gqk_ref[...] = proj(IN_GLA_QK, IN_GLA_V)
    gv_ref[...] = proj(IN_GLA_V, IN_GLA_G)
    gg_ref[...] = proj(IN_GLA_G, IN_TAIL_COL)
    glr_ref[...] = proj(IN_TAIL_COL, IN_TAIL_COL + LANES)
    lo = IN_NA
    for ref in (nq_ref, nk_ref, nv_ref):
        r = proj(lo, lo + NA_HEADS * NA_DH)
        for hh in range(NA_HEADS):
            ref[hh] = r[:, hh * NA_DH:(hh + 1) * NA_DH]
        lo += NA_HEADS * NA_DH


def _inproj(l, xs, mod, g_norm, w_in):
    tm = TOKEN_TILE
    first = isinstance(xs, tuple)
    tok = lambda w: pl.BlockSpec((tm, w), lambda i: (i, 0))
    head = pl.BlockSpec((NA_HEADS, tm, NA_DH), lambda i: (0, i, 0))
    tshape = lambda w: jax.ShapeDtypeStruct((T_ALL, w), F32)
    hshape = jax.ShapeDtypeStruct((NA_HEADS, T_ALL, NA_DH), F32)
    ctx_tiles = T_CTX // tm
    if first:
        x_specs = [pl.BlockSpec((tm, D_MODEL), lambda i: (jnp.minimum(i, ctx_tiles - 1), 0)),
                   pl.BlockSpec((tm, D_MODEL), lambda i: (jnp.maximum(i - ctx_tiles, 0), 0))]
        x_args = list(xs)
    else:
        x_specs, x_args = [tok(D_MODEL)], [xs]
    return pl.pallas_call(
        functools.partial(_inproj_kernel, first=first),
        grid=(T_ALL // tm,),
        in_specs=x_specs + [_mod_spec(l), _layer_spec(l, 4, D_MODEL),
                            _layer_spec(l, D_IN, D_MODEL, single_buffer=True)],
        out_specs=([tok(D_MODEL)] if first else [])
        + [tok(1024), pl.BlockSpec((tm // S5_CHUNK, S5_GROUPS * S5_W), lambda i: (i, 0)),
           tok(256), tok(256), tok(256), tok(LANES), head, head, head],
        out_shape=([tshape(D_MODEL)] if first else [])
        + [tshape(1024), jax.ShapeDtypeStruct((S5_ROWS, S5_GROUPS * S5_W), F32),
           tshape(256), tshape(256), tshape(256), tshape(LANES), hshape, hshape, hshape],
        scratch_shapes=[pltpu.VMEM((S5_GROUPS // S5_LANE_GROUPS, tm, LANES), F32)],
        compiler_params=_cparams(("parallel",)),
        name="in_proj",
    )(*x_args, mod, g_norm, w_in)


def _rope_rotate(x, lane):
    first = (lane % 32) < 16
    w = x.shape[-1]
    return jnp.where(first, pltpu.roll(x, w - 16, 1), pltpu.roll(x, 16, 1))


def _ret_kernel(ld_ref, ret_ref, gn_ref, *rest, layer, seq, latent):
    if latent:
        cos_ref, sin_ref, s0_ref, _, out_ref, dec_s = rest
    else:
        out_ref, st_ref, dec_s = rest
    tq = RET_QBLOCK
    nq = seq // tq
    width = dec_s.shape[-1]

    @pl.when(pl.program_id(0) == 0)
    def _build_decay():
        rel = (lax.broadcasted_iota(jnp.int32, (tq, width), 0) + (nq - 1) * tq
               - lax.broadcasted_iota(jnp.int32, (tq, width), 1)).astype(F32)
        for h in range(RET_HEADS):
            dec_s[h] = (jnp.where(rel >= 0, jnp.exp(ld_ref[layer, 0, h] * jnp.maximum(rel, 0.0)), 0.0)
                        + jnp.where(rel <= 0, jnp.exp(ld_ref[layer, 1, h] * jnp.maximum(-rel, 0.0)), 0.0))

    nrows = ret_ref.shape[0]
    nsub = nrows // seq
    q = ret_ref[:, 0:256]
    k = ret_ref[:, 256:512]
    if latent:
        lane = lax.broadcasted_iota(jnp.int32, (nrows, 256), 1)
        cos = cos_ref[...]
        sin = sin_ref[...]
        q = q * cos + _rope_rotate(q, lane) * sin
        k = k * cos + _rope_rotate(k, lane) * sin
    k = k * (RET_DK ** -0.5)
    pos_c = lax.broadcasted_iota(jnp.int32, (seq, 1), 0).astype(F32)
    tiles = [(s, qb) for s in range(nsub) for qb in range(nq)]
    for h in range(RET_HEADS):
        lgf = ld_ref[layer, 0, h]
        lgb = ld_ref[layer, 1, h]
        sl = slice(h * RET_DK, (h + 1) * RET_DK)
        qh = q[:, sl]
        kh = k[:, sl]
        kb = kh.astype(BF16)
        vb = ret_ref[:, 512 + h * RET_DV:512 + (h + 1) * RET_DV].astype(BF16)
        if latent:
            q_init = jnp.concatenate([qh * jnp.exp(lgf * (pos_c + 1.0)),
                                      qh * jnp.exp(lgb * (seq - pos_c))], axis=1)
            s_init = jnp.concatenate([s0_ref[0, 0, 0, h], s0_ref[0, 0, 1, h]], axis=0)

        def score(s, qb):
            w0 = (nq - 1 - qb) * tq
            rows = slice(s * seq + qb * tq, s * seq + (qb + 1) * tq)
            keys = slice(s * seq, (s + 1) * seq)
            return (_bdot_nt(qh[rows], kb[keys]) * dec_s[h, :, w0:w0 + seq]).astype(BF16)

        def values(sc, s, qb):
            o = jnp.dot(sc, vb[s * seq:(s + 1) * seq], preferred_element_type=F32)
            if latent:
                o = o + _bdot(q_init[qb * tq:(qb + 1) * tq], s_init)
            return o

        def finish(o, s, qb):
            out_ref[s * seq + qb * tq:s * seq + (qb + 1) * tq, sl] = o

        if nsub > 1:
            scores = [score(s, qb) for s, qb in tiles]
            outs = [values(sc, s, qb) for sc, (s, qb) in zip(scores, tiles)]
            for o, (s, qb) in zip(outs, tiles):
                finish(o, s, qb)
        else:
            for s, qb in tiles:
                finish(values(score(s, qb), s, qb), s, qb)
        if not latent:
            for s in range(nsub):
                keys = slice(s * seq, (s + 1) * seq)
                st_ref[s, 0, h] = _bdot_tn(kh[keys] * jnp.exp(lgf * (seq - 1.0 - pos_c)), vb[keys])
                st_ref[s, 1, h] = _bdot_tn(kh[keys] * jnp.exp(lgb * pos_c), vb[keys])
    for r0 in range(0, nrows, tq):
        rows = slice(r0, r0 + tq)
        out_ref[rows, :] = _head_norm(out_ref[rows, :], gn_ref[0], RET_DV) * _silu(ret_ref[rows, 768:1024])


def _retention(l, ret, ld, gn, *, latent, cos=None, sin=None, s0=None, prev=None):
    seq = DEC_SEQ if latent else SEQ
    nsub = 1 if latent else CTX_SEQS_PER_STEP
    nb = DEC_BATCH if latent else BATCH // nsub
    off = LAT_BLOCK0 if latent else 0
    rows = nsub * seq
    in_specs = [pl.BlockSpec(memory_space=pltpu.SMEM),
                pl.BlockSpec((rows, 1024), lambda b: (b + off, 0)),
                _layer_spec(l, 1, 256)]
    args = [ld, ret, gn]
    out_specs = [pl.BlockSpec((rows, 256), lambda b: (b + off, 0))]
    out_shape = [jax.ShapeDtypeStruct((T_ALL, 256), F32)]
    aliases = {}
    if latent:
        in_specs += [pl.BlockSpec((seq, 256), lambda b: (0, 0)),
                     pl.BlockSpec((seq, 256), lambda b: (0, 0)),
                     pl.BlockSpec((1, 1, 2, RET_HEADS, RET_DK, RET_DV), lambda b: (b, l, 0, 0, 0, 0)),
                     _ANY]
        args += [cos, sin, s0, prev]
        aliases = {6: 0}
    else:
        out_specs.append(pl.BlockSpec((nsub, 2, RET_HEADS, RET_DK, RET_DV), lambda b: (b, 0, 0, 0, 0)))
        out_shape.append(jax.ShapeDtypeStruct((BATCH, 2, RET_HEADS, RET_DK, RET_DV), F32))
    return pl.pallas_call(
        functools.partial(_ret_kernel, layer=l, seq=seq, latent=latent),
        grid=(nb,),
        in_specs=in_specs, out_specs=out_specs, out_shape=out_shape,
        input_output_aliases=aliases,
        scratch_shapes=[pltpu.VMEM((RET_HEADS, RET_QBLOCK, 2 * seq - RET_QBLOCK), F32)],
        compiler_params=_cparams(("arbitrary",)),
        name="retention_lat" if latent else "retention_ctx",
    )(*args)


def _gla_kernel(gqk_ref, gv_ref, gg_ref, glr_ref, wg_ref, bg_ref, gn_ref, *rest, seq, latent):
    if latent:
        s0_ref, _, out_ref, gate_s, o_s, st_s, qst_s, ds_s, e_s = rest
    else:
        out_ref, st_ref, gate_s, o_s, st_s, qst_s, ds_s, e_s = rest
    c = GLA_CHUNK
    n = seq // c
    nsub = gqk_ref.shape[0] // seq
    hk = GLA_HEADS * GLA_DK
    lr = glr_ref[...]
    for d in range(2):
        pre = _bdot(lr[:, d * GLA_RANK:(d + 1) * GLA_RANK], wg_ref[0, d]) + bg_ref[0, d:d + 1]
        gate_s[d] = (jnp.minimum(pre, 0.0) - jnp.log(1.0 + jnp.exp(-jnp.abs(pre)))) / GLA_TAU
        for s in range(nsub):
            st_s[2 * s + d] = s0_ref[0, 0, d] if latent else jnp.zeros((GLA_HEADS * GLA_DV, hk), F32)

    nc = GLA_BLOCK_CHUNKS
    rb = nc * c
    ti = lax.broadcasted_iota(jnp.int32, (rb, rb), 0)
    tj = lax.broadcasted_iota(jnp.int32, (rb, rb), 1)
    same = (ti // c) == (tj // c)
    ones = lambda m: (same & m).astype(BF16)
    tri = [ones(tj <= ti), ones(tj >= ti)]
    mid = [ones((tj % c) < c // 2), ones((tj % c) >= c // 2)]
    tot = ones(tj == tj)
    lane_k = lax.broadcasted_iota(jnp.int32, (c, hk), 1)
    head_mask = [(lane_k // GLA_DK) == h for h in range(GLA_HEADS)]
    ai = lax.broadcasted_iota(jnp.int32, (GLA_HEADS * c, c), 0) % c
    aj = lax.broadcasted_iota(jnp.int32, (GLA_HEADS * c, c), 1)
    keep = [aj <= ai, aj >= ai]
    sr = lax.broadcasted_iota(jnp.int32, (GLA_HEADS * GLA_DV, hk), 0) // GLA_DV
    sc = lax.broadcasted_iota(jnp.int32, (GLA_HEADS * GLA_DV, hk), 1) // GLA_DK
    diag = sr == sc
    scale = GLA_DK ** -0.5
    d32 = functools.partial(jnp.dot, preferred_element_type=F32)

    def rows_of(i, size):
        return pl.ds(i * size, size) if isinstance(i, int) else pl.ds(pl.multiple_of(i * size, size), size)

    def local(block_ids):
        vs, q_att, k_att, k_st = {}, {}, {}, {}
        for bi in block_ids:
            rows = rows_of(bi, rb)
            q = gqk_ref[rows, 0:hk] * scale
            k = gqk_ref[rows, hk:2 * hk]
            vs[bi] = gv_ref[rows, :].astype(BF16)
            for d in range(2):
                parts = jnp.concatenate(_split(gate_s[d, rows, :]), axis=1)
                sums = lambda m: (lambda r: r[:, 0:hk] + r[:, hk:2 * hk])(d32(m, parts))
                b, b_mid, b_end = sums(tri[d]), sums(mid[d]), sums(tot)
                q_att[bi, d] = q * jnp.exp(b - b_mid)
                k_att[bi, d] = k * jnp.exp(b_mid - b)
                k_st[bi, d] = (k * jnp.exp(b_end - b)).astype(BF16)
                qst_s[d, rows, :] = (q * jnp.exp(b)).astype(BF16)
                decay = jnp.exp(b_end)
                for cc in range(nc):
                    e_s[d, bi * nc + cc] = decay[cc * c:cc * c + 8]
        tiles = [(bi, d, cc) for bi in block_ids for d in range(2) for cc in range(nc)]
        att = {}
        for bi, d, cc in tiles:
            r = slice(cc * c, (cc + 1) * c)
            qa = q_att[bi, d][r]
            q_stack = jnp.concatenate([jnp.where(head_mask[h], qa, 0.0) for h in range(GLA_HEADS)], axis=0)
            att[bi, d, cc] = jnp.where(keep[d], _bdot_nt(q_stack, k_att[bi, d][r]), 0.0).astype(BF16)
        for bi, d, cc in tiles:
            r = slice(cc * c, (cc + 1) * c)
            o = jnp.concatenate([d32(att[bi, d, cc][h * c:(h + 1) * c], vs[bi][r, h * GLA_DV:(h + 1) * GLA_DV])
                                 for h in range(GLA_HEADS)], axis=1)
            o_s[d, rows_of(bi * nc + cc, c), :] = o
        for bi, d, cc in tiles:
            r = slice(cc * c, (cc + 1) * c)
            ds_s[d, bi * nc + cc] = jnp.where(diag, lax.dot_general(
                vs[bi][r], k_st[bi, d][r], (((0,), (0,)), ((), ())), preferred_element_type=F32), 0.0)

    def recur(s, ci, d):
        g = s * n + ci
        rows = rows_of(g, c)
        st = st_s[2 * s + d]
        o_s[d, rows, :] += _bdot_nt(qst_s[d, rows, :], st)
        st_s[2 * s + d] = st * e_s[d, g, 0:1] + ds_s[d, g]

    def recur_body(i, carry):
        for s in range(nsub):
            recur(s, i, 0)
            recur(s, n - 1 - i, 1)
        return carry

    local(list(range(nsub * n // nc)))
    if n <= 4:
        for i in range(n):
            recur_body(i, 0)
    else:
        lax.fori_loop(0, n, recur_body, 0, unroll=2)

    for r0 in range(0, nsub * seq, rb):
        rows = slice(r0, r0 + rb)
        out_ref[rows, :] = (_head_norm(o_s[0, rows, :] + o_s[1, rows, :], gn_ref[0], GLA_DV)
                            * _silu(gg_ref[rows, :]))
    if not latent:
        hv = GLA_HEADS * GLA_DV
        eye = (lax.broadcasted_iota(jnp.int32, (hv, hv), 0)
               == lax.broadcasted_iota(jnp.int32, (hv, hv), 1)).astype(BF16)
        tn = lambda a: lax.dot_general(a, eye, (((0,), (0,)), ((), ())), preferred_element_type=F32)
        for s in range(nsub):
            for d in range(2):
                st = st_s[2 * s + d]
                hi, lo = _split(st)
                lo2 = (st - hi.astype(F32) - lo.astype(F32)).astype(BF16)
                s_all = tn(hi) + tn(lo) + tn(lo2)
                for h in range(GLA_HEADS):
                    st_ref[s, d, h] = s_all[h * GLA_DK:(h + 1) * GLA_DK, h * GLA_DV:(h + 1) * GLA_DV]


def _gla(l, gqk, gv, gg, glr, wg, bg, gn, *, latent, s0=None, prev=None):
    seq = DEC_SEQ if latent else SEQ
    nsub = 1 if latent else CTX_SEQS_PER_STEP
    nb = DEC_BATCH if latent else BATCH // nsub
    off = LAT_BLOCK0 if latent else 0
    rows = nsub * seq
    hk = GLA_HEADS * GLA_DK
    hv = GLA_HEADS * GLA_DV
    tok = lambda w: pl.BlockSpec((rows, w), lambda b: (b + off, 0))
    in_specs = [tok(256), tok(256), tok(256), tok(LANES),
                _layer_spec(l, 2, GLA_RANK, hk), _layer_spec(l, 2, hk), _layer_spec(l, 1, 256)]
    args = [gqk, gv, gg, glr, wg, bg, gn]
    out_specs = [tok(256)]
    out_shape = [jax.ShapeDtypeStruct((T_ALL, 256), F32)]
    aliases = {}
    if latent:
        in_specs += [pl.BlockSpec((1, 1, 2, hv, hk), lambda b: (b, l, 0, 0, 0)), _ANY]
        args += [s0, prev]
        aliases = {8: 0}
    else:
        out_specs.append(pl.BlockSpec((nsub, 2, GLA_HEADS, GLA_DK, GLA_DV), lambda b: (b, 0, 0, 0, 0)))
        out_shape.append(jax.ShapeDtypeStruct((BATCH, 2, GLA_HEADS, GLA_DK, GLA_DV), F32))
    return pl.pallas_call(
        functools.partial(_gla_kernel, seq=seq, latent=latent),
        grid=(nb,),
        in_specs=in_specs, out_specs=out_specs, out_shape=out_shape,
        input_output_aliases=aliases,
        scratch_shapes=[pltpu.VMEM((2, rows, hk), F32), pltpu.VMEM((2, rows, hv), F32),
                        pltpu.VMEM((2 * nsub, hv, hk), F32), pltpu.VMEM((2, rows, hk), BF16),
                        pltpu.VMEM((2, rows // GLA_CHUNK, hv, hk), F32),
                        pltpu.VMEM((2, rows // GLA_CHUNK, 8, hk), F32)],
        compiler_params=_cparams(("parallel",)),
        name="gla_lat" if latent else "gla_ctx",
    )(*args)


S5_W = S5_CHUNK * S5_GROUP_CH
S5_P2 = 2 * S5_STATE
S5_ROWS = T_ALL // S5_CHUNK
S5_ROWS_CTX = T_CTX // S5_CHUNK
S5_LANE_GROUPS = LANES // S5_GROUP_CH


S5_TE_ROWS = S5_W + 4 * S5_P2


def _s5_toeplitz_kernel(cc_ref, pwr_ref, pwi_ref, bbr_ref, bbi_ref, ctr_ref, cti_ref, te_ref, ffb_ref):
    lane = lax.broadcasted_iota(jnp.int32, (S5_GROUP_CH, S5_W), 1)
    swapped = lambda t: jnp.concatenate([t[S5_STATE:], t[:S5_STATE]], axis=0)
    for t in range(S5_TABLE_GROUPS):
        def times_b(v, d):
            pr, pi, br, bi = pwr_ref[0, t, v], pwi_ref[0, t, v], bbr_ref[0, d, t], bbi_ref[0, d, t]
            return jnp.concatenate([pr * br - pi * bi, pr * bi + pi * br], axis=0)

        def times_c(v, d):
            pr, pi, cr, ci = pwr_ref[0, t, v], pwi_ref[0, t, v], ctr_ref[0, d, t], cti_ref[0, d, t]
            return jnp.concatenate([cr * pr - ci * pi, -(cr * pi + ci * pr)], axis=0)

        wfr = times_b(0, 0)
        wb = times_b(1, 1)
        kf = _dot3(cc_ref[0, 0, t], wfr)
        kb = _dot3(cc_ref[0, 1, t], wb)
        blocks = []
        for i in range(S5_CHUNK):
            sf = ((i + 1 - S5_CHUNK) * S5_GROUP_CH) % S5_W
            fwd = jnp.where(lane < (i + 1) * S5_GROUP_CH, pltpu.roll(kf, sf, 1) if sf else kf, 0.0)
            bwd = jnp.where(lane >= i * S5_GROUP_CH, pltpu.roll(kb, i * S5_GROUP_CH, 1) if i else kb, 0.0)
            blocks.append(fwd + bwd)
        te_ref[0, t] = jnp.concatenate(blocks + [wfr, wb, swapped(wfr), swapped(wb)], axis=0).astype(BF16)
        ffb_ref[0, t] = jnp.concatenate([times_c(2, 0), times_c(3, 1)], axis=0).astype(BF16)


S5_TABLE_GROUPS = 4


def _s5_toeplitz(cc, pwr, pwi, bbr, bbi, ctr, cti):
    tg = S5_TABLE_GROUPS
    nl, _, ngroups = cc.shape[:3]
    by_dir = lambda *tail: pl.BlockSpec((1, 2, tg) + tail, lambda l, i: (l, 0, i) + (0,) * len(tail))
    by_group = lambda *tail: pl.BlockSpec((1, tg) + tail, lambda l, i: (l, i) + (0,) * len(tail))
    lanes = by_dir(S5_STATE, S5_W)
    powers = by_group(4, S5_STATE, S5_W)
    return pl.pallas_call(
        _s5_toeplitz_kernel,
        grid=(nl, ngroups // tg),
        in_specs=[by_dir(S5_GROUP_CH, S5_P2), powers, powers, lanes, lanes, lanes, lanes],
        out_specs=[by_group(S5_TE_ROWS, S5_W), by_group(2 * S5_P2, S5_W)],
        out_shape=[jax.ShapeDtypeStruct((nl, ngroups, S5_TE_ROWS, S5_W), BF16),
                   jax.ShapeDtypeStruct((nl, ngroups, 2 * S5_P2, S5_W), BF16)],
        compiler_params=_cparams(("parallel", "parallel")),
        name="s5_toeplitz",
    )(cc, pwr, pwi, bbr, bbi, ctr, cti)


def _s5_kernel(u_ref, te_ref, ff_ref, d_ref, lam_ref, x0_ref, y_ref, fin_ref, xs_s, ps_s):
    C = S5_CHUNK
    ng = S5_LANE_GROUPS

    for g in range(ng):
        u = u_ref[:, g * S5_W:(g + 1) * S5_W]
        r = _bdot_nt(u, te_ref[0, g])
        y_ref[:, g * S5_W:(g + 1) * S5_W] = r[:, 0:S5_W] + d_ref[0, :, g * S5_W:(g + 1) * S5_W] * u
        for t in range(4):
            xs_s[t * ng + g] = r[:, S5_W + t * S5_P2:S5_W + (t + 1) * S5_P2]

    w = ng * S5_P2

    def carry(base, nseq, nchunks, init_f, init_b):
        a_f, b_f = lam_ref[0, 0, 0:1], lam_ref[0, 0, 1:2]
        a_b, b_b = lam_ref[0, 1, 0:1], lam_ref[0, 1, 1:2]
        load = lambda t, rows: jnp.concatenate([xs_s[t * ng + g, rows, :] for g in range(ng)], axis=1)

        def body(i, st):
            sf, tf, sb, tb = st
            rf = pl.ds(base + i, nseq, stride=nchunks)
            rb = pl.ds(base + (nchunks - 1 - i), nseq, stride=nchunks)
            for g in range(ng):
                ps_s[g, rf, :] = sf[:, g * S5_P2:(g + 1) * S5_P2]
                ps_s[ng + g, rb, :] = sb[:, g * S5_P2:(g + 1) * S5_P2]
            return (a_f * sf + b_f * tf + load(0, rf), a_f * tf - b_f * sf + load(2, rf),
                    a_b * sb + b_b * tb + load(1, rb), a_b * tb - b_b * sb + load(3, rb))

        first = (lax.broadcasted_iota(jnp.int32, (1, w), 1) % S5_P2) < S5_STATE
        swap = lambda s: jnp.where(first, pltpu.roll(s, w - S5_STATE, 1), pltpu.roll(s, S5_STATE, 1))
        return lax.fori_loop(0, nchunks, body, (init_f, swap(init_f), init_b, swap(init_b)))

    zeros = jnp.zeros((BATCH, w), F32)
    fin = carry(0, BATCH, SEQ // C, zeros, zeros)
    fin_ref[0] = fin[0]
    fin_ref[1] = fin[2]
    carry(S5_ROWS_CTX, DEC_BATCH, DEC_SEQ // C, x0_ref[0, 0], x0_ref[0, 1])

    for g in range(ng):
        p = jnp.concatenate([ps_s[g], ps_s[ng + g]], axis=1)
        y_ref[:, g * S5_W:(g + 1) * S5_W] += _bdot(p, ff_ref[0, g])


def _s5(l, su_rows, te, ff, d_rows, lam, x0):
    ng = S5_LANE_GROUPS
    w = ng * S5_P2
    rows = pl.BlockSpec((S5_ROWS, ng * S5_W), lambda i: (0, i))
    return pl.pallas_call(
        _s5_kernel,
        grid=(S5_GROUPS // ng,),
        in_specs=[rows,
                  pl.BlockSpec((1, ng, S5_TE_ROWS, S5_W), lambda i: (l, i, 0, 0)),
                  pl.BlockSpec((1, ng, 2 * S5_P2, S5_W), lambda i: (l, i, 0, 0)),
                  pl.BlockSpec((1, 1, ng * S5_W), lambda i: (l, 0, i)),
                  pl.BlockSpec((1, 2, 2, w), lambda i: (l, 0, 0, i)),
                  pl.BlockSpec((1, 2, DEC_BATCH, w), lambda i: (l, 0, 0, i))],
        out_specs=[rows, pl.BlockSpec((2, BATCH, w), lambda i: (0, 0, i))],
        out_shape=[jax.ShapeDtypeStruct((S5_ROWS, S5_GROUPS * S5_W), F32),
                   jax.ShapeDtypeStruct((2, BATCH, S5_GROUPS * S5_P2), F32)],
        scratch_shapes=[pltpu.VMEM((4 * ng, S5_ROWS, S5_P2), F32), pltpu.VMEM((2 * ng, S5_ROWS, S5_P2), F32)],
        compiler_params=_cparams(("parallel",)),
        name="s5_scan",
    )(su_rows, te, ff, d_rows, lam, x0)


def _s5_tables(lam_re, lam_im, log_dt, b_re, b_im, c_re, c_im):
    C, G, P, H = S5_CHUNK, S5_GROUPS, S5_STATE, S5_GROUP_CH
    L = lam_re.shape[0]
    dt = jnp.exp(log_dt)[..., None]
    ar, ai = lam_re * dt, lam_im * dt

    steps = jnp.arange(C + 1, dtype=F32)
    mag = jnp.exp(ar[..., None] * steps)
    pw_re, pw_im = mag * jnp.cos(ai[..., None] * steps), mag * jnp.sin(ai[..., None] * steps)
    exact = functools.partial(jnp.einsum, precision=lax.Precision.HIGHEST)
    tau_np = np.arange(C * H) // H

    patterns = [(0, (C - 1) - tau_np), (1, tau_np), (0, tau_np + 1), (1, C - tau_np)]
    sel = np.zeros((2, len(patterns), C + 1, C * H), np.float32)
    for v, (d, t) in enumerate(patterns):
        sel[d, v, t, np.arange(C * H)] = 1.0
    sel = jnp.asarray(sel)
    pwr = exact('ldgpt,dvtn->lgvpn', pw_re, sel)
    pwi = exact('ldgpt,dvtn->lgvpn', pw_im, sel)

    lr, li = pw_re[..., 1], pw_im[..., 1]
    den = lam_re * lam_re + lam_im * lam_im
    qr = ((lr - 1.0) * lam_re + li * lam_im) / den
    qi = (li * lam_re - (lr - 1.0) * lam_im) / den
    bbr = qr[..., None] * b_re - qi[..., None] * b_im
    bbi = qr[..., None] * b_im + qi[..., None] * b_re
    chan = jnp.asarray((np.arange(H)[:, None] == (np.arange(C * H) % H)[None, :]).astype(np.float32))
    lanes = lambda a: exact('ldgph,hn->ldgpn', a, chan)
    bbr, bbi = lanes(bbr), lanes(bbi)
    c_t = lambda a: exact('ldghp,hn->ldgpn', a, chan)
    ctr, cti = c_t(c_re), c_t(c_im)

    cc = jnp.concatenate([c_re, -c_im], axis=-1)
    tables = _s5_toeplitz(cc, pwr, pwi, bbr, bbi, ctr, cti)
    cr, ci = pw_re[..., C], pw_im[..., C]
    a = jnp.concatenate([cr, cr], axis=-1).reshape(L, 2, 1, G * 2 * P)
    b = jnp.concatenate([-ci, ci], axis=-1).reshape(L, 2, 1, G * 2 * P)
    return tables, jnp.concatenate([a, b], axis=2)


def _softmax_pv(s_parts, v_parts):
    m = s_parts[0].max(axis=-1, keepdims=True)
    for s in s_parts[1:]:
        m = jnp.maximum(m, s.max(axis=-1, keepdims=True))
    o = None
    l = None
    for s, v in zip(s_parts, v_parts):
        p = jnp.exp(s - m)
        pl_ = p.sum(axis=-1, keepdims=True)
        po = _bdot(p, v)
        o = po if o is None else o + po
        l = pl_ if l is None else l + pl_
    return o / l


def _attn_ctx_kernel(q_ref, k_ref, v_ref, o_ref):
    scale = NA_DH ** -0.5
    tiles = [(h, slice(s * SEQ, (s + 1) * SEQ)) for h in range(NA_HEADS) for s in range(CTX_SEQS_PER_STEP)]
    scores = [_bdot_nt(q_ref[h, rows, :], k_ref[h, rows, :]) * scale for h, rows in tiles]
    for s, (h, rows) in zip(scores, tiles):
        o_ref[h, rows, :] = _softmax_pv([s], [v_ref[h, rows, :]])


def _attn_ctx(nq, nk, nv):
    spec = pl.BlockSpec((NA_HEADS, CTX_SEQS_PER_STEP * SEQ, NA_DH), lambda b: (0, b, 0))
    return pl.pallas_call(
        _attn_ctx_kernel,
        grid=(BATCH // CTX_SEQS_PER_STEP,),
        in_specs=[spec, spec, spec],
        out_specs=spec,
        out_shape=jax.ShapeDtypeStruct((NA_HEADS, T_ALL, NA_DH), F32),
        compiler_params=_cparams(("parallel",)),
        name="attn_ctx",
    )(nq, nk, nv)


def _attn_lat_kernel(q_ref, k_ref, v_ref, kc_ref, vc_ref, tb_ref, _, o_ref, bias_s):
    @pl.when(pl.program_id(1) == 0)
    def _build_bias():
        bias_s[...] = jnp.full((DEC_SEQ, DEC_SEQ), -jnp.inf, F32)
        for r in range(GRID_ROWS):
            rs = min(max(r - NA_KH // 2, 0), GRID_ROWS - NA_KH)
            dr0 = rs - r + NA_WIN_H - 1
            bias_s[r * GRID_W:(r + 1) * GRID_W, rs * GRID_W:(rs + NA_KH) * GRID_W] = (
                tb_ref[0, 0, :, dr0 * GRID_W:(dr0 + NA_KH) * GRID_W])

    scale = NA_DH ** -0.5
    kb = k_ref[0].astype(BF16)
    vb = v_ref[0].astype(BF16)
    kc = kc_ref[0, 0, 0].astype(BF16)
    vc = vc_ref[0, 0, 0].astype(BF16)
    tq = NA_QBLOCK
    q_rows = tq // GRID_W
    tiles = []
    for qb in range(DEC_SEQ // tq):
        rows = slice(qb * tq, (qb + 1) * tq)
        starts = [min(max(r - NA_KH // 2, 0), GRID_ROWS - NA_KH) for r in range(qb * q_rows, (qb + 1) * q_rows)]
        keys = slice(min(starts) * GRID_W // LANES * LANES,
                     -(-(max(starts) + NA_KH) * GRID_W // LANES) * LANES)
        tiles.append((rows, keys))
    scores = []
    for rows, keys in tiles:
        qh = q_ref[0, rows, :].astype(BF16)
        scores.append((_bdot_nt(qh, kb[keys]) * scale + bias_s[rows, keys], _bdot_nt(qh, kc) * scale))
    for (s_loc, s_ctx), (rows, keys) in zip(scores, tiles):
        o_ref[0, rows, :] = _softmax_pv([s_loc, s_ctx], [vb[keys], vc])


def _attn_lat(l, nq, nk, nv, kc, vc, tb, prev):
    tok = pl.BlockSpec((1, DEC_SEQ, NA_DH), lambda h, b: (h, b + LAT_BLOCK0, 0))
    cache = pl.BlockSpec((1, 1, 1, PAST_LEN, NA_DH), lambda h, b: (b, l, h, 0, 0))
    return pl.pallas_call(
        _attn_lat_kernel,
        grid=(NA_HEADS, DEC_BATCH),
        in_specs=[tok, tok, tok, cache, cache,
                  pl.BlockSpec((1, 1, GRID_W, NA_REL_ROWS * GRID_W), lambda h, b: (l, h, 0, 0)),
                  _ANY],
        out_specs=tok,
        out_shape=jax.ShapeDtypeStruct((NA_HEADS, T_ALL, NA_DH), F32),
        input_output_aliases={6: 0},
        scratch_shapes=[pltpu.VMEM((DEC_SEQ, DEC_SEQ), F32)],
        compiler_params=_cparams(("arbitrary", "arbitrary")),
        name="attn_lat",
    )(nq, nk, nv, kc, vc, tb, prev)


def _na_tables(rpb):
    col = np.arange(GRID_W)
    col_start = np.clip(col - NA_WIN_W // 2, 0, GRID_W - NA_WIN_W)
    col_in = (col[None, :] >= col_start[:, None]) & (col[None, :] < col_start[:, None] + NA_WIN_W)
    col_idx = np.clip(col[None, :] - col[:, None] + NA_WIN_W - 1, 0, 2 * NA_WIN_W - 2)
    onehot = (col_idx[:, :, None] == np.arange(2 * NA_WIN_W - 1)[None, None, :]).astype(np.float32)
    tb = jnp.einsum('lhrd,qkd->lhqrk', rpb, jnp.asarray(onehot), precision=lax.Precision.HIGHEST)
    tb = jnp.where(jnp.asarray(col_in)[None, None, :, None, :], tb, -jnp.inf)
    return tb.reshape(rpb.shape[0], NA_HEADS, GRID_W, NA_REL_ROWS * GRID_W)


def _merge_kernel(x_ref, mod_ref, g_ref, ret_ref, s5y_ref, gla_ref, na_ref,
                  wglu_ref, bglu_ref, wbr_ref, wmg_ref, bmg_ref, wout_ref, o_ref, y_s):
    x = x_ref[...]
    mod = mod_ref[0, 0]
    hb = (_rms(x, g_ref[0, 0:1]) * (1.0 + mod[1:2]) + mod[0:1]).astype(BF16)

    def gate_pre(n):
        return _bdot(hb, wmg_ref[0, :, n * D_MODEL:(n + 1) * D_MODEL]) + bmg_ref[0, :, n * D_MODEL:(n + 1) * D_MODEL]

    acc = _sigmoid(gate_pre(0)) * _bdot(ret_ref[...], wbr_ref[0, 0])
    acc += _sigmoid(gate_pre(2)) * _bdot(gla_ref[...], wbr_ref[0, 2])
    na = jnp.concatenate([na_ref[hh].astype(BF16) for hh in range(NA_HEADS)], axis=1)
    acc += _sigmoid(gate_pre(3)) * _bdot(na, wbr_ref[0, 3])
    s5_gate = gate_pre(1)

    rows = TOKEN_TILE // S5_CHUNK
    ng = S5_LANE_GROUPS
    piece = lax.broadcasted_iota(jnp.int32, (rows, LANES), 1) // S5_GROUP_CH
    for lb in range(S5_GROUPS // ng):
        for m in range(S5_CHUNK // ng):
            cols = [(lb * ng + g) * S5_W + m * LANES for g in range(ng)]
            out = _piece_transpose([s5y_ref[:, c0:c0 + LANES] for c0 in cols], piece)
            for il in range(ng):
                y_s[lb, pl.ds(m * ng + il, rows, stride=S5_CHUNK), :] = out[il]
    y = jnp.concatenate([y_s[lb] for lb in range(S5_GROUPS // ng)], axis=1)
    y = 0.5 * y * (1.0 + jnp.tanh(math.sqrt(2.0 / math.pi) * (y + 0.044715 * (y * y * y))))
    z = _bdot(y, wglu_ref[0]) + bglu_ref[0]
    s5_out = z[:, 0:BRANCH_W] * _sigmoid(z[:, BRANCH_W:2 * BRANCH_W])
    acc += _sigmoid(s5_gate) * _bdot(s5_out, wbr_ref[0, 1])
    m = _bdot(acc, wout_ref[0])
    o_ref[...] = x + mod[2:3] * _rms(m, g_ref[0, 1:2])


def _merge(l, x, mod, g_norm, ret_o, s5_y, gla_o, na_o, wglu, bglu, wbr, wmg, bmg, wout):
    tm = TOKEN_TILE
    tok = lambda w: pl.BlockSpec((tm, w), lambda i: (i, 0))
    return pl.pallas_call(
        _merge_kernel,
        grid=(T_ALL // tm,),
        in_specs=[tok(D_MODEL), _mod_spec(l), _layer_spec(l, 4, D_MODEL),
                  tok(256), pl.BlockSpec((tm // S5_CHUNK, S5_GROUPS * S5_W), lambda i: (i, 0)), tok(256),
                  pl.BlockSpec((NA_HEADS, tm, NA_DH), lambda i: (0, i, 0)),
                  _layer_spec(l, 256, 512), _layer_spec(l, 1, 512),
                  _layer_spec(l, N_BRANCH, BRANCH_W, D_MODEL, single_buffer=True),
                  _layer_spec(l, D_MODEL, N_BRANCH * D_MODEL, single_buffer=True),
                  _layer_spec(l, 1, N_BRANCH * D_MODEL),
                  _layer_spec(l, D_MODEL, D_MODEL, single_buffer=True)],
        out_specs=tok(D_MODEL),
        out_shape=jax.ShapeDtypeStruct((T_ALL, D_MODEL), F32),
        scratch_shapes=[pltpu.VMEM((S5_GROUPS // S5_LANE_GROUPS, tm, LANES), F32)],
        compiler_params=_cparams(("parallel",)),
        name="merge",
    )(x, mod, g_norm, ret_o, s5_y, gla_o, na_o, wglu, bglu, wbr, wmg, bmg, wout)


FF_TILE = 1024


def _mlp_kernel(x_ref, mod_ref, g_ref, w1_ref, w2_ref, *o_refs):
    x = x_ref[...]
    mod = mod_ref[0, 0]
    hb = (_rms(x, g_ref[0, 2:3]) * (1.0 + mod[4:5]) + mod[3:4]).astype(BF16)
    f = None
    for j in range(D_FF // FF_TILE):
        a = jnp.maximum(_bdot(hb, w1_ref[0, :, j * FF_TILE:(j + 1) * FF_TILE]), 0.0)
        part = _bdot(a * a, w2_ref[0, j * FF_TILE:(j + 1) * FF_TILE, :])
        f = part if f is None else f + part
    y = x + mod[5:6] * _rms(f, g_ref[0, 3:4])
    if len(o_refs) == 1:
        o_refs[0][...] = y
    else:
        ctx_tiles = T_CTX // TOKEN_TILE

        @pl.when(pl.program_id(0) < ctx_tiles)
        def _store_ctx():
            o_refs[0][...] = y

        @pl.when(pl.program_id(0) >= ctx_tiles)
        def _store_lat():
            o_refs[1][...] = y


def _mlp(l, x, mod, g_norm, w1, w2, *, split_out):
    tm = TOKEN_TILE
    tok = pl.BlockSpec((tm, D_MODEL), lambda i: (i, 0))
    if split_out:
        ctx_tiles = T_CTX // tm
        out_specs = [pl.BlockSpec((tm, D_MODEL), lambda i: (jnp.minimum(i, ctx_tiles - 1), 0)),
                     pl.BlockSpec((tm, D_MODEL), lambda i: (jnp.maximum(i - ctx_tiles, 0), 0))]
        out_shape = [jax.ShapeDtypeStruct((T_CTX, D_MODEL), F32), jax.ShapeDtypeStruct((T_LAT, D_MODEL), F32)]
    else:
        out_specs = tok
        out_shape = jax.ShapeDtypeStruct((T_ALL, D_MODEL), F32)
    return pl.pallas_call(
        _mlp_kernel,
        grid=(T_ALL // tm,),
        in_specs=[tok, _mod_spec(l), _layer_spec(l, 4, D_MODEL),
                  _layer_spec(l, D_MODEL, D_FF, single_buffer=True),
                  _layer_spec(l, D_FF, D_MODEL, single_buffer=True)],
        out_specs=out_specs,
        out_shape=out_shape,
        compiler_params=_cparams(("arbitrary",)),
        name="mlp",
    )(x, mod, g_norm, w1, w2)


def _rope_tables():
    half = RET_DK // 2
    nf = half // 2
    t = jnp.arange(DEC_SEQ)
    row = (t // GRID_W).astype(F32)
    col = (t % GRID_W).astype(F32)
    inv = ROPE_BASE ** (-jnp.arange(nf, dtype=F32) / nf)
    ang_r = row[:, None] * inv[None, :]
    ang_c = col[:, None] * inv[None, :]
    cos = jnp.concatenate([jnp.cos(ang_r)] * 2 + [jnp.cos(ang_c)] * 2, axis=1)
    sin = jnp.concatenate([-jnp.sin(ang_r), jnp.sin(ang_r), -jnp.sin(ang_c), jnp.sin(ang_c)], axis=1)
    return jnp.tile(cos, (1, RET_HEADS)), jnp.tile(sin, (1, RET_HEADS))


def _gla_state_in(st):
    eye = jnp.eye(GLA_HEADS, dtype=st.dtype)
    t = jnp.einsum('bldhkv,hg->bldhvgk', st, eye)
    return t.reshape(st.shape[0], st.shape[1], 2, GLA_HEADS * GLA_DV, GLA_HEADS * GLA_DK)


def kernel(x_prompt, x_sample, c, cache_na_k, cache_na_v, state_ret, state_s5, state_gla, c_ctx, w_ada, b_ada, g_norm, w_in, ret_log_decay, ret_gn, s5_lambda_re, s5_lambda_im, s5_log_dt, s5_b_re, s5_b_im, s5_c_re, s5_c_im, s5_d, s5_w_glu, s5_b_glu, gla_w_gate, gla_b_gate, gla_gn, na_rpb, w_branch, w_merge, b_merge, w_out, w_mlp1, w_mlp2):
    depth = w_in.shape[0]
    x = (x_prompt.reshape(T_CTX, D_MODEL), x_sample.reshape(T_LAT, D_MODEL))
    cc = jnp.concatenate([c_ctx[None], c, jnp.zeros((N_MOD_ROWS - 1 - DEC_BATCH, D_MODEL), F32)], axis=0)
    mod = _ada(cc, w_ada, b_ada).reshape(depth, N_MOD_ROWS, 6, D_MODEL)

    cos, sin = _rope_tables()
    w_in_t = w_in.transpose(0, 2, 1)
    ret_gn3, gla_gn3 = (a.reshape(depth, 1, BRANCH_W) for a in (ret_gn, gla_gn))
    s5_d_rows = jnp.broadcast_to(s5_d.reshape(depth, S5_GROUPS, 1, S5_GROUP_CH),
                                 (depth, S5_GROUPS, S5_CHUNK, S5_GROUP_CH)).reshape(depth, 1, S5_GROUPS * S5_W)
    b_glu3 = s5_b_glu.reshape(depth, 1, 2 * BRANCH_W)
    b_mg3 = b_merge.reshape(depth, 1, N_BRANCH * D_MODEL)
    cache_k = cache_na_k.transpose(0, 1, 3, 2, 4)
    cache_v = cache_na_v.transpose(0, 1, 3, 2, 4)
    na_tb = _na_tables(na_rpb)
    gla_s0 = _gla_state_in(state_gla)
    s5_tables, s5_lam = _s5_tables(s5_lambda_re, s5_lambda_im, s5_log_dt, s5_b_re, s5_b_im,
                                   s5_c_re, s5_c_im)
    s5_x0 = state_s5.transpose(1, 2, 0, 3, 5, 4).reshape(depth, 2, DEC_BATCH, S5_GROUPS * S5_P2)

    ks_l, vs_l, ret_l, s5_l, gla_l = [], [], [], [], []
    for l in range(depth):
        proj = _inproj(l, x, mod, g_norm, w_in_t)
        if l == 0:
            x, proj = proj[0], proj[1:]
        ret, su, gqk, gv, gg, glr, nq, nk, nv = proj

        ret_o, st_ret = _retention(l, ret, ret_log_decay, ret_gn3, latent=False)
        ret_o, = _retention(l, ret, ret_log_decay, ret_gn3, latent=True, cos=cos, sin=sin, s0=state_ret,
                            prev=ret_o)

        s5_y, s5_fin = _s5(l, su, *s5_tables, s5_d_rows, s5_lam, s5_x0)

        gla_o, st_gla = _gla(l, gqk, gv, gg, glr, gla_w_gate, gla_b_gate, gla_gn3, latent=False)
        gla_o, = _gla(l, gqk, gv, gg, glr, gla_w_gate, gla_b_gate, gla_gn3, latent=True, s0=gla_s0,
                      prev=gla_o)

        na_o = _attn_ctx(nq, nk, nv)
        na_o = _attn_lat(l, nq, nk, nv, cache_k, cache_v, na_tb, na_o)

        x = _merge(l, x, mod, g_norm, ret_o, s5_y, gla_o, na_o,
                   s5_w_glu, b_glu3, w_branch, w_merge, b_mg3, w_out)
        x = _mlp(l, x, mod, g_norm, w_mlp1, w_mlp2, split_out=(l == depth - 1))

        ks_l.append(nk)
        vs_l.append(nv)
        ret_l.append(st_ret)
        s5_l.append(s5_fin)
        gla_l.append(st_gla)

    y_prompt = x[0].reshape(BATCH, SEQ, D_MODEL)
    y_sample = x[1].reshape(DEC_BATCH, DEC_SEQ, D_MODEL)

    def cache_out(per_layer):
        a = jnp.stack(per_layer, axis=0)[:, :, :T_CTX].reshape(depth, NA_HEADS, BATCH, SEQ, NA_DH)
        return a.transpose(2, 0, 3, 1, 4)

    s5_out = jnp.stack(s5_l, axis=0).reshape(depth, 2, BATCH, S5_GROUPS, 2, S5_STATE)
    return (y_prompt, y_sample, cache_out(ks_l), cache_out(vs_l), jnp.stack(ret_l, axis=1),
            s5_out.transpose(2, 0, 1, 3, 5, 4), jnp.stack(gla_l, axis=1))
```

```python
import functools
import math

import numpy as np
import jax
import jax.numpy as jnp
from jax import lax
from jax.experimental import pallas as pl
from jax.experimental.pallas import tpu as pltpu

F32 = jnp.float32
BF16 = jnp.bfloat16

D_MODEL = 1024
BATCH = 16
SEQ = 256
DEPTH = 4
DEC_BATCH = 4
DEC_SEQ = 1024
PAST_LEN = 256
GRID_W = 64
N_BRANCH = 4
BRANCH_W = 256
RET_HEADS = 4
RET_DK = 64
RET_DV = 64
S5_GROUPS = 16
S5_GROUP_CH = 16
S5_STATE = 64
GLA_HEADS = 4
GLA_DK = 32
GLA_DV = 64
GLA_RANK = 16
GLA_TAU = 16.0
NA_HEADS = 4
NA_DH = 64
NA_WIN_H = 8
NA_WIN_W = 16
D_FF = 4 * D_MODEL
ROPE_BASE = 10000.0
EPS = 1e-6

T_CTX = BATCH * SEQ
T_LAT = DEC_BATCH * DEC_SEQ
T_ALL = T_CTX + T_LAT
LAT_BLOCK0 = T_CTX // DEC_SEQ
N_MOD_ROWS = 8
TOKEN_TILE = 512
CTX_SEQS_PER_STEP = DEC_SEQ // SEQ
GLA_CHUNK = 64
GLA_BLOCK_CHUNKS = 4
S5_CHUNK = 16
RET_QBLOCK = 256
NA_QBLOCK = 256
GRID_ROWS = DEC_SEQ // GRID_W
NA_KH = min(NA_WIN_H, GRID_ROWS)
NA_REL_ROWS = 2 * NA_WIN_H - 1
LANES = 128
VMEM_LIMIT = 56 * 1024 * 1024
IN_RET = 0
IN_S5 = IN_RET + 2 * RET_HEADS * RET_DK + 2 * RET_HEADS * RET_DV
IN_GLA_QK = IN_S5 + S5_GROUPS * S5_GROUP_CH
IN_GLA_V = IN_GLA_QK + 2 * GLA_HEADS * GLA_DK
IN_GLA_G = IN_GLA_V + GLA_HEADS * GLA_DV
IN_TAIL_COL = IN_GLA_G + GLA_HEADS * GLA_DV
IN_NA = IN_TAIL_COL + 2 * GLA_RANK
D_IN = IN_NA + 3 * NA_HEADS * NA_DH


def _cparams(sem):
    return pltpu.CompilerParams(dimension_semantics=sem, vmem_limit_bytes=VMEM_LIMIT)


def _bdot(a, b):
    return jnp.dot(a.astype(BF16), b.astype(BF16), preferred_element_type=F32)


def _bdot_nt(a, b):
    return lax.dot_general(a.astype(BF16), b.astype(BF16), (((1,), (1,)), ((), ())),
                           preferred_element_type=F32)


def _bdot_tn(a, b):
    return lax.dot_general(a.astype(BF16), b.astype(BF16), (((0,), (0,)), ((), ())),
                           preferred_element_type=F32)


def _split(a):
    hi = a.astype(BF16)
    lo = (a - hi.astype(F32)).astype(BF16)
    return hi, lo


def _dot3(a, b):
    ah, al = _split(a)
    bh, bl = _split(b)
    d = functools.partial(jnp.dot, preferred_element_type=F32)
    return d(ah, bh) + d(al, bh) + d(ah, bl)


def _sigmoid(x):
    return 0.5 * jnp.tanh(0.5 * x) + 0.5


def _silu(x):
    return x * _sigmoid(x)


def _rms(x, g):
    return x * lax.rsqrt(jnp.mean(x * x, axis=-1, keepdims=True) + EPS) * g


def _head_norm(o, g, width):
    n = o.shape[-1]
    hi = lax.broadcasted_iota(jnp.int32, (n, n), 0) // width
    hj = lax.broadcasted_iota(jnp.int32, (n, n), 1) // width
    avg = jnp.where(hi == hj, 1.0 / width, 0.0).astype(BF16)

    def head_mean(a):
        ah, al = _split(a)
        return (jnp.dot(ah, avg, preferred_element_type=F32) + jnp.dot(al, avg, preferred_element_type=F32))

    xc = o - head_mean(o)
    return xc * lax.rsqrt(head_mean(xc * xc) + EPS) * g


def _mod_row(i):
    ctx_tiles = T_CTX // TOKEN_TILE
    return jnp.where(i < ctx_tiles, 0, 1 + (i - ctx_tiles) // (DEC_SEQ // TOKEN_TILE))


def _mod_spec(l):
    return pl.BlockSpec((1, 1, 6, D_MODEL), lambda i: (l, _mod_row(i), 0, 0))


def _layer_spec(l, *shape, single_buffer=False):
    mode = pl.Buffered(1) if single_buffer else None
    return pl.BlockSpec((1,) + shape, lambda *_: (l,) + (0,) * len(shape), pipeline_mode=mode)


_ANY = pl.BlockSpec(memory_space=pl.ANY)


ADA_TILE = 1536


def _ada_kernel(c_ref, w_ref, b_ref, o_ref):
    a = _silu(c_ref[...])
    o_ref[0] = _bdot(a, w_ref[0]) + b_ref[0]


def _ada(cc, w_ada, b_ada):
    n = 6 * D_MODEL
    return pl.pallas_call(
        _ada_kernel,
        grid=(DEPTH, n // ADA_TILE),
        in_specs=[pl.BlockSpec((N_MOD_ROWS, D_MODEL), lambda l, j: (0, 0)),
                  pl.BlockSpec((1, D_MODEL, ADA_TILE), lambda l, j: (l, 0, j)),
                  pl.BlockSpec((1, 1, ADA_TILE), lambda l, j: (l, 0, j))],
        out_specs=pl.BlockSpec((1, N_MOD_ROWS, ADA_TILE), lambda l, j: (l, 0, j)),
        out_shape=jax.ShapeDtypeStruct((DEPTH, N_MOD_ROWS, n), F32),
        compiler_params=_cparams(("parallel", "parallel")),
        name="ada_mod",
    )(cc, w_ada, b_ada.reshape(DEPTH, 1, n))


def _piece_transpose(blocks, piece):
    x = list(blocks)
    n = len(x)
    d = n // 2
    while d >= 1:
        low = (piece & d) == 0
        for v in range(n):
            if v & d:
                continue
            a, b = x[v], x[v + d]
            x[v] = jnp.where(low, a, pltpu.roll(b, d * S5_GROUP_CH, 1))
            x[v + d] = jnp.where(low, pltpu.roll(a, LANES - d * S5_GROUP_CH, 1), b)
        d //= 2
    return x


def _inproj_kernel(*refs, first):
    if first:
        (xa_ref, xb_ref, mod_ref, g_ref, w_ref, x_out_ref,
         ret_ref, s5_ref, gqk_ref, gv_ref, gg_ref, glr_ref, nq_ref, nk_ref, nv_ref, su_s) = refs
        x = jnp.where(pl.program_id(0) < T_CTX // TOKEN_TILE, xa_ref[...], xb_ref[...])
        x_out_ref[...] = x
    else:
        (x_ref, mod_ref, g_ref, w_ref,
         ret_ref, s5_ref, gqk_ref, gv_ref, gg_ref, glr_ref, nq_ref, nk_ref, nv_ref, su_s) = refs
        x = x_ref[...]
    mod = mod_ref[0, 0]
    h = _rms(x, g_ref[0, 0:1]) * (1.0 + mod[1:2]) + mod[0:1]
    hb = h.astype(BF16)

    def proj(lo, hi):
        return _bdot_nt(hb, w_ref[0, lo:hi, :])

    ret_ref[...] = proj(IN_RET, IN_S5)
    su = proj(IN_S5, IN_GLA_QK)
    rows = TOKEN_TILE // S5_CHUNK
    ng = S5_LANE_GROUPS
    piece = lax.broadcasted_iota(jnp.int32, (rows, LANES), 1) // S5_GROUP_CH
    for lb in range(S5_GROUPS // ng):
        su_s[lb] = su[:, lb * LANES:(lb + 1) * LANES]
        for m in range(S5_CHUNK // ng):
            out = _piece_transpose(
                [su_s[lb, pl.ds(m * ng + jl, rows, stride=S5_CHUNK), :] for jl in range(ng)], piece)
            for g in range(ng):
                col = (lb * ng + g) * S5_W + m * LANES
                s5_ref[:, col:col + LANES] = out[g]
    gqk_ref[...] = proj(IN_GLA_QK, IN_GLA_V)
    gv_ref[...] = proj(IN_GLA_V, IN_GLA_G)
    gg_ref[...] = proj(IN_GLA_G, IN_TAIL_COL)
    glr_ref[...] = proj(IN_TAIL_COL, IN_TAIL_COL + LANES)
    lo = IN_NA
    for ref in (nq_ref, nk_ref, nv_ref):
        r = proj(lo, lo + NA_HEADS * NA_DH)
        for hh in range(NA_HEADS):
            ref[hh] = r[:, hh * NA_DH:(hh + 1) * NA_DH]
        lo += NA_HEADS * NA_DH


def _inproj(l, xs, mod, g_norm, w_in):
    tm = TOKEN_TILE
    first = isinstance(xs, tuple)
    tok = lambda w: pl.BlockSpec((tm, w), lambda i: (i, 0))
    head = pl.BlockSpec((NA_HEADS, tm, NA_DH), lambda i: (0, i, 0))
    tshape = lambda w: jax.ShapeDtypeStruct((T_ALL, w), F32)
    hshape = jax.ShapeDtypeStruct((NA_HEADS, T_ALL, NA_DH), F32)
    ctx_tiles = T_CTX // tm
    if first:
        x_specs = [pl.BlockSpec((tm, D_MODEL), lambda i: (jnp.minimum(i, ctx_tiles - 1), 0)),
                   pl.BlockSpec((tm, D_MODEL), lambda i: (jnp.maximum(i - ctx_tiles, 0), 0))]
        x_args = list(xs)
    else:
        x_specs, x_args = [tok(D_MODEL)], [xs]
    return pl.pallas_call(
        functools.partial(_inproj_kernel, first=first),
        grid=(T_ALL // tm,),
        in_specs=x_specs + [_mod_spec(l), _layer_spec(l, 4, D_MODEL),
                            _layer_spec(l, D_IN, D_MODEL, single_buffer=True)],
        out_specs=([tok(D_MODEL)] if first else [])
        + [tok(1024), pl.BlockSpec((tm // S5_CHUNK, S5_GROUPS * S5_W), lambda i: (i, 0)),
           tok(256), tok(256), tok(256), tok(LANES), head, head, head],
        out_shape=([tshape(D_MODEL)] if first else [])
        + [tshape(1024), jax.ShapeDtypeStruct((S5_ROWS, S5_GROUPS * S5_W), F32),
           tshape(256), tshape(256), tshape(256), tshape(LANES), hshape, hshape, hshape],
        scratch_shapes=[pltpu.VMEM((S5_GROUPS // S5_LANE_GROUPS, tm, LANES), F32)],
        compiler_params=_cparams(("parallel",)),
        name="in_proj",
    )(*x_args, mod, g_norm, w_in)


def _rope_rotate(x, lane):
    first = (lane % 32) < 16
    w = x.shape[-1]
    return jnp.where(first, pltpu.roll(x, w - 16, 1), pltpu.roll(x, 16, 1))


def _ret_kernel(ld_ref, ret_ref, gn_ref, *rest, layer, seq, latent):
    if latent:
        cos_ref, sin_ref, s0_ref, _, out_ref, dec_s = rest
    else:
        out_ref, st_ref, dec_s = rest
    tq = RET_QBLOCK
    nq = seq // tq
    width = dec_s.shape[-1]

    @pl.when(pl.program_id(0) == 0)
    def _build_decay():
        rel = (lax.broadcasted_iota(jnp.int32, (tq, width), 0) + (nq - 1) * tq
               - lax.broadcasted_iota(jnp.int32, (tq, width), 1)).astype(F32)
        for h in range(RET_HEADS):
            dec_s[h] = (jnp.where(rel >= 0, jnp.exp(ld_ref[layer, 0, h] * jnp.maximum(rel, 0.0)), 0.0)
                        + jnp.where(rel <= 0, jnp.exp(ld_ref[layer, 1, h] * jnp.maximum(-rel, 0.0)), 0.0))

    nrows = ret_ref.shape[0]
    nsub = nrows // seq
    q = ret_ref[:, 0:256]
    k = ret_ref[:, 256:512]
    if latent:
        lane = lax.broadcasted_iota(jnp.int32, (nrows, 256), 1)
        cos = cos_ref[...]
        sin = sin_ref[...]
        q = q * cos + _rope_rotate(q, lane) * sin
        k = k * cos + _rope_rotate(k, lane) * sin
    k = k * (RET_DK ** -0.5)
    pos_c = lax.broadcasted_iota(jnp.int32, (seq, 1), 0).astype(F32)
    tiles = [(s, qb) for s in range(nsub) for qb in range(nq)]
    for h in range(RET_HEADS):
        lgf = ld_ref[layer, 0, h]
        lgb = ld_ref[layer, 1, h]
        sl = slice(h * RET_DK, (h + 1) * RET_DK)
        qh = q[:, sl]
        kh = k[:, sl]
        kb = kh.astype(BF16)
        vb = ret_ref[:, 512 + h * RET_DV:512 + (h + 1) * RET_DV].astype(BF16)
        if latent:
            q_init = jnp.concatenate([qh * jnp.exp(lgf * (pos_c + 1.0)),
                                      qh * jnp.exp(lgb * (seq - pos_c))], axis=1)
            s_init = jnp.concatenate([s0_ref[0, 0, 0, h], s0_ref[0, 0, 1, h]], axis=0)

        def score(s, qb):
            w0 = (nq - 1 - qb) * tq
            rows = slice(s * seq + qb * tq, s * seq + (qb + 1) * tq)
            keys = slice(s * seq, (s + 1) * seq)
            return (_bdot_nt(qh[rows], kb[keys]) * dec_s[h, :, w0:w0 + seq]).astype(BF16)

        def values(sc, s, qb):
            o = jnp.dot(sc, vb[s * seq:(s + 1) * seq], preferred_element_type=F32)
            if latent:
                o = o + _bdot(q_init[qb * tq:(qb + 1) * tq], s_init)
            return o

        def finish(o, s, qb):
            out_ref[s * seq + qb * tq:s * seq + (qb + 1) * tq, sl] = o

        if nsub > 1:
            scores = [score(s, qb) for s, qb in tiles]
            outs = [values(sc, s, qb) for sc, (s, qb) in zip(scores, tiles)]
            for o, (s, qb) in zip(outs, tiles):
                finish(o, s, qb)
        else:
            for s, qb in tiles:
                finish(values(score(s, qb), s, qb), s, qb)
        if not latent:
            for s in range(nsub):
                keys = slice(s * seq, (s + 1) * seq)
                st_ref[s, 0, h] = _bdot_tn(kh[keys] * jnp.exp(lgf * (seq - 1.0 - pos_c)), vb[keys])
                st_ref[s, 1, h] = _bdot_tn(kh[keys] * jnp.exp(lgb * pos_c), vb[keys])
    for r0 in range(0, nrows, tq):
        rows = slice(r0, r0 + tq)
        out_ref[rows, :] = _head_norm(out_ref[rows, :], gn_ref[0], RET_DV) * _silu(ret_ref[rows, 768:1024])


def _retention(l, ret, ld, gn, *, latent, cos=None, sin=None, s0=None, prev=None):
    seq = DEC_SEQ if latent else SEQ
    nsub = 1 if latent else CTX_SEQS_PER_STEP
    nb = DEC_BATCH if latent else BATCH // nsub
    off = LAT_BLOCK0 if latent else 0
    rows = nsub * seq
    in_specs = [pl.BlockSpec(memory_space=pltpu.SMEM),
                pl.BlockSpec((rows, 1024), lambda b: (b + off, 0)),
                _layer_spec(l, 1, 256)]
    args = [ld, ret, gn]
    out_specs = [pl.BlockSpec((rows, 256), lambda b: (b + off, 0))]
    out_shape = [jax.ShapeDtypeStruct((T_ALL, 256), F32)]
    aliases = {}
    if latent:
        in_specs += [pl.BlockSpec((seq, 256), lambda b: (0, 0)),
                     pl.BlockSpec((seq, 256), lambda b: (0, 0)),
                     pl.BlockSpec((1, 1, 2, RET_HEADS, RET_DK, RET_DV), lambda b: (b, l, 0, 0, 0, 0)),
                     _ANY]
        args += [cos, sin, s0, prev]
        aliases = {6: 0}
    else:
        out_specs.append(pl.BlockSpec((nsub, 2, RET_HEADS, RET_DK, RET_DV), lambda b: (b, 0, 0, 0, 0)))
        out_shape.append(jax.ShapeDtypeStruct((BATCH, 2, RET_HEADS, RET_DK, RET_DV), F32))
    return pl.pallas_call(
        functools.partial(_ret_kernel, layer=l, seq=seq, latent=latent),
        grid=(nb,),
        in_specs=in_specs, out_specs=out_specs, out_shape=out_shape,
        input_output_aliases=aliases,
        scratch_shapes=[pltpu.VMEM((RET_HEADS, RET_QBLOCK, 2 * seq - RET_QBLOCK), F32)],
        compiler_params=_cparams(("arbitrary",)),
        name="retention_lat" if latent else "retention_ctx",
    )(*args)


def _gla_kernel(gqk_ref, gv_ref, gg_ref, glr_ref, wg_ref, bg_ref, gn_ref, *rest, seq, latent):
    if latent:
        s0_ref, _, out_ref, gate_s, o_s, st_s, qst_s, ds_s, e_s = rest
    else:
        out_ref, st_ref, gate_s, o_s, st_s, qst_s, ds_s, e_s = rest
    c = GLA_CHUNK
    n = seq // c
    nsub = gqk_ref.shape[0] // seq
    hk = GLA_HEADS * GLA_DK
    lr = glr_ref[...]
    for d in range(2):
        pre = _bdot(lr[:, d * GLA_RANK:(d + 1) * GLA_RANK], wg_ref[0, d]) + bg_ref[0, d:d + 1]
        gate_s[d] = (jnp.minimum(pre, 0.0) - jnp.log(1.0 + jnp.exp(-jnp.abs(pre)))) / GLA_TAU
        for s in range(nsub):
            st_s[2 * s + d] = s0_ref[0, 0, d] if latent else jnp.zeros((GLA_HEADS * GLA_DV, hk), F32)

    nc = GLA_BLOCK_CHUNKS
    rb = nc * c
    ti = lax.broadcasted_iota(jnp.int32, (rb, rb), 0)
    tj = lax.broadcasted_iota(jnp.int32, (rb, rb), 1)
    same = (ti // c) == (tj // c)
    ones = lambda m: (same & m).astype(BF16)
    tri = [ones(tj <= ti), ones(tj >= ti)]
    mid = [ones((tj % c) < c // 2), ones((tj % c) >= c // 2)]
    tot = ones(tj == tj)
    lane_k = lax.broadcasted_iota(jnp.int32, (c, hk), 1)
    head_mask = [(lane_k // GLA_DK) == h for h in range(GLA_HEADS)]
    ai = lax.broadcasted_iota(jnp.int32, (GLA_HEADS * c, c), 0) % c
    aj = lax.broadcasted_iota(jnp.int32, (GLA_HEADS * c, c), 1)
    keep = [aj <= ai, aj >= ai]
    sr = lax.broadcasted_iota(jnp.int32, (GLA_HEADS * GLA_DV, hk), 0) // GLA_DV
    sc = lax.broadcasted_iota(jnp.int32, (GLA_HEADS * GLA_DV, hk), 1) // GLA_DK
    diag = sr == sc
    scale = GLA_DK ** -0.5
    d32 = functools.partial(jnp.dot, preferred_element_type=F32)

    def rows_of(i, size):
        return pl.ds(i * size, size) if isinstance(i, int) else pl.ds(pl.multiple_of(i * size, size), size)

    def local(block_ids):
        vs, q_att, k_att, k_st = {}, {}, {}, {}
        for bi in block_ids:
            rows = rows_of(bi, rb)
            q = gqk_ref[rows, 0:hk] * scale
            k = gqk_ref[rows, hk:2 * hk]
            vs[bi] = gv_ref[rows, :].astype(BF16)
            for d in range(2):
                parts = jnp.concatenate(_split(gate_s[d, rows, :]), axis=1)
                sums = lambda m: (lambda r: r[:, 0:hk] + r[:, hk:2 * hk])(d32(m, parts))
                b, b_mid, b_end = sums(tri[d]), sums(mid[d]), sums(tot)
                q_att[bi, d] = q * jnp.exp(b - b_mid)
                k_att[bi, d] = k * jnp.exp(b_mid - b)
                k_st[bi, d] = (k * jnp.exp(b_end - b)).astype(BF16)
                qst_s[d, rows, :] = (q * jnp.exp(b)).astype(BF16)
                decay = jnp.exp(b_end)
                for cc in range(nc):
                    e_s[d, bi * nc + cc] = decay[cc * c:cc * c + 8]
        tiles = [(bi, d, cc) for bi in block_ids for d in range(2) for cc in range(nc)]
        att = {}
        for bi, d, cc in tiles:
            r = slice(cc * c, (cc + 1) * c)
            qa = q_att[bi, d][r]
            q_stack = jnp.concatenate([jnp.where(head_mask[h], qa, 0.0) for h in range(GLA_HEADS)], axis=0)
            att[bi, d, cc] = jnp.where(keep[d], _bdot_nt(q_stack, k_att[bi, d][r]), 0.0).astype(BF16)
        for bi, d, cc in tiles:
            r = slice(cc * c, (cc + 1) * c)
            o = jnp.concatenate([d32(att[bi, d, cc][h * c:(h + 1) * c], vs[bi][r, h * GLA_DV:(h + 1) * GLA_DV])
                                 for h in range(GLA_HEADS)], axis=1)
            o_s[d, rows_of(bi * nc + cc, c), :] = o
        for bi, d, cc in tiles:
            r = slice(cc * c, (cc + 1) * c)
            ds_s[d, bi * nc + cc] = jnp.where(diag, lax.dot_general(
                vs[bi][r], k_st[bi, d][r], (((0,), (0,)), ((), ())), preferred_element_type=F32), 0.0)

    def recur(s, ci, d):
        g = s * n + ci
        rows = rows_of(g, c)
        st = st_s[2 * s + d]
        o_s[d, rows, :] += _bdot_nt(qst_s[d, rows, :], st)
        st_s[2 * s + d] = st * e_s[d, g, 0:1] + ds_s[d, g]

    def recur_body(i, carry):
        for s in range(nsub):
            recur(s, i, 0)
            recur(s, n - 1 - i, 1)
        return carry

    local(list(range(nsub * n // nc)))
    if n <= 4:
        for i in range(n):
            recur_body(i, 0)
    else:
        lax.fori_loop(0, n, recur_body, 0, unroll=2)

    for r0 in range(0, nsub * seq, rb):
        rows = slice(r0, r0 + rb)
        out_ref[rows, :] = (_head_norm(o_s[0, rows, :] + o_s[1, rows, :], gn_ref[0], GLA_DV)
                            * _silu(gg_ref[rows, :]))
    if not latent:
        hv = GLA_HEADS * GLA_DV
        eye = (lax.broadcasted_iota(jnp.int32, (hv, hv), 0)
               == lax.broadcasted_iota(jnp.int32, (hv, hv), 1)).astype(BF16)
        tn = lambda a: lax.dot_general(a, eye, (((0,), (0,)), ((), ())), preferred_element_type=F32)
        for s in range(nsub):
            for d in range(2):
                st = st_s[2 * s + d]
                hi, lo = _split(st)
                lo2 = (st - hi.astype(F32) - lo.astype(F32)).astype(BF16)
                s_all = tn(hi) + tn(lo) + tn(lo2)
                for h in range(GLA_HEADS):
                    st_ref[s, d, h] = s_all[h * GLA_DK:(h + 1) * GLA_DK, h * GLA_DV:(h + 1) * GLA_DV]


def _gla(l, gqk, gv, gg, glr, wg, bg, gn, *, latent, s0=None, prev=None):
    seq = DEC_SEQ if latent else SEQ
    nsub = 1 if latent else CTX_SEQS_PER_STEP
    nb = DEC_BATCH if latent else BATCH // nsub
    off = LAT_BLOCK0 if latent else 0
    rows = nsub * seq
    hk = GLA_HEADS * GLA_DK
    hv = GLA_HEADS * GLA_DV
    tok = lambda w: pl.BlockSpec((rows, w), lambda b: (b + off, 0))
    in_specs = [tok(256), tok(256), tok(256), tok(LANES),
                _layer_spec(l, 2, GLA_RANK, hk), _layer_spec(l, 2, hk), _layer_spec(l, 1, 256)]
    args = [gqk, gv, gg, glr, wg, bg, gn]
    out_specs = [tok(256)]
    out_shape = [jax.ShapeDtypeStruct((T_ALL, 256), F32)]
    aliases = {}
    if latent:
        in_specs += [pl.BlockSpec((1, 1, 2, hv, hk), lambda b: (b, l, 0, 0, 0)), _ANY]
        args += [s0, prev]
        aliases = {8: 0}
    else:
        out_specs.append(pl.BlockSpec((nsub, 2, GLA_HEADS, GLA_DK, GLA_DV), lambda b: (b, 0, 0, 0, 0)))
        out_shape.append(jax.ShapeDtypeStruct((BATCH, 2, GLA_HEADS, GLA_DK, GLA_DV), F32))
    return pl.pallas_call(
        functools.partial(_gla_kernel, seq=seq, latent=latent),
        grid=(nb,),
        in_specs=in_specs, out_specs=out_specs, out_shape=out_shape,
        input_output_aliases=aliases,
        scratch_shapes=[pltpu.VMEM((2, rows, hk), F32), pltpu.VMEM((2, rows, hv), F32),
                        pltpu.VMEM((2 * nsub, hv, hk), F32), pltpu.VMEM((2, rows, hk), BF16),
                        pltpu.VMEM((2, rows // GLA_CHUNK, hv, hk), F32),
                        pltpu.VMEM((2, rows // GLA_CHUNK, 8, hk), F32)],
        compiler_params=_cparams(("parallel",)),
        name="gla_lat" if latent else "gla_ctx",
    )(*args)


S5_W = S5_CHUNK * S5_GROUP_CH
S5_P2 = 2 * S5_STATE
S5_ROWS = T_ALL // S5_CHUNK
S5_ROWS_CTX = T_CTX // S5_CHUNK
S5_LANE_GROUPS = LANES // S5_GROUP_CH


S5_TE_ROWS = S5_W + 4 * S5_P2


def _s5_toeplitz_kernel(cc_ref, pwr_ref, pwi_ref, bbr_ref, bbi_ref, ctr_ref, cti_ref, te_ref, ffb_ref):
    lane = lax.broadcasted_iota(jnp.int32, (S5_GROUP_CH, S5_W), 1)
    swapped = lambda t: jnp.concatenate([t[S5_STATE:], t[:S5_STATE]], axis=0)
    for t in range(S5_TABLE_GROUPS):
        def times_b(v, d):
            pr, pi, br, bi = pwr_ref[0, t, v], pwi_ref[0, t, v], bbr_ref[0, d, t], bbi_ref[0, d, t]
            return jnp.concatenate([pr * br - pi * bi, pr * bi + pi * br], axis=0)

        def times_c(v, d):
            pr, pi, cr, ci = pwr_ref[0, t, v], pwi_ref[0, t, v], ctr_ref[0, d, t], cti_ref[0, d, t]
            return jnp.concatenate([cr * pr - ci * pi, -(cr * pi + ci * pr)], axis=0)

        wfr = times_b(0, 0)
        wb = times_b(1, 1)
        kf = _dot3(cc_ref[0, 0, t], wfr)
        kb = _dot3(cc_ref[0, 1, t], wb)
        blocks = []
        for i in range(S5_CHUNK):
            sf = ((i + 1 - S5_CHUNK) * S5_GROUP_CH) % S5_W
            fwd = jnp.where(lane < (i + 1) * S5_GROUP_CH, pltpu.roll(kf, sf, 1) if sf else kf, 0.0)
            bwd = jnp.where(lane >= i * S5_GROUP_CH, pltpu.roll(kb, i * S5_GROUP_CH, 1) if i else kb, 0.0)
            blocks.append(fwd + bwd)
        te_ref[0, t] = jnp.concatenate(blocks + [wfr, wb, swapped(wfr), swapped(wb)], axis=0).astype(BF16)
        ffb_ref[0, t] = jnp.concatenate([times_c(2, 0), times_c(3, 1)], axis=0).astype(BF16)


S5_TABLE_GROUPS = 4


def _s5_toeplitz(cc, pwr, pwi, bbr, bbi, ctr, cti):
    tg = S5_TABLE_GROUPS
    nl, _, ngroups = cc.shape[:3]
    by_dir = lambda *tail: pl.BlockSpec((1, 2, tg) + tail, lambda l, i: (l, 0, i) + (0,) * len(tail))
    by_group = lambda *tail: pl.BlockSpec((1, tg) + tail, lambda l, i: (l, i) + (0,) * len(tail))
    lanes = by_dir(S5_STATE, S5_W)
    powers = by_group(4, S5_STATE, S5_W)
    return pl.pallas_call(
        _s5_toeplitz_kernel,
        grid=(nl, ngroups // tg),
        in_specs=[by_dir(S5_GROUP_CH, S5_P2), powers, powers, lanes, lanes, lanes, lanes],
        out_specs=[by_group(S5_TE_ROWS, S5_W), by_group(2 * S5_P2, S5_W)],
        out_shape=[jax.ShapeDtypeStruct((nl, ngroups, S5_TE_ROWS, S5_W), BF16),
                   jax.ShapeDtypeStruct((nl, ngroups, 2 * S5_P2, S5_W), BF16)],
        compiler_params=_cparams(("parallel", "parallel")),
        name="s5_toeplitz",
    )(cc, pwr, pwi, bbr, bbi, ctr, cti)


def _s5_kernel(u_ref, te_ref, ff_ref, d_ref, lam_ref, x0_ref, y_ref, fin_ref, xs_s, ps_s):
    C = S5_CHUNK
    ng = S5_LANE_GROUPS

    for g in range(ng):
        u = u_ref[:, g * S5_W:(g + 1) * S5_W]
        r = _bdot_nt(u, te_ref[0, g])
        y_ref[:, g * S5_W:(g + 1) * S5_W] = r[:, 0:S5_W] + d_ref[0, :, g * S5_W:(g + 1) * S5_W] * u
        for t in range(4):
            xs_s[t * ng + g] = r[:, S5_W + t * S5_P2:S5_W + (t + 1) * S5_P2]

    w = ng * S5_P2

    def carry(base, nseq, nchunks, init_f, init_b):
        a_f, b_f = lam_ref[0, 0, 0:1], lam_ref[0, 0, 1:2]
        a_b, b_b = lam_ref[0, 1, 0:1], lam_ref[0, 1, 1:2]
        load = lambda t, rows: jnp.concatenate([xs_s[t * ng + g, rows, :] for g in range(ng)], axis=1)

        def body(i, st):
            sf, tf, sb, tb = st
            rf = pl.ds(base + i, nseq, stride=nchunks)
            rb = pl.ds(base + (nchunks - 1 - i), nseq, stride=nchunks)
            for g in range(ng):
                ps_s[g, rf, :] = sf[:, g * S5_P2:(g + 1) * S5_P2]
                ps_s[ng + g, rb, :] = sb[:, g * S5_P2:(g + 1) * S5_P2]
            return (a_f * sf + b_f * tf + load(0, rf), a_f * tf - b_f * sf + load(2, rf),
                    a_b * sb + b_b * tb + load(1, rb), a_b * tb - b_b * sb + load(3, rb))

        first = (lax.broadcasted_iota(jnp.int32, (1, w), 1) % S5_P2) < S5_STATE
        swap = lambda s: jnp.where(first, pltpu.roll(s, w - S5_STATE, 1), pltpu.roll(s, S5_STATE, 1))
        return lax.fori_loop(0, nchunks, body, (init_f, swap(init_f), init_b, swap(init_b)))

    zeros = jnp.zeros((BATCH, w), F32)
    fin = carry(0, BATCH, SEQ // C, zeros, zeros)
    fin_ref[0] = fin[0]
    fin_ref[1] = fin[2]
    carry(S5_ROWS_CTX, DEC_BATCH, DEC_SEQ // C, x0_ref[0, 0], x0_ref[0, 1])

    for g in range(ng):
        p = jnp.concatenate([ps_s[g], ps_s[ng + g]], axis=1)
        y_ref[:, g * S5_W:(g + 1) * S5_W] += _bdot(p, ff_ref[0, g])


def _s5(l, su_rows, te, ff, d_rows, lam, x0):
    ng = S5_LANE_GROUPS
    w = ng * S5_P2
    rows = pl.BlockSpec((S5_ROWS, ng * S5_W), lambda i: (0, i))
    return pl.pallas_call(
        _s5_kernel,
        grid=(S5_GROUPS // ng,),
        in_specs=[rows,
                  pl.BlockSpec((1, ng, S5_TE_ROWS, S5_W), lambda i: (l, i, 0, 0)),
                  pl.BlockSpec((1, ng, 2 * S5_P2, S5_W), lambda i: (l, i, 0, 0)),
                  pl.BlockSpec((1, 1, ng * S5_W), lambda i: (l, 0, i)),
                  pl.BlockSpec((1, 2, 2, w), lambda i: (l, 0, 0, i)),
                  pl.BlockSpec((1, 2, DEC_BATCH, w), lambda i: (l, 0, 0, i))],
        out_specs=[rows, pl.BlockSpec((2, BATCH, w), lambda i: (0, 0, i))],
        out_shape=[jax.ShapeDtypeStruct((S5_ROWS, S5_GROUPS * S5_W), F32),
                   jax.ShapeDtypeStruct((2, BATCH, S5_GROUPS * S5_P2), F32)],
        scratch_shapes=[pltpu.VMEM((4 * ng, S5_ROWS, S5_P2), F32), pltpu.VMEM((2 * ng, S5_ROWS, S5_P2), F32)],
        compiler_params=_cparams(("parallel",)),
        name="s5_scan",
    )(su_rows, te, ff, d_rows, lam, x0)


def _s5_tables(lam_re, lam_im, log_dt, b_re, b_im, c_re, c_im):
    C, G, P, H = S5_CHUNK, S5_GROUPS, S5_STATE, S5_GROUP_CH
    L = lam_re.shape[0]
    dt = jnp.exp(log_dt)[..., None]
    ar, ai = lam_re * dt, lam_im * dt

    steps = jnp.arange(C + 1, dtype=F32)
    mag = jnp.exp(ar[..., None] * steps)
    pw_re, pw_im = mag * jnp.cos(ai[..., None] * steps), mag * jnp.sin(ai[..., None] * steps)
    exact = functools.partial(jnp.einsum, precision=lax.Precision.HIGHEST)
    tau_np = np.arange(C * H) // H

    patterns = [(0, (C - 1) - tau_np), (1, tau_np), (0, tau_np + 1), (1, C - tau_np)]
    sel = np.zeros((2, len(patterns), C + 1, C * H), np.float32)
    for v, (d, t) in enumerate(patterns):
        sel[d, v, t, np.arange(C * H)] = 1.0
    sel = jnp.asarray(sel)
    pwr = exact('ldgpt,dvtn->lgvpn', pw_re, sel)
    pwi = exact('ldgpt,dvtn->lgvpn', pw_im, sel)

    lr, li = pw_re[..., 1], pw_im[..., 1]
    den = lam_re * lam_re + lam_im * lam_im
    qr = ((lr - 1.0) * lam_re + li * lam_im) / den
    qi = (li * lam_re - (lr - 1.0) * lam_im) / den
    bbr = qr[..., None] * b_re - qi[..., None] * b_im
    bbi = qr[..., None] * b_im + qi[..., None] * b_re
    chan = jnp.asarray((np.arange(H)[:, None] == (np.arange(C * H) % H)[None, :]).astype(np.float32))
    lanes = lambda a: exact('ldgph,hn->ldgpn', a, chan)
    bbr, bbi = lanes(bbr), lanes(bbi)
    c_t = lambda a: exact('ldghp,hn->ldgpn', a, chan)
    ctr, cti = c_t(c_re), c_t(c_im)

    cc = jnp.concatenate([c_re, -c_im], axis=-1)
    tables = _s5_toeplitz(cc, pwr, pwi, bbr, bbi, ctr, cti)
    cr, ci = pw_re[..., C], pw_im[..., C]
    a = jnp.concatenate([cr, cr], axis=-1).reshape(L, 2, 1, G * 2 * P)
    b = jnp.concatenate([-ci, ci], axis=-1).reshape(L, 2, 1, G * 2 * P)
    return tables, jnp.concatenate([a, b], axis=2)


def _softmax_pv(s_parts, v_parts):
    m = s_parts[0].max(axis=-1, keepdims=True)
    for s in s_parts[1:]:
        m = jnp.maximum(m, s.max(axis=-1, keepdims=True))
    o = None
    l = None
    for s, v in zip(s_parts, v_parts):
        p = jnp.exp(s - m)
        pl_ = p.sum(axis=-1, keepdims=True)
        po = _bdot(p, v)
        o = po if o is None else o + po
        l = pl_ if l is None else l + pl_
    return o / l


def _attn_ctx_kernel(q_ref, k_ref, v_ref, o_ref):
    scale = NA_DH ** -0.5
    tiles = [(h, slice(s * SEQ, (s + 1) * SEQ)) for h in range(NA_HEADS) for s in range(CTX_SEQS_PER_STEP)]
    scores = [_bdot_nt(q_ref[h, rows, :], k_ref[h, rows, :]) * scale for h, rows in tiles]
    for s, (h, rows) in zip(scores, tiles):
        o_ref[h, rows, :] = _softmax_pv([s], [v_ref[h, rows, :]])


def _attn_ctx(nq, nk, nv):
    spec = pl.BlockSpec((NA_HEADS, CTX_SEQS_PER_STEP * SEQ, NA_DH), lambda b: (0, b, 0))
    return pl.pallas_call(
        _attn_ctx_kernel,
        grid=(BATCH // CTX_SEQS_PER_STEP,),
        in_specs=[spec, spec, spec],
        out_specs=spec,
        out_shape=jax.ShapeDtypeStruct((NA_HEADS, T_ALL, NA_DH), F32),
        compiler_params=_cparams(("parallel",)),
        name="attn_ctx",
    )(nq, nk, nv)


def _attn_lat_kernel(q_ref, k_ref, v_ref, kc_ref, vc_ref, tb_ref, _, o_ref, bias_s):
    @pl.when(pl.program_id(1) == 0)
    def _build_bias():
        bias_s[...] = jnp.full((DEC_SEQ, DEC_SEQ), -jnp.inf, F32)
        for r in range(GRID_ROWS):
            rs = min(max(r - NA_KH // 2, 0), GRID_ROWS - NA_KH)
            dr0 = rs - r + NA_WIN_H - 1
            bias_s[r * GRID_W:(r + 1) * GRID_W, rs * GRID_W:(rs + NA_KH) * GRID_W] = (
                tb_ref[0, 0, :, dr0 * GRID_W:(dr0 + NA_KH) * GRID_W])

    scale = NA_DH ** -0.5
    kb = k_ref[0].astype(BF16)
    vb = v_ref[0].astype(BF16)
    kc = kc_ref[0, 0, 0].astype(BF16)
    vc = vc_ref[0, 0, 0].astype(BF16)
    tq = NA_QBLOCK
    q_rows = tq // GRID_W
    tiles = []
    for qb in range(DEC_SEQ // tq):
        rows = slice(qb * tq, (qb + 1) * tq)
        starts = [min(max(r - NA_KH // 2, 0), GRID_ROWS - NA_KH) for r in range(qb * q_rows, (qb + 1) * q_rows)]
        keys = slice(min(starts) * GRID_W // LANES * LANES,
                     -(-(max(starts) + NA_KH) * GRID_W // LANES) * LANES)
        tiles.append((rows, keys))
    scores = []
    for rows, keys in tiles:
        qh = q_ref[0, rows, :].astype(BF16)
        scores.append((_bdot_nt(qh, kb[keys]) * scale + bias_s[rows, keys], _bdot_nt(qh, kc) * scale))
    for (s_loc, s_ctx), (rows, keys) in zip(scores, tiles):
        o_ref[0, rows, :] = _softmax_pv([s_loc, s_ctx], [vb[keys], vc])


def _attn_lat(l, nq, nk, nv, kc, vc, tb, prev):
    tok = pl.BlockSpec((1, DEC_SEQ, NA_DH), lambda h, b: (h, b + LAT_BLOCK0, 0))
    cache = pl.BlockSpec((1, 1, 1, PAST_LEN, NA_DH), lambda h, b: (b, l, h, 0, 0))
    return pl.pallas_call(
        _attn_lat_kernel,
        grid=(NA_HEADS, DEC_BATCH),
        in_specs=[tok, tok, tok, cache, cache,
                  pl.BlockSpec((1, 1, GRID_W, NA_REL_ROWS * GRID_W), lambda h, b: (l, h, 0, 0)),
                  _ANY],
        out_specs=tok,
        out_shape=jax.ShapeDtypeStruct((NA_HEADS, T_ALL, NA_DH), F32),
        input_output_aliases={6: 0},
        scratch_shapes=[pltpu.VMEM((DEC_SEQ, DEC_SEQ), F32)],
        compiler_params=_cparams(("arbitrary", "arbitrary")),
        name="attn_lat",
    )(nq, nk, nv, kc, vc, tb, prev)


def _na_tables(rpb):
    col = np.arange(GRID_W)
    col_start = np.clip(col - NA_WIN_W // 2, 0, GRID_W - NA_WIN_W)
    col_in = (col[None, :] >= col_start[:, None]) & (col[None, :] < col_start[:, None] + NA_WIN_W)
    col_idx = np.clip(col[None, :] - col[:, None] + NA_WIN_W - 1, 0, 2 * NA_WIN_W - 2)
    onehot = (col_idx[:, :, None] == np.arange(2 * NA_WIN_W - 1)[None, None, :]).astype(np.float32)
    tb = jnp.einsum('lhrd,qkd->lhqrk', rpb, jnp.asarray(onehot), precision=lax.Precision.HIGHEST)
    tb = jnp.where(jnp.asarray(col_in)[None, None, :, None, :], tb, -jnp.inf)
    return tb.reshape(rpb.shape[0], NA_HEADS, GRID_W, NA_REL_ROWS * GRID_W)


def _merge_kernel(x_ref, mod_ref, g_ref, ret_ref, s5y_ref, gla_ref, na_ref,
                  wglu_ref, bglu_ref, wbr_ref, wmg_ref, bmg_ref, wout_ref, o_ref, y_s):
    mod = mod_ref[0, 0]
    ng = S5_LANE_GROUPS
    half = TOKEN_TILE // 2
    crows = half // S5_CHUNK
    piece = lax.broadcasted_iota(jnp.int32, (crows, LANES), 1) // S5_GROUP_CH

    def half_tile(k):
        rows = slice(k * half, (k + 1) * half)
        x = x_ref[rows, :]
        hb = (_rms(x, g_ref[0, 0:1]) * (1.0 + mod[1:2]) + mod[0:1]).astype(BF16)
        yield

        def gate_pre(n):
            return (_bdot(hb, wmg_ref[0, :, n * D_MODEL:(n + 1) * D_MODEL])
                    + bmg_ref[0, :, n * D_MODEL:(n + 1) * D_MODEL])

        acc = _sigmoid(gate_pre(0)) * _bdot(ret_ref[rows, :], wbr_ref[0, 0])
        yield
        acc += _sigmoid(gate_pre(2)) * _bdot(gla_ref[rows, :], wbr_ref[0, 2])
        yield
        na = jnp.concatenate([na_ref[hh, rows, :].astype(BF16) for hh in range(NA_HEADS)], axis=1)
        acc += _sigmoid(gate_pre(3)) * _bdot(na, wbr_ref[0, 3])
        yield
        s5_gate = gate_pre(1)
        for lb in range(S5_GROUPS // ng):
            for m in range(S5_CHUNK // ng):
                cols = [(lb * ng + g) * S5_W + m * LANES for g in range(ng)]
                out = _piece_transpose([s5y_ref[k * crows:(k + 1) * crows, c0:c0 + LANES] for c0 in cols], piece)
                for il in range(ng):
                    y_s[lb, pl.ds(k * half + m * ng + il, crows, stride=S5_CHUNK), :] = out[il]
        y = jnp.concatenate([y_s[lb, rows, :] for lb in range(S5_GROUPS // ng)], axis=1)
        y = 0.5 * y * (1.0 + jnp.tanh(math.sqrt(2.0 / math.pi) * (y + 0.044715 * (y * y * y))))
        z = _bdot(y, wglu_ref[0]) + bglu_ref[0]
        yield
        s5_out = z[:, 0:BRANCH_W] * _sigmoid(z[:, BRANCH_W:2 * BRANCH_W])
        acc += _sigmoid(s5_gate) * _bdot(s5_out, wbr_ref[0, 1])
        yield
        m = _bdot(acc, wout_ref[0])
        yield
        o_ref[rows, :] = x + mod[2:3] * _rms(m, g_ref[0, 1:2])

    running = [half_tile(0), half_tile(1)]
    while running:
        for gen in list(running):
            if next(gen, running) is running:
                running.remove(gen)


def _merge(l, x, mod, g_norm, ret_o, s5_y, gla_o, na_o, wglu, bglu, wbr, wmg, bmg, wout):
    tm = TOKEN_TILE
    tok = lambda w: pl.BlockSpec((tm, w), lambda i: (i, 0))
    return pl.pallas_call(
        _merge_kernel,
        grid=(T_ALL // tm,),
        in_specs=[tok(D_MODEL), _mod_spec(l), _layer_spec(l, 4, D_MODEL),
                  tok(256), pl.BlockSpec((tm // S5_CHUNK, S5_GROUPS * S5_W), lambda i: (i, 0)), tok(256),
                  pl.BlockSpec((NA_HEADS, tm, NA_DH), lambda i: (0, i, 0)),
                  _layer_spec(l, 256, 512), _layer_spec(l, 1, 512),
                  _layer_spec(l, N_BRANCH, BRANCH_W, D_MODEL, single_buffer=True),
                  _layer_spec(l, D_MODEL, N_BRANCH * D_MODEL, single_buffer=True),
                  _layer_spec(l, 1, N_BRANCH * D_MODEL),
                  _layer_spec(l, D_MODEL, D_MODEL, single_buffer=True)],
        out_specs=tok(D_MODEL),
        out_shape=jax.ShapeDtypeStruct((T_ALL, D_MODEL), F32),
        scratch_shapes=[pltpu.VMEM((S5_GROUPS // S5_LANE_GROUPS, tm, LANES), F32)],
        compiler_params=_cparams(("parallel",)),
        name="merge",
    )(x, mod, g_norm, ret_o, s5_y, gla_o, na_o, wglu, bglu, wbr, wmg, bmg, wout)


FF_TILE = 1024


def _mlp_kernel(x_ref, mod_ref, g_ref, w1_ref, w2_ref, *o_refs):
    mod = mod_ref[0, 0]
    nj = D_FF // FF_TILE
    halves = [slice(0, TOKEN_TILE // 2), slice(TOKEN_TILE // 2, TOKEN_TILE)]
    xs = [x_ref[r, :] for r in halves]
    hbs = [(_rms(x, g_ref[0, 2:3]) * (1.0 + mod[4:5]) + mod[3:4]).astype(BF16) for x in xs]
    up = lambda hb, j: _bdot(hb, w1_ref[0, :, j * FF_TILE:(j + 1) * FF_TILE])
    fs = [None, None]
    pres = [up(hb, 0) for hb in hbs]
    for j in range(nj):
        for i in range(2):
            a = jnp.maximum(pres[i], 0.0)
            part = _bdot(a * a, w2_ref[0, j * FF_TILE:(j + 1) * FF_TILE, :])
            fs[i] = part if fs[i] is None else fs[i] + part
            if j + 1 < nj:
                pres[i] = up(hbs[i], j + 1)
    y = jnp.concatenate([x + mod[5:6] * _rms(f, g_ref[0, 3:4]) for x, f in zip(xs, fs)], axis=0)
    if len(o_refs) == 1:
        o_refs[0][...] = y
    else:
        ctx_tiles = T_CTX // TOKEN_TILE

        @pl.when(pl.program_id(0) < ctx_tiles)
        def _store_ctx():
            o_refs[0][...] = y

        @pl.when(pl.program_id(0) >= ctx_tiles)
        def _store_lat():
            o_refs[1][...] = y


def _mlp(l, x, mod, g_norm, w1, w2, *, split_out):
    tm = TOKEN_TILE
    tok = pl.BlockSpec((tm, D_MODEL), lambda i: (i, 0))
    if split_out:
        ctx_tiles = T_CTX // tm
        out_specs = [pl.BlockSpec((tm, D_MODEL), lambda i: (jnp.minimum(i, ctx_tiles - 1), 0)),
                     pl.BlockSpec((tm, D_MODEL), lambda i: (jnp.maximum(i - ctx_tiles, 0), 0))]
        out_shape = [jax.ShapeDtypeStruct((T_CTX, D_MODEL), F32), jax.ShapeDtypeStruct((T_LAT, D_MODEL), F32)]
    else:
        out_specs = tok
        out_shape = jax.ShapeDtypeStruct((T_ALL, D_MODEL), F32)
    return pl.pallas_call(
        _mlp_kernel,
        grid=(T_ALL // tm,),
        in_specs=[tok, _mod_spec(l), _layer_spec(l, 4, D_MODEL),
                  _layer_spec(l, D_MODEL, D_FF, single_buffer=True),
                  _layer_spec(l, D_FF, D_MODEL, single_buffer=True)],
        out_specs=out_specs,
        out_shape=out_shape,
        compiler_params=_cparams(("arbitrary",)),
        name="mlp",
    )(x, mod, g_norm, w1, w2)


def _rope_tables():
    half = RET_DK // 2
    nf = half // 2
    t = jnp.arange(DEC_SEQ)
    row = (t // GRID_W).astype(F32)
    col = (t % GRID_W).astype(F32)
    inv = ROPE_BASE ** (-jnp.arange(nf, dtype=F32) / nf)
    ang_r = row[:, None] * inv[None, :]
    ang_c = col[:, None] * inv[None, :]
    cos = jnp.concatenate([jnp.cos(ang_r)] * 2 + [jnp.cos(ang_c)] * 2, axis=1)
    sin = jnp.concatenate([-jnp.sin(ang_r), jnp.sin(ang_r), -jnp.sin(ang_c), jnp.sin(ang_c)], axis=1)
    return jnp.tile(cos, (1, RET_HEADS)), jnp.tile(sin, (1, RET_HEADS))


def _gla_state_in(st):
    eye = jnp.eye(GLA_HEADS, dtype=st.dtype)
    t = jnp.einsum('bldhkv,hg->bldhvgk', st, eye)
    return t.reshape(st.shape[0], st.shape[1], 2, GLA_HEADS * GLA_DV, GLA_HEADS * GLA_DK)


def kernel(x_prompt, x_sample, c, cache_na_k, cache_na_v, state_ret, state_s5, state_gla, c_ctx, w_ada, b_ada, g_norm, w_in, ret_log_decay, ret_gn, s5_lambda_re, s5_lambda_im, s5_log_dt, s5_b_re, s5_b_im, s5_c_re, s5_c_im, s5_d, s5_w_glu, s5_b_glu, gla_w_gate, gla_b_gate, gla_gn, na_rpb, w_branch, w_merge, b_merge, w_out, w_mlp1, w_mlp2):
    depth = w_in.shape[0]
    x = (x_prompt.reshape(T_CTX, D_MODEL), x_sample.reshape(T_LAT, D_MODEL))
    cc = jnp.concatenate([c_ctx[None], c, jnp.zeros((N_MOD_ROWS - 1 - DEC_BATCH, D_MODEL), F32)], axis=0)
    mod = _ada(cc, w_ada, b_ada).reshape(depth, N_MOD_ROWS, 6, D_MODEL)

    cos, sin = _rope_tables()
    w_in_t = w_in.transpose(0, 2, 1)
    ret_gn3, gla_gn3 = (a.reshape(depth, 1, BRANCH_W) for a in (ret_gn, gla_gn))
    s5_d_rows = jnp.broadcast_to(s5_d.reshape(depth, S5_GROUPS, 1, S5_GROUP_CH),
                                 (depth, S5_GROUPS, S5_CHUNK, S5_GROUP_CH)).reshape(depth, 1, S5_GROUPS * S5_W)
    b_glu3 = s5_b_glu.reshape(depth, 1, 2 * BRANCH_W)
    b_mg3 = b_merge.reshape(depth, 1, N_BRANCH * D_MODEL)
    cache_k = cache_na_k.transpose(0, 1, 3, 2, 4)
    cache_v = cache_na_v.transpose(0, 1, 3, 2, 4)
    na_tb = _na_tables(na_rpb)
    gla_s0 = _gla_state_in(state_gla)
    s5_tables, s5_lam = _s5_tables(s5_lambda_re, s5_lambda_im, s5_log_dt, s5_b_re, s5_b_im,
                                   s5_c_re, s5_c_im)
    s5_x0 = state_s5.transpose(1, 2, 0, 3, 5, 4).reshape(depth, 2, DEC_BATCH, S5_GROUPS * S5_P2)

    ks_l, vs_l, ret_l, s5_l, gla_l = [], [], [], [], []
    for l in range(depth):
        proj = _inproj(l, x, mod, g_norm, w_in_t)
        if l == 0:
            x, proj = proj[0], proj[1:]
        ret, su, gqk, gv, gg, glr, nq, nk, nv = proj

        ret_o, st_ret = _retention(l, ret, ret_log_decay, ret_gn3, latent=False)
        ret_o, = _retention(l, ret, ret_log_decay, ret_gn3, latent=True, cos=cos, sin=sin, s0=state_ret,
                            prev=ret_o)

        s5_y, s5_fin = _s5(l, su, *s5_tables, s5_d_rows, s5_lam, s5_x0)

        gla_o, st_gla = _gla(l, gqk, gv, gg, glr, gla_w_gate, gla_b_gate, gla_gn3, latent=False)
        gla_o, = _gla(l, gqk, gv, gg, glr, gla_w_gate, gla_b_gate, gla_gn3, latent=True, s0=gla_s0,
                      prev=gla_o)

        na_o = _attn_ctx(nq, nk, nv)
        na_o = _attn_lat(l, nq, nk, nv, cache_k, cache_v, na_tb, na_o)

        x = _merge(l, x, mod, g_norm, ret_o, s5_y, gla_o, na_o,
                   s5_w_glu, b_glu3, w_branch, w_merge, b_mg3, w_out)
        x = _mlp(l, x, mod, g_norm, w_mlp1, w_mlp2, split_out=(l == depth - 1))

        ks_l.append(nk)
        vs_l.append(nv)
        ret_l.append(st_ret)
        s5_l.append(s5_fin)
        gla_l.append(st_gla)

    y_prompt = x[0].reshape(BATCH, SEQ, D_MODEL)
    y_sample = x[1].reshape(DEC_BATCH, DEC_SEQ, D_MODEL)

    def cache_out(per_layer):
        a = jnp.stack(per_layer, axis=0)[:, :, :T_CTX].reshape(depth, NA_HEADS, BATCH, SEQ, NA_DH)
        return a.transpose(2, 0, 3, 1, 4)

    s5_out = jnp.stack(s5_l, axis=0).reshape(depth, 2, BATCH, S5_GROUPS, 2, S5_STATE)
    return (y_prompt, y_sample, cache_out(ks_l), cache_out(vs_l), jnp.stack(ret_l, axis=1),
            s5_out.transpose(2, 0, 1, 3, 5, 4), jnp.stack(gla_l, axis=1))
```

```python
import functools
import math

import numpy as np
import jax
import jax.numpy as jnp
from jax import lax
from jax.experimental import pallas as pl
from jax.experimental.pallas import tpu as pltpu

F32 = jnp.float32
BF16 = jnp.bfloat16

D_MODEL = 1024
BATCH = 16
SEQ = 256
DEPTH = 4
DEC_BATCH = 4
DEC_SEQ = 1024
PAST_LEN = 256
GRID_W = 64
N_BRANCH = 4
BRANCH_W = 256
RET_HEADS = 4
RET_DK = 64
RET_DV = 64
S5_GROUPS = 16
S5_GROUP_CH = 16
S5_STATE = 64
GLA_HEADS = 4
GLA_DK = 32
GLA_DV = 64
GLA_RANK = 16
GLA_TAU = 16.0
NA_HEADS = 4
NA_DH = 64
NA_WIN_H = 8
NA_WIN_W = 16
D_FF = 4 * D_MODEL
ROPE_BASE = 10000.0
EPS = 1e-6

T_CTX = BATCH * SEQ
T_LAT = DEC_BATCH * DEC_SEQ
T_ALL = T_CTX + T_LAT
LAT_BLOCK0 = T_CTX // DEC_SEQ
N_MOD_ROWS = 8
TOKEN_TILE = 512
CTX_SEQS_PER_STEP = DEC_SEQ // SEQ
GLA_CHUNK = 64
GLA_BLOCK_CHUNKS = 4
S5_CHUNK = 16
RET_QBLOCK = 256
NA_QBLOCK = 256
GRID_ROWS = DEC_SEQ // GRID_W
NA_KH = min(NA_WIN_H, GRID_ROWS)
NA_REL_ROWS = 2 * NA_WIN_H - 1
LANES = 128
VMEM_LIMIT = 56 * 1024 * 1024
MLP_VMEM_LIMIT = 60 * 1024 * 1024
IN_RET = 0
IN_S5 = IN_RET + 2 * RET_HEADS * RET_DK + 2 * RET_HEADS * RET_DV
IN_GLA_QK = IN_S5 + S5_GROUPS * S5_GROUP_CH
IN_GLA_V = IN_GLA_QK + 2 * GLA_HEADS * GLA_DK
IN_GLA_G = IN_GLA_V + GLA_HEADS * GLA_DV
IN_TAIL_COL = IN_GLA_G + GLA_HEADS * GLA_DV
IN_NA = IN_TAIL_COL + 2 * GLA_RANK
D_IN = IN_NA + 3 * NA_HEADS * NA_DH


def _cparams(sem, vmem_limit=VMEM_LIMIT):
    return pltpu.CompilerParams(dimension_semantics=sem, vmem_limit_bytes=vmem_limit)


def _bdot(a, b):
    return jnp.dot(a.astype(BF16), b.astype(BF16), preferred_element_type=F32)


def _bdot_nt(a, b):
    return lax.dot_general(a.astype(BF16), b.astype(BF16), (((1,), (1,)), ((), ())),
                           preferred_element_type=F32)


def _bdot_tn(a, b):
    return lax.dot_general(a.astype(BF16), b.astype(BF16), (((0,), (0,)), ((), ())),
                           preferred_element_type=F32)


def _split(a):
    hi = a.astype(BF16)
    lo = (a - hi.astype(F32)).astype(BF16)
    return hi, lo


def _dot3(a, b):
    ah, al = _split(a)
    bh, bl = _split(b)
    d = functools.partial(jnp.dot, preferred_element_type=F32)
    return d(ah, bh) + d(al, bh) + d(ah, bl)


def _sigmoid(x):
    return 0.5 * jnp.tanh(0.5 * x) + 0.5


def _silu(x):
    return x * _sigmoid(x)


def _rms(x, g):
    return x * lax.rsqrt(jnp.mean(x * x, axis=-1, keepdims=True) + EPS) * g


def _head_norm(o, g, width):
    n = o.shape[-1]
    hi = lax.broadcasted_iota(jnp.int32, (n, n), 0) // width
    hj = lax.broadcasted_iota(jnp.int32, (n, n), 1) // width
    avg = jnp.where(hi == hj, 1.0 / width, 0.0).astype(BF16)

    def head_mean(a):
        ah, al = _split(a)
        return (jnp.dot(ah, avg, preferred_element_type=F32) + jnp.dot(al, avg, preferred_element_type=F32))

    xc = o - head_mean(o)
    return xc * lax.rsqrt(head_mean(xc * xc) + EPS) * g


def _mod_row(i):
    ctx_tiles = T_CTX // TOKEN_TILE
    return jnp.where(i < ctx_tiles, 0, 1 + (i - ctx_tiles) // (DEC_SEQ // TOKEN_TILE))


def _mod_spec(l):
    return pl.BlockSpec((1, 1, 6, D_MODEL), lambda i: (l, _mod_row(i), 0, 0))


def _layer_spec(l, *shape, single_buffer=False):
    mode = pl.Buffered(1) if single_buffer else None
    return pl.BlockSpec((1,) + shape, lambda *_: (l,) + (0,) * len(shape), pipeline_mode=mode)


_ANY = pl.BlockSpec(memory_space=pl.ANY)


ADA_TILE = 1536


def _ada_kernel(c_ref, w_ref, b_ref, o_ref):
    a = _silu(c_ref[...])
    o_ref[0] = _bdot(a, w_ref[0]) + b_ref[0]


def _ada(cc, w_ada, b_ada):
    n = 6 * D_MODEL
    return pl.pallas_call(
        _ada_kernel,
        grid=(DEPTH, n // ADA_TILE),
        in_specs=[pl.BlockSpec((N_MOD_ROWS, D_MODEL), lambda l, j: (0, 0)),
                  pl.BlockSpec((1, D_MODEL, ADA_TILE), lambda l, j: (l, 0, j)),
                  pl.BlockSpec((1, 1, ADA_TILE), lambda l, j: (l, 0, j))],
        out_specs=pl.BlockSpec((1, N_MOD_ROWS, ADA_TILE), lambda l, j: (l, 0, j)),
        out_shape=jax.ShapeDtypeStruct((DEPTH, N_MOD_ROWS, n), F32),
        compiler_params=_cparams(("parallel", "parallel")),
        name="ada_mod",
    )(cc, w_ada, b_ada.reshape(DEPTH, 1, n))


def _piece_transpose(blocks, piece):
    x = list(blocks)
    n = len(x)
    d = n // 2
    while d >= 1:
        low = (piece & d) == 0
        for v in range(n):
            if v & d:
                continue
            a, b = x[v], x[v + d]
            x[v] = jnp.where(low, a, pltpu.roll(b, d * S5_GROUP_CH, 1))
            x[v + d] = jnp.where(low, pltpu.roll(a, LANES - d * S5_GROUP_CH, 1), b)
        d //= 2
    return x


def _inproj_kernel(*refs, first):
    if first:
        (xa_ref, xb_ref, mod_ref, g_ref, w_ref, x_out_ref,
         ret_ref, s5_ref, gqk_ref, gv_ref, gg_ref, glr_ref, nq_ref, nk_ref, nv_ref, su_s) = refs
        x = jnp.where(pl.program_id(0) < T_CTX // TOKEN_TILE, xa_ref[...], xb_ref[...])
        x_out_ref[...] = x
    else:
        (x_ref, mod_ref, g_ref, w_ref,
         ret_ref, s5_ref, gqk_ref, gv_ref, gg_ref, glr_ref, nq_ref, nk_ref, nv_ref, su_s) = refs
        x = x_ref[...]
    mod = mod_ref[0, 0]
    h = _rms(x, g_ref[0, 0:1]) * (1.0 + mod[1:2]) + mod[0:1]
    hb = h.astype(BF16)

    def proj(lo, hi):
        return _bdot_nt(hb, w_ref[0, lo:hi, :])

    ret_ref[...] = proj(IN_RET, IN_S5)
    su = proj(IN_S5, IN_GLA_QK)
    rows = TOKEN_TILE // S5_CHUNK
    ng = S5_LANE_GROUPS
    piece = lax.broadcasted_iota(jnp.int32, (rows, LANES), 1) // S5_GROUP_CH
    for lb in range(S5_GROUPS // ng):
        su_s[lb] = su[:, lb * LANES:(lb + 1) * LANES]
        for m in range(S5_CHUNK // ng):
            out = _piece_transpose(
                [su_s[lb, pl.ds(m * ng + jl, rows, stride=S5_CHUNK), :] for jl in range(ng)], piece)
            for g in range(ng):
                col = (lb * ng + g) * S5_W + m * LANES
                s5_ref[:, col:col + LANES] = out[g]
    gqk_ref[...] = proj(IN_GLA_QK, IN_GLA_V)
    gv_ref[...] = proj(IN_GLA_V, IN_GLA_G)
    gg_ref[...] = proj(IN_GLA_G, IN_TAIL_COL)
    glr_ref[...] = proj(IN_TAIL_COL, IN_TAIL_COL + LANES)
    lo = IN_NA
    for ref in (nq_ref, nk_ref, nv_ref):
        r = proj(lo, lo + NA_HEADS * NA_DH)
        for hh in range(NA_HEADS):
            ref[hh] = r[:, hh * NA_DH:(hh + 1) * NA_DH]
        lo += NA_HEADS * NA_DH


def _inproj(l, xs, mod, g_norm, w_in):
    tm = TOKEN_TILE
    first = isinstance(xs, tuple)
    tok = lambda w: pl.BlockSpec((tm, w), lambda i: (i, 0))
    head = pl.BlockSpec((NA_HEADS, tm, NA_DH), lambda i: (0, i, 0))
    tshape = lambda w: jax.ShapeDtypeStruct((T_ALL, w), F32)
    hshape = jax.ShapeDtypeStruct((NA_HEADS, T_ALL, NA_DH), F32)
    ctx_tiles = T_CTX // tm
    if first:
        x_specs = [pl.BlockSpec((tm, D_MODEL), lambda i: (jnp.minimum(i, ctx_tiles - 1), 0)),
                   pl.BlockSpec((tm, D_MODEL), lambda i: (jnp.maximum(i - ctx_tiles, 0), 0))]
        x_args = list(xs)
    else:
        x_specs, x_args = [tok(D_MODEL)], [xs]
    return pl.pallas_call(
        functools.partial(_inproj_kernel, first=first),
        grid=(T_ALL // tm,),
        in_specs=x_specs + [_mod_spec(l), _layer_spec(l, 4, D_MODEL),
                            _layer_spec(l, D_IN, D_MODEL, single_buffer=True)],
        out_specs=([tok(D_MODEL)] if first else [])
        + [tok(1024), pl.BlockSpec((tm // S5_CHUNK, S5_GROUPS * S5_W), lambda i: (i, 0)),
           tok(256), tok(256), tok(256), tok(LANES), head, head, head],
        out_shape=([tshape(D_MODEL)] if first else [])
        + [tshape(1024), jax.ShapeDtypeStruct((S5_ROWS, S5_GROUPS * S5_W), F32),
           tshape(256), tshape(256), tshape(256), tshape(LANES), hshape, hshape, hshape],
        scratch_shapes=[pltpu.VMEM((S5_GROUPS // S5_LANE_GROUPS, tm, LANES), F32)],
        compiler_params=_cparams(("parallel",)),
        name="in_proj",
    )(*x_args, mod, g_norm, w_in)


def _rope_rotate(x, lane):
    first = (lane % 32) < 16
    w = x.shape[-1]
    return jnp.where(first, pltpu.roll(x, w - 16, 1), pltpu.roll(x, 16, 1))


def _ret_kernel(ld_ref, ret_ref, gn_ref, *rest, layer, seq, latent):
    if latent:
        cos_ref, sin_ref, s0_ref, _, out_ref, dec_s = rest
    else:
        out_ref, st_ref, dec_s = rest
    tq = RET_QBLOCK
    nq = seq // tq
    width = dec_s.shape[-1]

    @pl.when(pl.program_id(0) == 0)
    def _build_decay():
        rel = (lax.broadcasted_iota(jnp.int32, (tq, width), 0) + (nq - 1) * tq
               - lax.broadcasted_iota(jnp.int32, (tq, width), 1)).astype(F32)
        for h in range(RET_HEADS):
            dec_s[h] = (jnp.where(rel >= 0, jnp.exp(ld_ref[layer, 0, h] * jnp.maximum(rel, 0.0)), 0.0)
                        + jnp.where(rel <= 0, jnp.exp(ld_ref[layer, 1, h] * jnp.maximum(-rel, 0.0)), 0.0))

    nrows = ret_ref.shape[0]
    nsub = nrows // seq
    q = ret_ref[:, 0:256]
    k = ret_ref[:, 256:512]
    if latent:
        lane = lax.broadcasted_iota(jnp.int32, (nrows, 256), 1)
        cos = cos_ref[...]
        sin = sin_ref[...]
        q = q * cos + _rope_rotate(q, lane) * sin
        k = k * cos + _rope_rotate(k, lane) * sin
    k = k * (RET_DK ** -0.5)
    pos_c = lax.broadcasted_iota(jnp.int32, (seq, 1), 0).astype(F32)
    tiles = [(s, qb) for s in range(nsub) for qb in range(nq)]
    for h in range(RET_HEADS):
        lgf = ld_ref[layer, 0, h]
        lgb = ld_ref[layer, 1, h]
        sl = slice(h * RET_DK, (h + 1) * RET_DK)
        qh = q[:, sl]
        kh = k[:, sl]
        kb = kh.astype(BF16)
        vb = ret_ref[:, 512 + h * RET_DV:512 + (h + 1) * RET_DV].astype(BF16)
        if latent:
            q_init = jnp.concatenate([qh * jnp.exp(lgf * (pos_c + 1.0)),
                                      qh * jnp.exp(lgb * (seq - pos_c))], axis=1)
            s_init = jnp.concatenate([s0_ref[0, 0, 0, h], s0_ref[0, 0, 1, h]], axis=0)

        def score(s, qb):
            w0 = (nq - 1 - qb) * tq
            rows = slice(s * seq + qb * tq, s * seq + (qb + 1) * tq)
            keys = slice(s * seq, (s + 1) * seq)
            return (_bdot_nt(qh[rows], kb[keys]) * dec_s[h, :, w0:w0 + seq]).astype(BF16)

        def values(sc, s, qb):
            o = jnp.dot(sc, vb[s * seq:(s + 1) * seq], preferred_element_type=F32)
            if latent:
                o = o + _bdot(q_init[qb * tq:(qb + 1) * tq], s_init)
            return o

        def finish(o, s, qb):
            out_ref[s * seq + qb * tq:s * seq + (qb + 1) * tq, sl] = o

        if nsub > 1:
            scores = [score(s, qb) for s, qb in tiles]
            outs = [values(sc, s, qb) for sc, (s, qb) in zip(scores, tiles)]
            for o, (s, qb) in zip(outs, tiles):
                finish(o, s, qb)
        else:
            for s, qb in tiles:
                finish(values(score(s, qb), s, qb), s, qb)
        if not latent:
            for s in range(nsub):
                keys = slice(s * seq, (s + 1) * seq)
                st_ref[s, 0, h] = _bdot_tn(kh[keys] * jnp.exp(lgf * (seq - 1.0 - pos_c)), vb[keys])
                st_ref[s, 1, h] = _bdot_tn(kh[keys] * jnp.exp(lgb * pos_c), vb[keys])
    for r0 in range(0, nrows, tq):
        rows = slice(r0, r0 + tq)
        out_ref[rows, :] = _head_norm(out_ref[rows, :], gn_ref[0], RET_DV) * _silu(ret_ref[rows, 768:1024])


def _retention(l, ret, ld, gn, *, latent, cos=None, sin=None, s0=None, prev=None):
    seq = DEC_SEQ if latent else SEQ
    nsub = 1 if latent else CTX_SEQS_PER_STEP
    nb = DEC_BATCH if latent else BATCH // nsub
    off = LAT_BLOCK0 if latent else 0
    rows = nsub * seq
    in_specs = [pl.BlockSpec(memory_space=pltpu.SMEM),
                pl.BlockSpec((rows, 1024), lambda b: (b + off, 0)),
                _layer_spec(l, 1, 256)]
    args = [ld, ret, gn]
    out_specs = [pl.BlockSpec((rows, 256), lambda b: (b + off, 0))]
    out_shape = [jax.ShapeDtypeStruct((T_ALL, 256), F32)]
    aliases = {}
    if latent:
        in_specs += [pl.BlockSpec((seq, 256), lambda b: (0, 0)),
                     pl.BlockSpec((seq, 256), lambda b: (0, 0)),
                     pl.BlockSpec((1, 1, 2, RET_HEADS, RET_DK, RET_DV), lambda b: (b, l, 0, 0, 0, 0)),
                     _ANY]
        args += [cos, sin, s0, prev]
        aliases = {6: 0}
    else:
        out_specs.append(pl.BlockSpec((nsub, 2, RET_HEADS, RET_DK, RET_DV), lambda b: (b, 0, 0, 0, 0)))
        out_shape.append(jax.ShapeDtypeStruct((BATCH, 2, RET_HEADS, RET_DK, RET_DV), F32))
    return pl.pallas_call(
        functools.partial(_ret_kernel, layer=l, seq=seq, latent=latent),
        grid=(nb,),
        in_specs=in_specs, out_specs=out_specs, out_shape=out_shape,
        input_output_aliases=aliases,
        scratch_shapes=[pltpu.VMEM((RET_HEADS, RET_QBLOCK, 2 * seq - RET_QBLOCK), F32)],
        compiler_params=_cparams(("arbitrary",)),
        name="retention_lat" if latent else "retention_ctx",
    )(*args)


def _gla_kernel(gqk_ref, gv_ref, gg_ref, glr_ref, wg_ref, bg_ref, gn_ref, *rest, seq, latent):
    if latent:
        s0_ref, _, out_ref, gate_s, o_s, st_s, qst_s, ds_s, e_s = rest
    else:
        out_ref, st_ref, gate_s, o_s, st_s, qst_s, ds_s, e_s = rest
    c = GLA_CHUNK
    n = seq // c
    nsub = gqk_ref.shape[0] // seq
    hk = GLA_HEADS * GLA_DK
    lr = glr_ref[...]
    for d in range(2):
        pre = _bdot(lr[:, d * GLA_RANK:(d + 1) * GLA_RANK], wg_ref[0, d]) + bg_ref[0, d:d + 1]
        gate_s[d] = (jnp.minimum(pre, 0.0) - jnp.log(1.0 + jnp.exp(-jnp.abs(pre)))) / GLA_TAU
        for s in range(nsub):
            st_s[2 * s + d] = s0_ref[0, 0, d] if latent else jnp.zeros((GLA_HEADS * GLA_DV, hk), F32)

    nc = GLA_BLOCK_CHUNKS
    rb = nc * c
    ti = lax.broadcasted_iota(jnp.int32, (rb, rb), 0)
    tj = lax.broadcasted_iota(jnp.int32, (rb, rb), 1)
    same = (ti // c) == (tj // c)
    ones = lambda m: (same & m).astype(BF16)
    tri = [ones(tj <= ti), ones(tj >= ti)]
    mid = [ones((tj % c) < c // 2), ones((tj % c) >= c // 2)]
    tot = ones(tj == tj)
    lane_k = lax.broadcasted_iota(jnp.int32, (c, hk), 1)
    head_mask = [(lane_k // GLA_DK) == h for h in range(GLA_HEADS)]
    ai = lax.broadcasted_iota(jnp.int32, (GLA_HEADS * c, c), 0) % c
    aj = lax.broadcasted_iota(jnp.int32, (GLA_HEADS * c, c), 1)
    keep = [aj <= ai, aj >= ai]
    sr = lax.broadcasted_iota(jnp.int32, (GLA_HEADS * GLA_DV, hk), 0) // GLA_DV
    sc = lax.broadcasted_iota(jnp.int32, (GLA_HEADS * GLA_DV, hk), 1) // GLA_DK
    diag = sr == sc
    scale = GLA_DK ** -0.5
    d32 = functools.partial(jnp.dot, preferred_element_type=F32)

    def rows_of(i, size):
        return pl.ds(i * size, size) if isinstance(i, int) else pl.ds(pl.multiple_of(i * size, size), size)

    def local(block_ids):
        vs, q_att, k_att, k_st = {}, {}, {}, {}
        for bi in block_ids:
            rows = rows_of(bi, rb)
            q = gqk_ref[rows, 0:hk] * scale
            k = gqk_ref[rows, hk:2 * hk]
            vs[bi] = gv_ref[rows, :].astype(BF16)
            for d in range(2):
                parts = jnp.concatenate(_split(gate_s[d, rows, :]), axis=1)
                sums = lambda m: (lambda r: r[:, 0:hk] + r[:, hk:2 * hk])(d32(m, parts))
                b, b_mid, b_end = sums(tri[d]), sums(mid[d]), sums(tot)
                q_att[bi, d] = q * jnp.exp(b - b_mid)
                k_att[bi, d] = k * jnp.exp(b_mid - b)
                k_st[bi, d] = (k * jnp.exp(b_end - b)).astype(BF16)
                qst_s[d, rows, :] = (q * jnp.exp(b)).astype(BF16)
                decay = jnp.exp(b_end)
                for cc in range(nc):
                    e_s[d, bi * nc + cc] = decay[cc * c:cc * c + 8]
        tiles = [(bi, d, cc) for bi in block_ids for d in range(2) for cc in range(nc)]
        att = {}
        for bi, d, cc in tiles:
            r = slice(cc * c, (cc + 1) * c)
            qa = q_att[bi, d][r]
            q_stack = jnp.concatenate([jnp.where(head_mask[h], qa, 0.0) for h in range(GLA_HEADS)], axis=0)
            att[bi, d, cc] = jnp.where(keep[d], _bdot_nt(q_stack, k_att[bi, d][r]), 0.0).astype(BF16)
        for bi, d, cc in tiles:
            r = slice(cc * c, (cc + 1) * c)
            o = jnp.concatenate([d32(att[bi, d, cc][h * c:(h + 1) * c], vs[bi][r, h * GLA_DV:(h + 1) * GLA_DV])
                                 for h in range(GLA_HEADS)], axis=1)
            o_s[d, rows_of(bi * nc + cc, c), :] = o
        for bi, d, cc in tiles:
            r = slice(cc * c, (cc + 1) * c)
            ds_s[d, bi * nc + cc] = jnp.where(diag, lax.dot_general(
                vs[bi][r], k_st[bi, d][r], (((0,), (0,)), ((), ())), preferred_element_type=F32), 0.0)

    def recur(s, ci, d):
        g = s * n + ci
        rows = rows_of(g, c)
        st = st_s[2 * s + d]
        o_s[d, rows, :] += _bdot_nt(qst_s[d, rows, :], st)
        st_s[2 * s + d] = st * e_s[d, g, 0:1] + ds_s[d, g]

    def recur_body(i, carry):
        for s in range(nsub):
            recur(s, i, 0)
            recur(s, n - 1 - i, 1)
        return carry

    local(list(range(nsub * n // nc)))
    if n <= 4:
        for i in range(n):
            recur_body(i, 0)
    else:
        lax.fori_loop(0, n, recur_body, 0, unroll=2)

    for r0 in range(0, nsub * seq, rb):
        rows = slice(r0, r0 + rb)
        out_ref[rows, :] = (_head_norm(o_s[0, rows, :] + o_s[1, rows, :], gn_ref[0], GLA_DV)
                            * _silu(gg_ref[rows, :]))
    if not latent:
        hv = GLA_HEADS * GLA_DV
        eye = (lax.broadcasted_iota(jnp.int32, (hv, hv), 0)
               == lax.broadcasted_iota(jnp.int32, (hv, hv), 1)).astype(BF16)
        tn = lambda a: lax.dot_general(a, eye, (((0,), (0,)), ((), ())), preferred_element_type=F32)
        for s in range(nsub):
            for d in range(2):
                st = st_s[2 * s + d]
                hi, lo = _split(st)
                lo2 = (st - hi.astype(F32) - lo.astype(F32)).astype(BF16)
                s_all = tn(hi) + tn(lo) + tn(lo2)
                for h in range(GLA_HEADS):
                    st_ref[s, d, h] = s_all[h * GLA_DK:(h + 1) * GLA_DK, h * GLA_DV:(h + 1) * GLA_DV]


def _gla(l, gqk, gv, gg, glr, wg, bg, gn, *, latent, s0=None, prev=None):
    seq = DEC_SEQ if latent else SEQ
    nsub = 1 if latent else CTX_SEQS_PER_STEP
    nb = DEC_BATCH if latent else BATCH // nsub
    off = LAT_BLOCK0 if latent else 0
    rows = nsub * seq
    hk = GLA_HEADS * GLA_DK
    hv = GLA_HEADS * GLA_DV
    tok = lambda w: pl.BlockSpec((rows, w), lambda b: (b + off, 0))
    in_specs = [tok(256), tok(256), tok(256), tok(LANES),
                _layer_spec(l, 2, GLA_RANK, hk), _layer_spec(l, 2, hk), _layer_spec(l, 1, 256)]
    args = [gqk, gv, gg, glr, wg, bg, gn]
    out_specs = [tok(256)]
    out_shape = [jax.ShapeDtypeStruct((T_ALL, 256), F32)]
    aliases = {}
    if latent:
        in_specs += [pl.BlockSpec((1, 1, 2, hv, hk), lambda b: (b, l, 0, 0, 0)), _ANY]
        args += [s0, prev]
        aliases = {8: 0}
    else:
        out_specs.append(pl.BlockSpec((nsub, 2, GLA_HEADS, GLA_DK, GLA_DV), lambda b: (b, 0, 0, 0, 0)))
        out_shape.append(jax.ShapeDtypeStruct((BATCH, 2, GLA_HEADS, GLA_DK, GLA_DV), F32))
    return pl.pallas_call(
        functools.partial(_gla_kernel, seq=seq, latent=latent),
        grid=(nb,),
        in_specs=in_specs, out_specs=out_specs, out_shape=out_shape,
        input_output_aliases=aliases,
        scratch_shapes=[pltpu.VMEM((2, rows, hk), F32), pltpu.VMEM((2, rows, hv), F32),
                        pltpu.VMEM((2 * nsub, hv, hk), F32), pltpu.VMEM((2, rows, hk), BF16),
                        pltpu.VMEM((2, rows // GLA_CHUNK, hv, hk), F32),
                        pltpu.VMEM((2, rows // GLA_CHUNK, 8, hk), F32)],
        compiler_params=_cparams(("parallel",)),
        name="gla_lat" if latent else "gla_ctx",
    )(*args)


S5_W = S5_CHUNK * S5_GROUP_CH
S5_P2 = 2 * S5_STATE
S5_ROWS = T_ALL // S5_CHUNK
S5_ROWS_CTX = T_CTX // S5_CHUNK
S5_LANE_GROUPS = LANES // S5_GROUP_CH


S5_TE_ROWS = S5_W + 4 * S5_P2


def _s5_toeplitz_kernel(cc_ref, pwr_ref, pwi_ref, bbr_ref, bbi_ref, ctr_ref, cti_ref, te_ref, ffb_ref):
    lane = lax.broadcasted_iota(jnp.int32, (S5_GROUP_CH, S5_W), 1)
    swapped = lambda t: jnp.concatenate([t[S5_STATE:], t[:S5_STATE]], axis=0)
    for t in range(S5_TABLE_GROUPS):
        def times_b(v, d):
            pr, pi, br, bi = pwr_ref[0, t, v], pwi_ref[0, t, v], bbr_ref[0, d, t], bbi_ref[0, d, t]
            return jnp.concatenate([pr * br - pi * bi, pr * bi + pi * br], axis=0)

        def times_c(v, d):
            pr, pi, cr, ci = pwr_ref[0, t, v], pwi_ref[0, t, v], ctr_ref[0, d, t], cti_ref[0, d, t]
            return jnp.concatenate([cr * pr - ci * pi, -(cr * pi + ci * pr)], axis=0)

        wfr = times_b(0, 0)
        wb = times_b(1, 1)
        kf = _dot3(cc_ref[0, 0, t], wfr)
        kb = _dot3(cc_ref[0, 1, t], wb)
        blocks = []
        for i in range(S5_CHUNK):
            sf = ((i + 1 - S5_CHUNK) * S5_GROUP_CH) % S5_W
            fwd = jnp.where(lane < (i + 1) * S5_GROUP_CH, pltpu.roll(kf, sf, 1) if sf else kf, 0.0)
            bwd = jnp.where(lane >= i * S5_GROUP_CH, pltpu.roll(kb, i * S5_GROUP_CH, 1) if i else kb, 0.0)
            blocks.append(fwd + bwd)
        te_ref[0, t] = jnp.concatenate(blocks + [wfr, wb, swapped(wfr), swapped(wb)], axis=0).astype(BF16)
        ffb_ref[0, t] = jnp.concatenate([times_c(2, 0), times_c(3, 1)], axis=0).astype(BF16)


S5_TABLE_GROUPS = 4


def _s5_toeplitz(cc, pwr, pwi, bbr, bbi, ctr, cti):
    tg = S5_TABLE_GROUPS
    nl, _, ngroups = cc.shape[:3]
    by_dir = lambda *tail: pl.BlockSpec((1, 2, tg) + tail, lambda l, i: (l, 0, i) + (0,) * len(tail))
    by_group = lambda *tail: pl.BlockSpec((1, tg) + tail, lambda l, i: (l, i) + (0,) * len(tail))
    lanes = by_dir(S5_STATE, S5_W)
    powers = by_group(4, S5_STATE, S5_W)
    return pl.pallas_call(
        _s5_toeplitz_kernel,
        grid=(nl, ngroups // tg),
        in_specs=[by_dir(S5_GROUP_CH, S5_P2), powers, powers, lanes, lanes, lanes, lanes],
        out_specs=[by_group(S5_TE_ROWS, S5_W), by_group(2 * S5_P2, S5_W)],
        out_shape=[jax.ShapeDtypeStruct((nl, ngroups, S5_TE_ROWS, S5_W), BF16),
                   jax.ShapeDtypeStruct((nl, ngroups, 2 * S5_P2, S5_W), BF16)],
        compiler_params=_cparams(("parallel", "parallel")),
        name="s5_toeplitz",
    )(cc, pwr, pwi, bbr, bbi, ctr, cti)


def _s5_kernel(u_ref, te_ref, ff_ref, d_ref, lam_ref, x0_ref, y_ref, fin_ref, xs_s, ps_s):
    C = S5_CHUNK
    ng = S5_LANE_GROUPS

    for g in range(ng):
        u = u_ref[:, g * S5_W:(g + 1) * S5_W]
        r = _bdot_nt(u, te_ref[0, g])
        y_ref[:, g * S5_W:(g + 1) * S5_W] = r[:, 0:S5_W] + d_ref[0, :, g * S5_W:(g + 1) * S5_W] * u
        for t in range(4):
            xs_s[t * ng + g] = r[:, S5_W + t * S5_P2:S5_W + (t + 1) * S5_P2]

    w = ng * S5_P2

    def carry(base, nseq, nchunks, init_f, init_b):
        a_f, b_f = lam_ref[0, 0, 0:1], lam_ref[0, 0, 1:2]
        a_b, b_b = lam_ref[0, 1, 0:1], lam_ref[0, 1, 1:2]
        load = lambda t, rows: jnp.concatenate([xs_s[t * ng + g, rows, :] for g in range(ng)], axis=1)

        def body(i, st):
            sf, tf, sb, tb = st
            rf = pl.ds(base + i, nseq, stride=nchunks)
            rb = pl.ds(base + (nchunks - 1 - i), nseq, stride=nchunks)
            for g in range(ng):
                ps_s[g, rf, :] = sf[:, g * S5_P2:(g + 1) * S5_P2]
                ps_s[ng + g, rb, :] = sb[:, g * S5_P2:(g + 1) * S5_P2]
            return (a_f * sf + b_f * tf + load(0, rf), a_f * tf - b_f * sf + load(2, rf),
                    a_b * sb + b_b * tb + load(1, rb), a_b * tb - b_b * sb + load(3, rb))

        first = (lax.broadcasted_iota(jnp.int32, (1, w), 1) % S5_P2) < S5_STATE
        swap = lambda s: jnp.where(first, pltpu.roll(s, w - S5_STATE, 1), pltpu.roll(s, S5_STATE, 1))
        return lax.fori_loop(0, nchunks, body, (init_f, swap(init_f), init_b, swap(init_b)))

    zeros = jnp.zeros((BATCH, w), F32)
    fin = carry(0, BATCH, SEQ // C, zeros, zeros)
    fin_ref[0] = fin[0]
    fin_ref[1] = fin[2]
    carry(S5_ROWS_CTX, DEC_BATCH, DEC_SEQ // C, x0_ref[0, 0], x0_ref[0, 1])

    for g in range(ng):
        p = jnp.concatenate([ps_s[g], ps_s[ng + g]], axis=1)
        y_ref[:, g * S5_W:(g + 1) * S5_W] += _bdot(p, ff_ref[0, g])


def _s5(l, su_rows, te, ff, d_rows, lam, x0):
    ng = S5_LANE_GROUPS
    w = ng * S5_P2
    rows = pl.BlockSpec((S5_ROWS, ng * S5_W), lambda i: (0, i))
    return pl.pallas_call(
        _s5_kernel,
        grid=(S5_GROUPS // ng,),
        in_specs=[rows,
                  pl.BlockSpec((1, ng, S5_TE_ROWS, S5_W), lambda i: (l, i, 0, 0)),
                  pl.BlockSpec((1, ng, 2 * S5_P2, S5_W), lambda i: (l, i, 0, 0)),
                  pl.BlockSpec((1, 1, ng * S5_W), lambda i: (l, 0, i)),
                  pl.BlockSpec((1, 2, 2, w), lambda i: (l, 0, 0, i)),
                  pl.BlockSpec((1, 2, DEC_BATCH, w), lambda i: (l, 0, 0, i))],
        out_specs=[rows, pl.BlockSpec((2, BATCH, w), lambda i: (0, 0, i))],
        out_shape=[jax.ShapeDtypeStruct((S5_ROWS, S5_GROUPS * S5_W), F32),
                   jax.ShapeDtypeStruct((2, BATCH, S5_GROUPS * S5_P2), F32)],
        scratch_shapes=[pltpu.VMEM((4 * ng, S5_ROWS, S5_P2), F32), pltpu.VMEM((2 * ng, S5_ROWS, S5_P2), F32)],
        compiler_params=_cparams(("parallel",)),
        name="s5_scan",
    )(su_rows, te, ff, d_rows, lam, x0)


def _s5_tables(lam_re, lam_im, log_dt, b_re, b_im, c_re, c_im):
    C, G, P, H = S5_CHUNK, S5_GROUPS, S5_STATE, S5_GROUP_CH
    L = lam_re.shape[0]
    dt = jnp.exp(log_dt)[..., None]
    ar, ai = lam_re * dt, lam_im * dt

    steps = jnp.arange(C + 1, dtype=F32)
    mag = jnp.exp(ar[..., None] * steps)
    pw_re, pw_im = mag * jnp.cos(ai[..., None] * steps), mag * jnp.sin(ai[..., None] * steps)
    exact = functools.partial(jnp.einsum, precision=lax.Precision.HIGHEST)
    tau_np = np.arange(C * H) // H

    patterns = [(0, (C - 1) - tau_np), (1, tau_np), (0, tau_np + 1), (1, C - tau_np)]
    sel = np.zeros((2, len(patterns), C + 1, C * H), np.float32)
    for v, (d, t) in enumerate(patterns):
        sel[d, v, t, np.arange(C * H)] = 1.0
    sel = jnp.asarray(sel)
    pwr = exact('ldgpt,dvtn->lgvpn', pw_re, sel)
    pwi = exact('ldgpt,dvtn->lgvpn', pw_im, sel)

    lr, li = pw_re[..., 1], pw_im[..., 1]
    den = lam_re * lam_re + lam_im * lam_im
    qr = ((lr - 1.0) * lam_re + li * lam_im) / den
    qi = (li * lam_re - (lr - 1.0) * lam_im) / den
    bbr = qr[..., None] * b_re - qi[..., None] * b_im
    bbi = qr[..., None] * b_im + qi[..., None] * b_re
    chan = jnp.asarray((np.arange(H)[:, None] == (np.arange(C * H) % H)[None, :]).astype(np.float32))
    lanes = lambda a: exact('ldgph,hn->ldgpn', a, chan)
    bbr, bbi = lanes(bbr), lanes(bbi)
    c_t = lambda a: exact('ldghp,hn->ldgpn', a, chan)
    ctr, cti = c_t(c_re), c_t(c_im)

    cc = jnp.concatenate([c_re, -c_im], axis=-1)
    tables = _s5_toeplitz(cc, pwr, pwi, bbr, bbi, ctr, cti)
    cr, ci = pw_re[..., C], pw_im[..., C]
    a = jnp.concatenate([cr, cr], axis=-1).reshape(L, 2, 1, G * 2 * P)
    b = jnp.concatenate([-ci, ci], axis=-1).reshape(L, 2, 1, G * 2 * P)
    return tables, jnp.concatenate([a, b], axis=2)


def _softmax_pv(s_parts, v_parts):
    m = s_parts[0].max(axis=-1, keepdims=True)
    for s in s_parts[1:]:
        m = jnp.maximum(m, s.max(axis=-1, keepdims=True))
    o = None
    l = None
    for s, v in zip(s_parts, v_parts):
        p = jnp.exp(s - m)
        pl_ = p.sum(axis=-1, keepdims=True)
        po = _bdot(p, v)
        o = po if o is None else o + po
        l = pl_ if l is None else l + pl_
    return o / l


def _attn_ctx_kernel(q_ref, k_ref, v_ref, o_ref):
    scale = NA_DH ** -0.5
    tiles = [(h, slice(s * SEQ, (s + 1) * SEQ)) for h in range(NA_HEADS) for s in range(CTX_SEQS_PER_STEP)]
    scores = [_bdot_nt(q_ref[h, rows, :], k_ref[h, rows, :]) * scale for h, rows in tiles]
    for s, (h, rows) in zip(scores, tiles):
        o_ref[h, rows, :] = _softmax_pv([s], [v_ref[h, rows, :]])


def _attn_ctx(nq, nk, nv):
    spec = pl.BlockSpec((NA_HEADS, CTX_SEQS_PER_STEP * SEQ, NA_DH), lambda b: (0, b, 0))
    return pl.pallas_call(
        _attn_ctx_kernel,
        grid=(BATCH // CTX_SEQS_PER_STEP,),
        in_specs=[spec, spec, spec],
        out_specs=spec,
        out_shape=jax.ShapeDtypeStruct((NA_HEADS, T_ALL, NA_DH), F32),
        compiler_params=_cparams(("parallel",)),
        name="attn_ctx",
    )(nq, nk, nv)


def _attn_lat_kernel(q_ref, k_ref, v_ref, kc_ref, vc_ref, tb_ref, _, o_ref, bias_s):
    @pl.when(pl.program_id(1) == 0)
    def _build_bias():
        bias_s[...] = jnp.full((DEC_SEQ, DEC_SEQ), -jnp.inf, F32)
        for r in range(GRID_ROWS):
            rs = min(max(r - NA_KH // 2, 0), GRID_ROWS - NA_KH)
            dr0 = rs - r + NA_WIN_H - 1
            bias_s[r * GRID_W:(r + 1) * GRID_W, rs * GRID_W:(rs + NA_KH) * GRID_W] = (
                tb_ref[0, 0, :, dr0 * GRID_W:(dr0 + NA_KH) * GRID_W])

    scale = NA_DH ** -0.5
    kb = k_ref[0].astype(BF16)
    vb = v_ref[0].astype(BF16)
    kc = kc_ref[0, 0, 0].astype(BF16)
    vc = vc_ref[0, 0, 0].astype(BF16)
    tq = NA_QBLOCK
    q_rows = tq // GRID_W
    tiles = []
    for qb in range(DEC_SEQ // tq):
        rows = slice(qb * tq, (qb + 1) * tq)
        starts = [min(max(r - NA_KH // 2, 0), GRID_ROWS - NA_KH) for r in range(qb * q_rows, (qb + 1) * q_rows)]
        keys = slice(min(starts) * GRID_W // LANES * LANES,
                     -(-(max(starts) + NA_KH) * GRID_W // LANES) * LANES)
        tiles.append((rows, keys))
    scores = []
    for rows, keys in tiles:
        qh = q_ref[0, rows, :].astype(BF16)
        scores.append((_bdot_nt(qh, kb[keys]) * scale + bias_s[rows, keys], _bdot_nt(qh, kc) * scale))
    for (s_loc, s_ctx), (rows, keys) in zip(scores, tiles):
        o_ref[0, rows, :] = _softmax_pv([s_loc, s_ctx], [vb[keys], vc])


def _attn_lat(l, nq, nk, nv, kc, vc, tb, prev):
    tok = pl.BlockSpec((1, DEC_SEQ, NA_DH), lambda h, b: (h, b + LAT_BLOCK0, 0))
    cache = pl.BlockSpec((1, 1, 1, PAST_LEN, NA_DH), lambda h, b: (b, l, h, 0, 0))
    return pl.pallas_call(
        _attn_lat_kernel,
        grid=(NA_HEADS, DEC_BATCH),
        in_specs=[tok, tok, tok, cache, cache,
                  pl.BlockSpec((1, 1, GRID_W, NA_REL_ROWS * GRID_W), lambda h, b: (l, h, 0, 0)),
                  _ANY],
        out_specs=tok,
        out_shape=jax.ShapeDtypeStruct((NA_HEADS, T_ALL, NA_DH), F32),
        input_output_aliases={6: 0},
        scratch_shapes=[pltpu.VMEM((DEC_SEQ, DEC_SEQ), F32)],
        compiler_params=_cparams(("arbitrary", "arbitrary")),
        name="attn_lat",
    )(nq, nk, nv, kc, vc, tb, prev)


def _na_tables(rpb):
    col = np.arange(GRID_W)
    col_start = np.clip(col - NA_WIN_W // 2, 0, GRID_W - NA_WIN_W)
    col_in = (col[None, :] >= col_start[:, None]) & (col[None, :] < col_start[:, None] + NA_WIN_W)
    col_idx = np.clip(col[None, :] - col[:, None] + NA_WIN_W - 1, 0, 2 * NA_WIN_W - 2)
    onehot = (col_idx[:, :, None] == np.arange(2 * NA_WIN_W - 1)[None, None, :]).astype(np.float32)
    tb = jnp.einsum('lhrd,qkd->lhqrk', rpb, jnp.asarray(onehot), precision=lax.Precision.HIGHEST)
    tb = jnp.where(jnp.asarray(col_in)[None, None, :, None, :], tb, -jnp.inf)
    return tb.reshape(rpb.shape[0], NA_HEADS, GRID_W, NA_REL_ROWS * GRID_W)


def _merge_kernel(x_ref, mod_ref, g_ref, ret_ref, s5y_ref, gla_ref, na_ref,
                  wglu_ref, bglu_ref, wbr_ref, wmg_ref, bmg_ref, wout_ref, o_ref, y_s):
    mod = mod_ref[0, 0]
    ng = S5_LANE_GROUPS
    half = TOKEN_TILE // 2
    crows = half // S5_CHUNK
    piece = lax.broadcasted_iota(jnp.int32, (crows, LANES), 1) // S5_GROUP_CH

    def half_tile(k):
        rows = slice(k * half, (k + 1) * half)
        x = x_ref[rows, :]
        hb = (_rms(x, g_ref[0, 0:1]) * (1.0 + mod[1:2]) + mod[0:1]).astype(BF16)
        yield

        def gate_pre(n):
            return (_bdot(hb, wmg_ref[0, :, n * D_MODEL:(n + 1) * D_MODEL])
                    + bmg_ref[0, :, n * D_MODEL:(n + 1) * D_MODEL])

        acc = _sigmoid(gate_pre(0)) * _bdot(ret_ref[rows, :], wbr_ref[0, 0])
        yield
        acc += _sigmoid(gate_pre(2)) * _bdot(gla_ref[rows, :], wbr_ref[0, 2])
        yield
        na = jnp.concatenate([na_ref[hh, rows, :].astype(BF16) for hh in range(NA_HEADS)], axis=1)
        acc += _sigmoid(gate_pre(3)) * _bdot(na, wbr_ref[0, 3])
        yield
        s5_gate = gate_pre(1)
        for lb in range(S5_GROUPS // ng):
            for m in range(S5_CHUNK // ng):
                cols = [(lb * ng + g) * S5_W + m * LANES for g in range(ng)]
                out = _piece_transpose([s5y_ref[k * crows:(k + 1) * crows, c0:c0 + LANES] for c0 in cols], piece)
                for il in range(ng):
                    y_s[lb, pl.ds(k * half + m * ng + il, crows, stride=S5_CHUNK), :] = out[il]
        y = jnp.concatenate([y_s[lb, rows, :] for lb in range(S5_GROUPS // ng)], axis=1)
        y = 0.5 * y * (1.0 + jnp.tanh(math.sqrt(2.0 / math.pi) * (y + 0.044715 * (y * y * y))))
        z = _bdot(y, wglu_ref[0]) + bglu_ref[0]
        yield
        s5_out = z[:, 0:BRANCH_W] * _sigmoid(z[:, BRANCH_W:2 * BRANCH_W])
        acc += _sigmoid(s5_gate) * _bdot(s5_out, wbr_ref[0, 1])
        yield
        m = _bdot(acc, wout_ref[0])
        yield
        o_ref[rows, :] = x + mod[2:3] * _rms(m, g_ref[0, 1:2])

    running = [half_tile(0), half_tile(1)]
    while running:
        for gen in list(running):
            if next(gen, running) is running:
                running.remove(gen)


def _merge(l, x, mod, g_norm, ret_o, s5_y, gla_o, na_o, wglu, bglu, wbr, wmg, bmg, wout):
    tm = TOKEN_TILE
    tok = lambda w: pl.BlockSpec((tm, w), lambda i: (i, 0))
    return pl.pallas_call(
        _merge_kernel,
        grid=(T_ALL // tm,),
        in_specs=[tok(D_MODEL), _mod_spec(l), _layer_spec(l, 4, D_MODEL),
                  tok(256), pl.BlockSpec((tm // S5_CHUNK, S5_GROUPS * S5_W), lambda i: (i, 0)), tok(256),
                  pl.BlockSpec((NA_HEADS, tm, NA_DH), lambda i: (0, i, 0)),
                  _layer_spec(l, 256, 512), _layer_spec(l, 1, 512),
                  _layer_spec(l, N_BRANCH, BRANCH_W, D_MODEL, single_buffer=True),
                  _layer_spec(l, D_MODEL, N_BRANCH * D_MODEL, single_buffer=True),
                  _layer_spec(l, 1, N_BRANCH * D_MODEL),
                  _layer_spec(l, D_MODEL, D_MODEL, single_buffer=True)],
        out_specs=tok(D_MODEL),
        out_shape=jax.ShapeDtypeStruct((T_ALL, D_MODEL), F32),
        scratch_shapes=[pltpu.VMEM((S5_GROUPS // S5_LANE_GROUPS, tm, LANES), F32)],
        compiler_params=_cparams(("parallel",)),
        name="merge",
    )(x, mod, g_norm, ret_o, s5_y, gla_o, na_o, wglu, bglu, wbr, wmg, bmg, wout)


FF_TILE = 1024


def _mlp_kernel(x_ref, mod_ref, g_ref, w1_hbm, w2_hbm, *rest, layer):
    *o_refs, w1_s, w2_s, sem = rest
    mod = mod_ref[0, 0]
    nj = D_FF // FF_TILE
    first_step = pl.program_id(0) == 0

    def w1_copy(j):
        cols = pl.ds(j * FF_TILE, FF_TILE)
        return pltpu.make_async_copy(w1_hbm.at[layer, :, cols], w1_s.at[:, cols], sem.at[j])

    def w2_copy(j):
        rows = pl.ds(j * FF_TILE, FF_TILE)
        return pltpu.make_async_copy(w2_hbm.at[layer, rows, :], w2_s.at[rows, :], sem.at[nj + j])

    @pl.when(first_step)
    def _start_weight_copies():
        for j in range(nj):
            w1_copy(j).start()
            w2_copy(j).start()

    def wait_on_first_step(copy):
        pl.when(first_step)(copy.wait)

    halves = [slice(0, TOKEN_TILE // 2), slice(TOKEN_TILE // 2, TOKEN_TILE)]
    xs = [x_ref[r, :] for r in halves]
    hbs = [(_rms(x, g_ref[0, 2:3]) * (1.0 + mod[4:5]) + mod[3:4]).astype(BF16) for x in xs]
    up = lambda hb, j: _bdot(hb, w1_s[:, j * FF_TILE:(j + 1) * FF_TILE])
    fs = [None, None]
    wait_on_first_step(w1_copy(0))
    pres = [up(hb, 0) for hb in hbs]
    for j in range(nj):
        for i in range(2):
            if i == 0:
                wait_on_first_step(w2_copy(j))
            a = jnp.maximum(pres[i], 0.0)
            part = _bdot(a * a, w2_s[j * FF_TILE:(j + 1) * FF_TILE, :])
            fs[i] = part if fs[i] is None else fs[i] + part
            if j + 1 < nj:
                if i == 0:
                    wait_on_first_step(w1_copy(j + 1))
                pres[i] = up(hbs[i], j + 1)
    y = jnp.concatenate([x + mod[5:6] * _rms(f, g_ref[0, 3:4]) for x, f in zip(xs, fs)], axis=0)
    if len(o_refs) == 1:
        o_refs[0][...] = y
    else:
        ctx_tiles = T_CTX // TOKEN_TILE

        @pl.when(pl.program_id(0) < ctx_tiles)
        def _store_ctx():
            o_refs[0][...] = y

        @pl.when(pl.program_id(0) >= ctx_tiles)
        def _store_lat():
            o_refs[1][...] = y


def _mlp(l, x, mod, g_norm, w1, w2, *, split_out):
    tm = TOKEN_TILE
    tok = pl.BlockSpec((tm, D_MODEL), lambda i: (i, 0))
    if split_out:
        ctx_tiles = T_CTX // tm
        out_specs = [pl.BlockSpec((tm, D_MODEL), lambda i: (jnp.minimum(i, ctx_tiles - 1), 0)),
                     pl.BlockSpec((tm, D_MODEL), lambda i: (jnp.maximum(i - ctx_tiles, 0), 0))]
        out_shape = [jax.ShapeDtypeStruct((T_CTX, D_MODEL), F32), jax.ShapeDtypeStruct((T_LAT, D_MODEL), F32)]
    else:
        out_specs = tok
        out_shape = jax.ShapeDtypeStruct((T_ALL, D_MODEL), F32)
    return pl.pallas_call(
        functools.partial(_mlp_kernel, layer=l),
        grid=(T_ALL // tm,),
        in_specs=[tok, _mod_spec(l), _layer_spec(l, 4, D_MODEL), _ANY, _ANY],
        out_specs=out_specs,
        out_shape=out_shape,
        scratch_shapes=[pltpu.VMEM((D_MODEL, D_FF), F32), pltpu.VMEM((D_FF, D_MODEL), F32),
                        pltpu.SemaphoreType.DMA((2 * D_FF // FF_TILE,))],
        compiler_params=_cparams(("arbitrary",), MLP_VMEM_LIMIT),
        name="mlp",
    )(x, mod, g_norm, w1, w2)


def _rope_tables():
    half = RET_DK // 2
    nf = half // 2
    t = jnp.arange(DEC_SEQ)
    row = (t // GRID_W).astype(F32)
    col = (t % GRID_W).astype(F32)
    inv = ROPE_BASE ** (-jnp.arange(nf, dtype=F32) / nf)
    ang_r = row[:, None] * inv[None, :]
    ang_c = col[:, None] * inv[None, :]
    cos = jnp.concatenate([jnp.cos(ang_r)] * 2 + [jnp.cos(ang_c)] * 2, axis=1)
    sin = jnp.concatenate([-jnp.sin(ang_r), jnp.sin(ang_r), -jnp.sin(ang_c), jnp.sin(ang_c)], axis=1)
    return jnp.tile(cos, (1, RET_HEADS)), jnp.tile(sin, (1, RET_HEADS))


def _gla_state_in(st):
    eye = jnp.eye(GLA_HEADS, dtype=st.dtype)
    t = jnp.einsum('bldhkv,hg->bldhvgk', st, eye)
    return t.reshape(st.shape[0], st.shape[1], 2, GLA_HEADS * GLA_DV, GLA_HEADS * GLA_DK)


def kernel(x_prompt, x_sample, c, cache_na_k, cache_na_v, state_ret, state_s5, state_gla, c_ctx, w_ada, b_ada, g_norm, w_in, ret_log_decay, ret_gn, s5_lambda_re, s5_lambda_im, s5_log_dt, s5_b_re, s5_b_im, s5_c_re, s5_c_im, s5_d, s5_w_glu, s5_b_glu, gla_w_gate, gla_b_gate, gla_gn, na_rpb, w_branch, w_merge, b_merge, w_out, w_mlp1, w_mlp2):
    depth = w_in.shape[0]
    x = (x_prompt.reshape(T_CTX, D_MODEL), x_sample.reshape(T_LAT, D_MODEL))
    cc = jnp.concatenate([c_ctx[None], c, jnp.zeros((N_MOD_ROWS - 1 - DEC_BATCH, D_MODEL), F32)], axis=0)
    mod = _ada(cc, w_ada, b_ada).reshape(depth, N_MOD_ROWS, 6, D_MODEL)

    cos, sin = _rope_tables()
    w_in_t = w_in.transpose(0, 2, 1)
    ret_gn3, gla_gn3 = (a.reshape(depth, 1, BRANCH_W) for a in (ret_gn, gla_gn))
    s5_d_rows = jnp.broadcast_to(s5_d.reshape(depth, S5_GROUPS, 1, S5_GROUP_CH),
                                 (depth, S5_GROUPS, S5_CHUNK, S5_GROUP_CH)).reshape(depth, 1, S5_GROUPS * S5_W)
    b_glu3 = s5_b_glu.reshape(depth, 1, 2 * BRANCH_W)
    b_mg3 = b_merge.reshape(depth, 1, N_BRANCH * D_MODEL)
    cache_k = cache_na_k.transpose(0, 1, 3, 2, 4)
    cache_v = cache_na_v.transpose(0, 1, 3, 2, 4)
    na_tb = _na_tables(na_rpb)
    gla_s0 = _gla_state_in(state_gla)
    s5_tables, s5_lam = _s5_tables(s5_lambda_re, s5_lambda_im, s5_log_dt, s5_b_re, s5_b_im,
                                   s5_c_re, s5_c_im)
    s5_x0 = state_s5.transpose(1, 2, 0, 3, 5, 4).reshape(depth, 2, DEC_BATCH, S5_GROUPS * S5_P2)

    ks_l, vs_l, ret_l, s5_l, gla_l = [], [], [], [], []
    for l in range(depth):
        proj = _inproj(l, x, mod, g_norm, w_in_t)
        if l == 0:
            x, proj = proj[0], proj[1:]
        ret, su, gqk, gv, gg, glr, nq, nk, nv = proj

        ret_o, st_ret = _retention(l, ret, ret_log_decay, ret_gn3, latent=False)
        ret_o, = _retention(l, ret, ret_log_decay, ret_gn3, latent=True, cos=cos, sin=sin, s0=state_ret,
                            prev=ret_o)

        s5_y, s5_fin = _s5(l, su, *s5_tables, s5_d_rows, s5_lam, s5_x0)

        gla_o, st_gla = _gla(l, gqk, gv, gg, glr, gla_w_gate, gla_b_gate, gla_gn3, latent=False)
        gla_o, = _gla(l, gqk, gv, gg, glr, gla_w_gate, gla_b_gate, gla_gn3, latent=True, s0=gla_s0,
                      prev=gla_o)

        na_o = _attn_ctx(nq, nk, nv)
        na_o = _attn_lat(l, nq, nk, nv, cache_k, cache_v, na_tb, na_o)

        x = _merge(l, x, mod, g_norm, ret_o, s5_y, gla_o, na_o,
                   s5_w_glu, b_glu3, w_branch, w_merge, b_mg3, w_out)
        x = _mlp(l, x, mod, g_norm, w_mlp1, w_mlp2, split_out=(l == depth - 1))

        ks_l.append(nk)
        vs_l.append(nv)
        ret_l.append(st_ret)
        s5_l.append(s5_fin)
        gla_l.append(st_gla)

    y_prompt = x[0].reshape(BATCH, SEQ, D_MODEL)
    y_sample = x[1].reshape(DEC_BATCH, DEC_SEQ, D_MODEL)

    def cache_out(per_layer):
        a = jnp.stack(per_layer, axis=0)[:, :, :T_CTX].reshape(depth, NA_HEADS, BATCH, SEQ, NA_DH)
        return a.transpose(2, 0, 3, 1, 4)

    s5_out = jnp.stack(s5_l, axis=0).reshape(depth, 2, BATCH, S5_GROUPS, 2, S5_STATE)
    return (y_prompt, y_sample, cache_out(ks_l), cache_out(vs_l), jnp.stack(ret_l, axis=1),
            s5_out.transpose(2, 0, 1, 3, 5, 4), jnp.stack(gla_l, axis=1))
```
